```python
import math
import jax, jax.numpy as jnp
from jax import lax
import numpy as np

D_MODEL = 2048
BATCH = 4
SEQ = 4096
DEPTH = 2

CTX_LEN = 256
GRID_W = 64
N_EVEN = (DEPTH + 1) // 2
N_ODD = DEPTH // 2

GDN_HEADS = 8
GDN_HEAD_DIM = 128
GDN_WIDTH = GDN_HEADS * GDN_HEAD_DIM
GDN_CHUNK = 64
SHORT_CONV = 3
S5_WIDTH = D_MODEL - GDN_WIDTH
S5_GROUP = 16
S5_GROUPS = S5_WIDTH // S5_GROUP
S5_STATE = 64
EVEN_PROJ = 4 * GDN_WIDTH + 4 * GDN_HEADS + S5_WIDTH
ATT_HEADS = 16
ATT_KV_HEADS = 4
ATT_HEAD_DIM = 128
ATT_Q_WIDTH = ATT_HEADS * ATT_HEAD_DIM
ATT_KV_WIDTH = ATT_KV_HEADS * ATT_HEAD_DIM
ATT_SCALE = ATT_HEAD_DIM ** -0.5
Q_BLOCK = 128
ROPE_AXIS_DIM = ATT_HEAD_DIM // 2
ROPE_THETA = 10000.0
FFN_HIDDEN = 5632
FFN_CONV = 3
ALPHA = (2 * DEPTH) ** 0.25
BETA = (8 * DEPTH) ** -0.25
LN_EPS = 1e-6
RMS_EPS = 1e-6
L2_EPS = 1e-6
F32 = jnp.float32

kernel_name = 'hybrid_gdn_s5_gqa_diffusion_block'


def _layernorm(x, g, b):
    xf = x.astype(F32)
    xc = xf - jnp.mean(xf, -1, keepdims=True)
    var = jnp.mean(xc * xc, -1, keepdims=True)
    return (xc * lax.rsqrt(var + LN_EPS) * g.astype(F32) + b.astype(F32)).astype(x.dtype)


def _rms(x, eps):
    xf = x.astype(F32)
    return xf * lax.rsqrt(jnp.mean(xf * xf, -1, keepdims=True) + eps)


def _l2norm(x):
    xf = x.astype(F32)
    return xf * lax.rsqrt(jnp.sum(xf * xf, -1, keepdims=True) + L2_EPS)


def _dwconv(x, w):
    pad = w.shape[0] // 2
    return lax.conv_general_dilated(x, w[:, None, :].astype(x.dtype), window_strides=(1,),
                                    padding=[(pad, pad)], dimension_numbers=('NWC', 'WIO', 'NWC'),
                                    feature_group_count=x.shape[-1])


def _gdn_chunk_scan(q, k, v, g, beta, s0):
    b, t, h, dk = q.shape
    dv = v.shape[-1]
    cs = GDN_CHUNK
    n = t // cs

    def to_chunks(a):
        a = a.reshape((b, n, cs, h) + a.shape[3:])
        return jnp.moveaxis(a, (1, 3), (0, 2))

    qc, kc, vc = to_chunks(q), to_chunks(k), to_chunks(v)
    gc = jnp.cumsum(to_chunks(g), axis=-1)
    bc = to_chunks(beta)
    incl = jnp.tril(jnp.ones((cs, cs), bool))
    strict = jnp.tril(jnp.ones((cs, cs), bool), -1)
    diff = gc[..., :, None] - gc[..., None, :]
    decay = jnp.where(incl, jnp.exp(jnp.where(incl, diff, 0.0)), 0.0)
    kb = kc * bc[..., None]
    m = jnp.where(strict, jnp.einsum('nbhid,nbhjd->nbhij', kb, kc) * decay, 0.0)
    rhs = jnp.concatenate([vc * bc[..., None], kb * jnp.exp(gc)[..., None]], axis=-1)
    sol = lax.linalg.triangular_solve(m + jnp.eye(cs, dtype=F32), rhs, left_side=True, lower=True,
                                      unit_diagonal=True)
    uc, wc = sol[..., :dv], sol[..., dv:]
    att = jnp.where(incl, jnp.einsum('nbhid,nbhjd->nbhij', qc, kc) * decay, 0.0)

    def step(s, inp):
        q_i, k_i, u_i, w_i, a_i, g_i = inp
        v_new = u_i - jnp.einsum('bhck,bhkv->bhcv', w_i, s)
        o_i = (jnp.einsum('bhck,bhkv->bhcv', q_i * jnp.exp(g_i)[..., None], s)
               + jnp.einsum('bhij,bhjv->bhiv', a_i, v_new))
        g_end = g_i[..., -1:]
        s = (s * jnp.exp(g_end)[..., None]
             + jnp.einsum('bhck,bhcv->bhkv', k_i * jnp.exp(g_end - g_i)[..., None], v_new))
        return s, o_i

    s_fin, o = lax.scan(step, s0, (qc, kc, uc, wc, att, gc))
    o = jnp.moveaxis(o, (0, 2), (1, 3)).reshape(b, t, h, dv)
    return o, s_fin


def _gdn_prep(proj, conv_w, a_log, dt_bias):
    b, t, _ = proj.shape
    hd = (b, t, GDN_HEADS, GDN_HEAD_DIM)
    qkv = jax.nn.silu(_dwconv(proj[..., :3 * GDN_WIDTH], conv_w))
    q = _l2norm(qkv[..., :GDN_WIDTH].reshape(hd)) * GDN_HEAD_DIM ** -0.5
    k = _l2norm(qkv[..., GDN_WIDTH:2 * GDN_WIDTH].reshape(hd))
    v = qkv[..., 2 * GDN_WIDTH:3 * GDN_WIDTH].reshape(hd).astype(F32)
    z = proj[..., 3 * GDN_WIDTH:4 * GDN_WIDTH].reshape(hd)
    ab = proj[..., 4 * GDN_WIDTH:4 * GDN_WIDTH + 4 * GDN_HEADS].reshape(b, t, 4, GDN_HEADS).astype(F32)
    g = -jnp.exp(a_log.astype(F32)) * jax.nn.softplus(ab[:, :, :2] + dt_bias.astype(F32))
    beta = jax.nn.sigmoid(ab[:, :, 2:])
    u = proj[..., 4 * GDN_WIDTH + 4 * GDN_HEADS:]
    return q, k, v, g, beta, z, u


def _gdn_bidir(q, k, v, g, beta, s_f, s_b):
    flip = lambda a: jnp.flip(a, axis=1)
    o_f, s_f = _gdn_chunk_scan(q, k, v, g[:, :, 0], beta[:, :, 0], s_f)
    o_b, s_b = _gdn_chunk_scan(flip(q), flip(k), flip(v), flip(g[:, :, 1]), flip(beta[:, :, 1]), s_b)
    return o_f + flip(o_b), s_f, s_b


def _lin_rec_op(e_i, e_j):
    a_i, b_i = e_i
    a_j, b_j = e_j
    return a_j * a_i, a_j * b_i + b_j


def _s5_scan(u, lam_re, lam_im, log_dt, b_re, b_im, c_re, c_im, x0):
    t = u.shape[1]
    lam = lax.complex(lam_re.astype(F32), lam_im.astype(F32))
    dt = jnp.exp(log_dt.astype(F32))[:, None]
    lam_bar = jnp.exp(lam * dt)
    b_bar = ((lam_bar - 1.0) / lam)[:, :, None] * lax.complex(b_re.astype(F32), b_im.astype(F32))
    bu = jnp.einsum('gpc,btgc->btgp', b_bar, u.astype(jnp.complex64))
    bu = bu.at[:, 0].add(lam_bar * x0)
    a = jnp.broadcast_to(lam_bar, (1, t) + lam_bar.shape)
    _, xs = lax.associative_scan(_lin_rec_op, (a, bu), axis=1)
    c = lax.complex(c_re.astype(F32), c_im.astype(F32))
    y = jnp.real(jnp.einsum('gcp,btgp->btgc', c, xs))
    return y, xs[:, -1]


def _s5_bidir(u, lam_re, lam_im, log_dt, b_re, b_im, c_re, c_im, x_f, x_b):
    b, t, _ = u.shape
    ug = u.astype(F32).reshape(b, t, S5_GROUPS, S5_GROUP)
    y_f, x_f = _s5_scan(ug, lam_re[0], lam_im[0], log_dt[0], b_re[0], b_im[0], c_re[0], c_im[0], x_f)
    y_b, x_b = _s5_scan(jnp.flip(ug, 1), lam_re[1], lam_im[1], log_dt[1], b_re[1], b_im[1], c_re[1], c_im[1], x_b)
    return (y_f + jnp.flip(y_b, 1)).reshape(b, t, S5_WIDTH), x_f, x_b


def _even_merge(o, z, y, u, gdn_norm, d_skip, w_glu, b_glu, w_out):
    b, t = o.shape[:2]
    a_out = (_rms(o, RMS_EPS) * gdn_norm.astype(F32) * jax.nn.silu(z.astype(F32))).reshape(b, t, GDN_WIDTH)
    s = jax.nn.gelu(y + d_skip.astype(F32) * u.astype(F32))
    s = s * jax.nn.sigmoid(s @ w_glu.astype(F32) + b_glu.astype(F32))
    return jnp.concatenate([a_out, s], axis=-1).astype(w_out.dtype) @ w_out


def _even_mixer(a_lat, a_ctx, w_in, conv_qkv, a_log, dt_bias, gdn_norm, lam_re, lam_im, log_dt,
                b_re, b_im, c_re, c_im, d_skip, w_glu, b_glu, w_out, need_ctx):
    bsz = a_lat.shape[0]
    s0 = jnp.zeros((bsz, GDN_HEADS, GDN_HEAD_DIM, GDN_HEAD_DIM), F32)
    x0 = jnp.zeros((bsz, S5_GROUPS, S5_STATE), jnp.complex64)
    ssm = (lam_re, lam_im, log_dt, b_re, b_im, c_re, c_im)
    qc, kc, vc, gc, bc, zc, uc = _gdn_prep(a_ctx @ w_in, conv_qkv, a_log, dt_bias)
    oc, sf, sb = _gdn_bidir(qc, kc, vc, gc, bc, s0, s0)
    yc, xf, xb = _s5_bidir(uc, *ssm, x0, x0)
    ql, kl, vl, gl, bl, zl, ul = _gdn_prep(a_lat @ w_in, conv_qkv, a_log, dt_bias)
    ol, _, _ = _gdn_bidir(ql, kl, vl, gl, bl, sf, sb)
    yl, _, _ = _s5_bidir(ul, *ssm, xf, xb)
    m_lat = _even_merge(ol, zl, yl, ul, gdn_norm, d_skip, w_glu, b_glu, w_out)
    m_ctx = _even_merge(oc, zc, yc, uc, gdn_norm, d_skip, w_glu, b_glu, w_out) if need_ctx else None
    return m_lat, m_ctx


def _axial_rope(x):
    b, n, h, d = x.shape
    rows = n // GRID_W
    row = jnp.repeat(jnp.arange(rows, dtype=F32), GRID_W)
    col = jnp.tile(jnp.arange(GRID_W, dtype=F32), rows)
    half = ROPE_AXIS_DIM // 2
    inv = ROPE_THETA ** (-jnp.arange(half, dtype=F32) / half)
    ang = jnp.stack([row[:, None] * inv, col[:, None] * inv], axis=1)
    cos = jnp.cos(ang)[None, :, None]
    sin = jnp.sin(ang)[None, :, None]
    xr = x.astype(F32).reshape(b, n, h, 2, ROPE_AXIS_DIM)
    x1, x2 = xr[..., :half], xr[..., half:]
    out = jnp.concatenate([x1 * cos - x2 * sin, x2 * cos + x1 * sin], axis=-1)
    return out.reshape(b, n, h, d)


def _attend_blocks(q, k, v):
    b, t, h, d = q.shape
    kvh = k.shape[2]
    grp = h // kvh
    qb = q.reshape(b, t // Q_BLOCK, Q_BLOCK, kvh, grp, d).transpose(1, 0, 2, 3, 4, 5)

    def block(qblk):
        s = jnp.einsum('bqkgd,bskd->bkgqs', qblk, k, preferred_element_type=F32) * ATT_SCALE
        p = jax.nn.softmax(s, axis=-1).astype(v.dtype)
        return jnp.einsum('bkgqs,bskd->bqkgd', p, v)

    o = lax.map(block, qb)
    return o.transpose(1, 0, 2, 3, 4, 5).reshape(b, t, h * d)


def _odd_mixer(a_lat, a_ctx, w_in, q_norm, k_norm, w_out, need_ctx):
    b, t, _ = a_lat.shape
    sc = a_ctx.shape[1]
    dt = a_lat.dtype
    qn, kn = q_norm.astype(F32), k_norm.astype(F32)
    proj = a_lat @ w_in
    q = _rms(proj[..., :ATT_Q_WIDTH].reshape(b, t, ATT_HEADS, ATT_HEAD_DIM), RMS_EPS) * qn
    k = _rms(proj[..., ATT_Q_WIDTH:ATT_Q_WIDTH + ATT_KV_WIDTH].reshape(b, t, ATT_KV_HEADS, ATT_HEAD_DIM), RMS_EPS) * kn
    v = proj[..., ATT_Q_WIDTH + ATT_KV_WIDTH:].reshape(b, t, ATT_KV_HEADS, ATT_HEAD_DIM)
    q, k = _axial_rope(q), _axial_rope(k)
    kv_c = a_ctx @ w_in[:, ATT_Q_WIDTH:]
    k_c = _rms(kv_c[..., :ATT_KV_WIDTH].reshape(b, sc, ATT_KV_HEADS, ATT_HEAD_DIM), RMS_EPS) * kn
    v_c = kv_c[..., ATT_KV_WIDTH:].reshape(b, sc, ATT_KV_HEADS, ATT_HEAD_DIM)
    k_all = jnp.concatenate([k_c, k], axis=1).astype(dt)
    v_all = jnp.concatenate([v_c, v], axis=1)
    m_lat = _attend_blocks(q.astype(dt), k_all, v_all) @ w_out
    m_ctx = None
    if need_ctx:
        q_c = _rms((a_ctx @ w_in[:, :ATT_Q_WIDTH]).reshape(b, sc, ATT_HEADS, ATT_HEAD_DIM), RMS_EPS) * qn
        m_ctx = _attend_blocks(q_c.astype(dt), k_c.astype(dt), v_c) @ w_out
    return m_lat, m_ctx


def _conv_ffn(h, w_up, conv_w, conv_b, w_down):
    a = _dwconv(h @ w_up, conv_w) + conv_b
    val, gate = jnp.split(a, 2, axis=-1)
    return (val * jax.nn.silu(gate)) @ w_down


def setup_inputs(seed: int = 0) -> dict:
    key = jax.random.key(seed)
    ks = iter(jax.random.split(key, 48))
    f = jnp.float32

    def nrm(shape, scale):
        return jax.random.normal(next(ks), shape, f) * scale

    def unif(shape, lo, hi):
        return jax.random.uniform(next(ks), shape, f, minval=lo, maxval=hi)

    D, F = D_MODEL, FFN_HIDDEN
    G, P, CG = S5_GROUPS, S5_STATE, S5_GROUP
    x = nrm((BATCH, SEQ, D), 1.0)
    c = nrm((BATCH, D), 1.0)
    ctx = nrm((BATCH, CTX_LEN, D), 1.0)
    c_ctx = nrm((D,), 1.0)
    mod_w = nrm((DEPTH, D, 6 * D), 0.5 * D ** -0.5)
    mod_b = nrm((DEPTH, 6 * D), 0.02)
    ln1_g = 1.0 + nrm((DEPTH, D), 0.02)
    ln1_b = nrm((DEPTH, D), 0.02)
    ln2_g = 1.0 + nrm((DEPTH, D), 0.02)
    ln2_b = nrm((DEPTH, D), 0.02)
    ffn_w_up = nrm((DEPTH, D, 2 * F), D ** -0.5)
    ffn_conv_w = nrm((DEPTH, FFN_CONV, 2 * F), FFN_CONV ** -0.5)
    ffn_conv_b = nrm((DEPTH, 2 * F), 0.02)
    ffn_w_down = nrm((DEPTH, F, D), F ** -0.5 * BETA)
    e_w_in = nrm((N_EVEN, D, EVEN_PROJ), D ** -0.5)
    e_conv_qkv = nrm((N_EVEN, SHORT_CONV, 3 * GDN_WIDTH), SHORT_CONV ** -0.5)
    e_a_log = jnp.log(unif((N_EVEN, 2, GDN_HEADS), 1.0, 16.0))
    dt0 = jnp.exp(unif((N_EVEN, 2, GDN_HEADS), math.log(1e-3), math.log(1e-1)))
    e_dt_bias = dt0 + jnp.log(-jnp.expm1(-dt0))
    e_gdn_norm = 1.0 + nrm((N_EVEN, GDN_HEAD_DIM), 0.02)
    e_lam_re = -0.5 + nrm((N_EVEN, 2, G, P), 0.01)
    e_lam_im = jnp.broadcast_to(math.pi * jnp.arange(P, dtype=f), (N_EVEN, 2, G, P))
    e_log_dt = unif((N_EVEN, 2, G), math.log(1e-3), math.log(1e-1))
    e_b_re = nrm((N_EVEN, 2, G, P, CG), (2 * CG) ** -0.5)
    e_b_im = nrm((N_EVEN, 2, G, P, CG), (2 * CG) ** -0.5)
    e_c_re = nrm((N_EVEN, 2, G, CG, P), P ** -0.5)
    e_c_im = nrm((N_EVEN, 2, G, CG, P), P ** -0.5)
    e_d_skip = nrm((N_EVEN, S5_WIDTH), 1.0)
    e_w_glu = nrm((N_EVEN, S5_WIDTH, S5_WIDTH), S5_WIDTH ** -0.5)
    e_b_glu = nrm((N_EVEN, S5_WIDTH), 0.02)
    e_w_out = nrm((N_EVEN, D, D), D ** -0.5 * BETA)
    o_w_in = nrm((N_ODD, D, ATT_Q_WIDTH + 2 * ATT_KV_WIDTH), D ** -0.5)
    o_q_norm = 1.0 + nrm((N_ODD, ATT_HEAD_DIM), 0.02)
    o_k_norm = 1.0 + nrm((N_ODD, ATT_HEAD_DIM), 0.02)
    o_w_out = nrm((N_ODD, ATT_Q_WIDTH, D), ATT_Q_WIDTH ** -0.5 * BETA)
    return {'x': x, 'c': c, 'ctx': ctx, 'c_ctx': c_ctx, 'mod_w': mod_w, 'mod_b': mod_b,
            'ln1_g': ln1_g, 'ln1_b': ln1_b, 'ln2_g': ln2_g, 'ln2_b': ln2_b,
            'ffn_w_up': ffn_w_up, 'ffn_conv_w': ffn_conv_w, 'ffn_conv_b': ffn_conv_b, 'ffn_w_down': ffn_w_down,
            'e_w_in': e_w_in, 'e_conv_qkv': e_conv_qkv, 'e_a_log': e_a_log, 'e_dt_bias': e_dt_bias,
            'e_gdn_norm': e_gdn_norm, 'e_lam_re': e_lam_re, 'e_lam_im': e_lam_im, 'e_log_dt': e_log_dt,
            'e_b_re': e_b_re, 'e_b_im': e_b_im, 'e_c_re': e_c_re, 'e_c_im': e_c_im, 'e_d_skip': e_d_skip,
            'e_w_glu': e_w_glu, 'e_b_glu': e_b_glu, 'e_w_out': e_w_out,
            'o_w_in': o_w_in, 'o_q_norm': o_q_norm, 'o_k_norm': o_k_norm, 'o_w_out': o_w_out}


def reference(x, c, ctx, c_ctx, mod_w, mod_b, ln1_g, ln1_b, ln2_g, ln2_b,
              ffn_w_up, ffn_conv_w, ffn_conv_b, ffn_w_down,
              e_w_in, e_conv_qkv, e_a_log, e_dt_bias, e_gdn_norm, e_lam_re, e_lam_im, e_log_dt,
              e_b_re, e_b_im, e_c_re, e_c_im, e_d_skip, e_w_glu, e_b_glu, e_w_out,
              o_w_in, o_q_norm, o_k_norm, o_w_out):
    h_lat, h_ctx = x, ctx
    s_lat = jax.nn.silu(c)[:, None, :]
    s_ctx = jax.nn.silu(c_ctx)[None, None, :]
    for i in range(DEPTH):
        last = i == DEPTH - 1
        j = i // 2
        m_l = jnp.split(s_lat @ mod_w[i] + mod_b[i], 6, axis=-1)
        m_c = jnp.split(s_ctx @ mod_w[i] + mod_b[i], 6, axis=-1)
        a_lat = h_lat * (1.0 + m_l[1]) + m_l[0]
        a_ctx = h_ctx * (1.0 + m_c[1]) + m_c[0]
        if i % 2 == 0:
            mix_lat, mix_ctx = _even_mixer(a_lat, a_ctx, e_w_in[j], e_conv_qkv[j], e_a_log[j], e_dt_bias[j],
                                           e_gdn_norm[j], e_lam_re[j], e_lam_im[j], e_log_dt[j],
                                           e_b_re[j], e_b_im[j], e_c_re[j], e_c_im[j], e_d_skip[j],
                                           e_w_glu[j], e_b_glu[j], e_w_out[j], not last)
        else:
            mix_lat, mix_ctx = _odd_mixer(a_lat, a_ctx, o_w_in[j], o_q_norm[j], o_k_norm[j], o_w_out[j], not last)
        h_lat = _layernorm(ALPHA * h_lat + m_l[2] * mix_lat, ln1_g[i], ln1_b[i])
        f_lat = _conv_ffn(h_lat * (1.0 + m_l[4]) + m_l[3], ffn_w_up[i], ffn_conv_w[i], ffn_conv_b[i], ffn_w_down[i])
        h_lat = _layernorm(ALPHA * h_lat + m_l[5] * f_lat, ln2_g[i], ln2_b[i])
        if not last:
            h_ctx = _layernorm(ALPHA * h_ctx + m_c[2] * mix_ctx, ln1_g[i], ln1_b[i])
            f_ctx = _conv_ffn(h_ctx * (1.0 + m_c[4]) + m_c[3], ffn_w_up[i], ffn_conv_w[i], ffn_conv_b[i], ffn_w_down[i])
            h_ctx = _layernorm(ALPHA * h_ctx + m_c[5] * f_ctx, ln2_g[i], ln2_b[i])
    return h_lat
```

```python
import functools
import math

import jax
import jax.numpy as jnp
from jax import lax
from jax.experimental import pallas as pl
from jax.experimental.pallas import tpu as pltpu

F32 = jnp.float32
BF16 = jnp.bfloat16

DEPTH = 2
GDN_HEADS = 8
HEAD_DIM = 128
GDN_WIDTH = GDN_HEADS * HEAD_DIM
GDN_CHUNK = 64
S5_GROUP = 16
S5_STATE = 64
S5_CHUNK = 16
S5_TILE = S5_CHUNK * S5_GROUP
S5_GROUPS_PER_STEP = 4
ATT_HEADS = 16
ATT_KV_HEADS = 4
ATT_GROUP = ATT_HEADS // ATT_KV_HEADS
ATT_SCALE = HEAD_DIM ** -0.5
GRID_W = 64
ROPE_THETA = 10000.0
FFN_CHUNK = 512
ALPHA = (2 * DEPTH) ** 0.25
LN_EPS = 1e-6
RMS_EPS = 1e-6
L2_EPS = 1e-6
VMEM_LIMIT_BYTES = 56 * 1024 * 1024
NEG_BIG = -1e30


def _pick(n, target, mult):
    best = None
    for d in range(mult, min(n, target) + 1, mult):
        if n % d == 0:
            best = d
    assert best is not None, (n, target, mult)
    return best


def _params(*sem):
    return pltpu.CompilerParams(dimension_semantics=sem, vmem_limit_bytes=VMEM_LIMIT_BYTES)


def _dot(a, b):
    return jnp.dot(a, b, preferred_element_type=F32)


def _dot_nt(a, b):
    return lax.dot_general(a, b, (((1,), (1,)), ((), ())), preferred_element_type=F32)


def _split3(a):
    p0 = a.astype(BF16)
    r = a - p0.astype(F32)
    p1 = r.astype(BF16)
    p2 = (r - p1.astype(F32)).astype(BF16)
    return p0, p1, p2


def _rows_in_batch(i, tm, tiles_per_batch):
    return (i % tiles_per_batch) * tm + lax.broadcasted_iota(jnp.int32, (tm, 1), 0)


def _modulate(x, is_ctx, mb_ref, mc_ref, shift_i, scale_i):
    scale = jnp.where(is_ctx, mc_ref[scale_i:scale_i + 1, :], mb_ref[scale_i:scale_i + 1, :])
    shift = jnp.where(is_ctx, mc_ref[shift_i:shift_i + 1, :], mb_ref[shift_i:shift_i + 1, :])
    return x * (1.0 + scale) + shift


def _layernorm(r, g, b):
    xc = r - jnp.mean(r, -1, keepdims=True)
    var = jnp.mean(xc * xc, -1, keepdims=True)
    return xc * lax.rsqrt(var + LN_EPS) * g + b


def _silu(x):
    return x * jax.nn.sigmoid(x)


def _seq_edges(rib, ctx, period):
    has_prev = (rib != 0) & (rib != ctx)
    has_next = (rib != ctx - 1) & (rib != period - 1)
    return has_prev, has_next


def _mod_body(c_ref, w_ref, b_ref, o_ref):
    s = _silu(c_ref[...])
    o_ref[...] = _dot(s.astype(BF16), w_ref[...].astype(BF16)) + b_ref[...]


def _mod_call(cs, mod_w, mod_b):
    depth, d, n = mod_w.shape
    tn = _pick(n, 1024, 128)
    return pl.pallas_call(
        _mod_body,
        grid=(depth, n // tn),
        in_specs=[pl.BlockSpec((8, d), lambda l, j: (0, 0)),
                  pl.BlockSpec((None, d, tn), lambda l, j: (l, 0, j)),
                  pl.BlockSpec((None, 1, tn), lambda l, j: (l, 0, j))],
        out_specs=pl.BlockSpec((None, 8, tn), lambda l, j: (l, 0, j)),
        out_shape=jax.ShapeDtypeStruct((depth, 8, n), F32),
        compiler_params=_params("arbitrary", "arbitrary"),
        name="adaln_mod",
    )(cs, mod_w, mod_b.reshape(depth, 1, n))


def _inproj_even_body(x_ref, mb_ref, mc_ref, w_ref, wab_ref, o_ref, ab_ref, xb_ref, *, tm, tpb, ctx):
    i = pl.program_id(0)

    @pl.when(pl.program_id(1) == 0)
    def _():
        rib = _rows_in_batch(i, tm, tpb)
        xb = _modulate(x_ref[...], rib < ctx, mb_ref, mc_ref, 0, 1).astype(BF16)
        xb_ref[...] = xb
        ab_ref[...] = _dot(xb, wab_ref[...])

    o_ref[...] = _dot(xb_ref[...], w_ref[...]).astype(o_ref.dtype)


def _inproj_even_call(h, mod, w_main, w_ab, nb, period, ctx):
    t, d = h.shape
    n = w_main.shape[1]
    tm = _pick(period, 544, 16)
    tn = _pick(n, 1280, 256)
    tpb = period // tm
    body = functools.partial(_inproj_even_body, tm=tm, tpb=tpb, ctx=ctx)
    return pl.pallas_call(
        body,
        grid=(t // tm, n // tn),
        in_specs=[pl.BlockSpec((tm, d), lambda i, j: (i, 0)),
                  pl.BlockSpec((None, 6, d), lambda i, j: (i // tpb, 0, 0)),
                  pl.BlockSpec((None, 6, d), lambda i, j: (nb, 0, 0)),
                  pl.BlockSpec((d, tn), lambda i, j: (0, j)),
                  pl.BlockSpec((d, 128), lambda i, j: (0, 0))],
        out_specs=[pl.BlockSpec((tm, tn), lambda i, j: (i, j)),
                   pl.BlockSpec((tm, 128), lambda i, j: (i, 0))],
        out_shape=[jax.ShapeDtypeStruct((t, n), BF16), jax.ShapeDtypeStruct((t, 128), F32)],
        scratch_shapes=[pltpu.VMEM((tm, d), BF16)],
        compiler_params=_params("arbitrary", "arbitrary"),
        name="even_inproj",
    )(h, mod, mod, w_main, w_ab)


def _gdn_prep_body(x_ref, xn_ref, xp_ref, cw_ref, ab_ref, alog_ref, dtb_ref, tril_ref, triu_ref, e_ref,
                   qkv_ref, kt_ref, gh_ref, gct_ref, *, tr, tpb, ctx, period):
    i = pl.program_id(0)
    sec = pl.program_id(1)
    rib = _rows_in_batch(i, tr, tpb)
    has_prev, has_next = _seq_edges(rib, ctx, period)
    ext = jnp.concatenate([x_ref[...].astype(F32), xn_ref[...].astype(F32), xp_ref[...].astype(F32)], axis=0)
    n = tr + 32
    xprev = pltpu.roll(ext, 1, 0)[:tr]
    xnext = pltpu.roll(ext, n - 1, 0)[:tr]
    cw = cw_ref[...]
    y = (jnp.where(has_prev, xprev, 0.0) * cw[0:1] + ext[:tr] * cw[1:2]
         + jnp.where(has_next, xnext, 0.0) * cw[2:3])
    y = _silu(y)

    def l2n(scale):
        parts = []
        for hh in range(GDN_HEADS):
            yh = y[:, hh * HEAD_DIM:(hh + 1) * HEAD_DIM]
            parts.append(yh * (lax.rsqrt(jnp.sum(yh * yh, -1, keepdims=True) + L2_EPS) * scale))
        return jnp.concatenate(parts, axis=1)

    @pl.when(sec == 0)
    def _():
        qkv_ref[...] = l2n(HEAD_DIM ** -0.5).astype(BF16)
        ab = ab_ref[...]
        lane = lax.broadcasted_iota(jnp.int32, (1, 128), 1)
        xg = ab + dtb_ref[...]
        softplus = jnp.maximum(xg, 0.0) + jnp.log1p(jnp.exp(-jnp.abs(xg)))
        g = -jnp.exp(alog_ref[...]) * softplus
        g0, g1, g2 = _split3(g)
        tril = tril_ref[...]
        triu = triu_ref[...]
        pre = _dot(tril, g0) + _dot(tril, g1) + _dot(tril, g2)
        suf = _dot(triu, g0) + _dot(triu, g1) + _dot(triu, g2)
        gc = jnp.where(lane < GDN_HEADS, pre, suf)
        vals = jnp.where(lane < 2 * GDN_HEADS, gc, jax.nn.sigmoid(ab))
        v0, v1, v2 = _split3(vals)
        e = e_ref[...]
        gh_ref[...] = _dot(v0, e) + _dot(v1, e) + _dot(v2, e)
        gct = gc.T
        for c in range(tr // GDN_CHUNK):
            gct_ref[c] = gct[:2 * GDN_HEADS, c * GDN_CHUNK:(c + 1) * GDN_CHUNK]

    @pl.when(sec == 1)
    def _():
        kn = l2n(1.0)
        qkv_ref[...] = kn.astype(BF16)
        knt = kn.T
        for c in range(tr // GDN_CHUNK):
            kt_ref[c] = knt[:, c * GDN_CHUNK:(c + 1) * GDN_CHUNK].astype(BF16)

    @pl.when(sec == 2)
    def _():
        qkv_ref[...] = y.astype(BF16)


def _gdn_prep_call(proj, ab, conv_w, alog_row, dtb_row, period, ctx):
    t = proj.shape[0]
    w = GDN_WIDTH
    tr = _pick(period, 256, GDN_CHUNK)
    tpb = period // tr
    hb = tr // 16
    nh = t // 16
    nck = tr // GDN_CHUNK
    r = jnp.arange(tr)
    same = (r[:, None] // GDN_CHUNK) == (r[None, :] // GDN_CHUNK)
    tril = (same & (r[:, None] >= r[None, :])).astype(BF16)
    triu = (same & (r[:, None] <= r[None, :])).astype(BF16)
    src = jnp.arange(128)[:, None]
    dst = jnp.arange(w)[None, :]
    expand = ((dst % HEAD_DIM < 4) & (src == (dst % HEAD_DIM) * GDN_HEADS + dst // HEAD_DIM)).astype(BF16)
    body = functools.partial(_gdn_prep_body, tr=tr, tpb=tpb, ctx=ctx, period=period)
    return pl.pallas_call(
        body,
        grid=(t // tr, 3),
        in_specs=[pl.BlockSpec((tr, w), lambda i, s: (i, s)),
                  pl.BlockSpec((16, w), lambda i, s: (jnp.minimum((i + 1) * hb, nh - 1), s)),
                  pl.BlockSpec((16, w), lambda i, s: (jnp.maximum(i * hb - 1, 0), s)),
                  pl.BlockSpec((3, w), lambda i, s: (0, s)),
                  pl.BlockSpec((tr, 128), lambda i, s: (i, 0)),
                  pl.BlockSpec((1, 128), lambda i, s: (0, 0)),
                  pl.BlockSpec((1, 128), lambda i, s: (0, 0)),
                  pl.BlockSpec((tr, tr), lambda i, s: (0, 0)),
                  pl.BlockSpec((tr, tr), lambda i, s: (0, 0)),
                  pl.BlockSpec((128, w), lambda i, s: (0, 0))],
        out_specs=[pl.BlockSpec((tr, w), lambda i, s: (i, s)),
                   pl.BlockSpec((nck, w, GDN_CHUNK), lambda i, s: (i, 0, 0)),
                   pl.BlockSpec((tr, w), lambda i, s: (i, 0)),
                   pl.BlockSpec((nck, 2 * GDN_HEADS, GDN_CHUNK), lambda i, s: (i, 0, 0))],
        out_shape=[jax.ShapeDtypeStruct((t, 3 * w), BF16),
                   jax.ShapeDtypeStruct((t // GDN_CHUNK, w, GDN_CHUNK), BF16),
                   jax.ShapeDtypeStruct((t, w), F32),
                   jax.ShapeDtypeStruct((t // GDN_CHUNK, 2 * GDN_HEADS, GDN_CHUNK), F32)],
        compiler_params=_params("arbitrary", "arbitrary"),
        name="gdn_prep",
    )(proj, proj, proj, conv_w, ab, alog_row, dtb_row, tril, triu, expand)


def _gdn_scan_body(q_ref, k_ref, v_ref, kt_ref, gh_ref, gct_ref, o_ref,
                   uf, ub, wqf, wqb, atf, atb, kpf, kpb, egf, egb, *, nch, nctx):
    head = pl.program_id(1)
    cs = GDN_CHUNK
    ii = lax.broadcasted_iota(jnp.int32, (cs, cs), 0)
    jj = lax.broadcasted_iota(jnp.int32, (cs, cs), 1)
    masks = ((ii >= jj, ii > jj), (ii <= jj, ii < jj))
    us, wqs, ats, kps, egs = (uf, ub), (wqf, wqb), (atf, atb), (kpf, kpb), (egf, egb)

    def intra(c, carry):
        r0 = pl.multiple_of(c * cs, cs)
        r1 = pl.multiple_of(c * 2 * cs, 2 * cs)
        q = q_ref[pl.ds(r0, cs), :]
        k = k_ref[pl.ds(r0, cs), :]
        v = v_ref[pl.ds(r0, cs), :]
        kt = kt_ref[c]
        g4 = gh_ref[pl.ds(r0, cs), :]
        raw_q = _dot(q, kt)
        raw_k = _dot(k, kt)
        qf, kf, vf, ktf = q.astype(F32), k.astype(F32), v.astype(F32), kt.astype(F32)
        for d in (0, 1):
            incl, strict = masks[d]
            gcol = g4[:, d:d + 1]
            bcol = g4[:, 2 + d:3 + d]
            grow = gct_ref[c, pl.ds(d * GDN_HEADS + head, 1), :]
            decay = jnp.where(incl, jnp.exp(jnp.where(incl, gcol - grow, 0.0)), 0.0)
            pm = jnp.where(strict, -(raw_k * bcol) * decay, 0.0)
            eg = jnp.exp(gcol)
            x = jnp.concatenate([vf * bcol, kf * (bcol * eg)], axis=1)
            x = x + _dot(pm.astype(BF16), x.astype(BF16))
            for _ in range(5):
                pb = pm.astype(BF16)
                pm = _dot(pb, pb)
                x = x + _dot(pm.astype(BF16), x.astype(BF16))
            us[d][pl.ds(r0, cs), :] = x[:, :HEAD_DIM]
            wqs[d][pl.ds(r1, cs), :] = x[:, HEAD_DIM:].astype(BF16)
            wqs[d][pl.ds(r1 + cs, cs), :] = (qf * eg).astype(BF16)
            ats[d][pl.ds(r0, cs), :] = (raw_q * decay).astype(BF16)
            gend = grow[:, cs - 1:cs] if d == 0 else grow[:, 0:1]
            kps[d][c] = (ktf * jnp.exp(gend - grow)).astype(BF16)
            egs[d][c] = jnp.broadcast_to(jnp.exp(gend), (8, HEAD_DIM))
        return carry

    def inter(i, carry):
        cb = jnp.where(i < nctx, nctx - 1 - i, nch - 1 - (i - nctx))
        new = []
        for d, c in ((0, i), (1, cb)):
            s = carry[d]
            r0 = pl.multiple_of(c * cs, cs)
            r1 = pl.multiple_of(c * 2 * cs, 2 * cs)
            r = _dot(wqs[d][pl.ds(r1, 2 * cs), :], s.astype(BF16))
            vb = (us[d][pl.ds(r0, cs), :] - r[:cs]).astype(BF16)
            o_ref[pl.ds(r0, cs), :] += r[cs:] + _dot(ats[d][pl.ds(r0, cs), :], vb)
            new.append(s * egs[d][c][0:1, :] + _dot(kps[d][c], vb))
        return tuple(new)

    o_ref[...] = jnp.zeros_like(o_ref)
    lax.fori_loop(0, nch, intra, 0)
    zero = jnp.zeros((HEAD_DIM, HEAD_DIM), F32)
    lax.fori_loop(0, nch, inter, (zero, zero))


def _gdn_scan_call(qkvn, kt3, gh, gct3, nb, period, ctx):
    t = qkvn.shape[0]
    nch = period // GDN_CHUNK
    nctx = ctx // GDN_CHUNK
    hd = HEAD_DIM
    body = functools.partial(_gdn_scan_body, nch=nch, nctx=nctx)
    dirs2 = lambda shape, dt: [pltpu.VMEM(shape, dt), pltpu.VMEM(shape, dt)]
    return pl.pallas_call(
        body,
        grid=(nb, GDN_HEADS),
        in_specs=[pl.BlockSpec((period, hd), lambda b, h: (b, h)),
                  pl.BlockSpec((period, hd), lambda b, h: (b, GDN_HEADS + h)),
                  pl.BlockSpec((period, hd), lambda b, h: (b, 2 * GDN_HEADS + h)),
                  pl.BlockSpec((nch, hd, GDN_CHUNK), lambda b, h: (b, h, 0)),
                  pl.BlockSpec((period, hd), lambda b, h: (b, h)),
                  pl.BlockSpec((nch, 2 * GDN_HEADS, GDN_CHUNK), lambda b, h: (b, 0, 0))],
        out_specs=pl.BlockSpec((period, hd), lambda b, h: (b, h)),
        out_shape=jax.ShapeDtypeStruct((t, GDN_WIDTH), F32),
        scratch_shapes=(dirs2((period, hd), F32) + dirs2((2 * period, hd), BF16)
                        + dirs2((period, GDN_CHUNK), BF16) + dirs2((nch, hd, GDN_CHUNK), BF16)
                        + dirs2((nch, 8, hd), F32)),
        compiler_params=_params("arbitrary", "arbitrary"),
        name="gdn_scan",
    )(qkvn, qkvn, qkvn, kt3, gh, gct3)


def _s5_tables_body(lr_ref, li_ref, ldt_ref, btr_ref, bti_ref, cr_ref, ci_ref,
                    ar_ref, ai_ref, cmr_ref, cmi_ref, k_ref, ll_ref):
    lr = lr_ref[...]
    li = li_ref[...]
    dt = jnp.exp(ldt_ref[...])
    kk = lax.broadcasted_iota(jnp.int32, (24, 1), 0).astype(F32)
    mag = jnp.exp(kk * (lr * dt))
    ang = kk * (li * dt)
    er = mag * jnp.cos(ang)
    ei = mag * jnp.sin(ang)
    nr = er[1:2] - 1.0
    ni = ei[1:2]
    den = lr * lr + li * li
    cfr = (nr * lr + ni * li) / den
    cfi = (ni * lr - nr * li) / den
    btr = btr_ref[...]
    bti = bti_ref[...]
    bbr = btr * cfr - bti * cfi
    bbi = btr * cfi + bti * cfr
    cr = cr_ref[...]
    ci = ci_ref[...]
    a_r, a_i, cm_r, cm_i, c0_r, c0_i = [], [], [], [], [], []
    for k in range(S5_CHUNK):
        e_r, e_i = er[k:k + 1], ei[k:k + 1]
        a_r.append(bbr * e_r - bbi * e_i)
        a_i.append(bbr * e_i + bbi * e_r)
        c0_r.append(cr * e_r - ci * e_i)
        c0_i.append(cr * e_i + ci * e_r)
        f_r, f_i = er[k + 1:k + 2], ei[k + 1:k + 2]
        cm_r.append(cr * f_r - ci * f_i)
        cm_i.append(-(cr * f_i + ci * f_r))
    ar_ref[...] = jnp.concatenate(a_r, axis=0)
    ai_ref[...] = jnp.concatenate(a_i, axis=0)
    cmr_ref[...] = jnp.concatenate(cm_r, axis=0)
    cmi_ref[...] = jnp.concatenate(cm_i, axis=0)
    c0r = jnp.concatenate(c0_r, axis=0)
    c0i = jnp.concatenate(c0_i, axis=0)

    def dot3_nt(a, b):
        a0 = a.astype(BF16)
        a1 = (a - a0.astype(F32)).astype(BF16)
        b0 = b.astype(BF16)
        b1 = (b - b0.astype(F32)).astype(BF16)
        return _dot_nt(a0, b0) + _dot_nt(a0, b1) + _dot_nt(a1, b0)

    k_ref[...] = dot3_nt(c0r, bbr) - dot3_nt(c0i, bbi)
    ll_ref[...] = jnp.concatenate([er[S5_CHUNK:S5_CHUNK + 1], ei[S5_CHUNK:S5_CHUNK + 1]], axis=0)


def _s5_tables_call(lam_re, lam_im, log_dt, b_re, b_im, c_re, c_im):
    _, g, p = lam_re.shape
    cg = S5_GROUP
    rows = S5_CHUNK * cg
    row4 = lambda a: a.reshape(2, g, 1, p)
    spec = lambda r, c: pl.BlockSpec((None, None, r, c), lambda d, gi: (d, gi, 0, 0))
    ldt = jnp.broadcast_to(log_dt[:, :, None, None], (2, g, 1, p))
    return pl.pallas_call(
        _s5_tables_body,
        grid=(2, g),
        in_specs=[spec(1, p), spec(1, p), spec(1, p), spec(cg, p), spec(cg, p), spec(cg, p), spec(cg, p)],
        out_specs=[spec(rows, p), spec(rows, p), spec(rows, p), spec(rows, p), spec(rows, cg), spec(2, p)],
        out_shape=[jax.ShapeDtypeStruct((2, g, rows, p), F32)] * 4
                  + [jax.ShapeDtypeStruct((2, g, rows, cg), F32), jax.ShapeDtypeStruct((2, g, 2, p), F32)],
        compiler_params=_params("arbitrary", "arbitrary"),
        name="s5_tables",
    )(row4(lam_re), row4(lam_im), ldt, jnp.swapaxes(b_re, -1, -2), jnp.swapaxes(b_im, -1, -2), c_re, c_im)


def _s5_operators(tables):
    a_r, a_i, cm_r, cm_i, kk, ll = tables
    g = a_r.shape[1]
    n, cg, p = S5_CHUNK, S5_GROUP, S5_STATE

    def win(a):
        a = a.reshape(2, g, n, cg, p)
        return jnp.concatenate([a[0][:, ::-1].reshape(g, n * cg, p), a[1].reshape(g, n * cg, p)], -1).astype(BF16)

    def wout(c):
        c = c.reshape(2, g, n, cg, p)
        full = jnp.concatenate([c[0].reshape(g, n * cg, p), c[1][:, ::-1].reshape(g, n * cg, p)], -1)
        return jnp.swapaxes(full, 1, 2).astype(BF16)

    k5 = kk.reshape(2, g, n, cg, cg)
    s_i = jnp.arange(n)[:, None]
    t_i = jnp.arange(n)[None, :]

    def toeplitz(k4, lag, valid):
        blk = k4[:, jnp.clip(lag, 0, n - 1)]
        blk = jnp.where(valid[None, :, :, None, None], blk, 0.0)
        return blk.transpose(0, 1, 4, 2, 3).reshape(g, n * cg, n * cg).astype(BF16)

    ktf = toeplitz(k5[0], t_i - s_i, t_i >= s_i)
    ktb = toeplitz(k5[1], s_i - t_i, s_i >= t_i)
    gb = S5_GROUPS_PER_STEP
    lam_r = jnp.concatenate([ll[0, :, 0], ll[1, :, 0]], -1).reshape(g // gb, 1, gb * 2 * p)
    lam_i = jnp.concatenate([ll[0, :, 1], ll[1, :, 1]], -1).reshape(g // gb, 1, gb * 2 * p)
    return ktf, ktb, win(a_r), win(a_i), wout(cm_r), wout(cm_i), lam_r, lam_i


def _s5_body(u_ref, ktf_ref, ktb_ref, wir_ref, wii_ref, wor_ref, woi_ref, lr_ref, li_ref, y_ref,
             *scratch, nb, nch, nctx):
    gb = S5_GROUPS_PER_STEP
    sw = 2 * S5_STATE
    xr, xi, sfr, sfi, sbr, sbi = (scratch[k * gb:(k + 1) * gb] for k in range(6))
    for g in range(gb):
        ug = u_ref[:, g * S5_TILE:(g + 1) * S5_TILE]
        xr[g][...] = _dot(ug, wir_ref[g])
        xi[g][...] = _dot(ug, wii_ref[g])
    is_f = lax.broadcasted_iota(jnp.int32, (1, sw), 1) < S5_STATE
    ar = [lr_ref[:, g * sw:(g + 1) * sw] for g in range(gb)]
    ai = [li_ref[:, g * sw:(g + 1) * sw] for g in range(gb)]

    def step(i, carry):
        cb = jnp.where(i < nctx, nctx - 1 - i, nch - 1 - (i - nctx))
        at_f = pl.ds(i, nb, stride=nch)
        at_b = pl.ds(cb, nb, stride=nch)
        new = []
        for g in range(gb):
            sr, si = carry[2 * g], carry[2 * g + 1]
            sfr[g][at_f, :] = sr
            sfi[g][at_f, :] = si
            sbr[g][at_b, :] = sr
            sbi[g][at_b, :] = si
            inr = jnp.where(is_f, xr[g][at_f, :], xr[g][at_b, :])
            ini = jnp.where(is_f, xi[g][at_f, :], xi[g][at_b, :])
            new.append(ar[g] * sr - ai[g] * si + inr)
            new.append(ar[g] * si + ai[g] * sr + ini)
        return tuple(new)

    zero = jnp.zeros((nb, sw), F32)
    lax.fori_loop(0, nch, step, (zero,) * (2 * gb))
    for g in range(gb):
        s_r = jnp.where(is_f, sfr[g][...], sbr[g][...]).astype(BF16)
        s_i = jnp.where(is_f, sfi[g][...], sbi[g][...]).astype(BF16)
        ug = u_ref[:, g * S5_TILE:(g + 1) * S5_TILE]
        y_ref[:, g * S5_TILE:(g + 1) * S5_TILE] = (_dot(ug, ktf_ref[g]) + _dot(ug, ktb_ref[g])
                                                   + _dot(s_r, wor_ref[g]) + _dot(s_i, woi_ref[g]))


def _s5_call(u_blk, ops, nb, period, ctx):
    ktf, ktb, wir, wii, wor, woi, lam_r, lam_i = ops
    rows, width = u_blk.shape
    g = width // S5_TILE
    gb = S5_GROUPS_PER_STEP
    sw = 2 * S5_STATE
    nch = period // S5_CHUNK
    nctx = ctx // S5_CHUNK
    body = functools.partial(_s5_body, nb=nb, nch=nch, nctx=nctx)
    wspec = lambda r, c: pl.BlockSpec((gb, r, c), lambda i: (i, 0, 0))
    return pl.pallas_call(
        body,
        grid=(g // gb,),
        in_specs=[pl.BlockSpec((rows, gb * S5_TILE), lambda i: (0, i)),
                  wspec(S5_TILE, S5_TILE), wspec(S5_TILE, S5_TILE),
                  wspec(S5_TILE, sw), wspec(S5_TILE, sw), wspec(sw, S5_TILE), wspec(sw, S5_TILE),
                  pl.BlockSpec((None, 1, gb * sw), lambda i: (i, 0, 0)),
                  pl.BlockSpec((None, 1, gb * sw), lambda i: (i, 0, 0))],
        out_specs=pl.BlockSpec((rows, gb * S5_TILE), lambda i: (0, i)),
        out_shape=jax.ShapeDtypeStruct((rows, width), F32),
        scratch_shapes=[pltpu.VMEM((rows, sw), F32) for _ in range(6 * gb)],
        compiler_params=_params("arbitrary"),
        name="s5_scan",
    )(u_blk, ktf, ktb, wir, wii, wor, woi, lam_r, lam_i)


def _merge_body(o_ref, z_ref, u_ref, y_ref, h_ref, mb_ref, mc_ref, gn_ref, ds_ref, wglu_ref, bglu_ref,
                wout_ref, lng_ref, lnb_ref, out_ref, cat_ref, *, tm, tpb, ctx):
    rib = _rows_in_batch(pl.program_id(0), tm, tpb)
    gn = gn_ref[...]
    for hh in range(GDN_HEADS):
        sl = slice(hh * HEAD_DIM, (hh + 1) * HEAD_DIM)
        oh = o_ref[:, sl]
        r = lax.rsqrt(jnp.mean(oh * oh, -1, keepdims=True) + RMS_EPS)
        cat_ref[:, sl] = (oh * r * gn * _silu(z_ref[:, sl].astype(F32))).astype(BF16)
    s = y_ref[...] + ds_ref[...] * u_ref[...].astype(F32)
    s = 0.5 * s * (1.0 + jnp.tanh(math.sqrt(2.0 / math.pi) * (s + 0.044715 * (s * s * s))))
    s = s * jax.nn.sigmoid(_dot(s.astype(BF16), wglu_ref[...]) + bglu_ref[...])
    cat_ref[:, GDN_WIDTH:] = s.astype(BF16)
    mix = _dot(cat_ref[...], wout_ref[...])
    gate = jnp.where(rib < ctx, mc_ref[2:3, :], mb_ref[2:3, :])
    out_ref[...] = _layernorm(ALPHA * h_ref[...] + gate * mix, lng_ref[...], lnb_ref[...])


def _merge_call(o, proj, y, h, mod, gn, d_skip, w_glu, b_glu, w_out, ln_g, ln_b, nb, period, ctx):
    t, d = h.shape
    w = GDN_WIDTH
    sw = d - w
    tm = _pick(period, 272, 16)
    tpb = period // tm
    body = functools.partial(_merge_body, tm=tm, tpb=tpb, ctx=ctx)
    const = lambda r, c: pl.BlockSpec((r, c), lambda i: (0, 0))
    return pl.pallas_call(
        body,
        grid=(t // tm,),
        in_specs=[pl.BlockSpec((tm, w), lambda i: (i, 0)),
                  pl.BlockSpec((tm, w), lambda i: (i, 3)),
                  pl.BlockSpec((tm, sw), lambda i: (i, 4 * w // sw)),
                  pl.BlockSpec((tm, sw), lambda i: (i, 0)),
                  pl.BlockSpec((tm, d), lambda i: (i, 0)),
                  pl.BlockSpec((None, 6, d), lambda i: (i // tpb, 0, 0)),
                  pl.BlockSpec((None, 6, d), lambda i: (nb, 0, 0)),
                  const(1, HEAD_DIM), const(1, sw), const(sw, sw), const(1, sw), const(d, d),
                  const(1, d), const(1, d)],
        out_specs=pl.BlockSpec((tm, d), lambda i: (i, 0)),
        out_shape=jax.ShapeDtypeStruct((t, d), F32),
        scratch_shapes=[pltpu.VMEM((tm, d), BF16)],
        compiler_params=_params("arbitrary"),
        name="even_merge",
    )(o, proj, proj, y, h, mod, mod, gn, d_skip, w_glu, b_glu, w_out, ln_g, ln_b)


def _ffn_body(x_ref, xh_ref, mb_ref, mc_ref, wv_ref, wg_ref, cwv_ref, cwg_ref, cbv_ref, cbg_ref, wd_ref,
              lng_ref, lnb_ref, o_ref, xb_ref, *, tm, tpb, ctx, period):
    i = pl.program_id(0)
    j = pl.program_id(1)
    rib = _rows_in_batch(i, tm, tpb)
    n = tm + 32

    @pl.when(j == 0)
    def _():
        xb_ref[0:tm, :] = _modulate(x_ref[...], rib < ctx, mb_ref, mc_ref, 3, 4).astype(BF16)
        base = (i % tpb) * tm
        off = lax.broadcasted_iota(jnp.int32, (16, 1), 0)
        xh = xh_ref[...]
        xb_ref[tm:tm + 16, :] = _modulate(xh[0:16], base + tm + off < ctx, mb_ref, mc_ref, 3, 4).astype(BF16)
        xb_ref[tm + 16:n, :] = _modulate(xh[16:32], base - 16 + off < ctx, mb_ref, mc_ref, 3, 4).astype(BF16)
        o_ref[...] = jnp.zeros_like(o_ref)

    has_prev, has_next = _seq_edges(rib, ctx, period)
    xb = xb_ref[...]

    def conv(w_ref, cw_ref, cb_ref):
        u = _dot(xb, w_ref[...])
        cw = cw_ref[...]
        up = pltpu.roll(u, 1, 0)[:tm]
        un = pltpu.roll(u, n - 1, 0)[:tm]
        return (jnp.where(has_prev, up, 0.0) * cw[0:1] + u[:tm] * cw[1:2]
                + jnp.where(has_next, un, 0.0) * cw[2:3] + cb_ref[...])

    val = conv(wv_ref, cwv_ref, cbv_ref)
    gate = conv(wg_ref, cwg_ref, cbg_ref)
    o_ref[...] += _dot((val * _silu(gate)).astype(BF16), wd_ref[...])

    @pl.when(j == pl.num_programs(1) - 1)
    def _():
        g = jnp.where(rib < ctx, mc_ref[5:6, :], mb_ref[5:6, :])
        o_ref[...] = _layernorm(ALPHA * x_ref[...] + g * o_ref[...], lng_ref[...], lnb_ref[...])


def _ffn_call(h, mod, w_up, conv_w, conv_b, w_down, ln_g, ln_b, nb, period, ctx):
    t, d = h.shape
    f = w_down.shape[0]
    tm = _pick(period, 544, 16)
    tpb = period // tm
    nt = t // tm
    fc = _pick(f, FFN_CHUNK, 128)
    nf = f // fc
    h3 = h.reshape(nt, tm, d)
    pad = jnp.zeros((1, 16, d), h.dtype)
    halo = jnp.concatenate([jnp.concatenate([h3[1:, :16], pad], 0),
                            jnp.concatenate([pad, h3[:-1, tm - 16:]], 0)], axis=1)
    body = functools.partial(_ffn_body, tm=tm, tpb=tpb, ctx=ctx, period=period)
    return pl.pallas_call(
        body,
        grid=(nt, nf),
        in_specs=[pl.BlockSpec((tm, d), lambda i, j: (i, 0)),
                  pl.BlockSpec((None, 32, d), lambda i, j: (i, 0, 0)),
                  pl.BlockSpec((None, 6, d), lambda i, j: (i // tpb, 0, 0)),
                  pl.BlockSpec((None, 6, d), lambda i, j: (nb, 0, 0)),
                  pl.BlockSpec((d, fc), lambda i, j: (0, j)),
                  pl.BlockSpec((d, fc), lambda i, j: (0, nf + j)),
                  pl.BlockSpec((3, fc), lambda i, j: (0, j)),
                  pl.BlockSpec((3, fc), lambda i, j: (0, nf + j)),
                  pl.BlockSpec((1, fc), lambda i, j: (0, j)),
                  pl.BlockSpec((1, fc), lambda i, j: (0, nf + j)),
                  pl.BlockSpec((fc, d), lambda i, j: (j, 0)),
                  pl.BlockSpec((1, d), lambda i, j: (0, 0)),
                  pl.BlockSpec((1, d), lambda i, j: (0, 0))],
        out_specs=pl.BlockSpec((tm, d), lambda i, j: (i, 0)),
        out_shape=jax.ShapeDtypeStruct((t, d), F32),
        scratch_shapes=[pltpu.VMEM((tm + 32, d), BF16)],
        compiler_params=_params("arbitrary", "arbitrary"),
        name="conv_ffn",
    )(h, halo, mod, mod, w_up, w_up, conv_w, conv_w, conv_b, conv_b, w_down, ln_g, ln_b)


def _qkv_body(x_ref, mb_ref, mc_ref, w_ref, cos_ref, sin_ref, qn_ref, kn_ref, o_ref, xb_ref,
              *, tm, tpb, ctx, nq_tiles):
    i = pl.program_id(0)
    j = pl.program_id(1)

    @pl.when(j == 0)
    def _():
        rib = _rows_in_batch(i, tm, tpb)
        xb_ref[...] = _modulate(x_ref[...], rib < ctx, mb_ref, mc_ref, 0, 1).astype(BF16)

    acc = _dot(xb_ref[...], w_ref[...])
    cos = cos_ref[...]
    sin = sin_ref[...]
    lane = lax.broadcasted_iota(jnp.int32, (1, HEAD_DIM), 1)
    first = (lane % 64) < 32
    heads = acc.shape[1] // HEAD_DIM

    def normrope(xh, wn, scale):
        xn = xh * lax.rsqrt(jnp.mean(xh * xh, -1, keepdims=True) + RMS_EPS) * wn
        partner = jnp.where(first, pltpu.roll(xn, HEAD_DIM - 32, 1), pltpu.roll(xn, 32, 1))
        return (xn * cos + partner * sin) * scale

    @pl.when(j < nq_tiles)
    def _():
        for hh in range(heads):
            sl = slice(hh * HEAD_DIM, (hh + 1) * HEAD_DIM)
            o_ref[:, sl] = normrope(acc[:, sl], qn_ref[...], ATT_SCALE).astype(BF16)

    @pl.when(j == nq_tiles)
    def _():
        for hh in range(heads):
            sl = slice(hh * HEAD_DIM, (hh + 1) * HEAD_DIM)
            if hh < ATT_KV_HEADS:
                o_ref[:, sl] = normrope(acc[:, sl], kn_ref[...], 1.0).astype(BF16)
            else:
                o_ref[:, sl] = acc[:, sl].astype(BF16)


def _qkv_call(h, mod, w_in, cos, sin, qn, kn, nb, period, ctx):
    t, d = h.shape
    n = w_in.shape[1]
    tn = 2 * ATT_KV_HEADS * HEAD_DIM
    nq_tiles = (n - tn) // tn
    tm = _pick(period, 544, 16)
    tpb = period // tm
    body = functools.partial(_qkv_body, tm=tm, tpb=tpb, ctx=ctx, nq_tiles=nq_tiles)
    return pl.pallas_call(
        body,
        grid=(t // tm, n // tn),
        in_specs=[pl.BlockSpec((tm, d), lambda i, j: (i, 0)),
                  pl.BlockSpec((None, 6, d), lambda i, j: (i // tpb, 0, 0)),
                  pl.BlockSpec((None, 6, d), lambda i, j: (nb, 0, 0)),
                  pl.BlockSpec((d, tn), lambda i, j: (0, j)),
                  pl.BlockSpec((tm, HEAD_DIM), lambda i, j: (i % tpb, 0)),
                  pl.BlockSpec((tm, HEAD_DIM), lambda i, j: (i % tpb, 0)),
                  pl.BlockSpec((1, HEAD_DIM), lambda i, j: (0, 0)),
                  pl.BlockSpec((1, HEAD_DIM), lambda i, j: (0, 0))],
        out_specs=pl.BlockSpec((tm, tn), lambda i, j: (i, j)),
        out_shape=jax.ShapeDtypeStruct((t, n), BF16),
        scratch_shapes=[pltpu.VMEM((tm, d), BF16)],
        compiler_params=_params("arbitrary", "arbitrary"),
        name="odd_qkv",
    )(h, mod, mod, w_in, cos, sin, qn, kn)


def _rope_tables(seq, ctx):
    rows = seq // GRID_W
    row = jnp.repeat(jnp.arange(rows, dtype=F32), GRID_W)
    col = jnp.tile(jnp.arange(GRID_W, dtype=F32), rows)
    half = HEAD_DIM // 4
    inv = ROPE_THETA ** (-jnp.arange(half, dtype=F32) / half)
    ar = row[:, None] * inv
    ac = col[:, None] * inv
    cos = jnp.concatenate([jnp.cos(ar), jnp.cos(ar), jnp.cos(ac), jnp.cos(ac)], -1)
    sin = jnp.concatenate([-jnp.sin(ar), jnp.sin(ar), -jnp.sin(ac), jnp.sin(ac)], -1)
    cos = jnp.concatenate([jnp.ones((ctx, HEAD_DIM), F32), cos], 0)
    sin = jnp.concatenate([jnp.zeros((ctx, HEAD_DIM), F32), sin], 0)
    return cos, sin


def _attn_body(q_ref, k_ref, v_ref, o_ref, *, tq, tk, nk):
    q = jnp.concatenate([q_ref[:, g * HEAD_DIM:(g + 1) * HEAD_DIM] for g in range(ATT_GROUP)], axis=0)
    rows = ATT_GROUP * tq
    m = jnp.full((rows, 1), NEG_BIG, F32)
    l = jnp.zeros((rows, 1), F32)
    acc = jnp.zeros((rows, HEAD_DIM), F32)
    for kb in range(nk):
        k = k_ref[kb * tk:(kb + 1) * tk, :]
        v = v_ref[kb * tk:(kb + 1) * tk, :]
        s = _dot_nt(q, k)
        m_new = jnp.maximum(m, jnp.max(s, -1, keepdims=True))
        a = jnp.exp(m - m_new)
        p = jnp.exp(s - m_new)
        l = a * l + jnp.sum(p, -1, keepdims=True)
        acc = a * acc + _dot(p.astype(BF16), v)
        m = m_new
    out = acc / l
    for g in range(ATT_GROUP):
        o_ref[:, g * HEAD_DIM:(g + 1) * HEAD_DIM] = out[g * tq:(g + 1) * tq].astype(o_ref.dtype)


def _attn_call(qkv, nb, seq, ctx):
    period = seq + ctx
    tq = _pick(math.gcd(seq, ctx), 256, 16)
    nk = 4
    tk = period // nk
    gw = ATT_GROUP * HEAD_DIM
    qpb = seq // tq
    rpb = period // tq
    skip = ctx // tq
    kcol = ATT_HEADS
    vcol = ATT_HEADS + ATT_KV_HEADS
    body = functools.partial(_attn_body, tq=tq, tk=tk, nk=nk)
    return pl.pallas_call(
        body,
        grid=(nb, ATT_KV_HEADS, qpb),
        in_specs=[pl.BlockSpec((tq, gw), lambda b, kv, qi: (b * rpb + skip + qi, kv)),
                  pl.BlockSpec((period, HEAD_DIM), lambda b, kv, qi: (b, kcol + kv)),
                  pl.BlockSpec((period, HEAD_DIM), lambda b, kv, qi: (b, vcol + kv))],
        out_specs=pl.BlockSpec((tq, gw), lambda b, kv, qi: (b * qpb + qi, kv)),
        out_shape=jax.ShapeDtypeStruct((nb * seq, ATT_HEADS * HEAD_DIM), BF16),
        compiler_params=_params("arbitrary", "arbitrary", "arbitrary"),
        name="gqa_attention",
    )(qkv, qkv, qkv)


def _outproj_body(a_ref, h_ref, mb_ref, w_ref, lng_ref, lnb_ref, o_ref):
    mix = _dot(a_ref[...], w_ref[...])
    o_ref[...] = _layernorm(ALPHA * h_ref[...] + mb_ref[2:3, :] * mix, lng_ref[...], lnb_ref[...])


def _outproj_call(att, h, mod, w_out, ln_g, ln_b, nb, seq, ctx):
    d = h.shape[1]
    period = seq + ctx
    tm = _pick(math.gcd(seq, ctx), 256, 16)
    qpb = seq // tm
    rpb = period // tm
    skip = ctx // tm
    return pl.pallas_call(
        _outproj_body,
        grid=(nb * qpb,),
        in_specs=[pl.BlockSpec((tm, d), lambda i: (i, 0)),
                  pl.BlockSpec((tm, d), lambda i: ((i // qpb) * rpb + skip + i % qpb, 0)),
                  pl.BlockSpec((None, 6, d), lambda i: (i // qpb, 0, 0)),
                  pl.BlockSpec((d, d), lambda i: (0, 0)),
                  pl.BlockSpec((1, d), lambda i: (0, 0)),
                  pl.BlockSpec((1, d), lambda i: (0, 0))],
        out_specs=pl.BlockSpec((tm, d), lambda i: (i, 0)),
        out_shape=jax.ShapeDtypeStruct((nb * seq, d), F32),
        compiler_params=_params("arbitrary"),
        name="odd_outproj",
    )(att, h, mod, w_out, ln_g, ln_b)


def kernel(x, c, ctx, c_ctx, mod_w, mod_b, ln1_g, ln1_b, ln2_g, ln2_b, ffn_w_up, ffn_conv_w, ffn_conv_b, ffn_w_down, e_w_in, e_conv_qkv, e_a_log, e_dt_bias, e_gdn_norm, e_lam_re, e_lam_im, e_log_dt, e_b_re, e_b_im, e_c_re, e_c_im, e_d_skip, e_w_glu, e_b_glu, e_w_out, o_w_in, o_q_norm, o_k_norm, o_w_out):
    nb, seq, d = x.shape
    nctx = ctx.shape[1]
    period = nctx + seq
    t = nb * period
    w = GDN_WIDTH
    assert mod_w.shape[0] == DEPTH == 2 and nb < 8

    h = jnp.concatenate([ctx, x], axis=1).reshape(t, d)
    cs = jnp.zeros((8, d), F32).at[:nb].set(c).at[nb].set(c_ctx)
    mod = _mod_call(cs, mod_w, mod_b).reshape(DEPTH, 8, 6, d)
    row = lambda a: a.reshape(1, -1)

    w_in = e_w_in[0]
    gates_at = 4 * w
    w_main = jnp.concatenate([w_in[:, :gates_at], w_in[:, gates_at + 4 * GDN_HEADS:]], axis=1).astype(BF16)
    w_ab = jnp.pad(w_in[:, gates_at:gates_at + 4 * GDN_HEADS], ((0, 0), (0, 128 - 4 * GDN_HEADS))).astype(BF16)
    proj, ab = _inproj_even_call(h, mod[0], w_main, w_ab, nb, period, nctx)
    pad_row = lambda a: jnp.pad(a.reshape(1, -1), ((0, 0), (0, 128 - 2 * GDN_HEADS)))
    qkvn, kt3, gh, gct3 = _gdn_prep_call(proj, ab, e_conv_qkv[0], pad_row(e_a_log[0]), pad_row(e_dt_bias[0]),
                                         period, nctx)
    o = _gdn_scan_call(qkvn, kt3, gh, gct3, nb, period, nctx)

    ops = _s5_operators(_s5_tables_call(e_lam_re[0], e_lam_im[0], e_log_dt[0], e_b_re[0], e_b_im[0],
                                        e_c_re[0], e_c_im[0]))
    sgroups = (d - w) // S5_GROUP
    blk = lambda a: a.reshape(t // S5_CHUNK, S5_CHUNK, sgroups, S5_GROUP).transpose(0, 2, 1, 3)
    u_blk = blk(proj[:, 4 * w:]).reshape(t // S5_CHUNK, sgroups * S5_TILE)
    y_blk = _s5_call(u_blk, ops, nb, period, nctx)
    y = y_blk.reshape(t // S5_CHUNK, sgroups, S5_CHUNK, S5_GROUP).transpose(0, 2, 1, 3).reshape(t, d - w)

    h = _merge_call(o, proj, y, h, mod[0], row(e_gdn_norm[0]), row(e_d_skip[0]), e_w_glu[0].astype(BF16),
                    row(e_b_glu[0]), e_w_out[0].astype(BF16), row(ln1_g[0]), row(ln1_b[0]), nb, period, nctx)
    h = _ffn_call(h, mod[0], ffn_w_up[0].astype(BF16), ffn_conv_w[0], row(ffn_conv_b[0]),
                  ffn_w_down[0].astype(BF16), row(ln2_g[0]), row(ln2_b[0]), nb, period, nctx)

    cos, sin = _rope_tables(seq, nctx)
    qkv = _qkv_call(h, mod[1], o_w_in[0].astype(BF16), cos, sin, row(o_q_norm[0]), row(o_k_norm[0]),
                    nb, period, nctx)
    att = _attn_call(qkv, nb, seq, nctx)
    hl = _outproj_call(att, h, mod[1], o_w_out[0].astype(BF16), row(ln1_g[1]), row(ln1_b[1]), nb, seq, nctx)
    out = _ffn_call(hl, mod[1], ffn_w_up[1].astype(BF16), ffn_conv_w[1], row(ffn_conv_b[1]),
                    ffn_w_down[1].astype(BF16), row(ln2_g[1]), row(ln2_b[1]), nb, seq, 0)
    return out.reshape(nb, seq, d)
```

```python
import functools
import math

import jax
import jax.numpy as jnp
from jax import lax
from jax.experimental import pallas as pl
from jax.experimental.pallas import tpu as pltpu

F32 = jnp.float32
BF16 = jnp.bfloat16

DEPTH = 2
GDN_HEADS = 8
HEAD_DIM = 128
GDN_WIDTH = GDN_HEADS * HEAD_DIM
GDN_CHUNK = 64
GDN_INTRA_BATCH = 4
S5_GROUP = 16
S5_STATE = 64
S5_CHUNK = 16
S5_TILE = S5_CHUNK * S5_GROUP
S5_GROUPS_PER_STEP = 4
ATT_HEADS = 16
ATT_KV_HEADS = 4
ATT_GROUP = ATT_HEADS // ATT_KV_HEADS
ATT_SCALE = HEAD_DIM ** -0.5
GRID_W = 64
ROPE_THETA = 10000.0
FFN_CHUNK = 512
FFN_SLAB = 256
ALPHA = (2 * DEPTH) ** 0.25
LN_EPS = 1e-6
RMS_EPS = 1e-6
L2_EPS = 1e-6
VMEM_LIMIT_BYTES = 56 * 1024 * 1024
NEG_BIG = -1e30


def _pick(n, target, mult):
    best = None
    for d in range(mult, min(n, target) + 1, mult):
        if n % d == 0:
            best = d
    assert best is not None, (n, target, mult)
    return best


def _params(*sem):
    return pltpu.CompilerParams(dimension_semantics=sem, vmem_limit_bytes=VMEM_LIMIT_BYTES)


def _dot(a, b):
    return jnp.dot(a, b, preferred_element_type=F32)


def _dot_nt(a, b):
    return lax.dot_general(a, b, (((1,), (1,)), ((), ())), preferred_element_type=F32)


def _split3(a):
    p0 = a.astype(BF16)
    r = a - p0.astype(F32)
    p1 = r.astype(BF16)
    p2 = (r - p1.astype(F32)).astype(BF16)
    return p0, p1, p2


def _rows_in_batch(i, tm, tiles_per_batch):
    return (i % tiles_per_batch) * tm + lax.broadcasted_iota(jnp.int32, (tm, 1), 0)


def _modulate(x, is_ctx, mb_ref, mc_ref, shift_i, scale_i):
    scale = jnp.where(is_ctx, mc_ref[scale_i:scale_i + 1, :], mb_ref[scale_i:scale_i + 1, :])
    shift = jnp.where(is_ctx, mc_ref[shift_i:shift_i + 1, :], mb_ref[shift_i:shift_i + 1, :])
    return x * (1.0 + scale) + shift


def _layernorm(r, g, b):
    xc = r - jnp.mean(r, -1, keepdims=True)
    var = jnp.mean(xc * xc, -1, keepdims=True)
    return xc * lax.rsqrt(var + LN_EPS) * g + b


def _silu(x):
    return x * jax.nn.sigmoid(x)


def _seq_edges(rib, ctx, period):
    has_prev = (rib != 0) & (rib != ctx)
    has_next = (rib != ctx - 1) & (rib != period - 1)
    return has_prev, has_next


def _mod_body(c_ref, w_ref, b_ref, o_ref):
    s = _silu(c_ref[...])
    o_ref[...] = _dot(s.astype(BF16), w_ref[...].astype(BF16)) + b_ref[...]


def _mod_call(cs, mod_w, mod_b):
    depth, d, n = mod_w.shape
    tn = _pick(n, 1024, 128)
    return pl.pallas_call(
        _mod_body,
        grid=(depth, n // tn),
        in_specs=[pl.BlockSpec((8, d), lambda l, j: (0, 0)),
                  pl.BlockSpec((None, d, tn), lambda l, j: (l, 0, j)),
                  pl.BlockSpec((None, 1, tn), lambda l, j: (l, 0, j))],
        out_specs=pl.BlockSpec((None, 8, tn), lambda l, j: (l, 0, j)),
        out_shape=jax.ShapeDtypeStruct((depth, 8, n), F32),
        compiler_params=_params("arbitrary", "arbitrary"),
        name="adaln_mod",
    )(cs, mod_w, mod_b.reshape(depth, 1, n))


def _inproj_even_body(x_ref, mb_ref, mc_ref, w_ref, wab_ref, o_ref, ab_ref, xb_ref, *, tm, tpb, ctx):
    i = pl.program_id(0)

    @pl.when(pl.program_id(1) == 0)
    def _():
        rib = _rows_in_batch(i, tm, tpb)
        xb = _modulate(x_ref[...], rib < ctx, mb_ref, mc_ref, 0, 1).astype(BF16)
        xb_ref[...] = xb
        ab_ref[...] = _dot(xb, wab_ref[...])

    o_ref[...] = _dot(xb_ref[...], w_ref[...]).astype(o_ref.dtype)


def _inproj_even_call(h, mod, w_main, w_ab, nb, period, ctx):
    t, d = h.shape
    n = w_main.shape[1]
    tm = _pick(period, 544, 16)
    tn = _pick(n, 1280, 256)
    tpb = period // tm
    body = functools.partial(_inproj_even_body, tm=tm, tpb=tpb, ctx=ctx)
    return pl.pallas_call(
        body,
        grid=(t // tm, n // tn),
        in_specs=[pl.BlockSpec((tm, d), lambda i, j: (i, 0)),
                  pl.BlockSpec((None, 6, d), lambda i, j: (i // tpb, 0, 0)),
                  pl.BlockSpec((None, 6, d), lambda i, j: (nb, 0, 0)),
                  pl.BlockSpec((d, tn), lambda i, j: (0, j)),
                  pl.BlockSpec((d, 128), lambda i, j: (0, 0))],
        out_specs=[pl.BlockSpec((tm, tn), lambda i, j: (i, j)),
                   pl.BlockSpec((tm, 128), lambda i, j: (i, 0))],
        out_shape=[jax.ShapeDtypeStruct((t, n), BF16), jax.ShapeDtypeStruct((t, 128), F32)],
        scratch_shapes=[pltpu.VMEM((tm, d), BF16)],
        compiler_params=_params("arbitrary", "arbitrary"),
        name="even_inproj",
    )(h, mod, mod, w_main, w_ab)


def _gdn_prep_body(x_ref, xn_ref, xp_ref, cw_ref, ab_ref, alog_ref, dtb_ref, tril_ref, triu_ref, e_ref,
                   qkv_ref, kt_ref, gh_ref, gct_ref, *, tr, tpb, ctx, period):
    i = pl.program_id(0)
    sec = pl.program_id(1)
    rib = _rows_in_batch(i, tr, tpb)
    has_prev, has_next = _seq_edges(rib, ctx, period)
    ext = jnp.concatenate([x_ref[...].astype(F32), xn_ref[...].astype(F32), xp_ref[...].astype(F32)], axis=0)
    n = tr + 32
    xprev = pltpu.roll(ext, 1, 0)[:tr]
    xnext = pltpu.roll(ext, n - 1, 0)[:tr]
    cw = cw_ref[...]
    y = (jnp.where(has_prev, xprev, 0.0) * cw[0:1] + ext[:tr] * cw[1:2]
         + jnp.where(has_next, xnext, 0.0) * cw[2:3])
    y = _silu(y)

    def l2n(scale):
        parts = []
        for hh in range(GDN_HEADS):
            yh = y[:, hh * HEAD_DIM:(hh + 1) * HEAD_DIM]
            parts.append(yh * (lax.rsqrt(jnp.sum(yh * yh, -1, keepdims=True) + L2_EPS) * scale))
        return jnp.concatenate(parts, axis=1)

    @pl.when(sec == 0)
    def _():
        qkv_ref[...] = l2n(HEAD_DIM ** -0.5).astype(BF16)
        ab = ab_ref[...]
        lane = lax.broadcasted_iota(jnp.int32, (1, 128), 1)
        xg = ab + dtb_ref[...]
        softplus = jnp.maximum(xg, 0.0) + jnp.log1p(jnp.exp(-jnp.abs(xg)))
        g = -jnp.exp(alog_ref[...]) * softplus
        g0, g1, g2 = _split3(g)
        tril = tril_ref[...]
        triu = triu_ref[...]
        pre = _dot(tril, g0) + _dot(tril, g1) + _dot(tril, g2)
        suf = _dot(triu, g0) + _dot(triu, g1) + _dot(triu, g2)
        gc = jnp.where(lane < GDN_HEADS, pre, suf)
        vals = jnp.where(lane < 2 * GDN_HEADS, gc, jax.nn.sigmoid(ab))
        v0, v1, v2 = _split3(vals)
        e = e_ref[...]
        gh_ref[...] = _dot(v0, e) + _dot(v1, e) + _dot(v2, e)
        gct = gc.T
        for c in range(tr // GDN_CHUNK):
            gct_ref[c] = gct[:2 * GDN_HEADS, c * GDN_CHUNK:(c + 1) * GDN_CHUNK]

    @pl.when(sec == 1)
    def _():
        kn = l2n(1.0)
        qkv_ref[...] = kn.astype(BF16)
        knt = kn.T
        for c in range(tr // GDN_CHUNK):
            kt_ref[c] = knt[:, c * GDN_CHUNK:(c + 1) * GDN_CHUNK].astype(BF16)

    @pl.when(sec == 2)
    def _():
        qkv_ref[...] = y.astype(BF16)


def _gdn_prep_call(proj, ab, conv_w, alog_row, dtb_row, period, ctx):
    t = proj.shape[0]
    w = GDN_WIDTH
    tr = _pick(period, 256, GDN_CHUNK)
    tpb = period // tr
    hb = tr // 16
    nh = t // 16
    nck = tr // GDN_CHUNK
    r = jnp.arange(tr)
    same = (r[:, None] // GDN_CHUNK) == (r[None, :] // GDN_CHUNK)
    tril = (same & (r[:, None] >= r[None, :])).astype(BF16)
    triu = (same & (r[:, None] <= r[None, :])).astype(BF16)
    src = jnp.arange(128)[:, None]
    dst = jnp.arange(w)[None, :]
    expand = ((dst % HEAD_DIM < 4) & (src == (dst % HEAD_DIM) * GDN_HEADS + dst // HEAD_DIM)).astype(BF16)
    body = functools.partial(_gdn_prep_body, tr=tr, tpb=tpb, ctx=ctx, period=period)
    return pl.pallas_call(
        body,
        grid=(t // tr, 3),
        in_specs=[pl.BlockSpec((tr, w), lambda i, s: (i, s)),
                  pl.BlockSpec((16, w), lambda i, s: (jnp.minimum((i + 1) * hb, nh - 1), s)),
                  pl.BlockSpec((16, w), lambda i, s: (jnp.maximum(i * hb - 1, 0), s)),
                  pl.BlockSpec((3, w), lambda i, s: (0, s)),
                  pl.BlockSpec((tr, 128), lambda i, s: (i, 0)),
                  pl.BlockSpec((1, 128), lambda i, s: (0, 0)),
                  pl.BlockSpec((1, 128), lambda i, s: (0, 0)),
                  pl.BlockSpec((tr, tr), lambda i, s: (0, 0)),
                  pl.BlockSpec((tr, tr), lambda i, s: (0, 0)),
                  pl.BlockSpec((128, w), lambda i, s: (0, 0))],
        out_specs=[pl.BlockSpec((tr, w), lambda i, s: (i, s)),
                   pl.BlockSpec((nck, w, GDN_CHUNK), lambda i, s: (i, 0, 0)),
                   pl.BlockSpec((tr, w), lambda i, s: (i, 0)),
                   pl.BlockSpec((nck, 2 * GDN_HEADS, GDN_CHUNK), lambda i, s: (i, 0, 0))],
        out_shape=[jax.ShapeDtypeStruct((t, 3 * w), BF16),
                   jax.ShapeDtypeStruct((t // GDN_CHUNK, w, GDN_CHUNK), BF16),
                   jax.ShapeDtypeStruct((t, w), F32),
                   jax.ShapeDtypeStruct((t // GDN_CHUNK, 2 * GDN_HEADS, GDN_CHUNK), F32)],
        compiler_params=_params("arbitrary", "arbitrary"),
        name="gdn_prep",
    )(proj, proj, proj, conv_w, ab, alog_row, dtb_row, tril, triu, expand)


def _gdn_scan_body(q_ref, k_ref, v_ref, kt_ref, gh_ref, gct_ref, o_ref,
                   aqf, aqb, bsf, bsb, egf, egb, *, nch, nctx, batch):
    head = pl.program_id(1)
    cs = GDN_CHUNK
    ii = lax.broadcasted_iota(jnp.int32, (cs, cs), 0)
    jj = lax.broadcasted_iota(jnp.int32, (cs, cs), 1)
    masks = ((ii >= jj, ii > jj), (ii <= jj, ii < jj))
    hd = HEAD_DIM
    aqs, bss, egs = (aqf, aqb), (bsf, bsb), (egf, egb)

    def intra(it, carry):
        chunks = [it * batch + j for j in range(batch)]
        r0s = [pl.multiple_of(c * cs, cs) for c in chunks]
        qs = [q_ref[pl.ds(r0, cs), :] for r0 in r0s]
        ks = [k_ref[pl.ds(r0, cs), :] for r0 in r0s]
        vs = [v_ref[pl.ds(r0, cs), :] for r0 in r0s]
        kts = [kt_ref[c] for c in chunks]
        raw_q = [_dot(q, kt) for q, kt in zip(qs, kts)]
        raw_k = [_dot(k, kt) for k, kt in zip(ks, kts)]
        pms, xs, atts, kps, qes, gends = [], [], [], [], [], []
        for j, c in enumerate(chunks):
            g4 = gh_ref[pl.ds(r0s[j], cs), :]
            qf, kf, vf = qs[j].astype(F32), ks[j].astype(F32), vs[j].astype(F32)
            ktf = kts[j].astype(F32)
            for d in (0, 1):
                incl, strict = masks[d]
                gcol = g4[:, d:d + 1]
                bcol = g4[:, 2 + d:3 + d]
                grow = gct_ref[c, pl.ds(d * GDN_HEADS + head, 1), :]
                decay = jnp.where(incl, jnp.exp(jnp.where(incl, gcol - grow, 0.0)), 0.0)
                eg = jnp.exp(gcol)
                gend = grow[:, cs - 1:cs] if d == 0 else grow[:, 0:1]
                pms.append(jnp.where(strict, -(raw_k[j] * bcol) * decay, 0.0))
                xs.append(jnp.concatenate([vf * bcol, kf * (bcol * eg)], axis=1))
                atts.append((raw_q[j] * decay).astype(BF16))
                kps.append((ktf * jnp.exp(gend - grow)).astype(BF16))
                qes.append(qf * eg)
                gends.append(gend)
        for stage in range(6):
            pbs = [pm.astype(BF16) for pm in pms]
            ys = [_dot(pb, x.astype(BF16)) for pb, x in zip(pbs, xs)]
            if stage < 5:
                pms = [_dot(pb, pb) for pb in pbs]
            xs = [x + y for x, y in zip(xs, ys)]
        xbs = [x.astype(BF16) for x in xs]
        aws = [_dot(att, xb) for att, xb in zip(atts, xbs)]
        kxs = [_dot(kp, xb) for kp, xb in zip(kps, xbs)]
        for j, c in enumerate(chunks):
            for d in (0, 1):
                n = 2 * j + d
                aqs[d][c, 0:hd, :] = (-kxs[n][:, hd:]).astype(BF16)
                aqs[d][c, hd:hd + cs, :] = (qes[n] - aws[n][:, hd:]).astype(BF16)
                bss[d][c] = kxs[n][:, :hd]
                egs[d][c] = jnp.broadcast_to(jnp.exp(gends[n]), (8, hd))
            o_ref[pl.ds(r0s[j], cs), :] = aws[2 * j][:, :hd] + aws[2 * j + 1][:, :hd]
        return carry

    def inter(i, carry):
        cb = jnp.where(i < nctx, nctx - 1 - i, nch - 1 - (i - nctx))
        new = []
        for d, c in ((0, i), (1, cb)):
            s = carry[d]
            r0 = pl.multiple_of(c * cs, cs)
            r = _dot(aqs[d][c], s.astype(BF16))
            o_ref[pl.ds(r0, cs), :] += r[hd:]
            new.append(s * egs[d][c][0:1, :] + r[:hd] + bss[d][c])
        return tuple(new)

    lax.fori_loop(0, nch // batch, intra, 0)
    zero = jnp.zeros((hd, hd), F32)
    lax.fori_loop(0, nch, inter, (zero, zero), unroll=2)


def _gdn_scan_call(qkvn, kt3, gh, gct3, nb, period, ctx):
    t = qkvn.shape[0]
    nch = period // GDN_CHUNK
    nctx = ctx // GDN_CHUNK
    hd = HEAD_DIM
    body = functools.partial(_gdn_scan_body, nch=nch, nctx=nctx, batch=_pick(nch, GDN_INTRA_BATCH, 1))
    dirs2 = lambda shape, dt: [pltpu.VMEM(shape, dt), pltpu.VMEM(shape, dt)]
    return pl.pallas_call(
        body,
        grid=(nb, GDN_HEADS),
        in_specs=[pl.BlockSpec((period, hd), lambda b, h: (b, h)),
                  pl.BlockSpec((period, hd), lambda b, h: (b, GDN_HEADS + h)),
                  pl.BlockSpec((period, hd), lambda b, h: (b, 2 * GDN_HEADS + h)),
                  pl.BlockSpec((nch, hd, GDN_CHUNK), lambda b, h: (b, h, 0)),
                  pl.BlockSpec((period, hd), lambda b, h: (b, h)),
                  pl.BlockSpec((nch, 2 * GDN_HEADS, GDN_CHUNK), lambda b, h: (b, 0, 0))],
        out_specs=pl.BlockSpec((period, hd), lambda b, h: (b, h)),
        out_shape=jax.ShapeDtypeStruct((t, GDN_WIDTH), F32),
        scratch_shapes=(dirs2((nch, hd + GDN_CHUNK, hd), BF16) + dirs2((nch, hd, hd), F32)
                        + dirs2((nch, 8, hd), F32)),
        compiler_params=_params("arbitrary", "arbitrary"),
        name="gdn_scan",
    )(qkvn, qkvn, qkvn, kt3, gh, gct3)


def _s5_tables_body(lr_ref, li_ref, ldt_ref, btr_ref, bti_ref, cr_ref, ci_ref,
                    ar_ref, ai_ref, cmr_ref, cmi_ref, k_ref, ll_ref):
    lr = lr_ref[...]
    li = li_ref[...]
    dt = jnp.exp(ldt_ref[...])
    kk = lax.broadcasted_iota(jnp.int32, (24, 1), 0).astype(F32)
    mag = jnp.exp(kk * (lr * dt))
    ang = kk * (li * dt)
    er = mag * jnp.cos(ang)
    ei = mag * jnp.sin(ang)
    nr = er[1:2] - 1.0
    ni = ei[1:2]
    den = lr * lr + li * li
    cfr = (nr * lr + ni * li) / den
    cfi = (ni * lr - nr * li) / den
    btr = btr_ref[...]
    bti = bti_ref[...]
    bbr = btr * cfr - bti * cfi
    bbi = btr * cfi + bti * cfr
    cr = cr_ref[...]
    ci = ci_ref[...]
    a_r, a_i, cm_r, cm_i, c0_r, c0_i = [], [], [], [], [], []
    for k in range(S5_CHUNK):
        e_r, e_i = er[k:k + 1], ei[k:k + 1]
        a_r.append(bbr * e_r - bbi * e_i)
        a_i.append(bbr * e_i + bbi * e_r)
        c0_r.append(cr * e_r - ci * e_i)
        c0_i.append(cr * e_i + ci * e_r)
        f_r, f_i = er[k + 1:k + 2], ei[k + 1:k + 2]
        cm_r.append(cr * f_r - ci * f_i)
        cm_i.append(-(cr * f_i + ci * f_r))
    ar_ref[...] = jnp.concatenate(a_r, axis=0)
    ai_ref[...] = jnp.concatenate(a_i, axis=0)
    cmr_ref[...] = jnp.concatenate(cm_r, axis=0)
    cmi_ref[...] = jnp.concatenate(cm_i, axis=0)
    c0r = jnp.concatenate(c0_r, axis=0)
    c0i = jnp.concatenate(c0_i, axis=0)

    def dot3_nt(a, b):
        a0 = a.astype(BF16)
        a1 = (a - a0.astype(F32)).astype(BF16)
        b0 = b.astype(BF16)
        b1 = (b - b0.astype(F32)).astype(BF16)
        return _dot_nt(a0, b0) + _dot_nt(a0, b1) + _dot_nt(a1, b0)

    k_ref[...] = dot3_nt(c0r, bbr) - dot3_nt(c0i, bbi)
    ll_ref[...] = jnp.concatenate([er[S5_CHUNK:S5_CHUNK + 1], ei[S5_CHUNK:S5_CHUNK + 1]], axis=0)


def _s5_tables_call(lam_re, lam_im, log_dt, b_re, b_im, c_re, c_im):
    _, g, p = lam_re.shape
    cg = S5_GROUP
    rows = S5_CHUNK * cg
    row4 = lambda a: a.reshape(2, g, 1, p)
    spec = lambda r, c: pl.BlockSpec((None, None, r, c), lambda d, gi: (d, gi, 0, 0))
    ldt = jnp.broadcast_to(log_dt[:, :, None, None], (2, g, 1, p))
    return pl.pallas_call(
        _s5_tables_body,
        grid=(2, g),
        in_specs=[spec(1, p), spec(1, p), spec(1, p), spec(cg, p), spec(cg, p), spec(cg, p), spec(cg, p)],
        out_specs=[spec(rows, p), spec(rows, p), spec(rows, p), spec(rows, p), spec(rows, cg), spec(2, p)],
        out_shape=[jax.ShapeDtypeStruct((2, g, rows, p), F32)] * 4
                  + [jax.ShapeDtypeStruct((2, g, rows, cg), F32), jax.ShapeDtypeStruct((2, g, 2, p), F32)],
        compiler_params=_params("arbitrary", "arbitrary"),
        name="s5_tables",
    )(row4(lam_re), row4(lam_im), ldt, jnp.swapaxes(b_re, -1, -2), jnp.swapaxes(b_im, -1, -2), c_re, c_im)


def _s5_operators(tables):
    a_r, a_i, cm_r, cm_i, kk, ll = tables
    g = a_r.shape[1]
    n, cg, p = S5_CHUNK, S5_GROUP, S5_STATE

    def win(a):
        a = a.reshape(2, g, n, cg, p)
        return jnp.concatenate([a[0][:, ::-1].reshape(g, n * cg, p), a[1].reshape(g, n * cg, p)], -1).astype(BF16)

    def wout(c):
        c = c.reshape(2, g, n, cg, p)
        full = jnp.concatenate([c[0].reshape(g, n * cg, p), c[1][:, ::-1].reshape(g, n * cg, p)], -1)
        return jnp.swapaxes(full, 1, 2).astype(BF16)

    k5 = kk.reshape(2, g, n, cg, cg)
    s_i = jnp.arange(n)[:, None]
    t_i = jnp.arange(n)[None, :]

    def toeplitz(k4, lag, valid):
        blk = k4[:, jnp.clip(lag, 0, n - 1)]
        blk = jnp.where(valid[None, :, :, None, None], blk, 0.0)
        return blk.transpose(0, 1, 4, 2, 3).reshape(g, n * cg, n * cg).astype(BF16)

    ktf = toeplitz(k5[0], t_i - s_i, t_i >= s_i)
    ktb = toeplitz(k5[1], s_i - t_i, s_i >= t_i)
    gb = S5_GROUPS_PER_STEP
    lam_r = jnp.concatenate([ll[0, :, 0], ll[1, :, 0]], -1).reshape(g // gb, 1, gb * 2 * p)
    lam_i = jnp.concatenate([ll[0, :, 1], ll[1, :, 1]], -1).reshape(g // gb, 1, gb * 2 * p)
    return ktf, ktb, win(a_r), win(a_i), wout(cm_r), wout(cm_i), lam_r, lam_i


def _s5_body(u_ref, ktf_ref, ktb_ref, wir_ref, wii_ref, wor_ref, woi_ref, lr_ref, li_ref, y_ref,
             *scratch, nb, nch, nctx):
    gb = S5_GROUPS_PER_STEP
    sw = 2 * S5_STATE
    xr, xi, sfr, sfi, sbr, sbi = (scratch[k * gb:(k + 1) * gb] for k in range(6))
    for g in range(gb):
        ug = u_ref[:, g * S5_TILE:(g + 1) * S5_TILE]
        xr[g][...] = _dot(ug, wir_ref[g])
        xi[g][...] = _dot(ug, wii_ref[g])
    is_f = lax.broadcasted_iota(jnp.int32, (1, sw), 1) < S5_STATE
    ar = [lr_ref[:, g * sw:(g + 1) * sw] for g in range(gb)]
    ai = [li_ref[:, g * sw:(g + 1) * sw] for g in range(gb)]

    def step(i, carry):
        cb = jnp.where(i < nctx, nctx - 1 - i, nch - 1 - (i - nctx))
        at_f = pl.ds(i, nb, stride=nch)
        at_b = pl.ds(cb, nb, stride=nch)
        new = []
        for g in range(gb):
            sr, si = carry[2 * g], carry[2 * g + 1]
            sfr[g][at_f, :] = sr
            sfi[g][at_f, :] = si
            sbr[g][at_b, :] = sr
            sbi[g][at_b, :] = si
            inr = jnp.where(is_f, xr[g][at_f, :], xr[g][at_b, :])
            ini = jnp.where(is_f, xi[g][at_f, :], xi[g][at_b, :])
            new.append(ar[g] * sr - ai[g] * si + inr)
            new.append(ar[g] * si + ai[g] * sr + ini)
        return tuple(new)

    zero = jnp.zeros((nb, sw), F32)
    lax.fori_loop(0, nch, step, (zero,) * (2 * gb))
    for g in range(gb):
        s_r = jnp.where(is_f, sfr[g][...], sbr[g][...]).astype(BF16)
        s_i = jnp.where(is_f, sfi[g][...], sbi[g][...]).astype(BF16)
        ug = u_ref[:, g * S5_TILE:(g + 1) * S5_TILE]
        y_ref[:, g * S5_TILE:(g + 1) * S5_TILE] = (_dot(ug, ktf_ref[g]) + _dot(ug, ktb_ref[g])
                                                   + _dot(s_r, wor_ref[g]) + _dot(s_i, woi_ref[g]))


def _s5_call(u_blk, ops, nb, period, ctx):
    ktf, ktb, wir, wii, wor, woi, lam_r, lam_i = ops
    rows, width = u_blk.shape
    g = width // S5_TILE
    gb = S5_GROUPS_PER_STEP
    sw = 2 * S5_STATE
    nch = period // S5_CHUNK
    nctx = ctx // S5_CHUNK
    body = functools.partial(_s5_body, nb=nb, nch=nch, nctx=nctx)
    wspec = lambda r, c: pl.BlockSpec((gb, r, c), lambda i: (i, 0, 0))
    return pl.pallas_call(
        body,
        grid=(g // gb,),
        in_specs=[pl.BlockSpec((rows, gb * S5_TILE), lambda i: (0, i)),
                  wspec(S5_TILE, S5_TILE), wspec(S5_TILE, S5_TILE),
                  wspec(S5_TILE, sw), wspec(S5_TILE, sw), wspec(sw, S5_TILE), wspec(sw, S5_TILE),
                  pl.BlockSpec((None, 1, gb * sw), lambda i: (i, 0, 0)),
                  pl.BlockSpec((None, 1, gb * sw), lambda i: (i, 0, 0))],
        out_specs=pl.BlockSpec((rows, gb * S5_TILE), lambda i: (0, i)),
        out_shape=jax.ShapeDtypeStruct((rows, width), F32),
        scratch_shapes=[pltpu.VMEM((rows, sw), F32) for _ in range(6 * gb)],
        compiler_params=_params("arbitrary"),
        name="s5_scan",
    )(u_blk, ktf, ktb, wir, wii, wor, woi, lam_r, lam_i)


def _merge_body(o_ref, z_ref, u_ref, y_ref, h_ref, mb_ref, mc_ref, gn_ref, ds_ref, wglu_ref, bglu_ref,
                wout_ref, lng_ref, lnb_ref, out_ref, cat_ref, *, tm, tpb, ctx):
    rib = _rows_in_batch(pl.program_id(0), tm, tpb)
    gn = gn_ref[...]
    for hh in range(GDN_HEADS):
        sl = slice(hh * HEAD_DIM, (hh + 1) * HEAD_DIM)
        oh = o_ref[:, sl]
        r = lax.rsqrt(jnp.mean(oh * oh, -1, keepdims=True) + RMS_EPS)
        cat_ref[:, sl] = (oh * r * gn * _silu(z_ref[:, sl].astype(F32))).astype(BF16)
    s = y_ref[...] + ds_ref[...] * u_ref[...].astype(F32)
    s = 0.5 * s * (1.0 + jnp.tanh(math.sqrt(2.0 / math.pi) * (s + 0.044715 * (s * s * s))))
    s = s * jax.nn.sigmoid(_dot(s.astype(BF16), wglu_ref[...]) + bglu_ref[...])
    cat_ref[:, GDN_WIDTH:] = s.astype(BF16)
    mix = _dot(cat_ref[...], wout_ref[...])
    gate = jnp.where(rib < ctx, mc_ref[2:3, :], mb_ref[2:3, :])
    out_ref[...] = _layernorm(ALPHA * h_ref[...] + gate * mix, lng_ref[...], lnb_ref[...])


def _merge_call(o, proj, y, h, mod, gn, d_skip, w_glu, b_glu, w_out, ln_g, ln_b, nb, period, ctx):
    t, d = h.shape
    w = GDN_WIDTH
    sw = d - w
    tm = _pick(period, 272, 16)
    tpb = period // tm
    body = functools.partial(_merge_body, tm=tm, tpb=tpb, ctx=ctx)
    const = lambda r, c: pl.BlockSpec((r, c), lambda i: (0, 0))
    return pl.pallas_call(
        body,
        grid=(t // tm,),
        in_specs=[pl.BlockSpec((tm, w), lambda i: (i, 0)),
                  pl.BlockSpec((tm, w), lambda i: (i, 3)),
                  pl.BlockSpec((tm, sw), lambda i: (i, 4 * w // sw)),
                  pl.BlockSpec((tm, sw), lambda i: (i, 0)),
                  pl.BlockSpec((tm, d), lambda i: (i, 0)),
                  pl.BlockSpec((None, 6, d), lambda i: (i // tpb, 0, 0)),
                  pl.BlockSpec((None, 6, d), lambda i: (nb, 0, 0)),
                  const(1, HEAD_DIM), const(1, sw), const(sw, sw), const(1, sw), const(d, d),
                  const(1, d), const(1, d)],
        out_specs=pl.BlockSpec((tm, d), lambda i: (i, 0)),
        out_shape=jax.ShapeDtypeStruct((t, d), F32),
        scratch_shapes=[pltpu.VMEM((tm, d), BF16)],
        compiler_params=_params("arbitrary"),
        name="even_merge",
    )(o, proj, proj, y, h, mod, mod, gn, d_skip, w_glu, b_glu, w_out, ln_g, ln_b)


def _ffn_body(x_ref, xh_ref, mb_ref, mc_ref, wv_ref, wg_ref, cwv_ref, cwg_ref, cbv_ref, cbg_ref, wd_ref,
              lng_ref, lnb_ref, o_ref, xb_ref, *, tm, tpb, ctx, period):
    i = pl.program_id(0)
    j = pl.program_id(1)
    rib = _rows_in_batch(i, tm, tpb)
    n = tm + 32

    @pl.when(j == 0)
    def _():
        xb_ref[0:tm, :] = _modulate(x_ref[...], rib < ctx, mb_ref, mc_ref, 3, 4).astype(BF16)
        base = (i % tpb) * tm
        off = lax.broadcasted_iota(jnp.int32, (16, 1), 0)
        xh = xh_ref[...]
        xb_ref[tm:tm + 16, :] = _modulate(xh[0:16], base + tm + off < ctx, mb_ref, mc_ref, 3, 4).astype(BF16)
        xb_ref[tm + 16:n, :] = _modulate(xh[16:32], base - 16 + off < ctx, mb_ref, mc_ref, 3, 4).astype(BF16)
        o_ref[...] = jnp.zeros_like(o_ref)

    has_prev, has_next = _seq_edges(rib, ctx, period)
    xb = xb_ref[...]

    def conv(u, cw_ref, cb_ref, sl):
        cw = cw_ref[:, sl]
        up = pltpu.roll(u, 1, 0)[:tm]
        un = pltpu.roll(u, n - 1, 0)[:tm]
        return (jnp.where(has_prev, up, 0.0) * cw[0:1] + u[:tm] * cw[1:2]
                + jnp.where(has_next, un, 0.0) * cw[2:3] + cb_ref[:, sl])

    fc = wv_ref.shape[1]
    slabs = [slice(a, a + FFN_SLAB) for a in range(0, fc, FFN_SLAB)]
    ups = [(_dot(xb, wv_ref[:, sl]), _dot(xb, wg_ref[:, sl])) for sl in slabs]
    for sl, (uv, ug) in zip(slabs, ups):
        act = conv(uv, cwv_ref, cbv_ref, sl) * _silu(conv(ug, cwg_ref, cbg_ref, sl))
        o_ref[...] += _dot(act.astype(BF16), wd_ref[sl, :])

    @pl.when(j == pl.num_programs(1) - 1)
    def _():
        g = jnp.where(rib < ctx, mc_ref[5:6, :], mb_ref[5:6, :])
        o_ref[...] = _layernorm(ALPHA * x_ref[...] + g * o_ref[...], lng_ref[...], lnb_ref[...])


def _ffn_call(h, mod, w_up, conv_w, conv_b, w_down, ln_g, ln_b, nb, period, ctx):
    t, d = h.shape
    f = w_down.shape[0]
    tm = _pick(period, 544, 16)
    tpb = period // tm
    nt = t // tm
    fc = _pick(f, FFN_CHUNK, FFN_SLAB)
    nf = f // fc
    h3 = h.reshape(nt, tm, d)
    pad = jnp.zeros((1, 16, d), h.dtype)
    halo = jnp.concatenate([jnp.concatenate([h3[1:, :16], pad], 0),
                            jnp.concatenate([pad, h3[:-1, tm - 16:]], 0)], axis=1)
    body = functools.partial(_ffn_body, tm=tm, tpb=tpb, ctx=ctx, period=period)
    return pl.pallas_call(
        body,
        grid=(nt, nf),
        in_specs=[pl.BlockSpec((tm, d), lambda i, j: (i, 0)),
                  pl.BlockSpec((None, 32, d), lambda i, j: (i, 0, 0)),
                  pl.BlockSpec((None, 6, d), lambda i, j: (i // tpb, 0, 0)),
                  pl.BlockSpec((None, 6, d), lambda i, j: (nb, 0, 0)),
                  pl.BlockSpec((d, fc), lambda i, j: (0, j)),
                  pl.BlockSpec((d, fc), lambda i, j: (0, nf + j)),
                  pl.BlockSpec((3, fc), lambda i, j: (0, j)),
                  pl.BlockSpec((3, fc), lambda i, j: (0, nf + j)),
                  pl.BlockSpec((1, fc), lambda i, j: (0, j)),
                  pl.BlockSpec((1, fc), lambda i, j: (0, nf + j)),
                  pl.BlockSpec((fc, d), lambda i, j: (j, 0)),
                  pl.BlockSpec((1, d), lambda i, j: (0, 0)),
                  pl.BlockSpec((1, d), lambda i, j: (0, 0))],
        out_specs=pl.BlockSpec((tm, d), lambda i, j: (i, 0)),
        out_shape=jax.ShapeDtypeStruct((t, d), F32),
        scratch_shapes=[pltpu.VMEM((tm + 32, d), BF16)],
        compiler_params=_params("arbitrary", "arbitrary"),
        name="conv_ffn",
    )(h, halo, mod, mod, w_up, w_up, conv_w, conv_w, conv_b, conv_b, w_down, ln_g, ln_b)


def _qkv_body(x_ref, mb_ref, mc_ref, w_ref, cos_ref, sin_ref, qn_ref, kn_ref, o_ref, xb_ref,
              *, tm, tpb, ctx, nq_tiles):
    i = pl.program_id(0)
    j = pl.program_id(1)

    @pl.when(j == 0)
    def _():
        rib = _rows_in_batch(i, tm, tpb)
        xb_ref[...] = _modulate(x_ref[...], rib < ctx, mb_ref, mc_ref, 0, 1).astype(BF16)

    acc = _dot(xb_ref[...], w_ref[...])
    cos = cos_ref[...]
    sin = sin_ref[...]
    lane = lax.broadcasted_iota(jnp.int32, (1, HEAD_DIM), 1)
    first = (lane % 64) < 32
    heads = acc.shape[1] // HEAD_DIM

    def normrope(xh, wn, scale):
        xn = xh * lax.rsqrt(jnp.mean(xh * xh, -1, keepdims=True) + RMS_EPS) * wn
        partner = jnp.where(first, pltpu.roll(xn, HEAD_DIM - 32, 1), pltpu.roll(xn, 32, 1))
        return (xn * cos + partner * sin) * scale

    @pl.when(j < nq_tiles)
    def _():
        for hh in range(heads):
            sl = slice(hh * HEAD_DIM, (hh + 1) * HEAD_DIM)
            o_ref[:, sl] = normrope(acc[:, sl], qn_ref[...], ATT_SCALE).astype(BF16)

    @pl.when(j == nq_tiles)
    def _():
        for hh in range(heads):
            sl = slice(hh * HEAD_DIM, (hh + 1) * HEAD_DIM)
            if hh < ATT_KV_HEADS:
                o_ref[:, sl] = normrope(acc[:, sl], kn_ref[...], 1.0).astype(BF16)
            else:
                o_ref[:, sl] = acc[:, sl].astype(BF16)


def _qkv_call(h, mod, w_in, cos, sin, qn, kn, nb, period, ctx):
    t, d = h.shape
    n = w_in.shape[1]
    tn = 2 * ATT_KV_HEADS * HEAD_DIM
    nq_tiles = (n - tn) // tn
    tm = _pick(period, 544, 16)
    tpb = period // tm
    body = functools.partial(_qkv_body, tm=tm, tpb=tpb, ctx=ctx, nq_tiles=nq_tiles)
    return pl.pallas_call(
        body,
        grid=(t // tm, n // tn),
        in_specs=[pl.BlockSpec((tm, d), lambda i, j: (i, 0)),
                  pl.BlockSpec((None, 6, d), lambda i, j: (i // tpb, 0, 0)),
                  pl.BlockSpec((None, 6, d), lambda i, j: (nb, 0, 0)),
                  pl.BlockSpec((d, tn), lambda i, j: (0, j)),
                  pl.BlockSpec((tm, HEAD_DIM), lambda i, j: (i % tpb, 0)),
                  pl.BlockSpec((tm, HEAD_DIM), lambda i, j: (i % tpb, 0)),
                  pl.BlockSpec((1, HEAD_DIM), lambda i, j: (0, 0)),
                  pl.BlockSpec((1, HEAD_DIM), lambda i, j: (0, 0))],
        out_specs=pl.BlockSpec((tm, tn), lambda i, j: (i, j)),
        out_shape=jax.ShapeDtypeStruct((t, n), BF16),
        scratch_shapes=[pltpu.VMEM((tm, d), BF16)],
        compiler_params=_params("arbitrary", "arbitrary"),
        name="odd_qkv",
    )(h, mod, mod, w_in, cos, sin, qn, kn)


def _rope_tables(seq, ctx):
    rows = seq // GRID_W
    row = jnp.repeat(jnp.arange(rows, dtype=F32), GRID_W)
    col = jnp.tile(jnp.arange(GRID_W, dtype=F32), rows)
    half = HEAD_DIM // 4
    inv = ROPE_THETA ** (-jnp.arange(half, dtype=F32) / half)
    ar = row[:, None] * inv
    ac = col[:, None] * inv
    cos = jnp.concatenate([jnp.cos(ar), jnp.cos(ar), jnp.cos(ac), jnp.cos(ac)], -1)
    sin = jnp.concatenate([-jnp.sin(ar), jnp.sin(ar), -jnp.sin(ac), jnp.sin(ac)], -1)
    cos = jnp.concatenate([jnp.ones((ctx, HEAD_DIM), F32), cos], 0)
    sin = jnp.concatenate([jnp.zeros((ctx, HEAD_DIM), F32), sin], 0)
    return cos, sin


def _attn_body(q_ref, k_ref, v_ref, o_ref, *, tq, tk, nk):
    q = jnp.concatenate([q_ref[:, g * HEAD_DIM:(g + 1) * HEAD_DIM] for g in range(ATT_GROUP)], axis=0)
    rows = ATT_GROUP * tq
    m = jnp.full((rows, 1), NEG_BIG, F32)
    l = jnp.zeros((rows, 1), F32)
    acc = jnp.zeros((rows, HEAD_DIM), F32)
    s_next = _dot_nt(q, k_ref[0:tk, :])
    for kb in range(nk):
        s = s_next
        if kb + 1 < nk:
            s_next = _dot_nt(q, k_ref[(kb + 1) * tk:(kb + 2) * tk, :])
        v = v_ref[kb * tk:(kb + 1) * tk, :]
        m_new = jnp.maximum(m, jnp.max(s, -1, keepdims=True))
        a = jnp.exp(m - m_new)
        p = jnp.exp(s - m_new)
        l = a * l + jnp.sum(p, -1, keepdims=True)
        acc = a * acc + _dot(p.astype(BF16), v)
        m = m_new
    out = acc / l
    for g in range(ATT_GROUP):
        o_ref[:, g * HEAD_DIM:(g + 1) * HEAD_DIM] = out[g * tq:(g + 1) * tq].astype(o_ref.dtype)


def _attn_call(qkv, nb, seq, ctx):
    period = seq + ctx
    tq = _pick(math.gcd(seq, ctx), 256, 16)
    nk = 4
    tk = period // nk
    gw = ATT_GROUP * HEAD_DIM
    qpb = seq // tq
    rpb = period // tq
    skip = ctx // tq
    kcol = ATT_HEADS
    vcol = ATT_HEADS + ATT_KV_HEADS
    body = functools.partial(_attn_body, tq=tq, tk=tk, nk=nk)
    return pl.pallas_call(
        body,
        grid=(nb, ATT_KV_HEADS, qpb),
        in_specs=[pl.BlockSpec((tq, gw), lambda b, kv, qi: (b * rpb + skip + qi, kv)),
                  pl.BlockSpec((period, HEAD_DIM), lambda b, kv, qi: (b, kcol + kv)),
                  pl.BlockSpec((period, HEAD_DIM), lambda b, kv, qi: (b, vcol + kv))],
        out_specs=pl.BlockSpec((tq, gw), lambda b, kv, qi: (b * qpb + qi, kv)),
        out_shape=jax.ShapeDtypeStruct((nb * seq, ATT_HEADS * HEAD_DIM), BF16),
        compiler_params=_params("arbitrary", "arbitrary", "arbitrary"),
        name="gqa_attention",
    )(qkv, qkv, qkv)


def _outproj_body(a_ref, h_ref, mb_ref, w_ref, lng_ref, lnb_ref, o_ref):
    mix = _dot(a_ref[...], w_ref[...])
    o_ref[...] = _layernorm(ALPHA * h_ref[...] + mb_ref[2:3, :] * mix, lng_ref[...], lnb_ref[...])


def _outproj_call(att, h, mod, w_out, ln_g, ln_b, nb, seq, ctx):
    d = h.shape[1]
    period = seq + ctx
    tm = _pick(math.gcd(seq, ctx), 256, 16)
    qpb = seq // tm
    rpb = period // tm
    skip = ctx // tm
    return pl.pallas_call(
        _outproj_body,
        grid=(nb * qpb,),
        in_specs=[pl.BlockSpec((tm, d), lambda i: (i, 0)),
                  pl.BlockSpec((tm, d), lambda i: ((i // qpb) * rpb + skip + i % qpb, 0)),
                  pl.BlockSpec((None, 6, d), lambda i: (i // qpb, 0, 0)),
                  pl.BlockSpec((d, d), lambda i: (0, 0)),
                  pl.BlockSpec((1, d), lambda i: (0, 0)),
                  pl.BlockSpec((1, d), lambda i: (0, 0))],
        out_specs=pl.BlockSpec((tm, d), lambda i: (i, 0)),
        out_shape=jax.ShapeDtypeStruct((nb * seq, d), F32),
        compiler_params=_params("arbitrary"),
        name="odd_outproj",
    )(att, h, mod, w_out, ln_g, ln_b)


def kernel(x, c, ctx, c_ctx, mod_w, mod_b, ln1_g, ln1_b, ln2_g, ln2_b, ffn_w_up, ffn_conv_w, ffn_conv_b, ffn_w_down, e_w_in, e_conv_qkv, e_a_log, e_dt_bias, e_gdn_norm, e_lam_re, e_lam_im, e_log_dt, e_b_re, e_b_im, e_c_re, e_c_im, e_d_skip, e_w_glu, e_b_glu, e_w_out, o_w_in, o_q_norm, o_k_norm, o_w_out):
    nb, seq, d = x.shape
    nctx = ctx.shape[1]
    period = nctx + seq
    t = nb * period
    w = GDN_WIDTH
    assert mod_w.shape[0] == DEPTH == 2 and nb < 8

    h = jnp.concatenate([ctx, x], axis=1).reshape(t, d)
    cs = jnp.zeros((8, d), F32).at[:nb].set(c).at[nb].set(c_ctx)
    mod = _mod_call(cs, mod_w, mod_b).reshape(DEPTH, 8, 6, d)
    row = lambda a: a.reshape(1, -1)

    w_in = e_w_in[0]
    gates_at = 4 * w
    w_main = jnp.concatenate([w_in[:, :gates_at], w_in[:, gates_at + 4 * GDN_HEADS:]], axis=1).astype(BF16)
    w_ab = jnp.pad(w_in[:, gates_at:gates_at + 4 * GDN_HEADS], ((0, 0), (0, 128 - 4 * GDN_HEADS))).astype(BF16)
    proj, ab = _inproj_even_call(h, mod[0], w_main, w_ab, nb, period, nctx)
    pad_row = lambda a: jnp.pad(a.reshape(1, -1), ((0, 0), (0, 128 - 2 * GDN_HEADS)))
    qkvn, kt3, gh, gct3 = _gdn_prep_call(proj, ab, e_conv_qkv[0], pad_row(e_a_log[0]), pad_row(e_dt_bias[0]),
                                         period, nctx)
    o = _gdn_scan_call(qkvn, kt3, gh, gct3, nb, period, nctx)

    ops = _s5_operators(_s5_tables_call(e_lam_re[0], e_lam_im[0], e_log_dt[0], e_b_re[0], e_b_im[0],
                                        e_c_re[0], e_c_im[0]))
    sgroups = (d - w) // S5_GROUP
    blk = lambda a: a.reshape(t // S5_CHUNK, S5_CHUNK, sgroups, S5_GROUP).transpose(0, 2, 1, 3)
    u_blk = blk(proj[:, 4 * w:]).reshape(t // S5_CHUNK, sgroups * S5_TILE)
    y_blk = _s5_call(u_blk, ops, nb, period, nctx)
    y = y_blk.reshape(t // S5_CHUNK, sgroups, S5_CHUNK, S5_GROUP).transpose(0, 2, 1, 3).reshape(t, d - w)

    h = _merge_call(o, proj, y, h, mod[0], row(e_gdn_norm[0]), row(e_d_skip[0]), e_w_glu[0].astype(BF16),
                    row(e_b_glu[0]), e_w_out[0].astype(BF16), row(ln1_g[0]), row(ln1_b[0]), nb, period, nctx)
    h = _ffn_call(h, mod[0], ffn_w_up[0].astype(BF16), ffn_conv_w[0], row(ffn_conv_b[0]),
                  ffn_w_down[0].astype(BF16), row(ln2_g[0]), row(ln2_b[0]), nb, period, nctx)

    cos, sin = _rope_tables(seq, nctx)
    qkv = _qkv_call(h, mod[1], o_w_in[0].astype(BF16), cos, sin, row(o_q_norm[0]), row(o_k_norm[0]),
                    nb, period, nctx)
    att = _attn_call(qkv, nb, seq, nctx)
    hl = _outproj_call(att, h, mod[1], o_w_out[0].astype(BF16), row(ln1_g[1]), row(ln1_b[1]), nb, seq, nctx)
    out = _ffn_call(hl, mod[1], ffn_w_up[1].astype(BF16), ffn_conv_w[1], row(ffn_conv_b[1]),
                    ffn_w_down[1].astype(BF16), row(ln2_g[1]), row(ln2_b[1]), nb, seq, 0)
    return out.reshape(nb, seq, d)
```

```python
import functools
import math

import jax
import jax.numpy as jnp
from jax import lax
from jax.experimental import pallas as pl
from jax.experimental.pallas import tpu as pltpu

F32 = jnp.float32
BF16 = jnp.bfloat16

DEPTH = 2
GDN_HEADS = 8
HEAD_DIM = 128
GDN_WIDTH = GDN_HEADS * HEAD_DIM
GDN_CHUNK = 64
GDN_INTRA_BATCH = 17
GDN_SOLVE_BASE = 8
S5_GROUP = 16
S5_STATE = 64
S5_CHUNK = 16
S5_TILE = S5_CHUNK * S5_GROUP
S5_GROUPS_PER_STEP = 4
ATT_HEADS = 16
ATT_KV_HEADS = 4
ATT_GROUP = ATT_HEADS // ATT_KV_HEADS
ATT_SCALE = HEAD_DIM ** -0.5
ATT_Q_TILE = 128
GRID_W = 64
ROPE_THETA = 10000.0
FFN_CHUNK = 512
FFN_SLAB = 256
ALPHA = (2 * DEPTH) ** 0.25
LN_EPS = 1e-6
RMS_EPS = 1e-6
L2_EPS = 1e-6
VMEM_LIMIT_BYTES = 56 * 1024 * 1024


def _pick(n, target, mult):
    best = None
    for d in range(mult, min(n, target) + 1, mult):
        if n % d == 0:
            best = d
    assert best is not None, (n, target, mult)
    return best


def _params(*sem):
    return pltpu.CompilerParams(dimension_semantics=sem, vmem_limit_bytes=VMEM_LIMIT_BYTES)


def _dot(a, b):
    return jnp.dot(a, b, preferred_element_type=F32)


def _dot_nt(a, b):
    return lax.dot_general(a, b, (((1,), (1,)), ((), ())), preferred_element_type=F32)


def _split3(a):
    p0 = a.astype(BF16)
    r = a - p0.astype(F32)
    p1 = r.astype(BF16)
    p2 = (r - p1.astype(F32)).astype(BF16)
    return p0, p1, p2


def _rows_in_batch(i, tm, tiles_per_batch):
    return (i % tiles_per_batch) * tm + lax.broadcasted_iota(jnp.int32, (tm, 1), 0)


def _modulate(x, is_ctx, mb_ref, mc_ref, shift_i, scale_i):
    scale = jnp.where(is_ctx, mc_ref[scale_i:scale_i + 1, :], mb_ref[scale_i:scale_i + 1, :])
    shift = jnp.where(is_ctx, mc_ref[shift_i:shift_i + 1, :], mb_ref[shift_i:shift_i + 1, :])
    return x * (1.0 + scale) + shift


def _layernorm(r, g, b):
    xc = r - jnp.mean(r, -1, keepdims=True)
    var = jnp.mean(xc * xc, -1, keepdims=True)
    return xc * lax.rsqrt(var + LN_EPS) * g + b


def _silu(x):
    return x * jax.nn.sigmoid(x)


def _seq_edges(rib, ctx, period):
    has_prev = (rib != 0) & (rib != ctx)
    has_next = (rib != ctx - 1) & (rib != period - 1)
    return has_prev, has_next


def _mod_body(c_ref, w_ref, b_ref, o_ref):
    s = _silu(c_ref[...])
    o_ref[...] = _dot(s.astype(BF16), w_ref[...].astype(BF16)) + b_ref[...]


def _mod_call(cs, mod_w, mod_b):
    depth, d, n = mod_w.shape
    tn = _pick(n, 1024, 128)
    return pl.pallas_call(
        _mod_body,
        grid=(depth, n // tn),
        in_specs=[pl.BlockSpec((8, d), lambda l, j: (0, 0)),
                  pl.BlockSpec((None, d, tn), lambda l, j: (l, 0, j)),
                  pl.BlockSpec((None, 1, tn), lambda l, j: (l, 0, j))],
        out_specs=pl.BlockSpec((None, 8, tn), lambda l, j: (l, 0, j)),
        out_shape=jax.ShapeDtypeStruct((depth, 8, n), F32),
        compiler_params=_params("arbitrary", "arbitrary"),
        name="adaln_mod",
    )(cs, mod_w, mod_b.reshape(depth, 1, n))


def _inproj_even_body(x_ref, mb_ref, mc_ref, w_ref, wab_ref, o_ref, ab_ref, u_ref, xb_ref,
                      *, tm, tpb, ctx, u_tile, u_off):
    i = pl.program_id(0)
    j = pl.program_id(1)

    @pl.when(j == 0)
    def _():
        rib = _rows_in_batch(i, tm, tpb)
        xb = _modulate(x_ref[...], rib < ctx, mb_ref, mc_ref, 0, 1).astype(BF16)
        xb_ref[...] = xb
        ab_ref[...] = _dot(xb, wab_ref[...])

    acc = _dot(xb_ref[...], w_ref[...])
    o_ref[...] = acc.astype(o_ref.dtype)

    @pl.when(j == u_tile)
    def _():
        u_ref[...] = acc[:, u_off:]


def _inproj_even_call(h, mod, w_main, w_ab, u_start, nb, period, ctx):
    t, d = h.shape
    n = w_main.shape[1]
    tm = _pick(period, 544, 16)
    tn = _pick(n, 1280, 256)
    tpb = period // tm
    u_tile = u_start // tn
    u_off = u_start - u_tile * tn
    assert u_tile == n // tn - 1, "the S5 columns must end the last column tile"
    body = functools.partial(_inproj_even_body, tm=tm, tpb=tpb, ctx=ctx, u_tile=u_tile, u_off=u_off)
    return pl.pallas_call(
        body,
        grid=(t // tm, n // tn),
        in_specs=[pl.BlockSpec((tm, d), lambda i, j: (i, 0)),
                  pl.BlockSpec((None, 6, d), lambda i, j: (i // tpb, 0, 0)),
                  pl.BlockSpec((None, 6, d), lambda i, j: (nb, 0, 0)),
                  pl.BlockSpec((d, tn), lambda i, j: (0, j)),
                  pl.BlockSpec((d, 128), lambda i, j: (0, 0))],
        out_specs=[pl.BlockSpec((tm, tn), lambda i, j: (i, j)),
                   pl.BlockSpec((tm, 128), lambda i, j: (i, 0)),
                   pl.BlockSpec((tm, n - u_start), lambda i, j: (i, 0))],
        out_shape=[jax.ShapeDtypeStruct((t, n), BF16), jax.ShapeDtypeStruct((t, 128), F32),
                   jax.ShapeDtypeStruct((t, n - u_start), F32)],
        scratch_shapes=[pltpu.VMEM((tm, d), BF16)],
        compiler_params=_params("arbitrary", "arbitrary"),
        name="even_inproj",
    )(h, mod, mod, w_main, w_ab)


def _gdn_prep_body(x_ref, xn_ref, xp_ref, cw_ref, ab_ref, alog_ref, dtb_ref, tril_ref, triu_ref, e_ref,
                   qkv_ref, kt_ref, gh_ref, gct_ref, *, tr, tpb, ctx, period):
    i = pl.program_id(0)
    sec = pl.program_id(1)
    rib = _rows_in_batch(i, tr, tpb)
    has_prev, has_next = _seq_edges(rib, ctx, period)
    ext = jnp.concatenate([x_ref[...].astype(F32), xn_ref[...].astype(F32), xp_ref[...].astype(F32)], axis=0)
    n = tr + 32
    xprev = pltpu.roll(ext, 1, 0)[:tr]
    xnext = pltpu.roll(ext, n - 1, 0)[:tr]
    cw = cw_ref[...]
    y = (jnp.where(has_prev, xprev, 0.0) * cw[0:1] + ext[:tr] * cw[1:2]
         + jnp.where(has_next, xnext, 0.0) * cw[2:3])
    y = _silu(y)

    def l2n(scale):
        parts = []
        for hh in range(GDN_HEADS):
            yh = y[:, hh * HEAD_DIM:(hh + 1) * HEAD_DIM]
            parts.append(yh * (lax.rsqrt(jnp.sum(yh * yh, -1, keepdims=True) + L2_EPS) * scale))
        return jnp.concatenate(parts, axis=1)

    @pl.when(sec == 0)
    def _():
        qkv_ref[...] = l2n(HEAD_DIM ** -0.5).astype(BF16)
        ab = ab_ref[...]
        lane = lax.broadcasted_iota(jnp.int32, (1, 128), 1)
        xg = ab + dtb_ref[...]
        softplus = jnp.maximum(xg, 0.0) + jnp.log1p(jnp.exp(-jnp.abs(xg)))
        g = -jnp.exp(alog_ref[...]) * softplus
        g0, g1, g2 = _split3(g)
        tril = tril_ref[...]
        triu = triu_ref[...]
        pre = _dot(tril, g0) + _dot(tril, g1) + _dot(tril, g2)
        suf = _dot(triu, g0) + _dot(triu, g1) + _dot(triu, g2)
        gc = jnp.where(lane < GDN_HEADS, pre, suf)
        vals = jnp.where(lane < 2 * GDN_HEADS, gc, jax.nn.sigmoid(ab))
        v0, v1, v2 = _split3(vals)
        e = e_ref[...]
        gh_ref[...] = _dot(v0, e) + _dot(v1, e) + _dot(v2, e)
        gct = gc.T
        for c in range(tr // GDN_CHUNK):
            gct_ref[c] = gct[:2 * GDN_HEADS, c * GDN_CHUNK:(c + 1) * GDN_CHUNK]

    @pl.when(sec == 1)
    def _():
        kn = l2n(1.0)
        qkv_ref[...] = kn.astype(BF16)
        knt = kn.T
        for c in range(tr // GDN_CHUNK):
            kt_ref[c] = knt[:, c * GDN_CHUNK:(c + 1) * GDN_CHUNK].astype(BF16)

    @pl.when(sec == 2)
    def _():
        qkv_ref[...] = y.astype(BF16)


def _gdn_prep_call(proj, ab, conv_w, alog_row, dtb_row, period, ctx):
    t = proj.shape[0]
    w = GDN_WIDTH
    tr = _pick(period, 256, GDN_CHUNK)
    tpb = period // tr
    hb = tr // 16
    nh = t // 16
    nck = tr // GDN_CHUNK
    r = jnp.arange(tr)
    same = (r[:, None] // GDN_CHUNK) == (r[None, :] // GDN_CHUNK)
    tril = (same & (r[:, None] >= r[None, :])).astype(BF16)
    triu = (same & (r[:, None] <= r[None, :])).astype(BF16)
    src = jnp.arange(128)[:, None]
    dst = jnp.arange(w)[None, :]
    expand = ((dst % HEAD_DIM < 4) & (src == (dst % HEAD_DIM) * GDN_HEADS + dst // HEAD_DIM)).astype(BF16)
    body = functools.partial(_gdn_prep_body, tr=tr, tpb=tpb, ctx=ctx, period=period)
    return pl.pallas_call(
        body,
        grid=(t // tr, 3),
        in_specs=[pl.BlockSpec((tr, w), lambda i, s: (i, s)),
                  pl.BlockSpec((16, w), lambda i, s: (jnp.minimum((i + 1) * hb, nh - 1), s)),
                  pl.BlockSpec((16, w), lambda i, s: (jnp.maximum(i * hb - 1, 0), s)),
                  pl.BlockSpec((3, w), lambda i, s: (0, s)),
                  pl.BlockSpec((tr, 128), lambda i, s: (i, 0)),
                  pl.BlockSpec((1, 128), lambda i, s: (0, 0)),
                  pl.BlockSpec((1, 128), lambda i, s: (0, 0)),
                  pl.BlockSpec((tr, tr), lambda i, s: (0, 0)),
                  pl.BlockSpec((tr, tr), lambda i, s: (0, 0)),
                  pl.BlockSpec((128, w), lambda i, s: (0, 0))],
        out_specs=[pl.BlockSpec((tr, w), lambda i, s: (i, s)),
                   pl.BlockSpec((nck, w, GDN_CHUNK), lambda i, s: (i, 0, 0)),
                   pl.BlockSpec((tr, w), lambda i, s: (i, 0)),
                   pl.BlockSpec((nck, 2 * GDN_HEADS, GDN_CHUNK), lambda i, s: (i, 0, 0))],
        out_shape=[jax.ShapeDtypeStruct((t, 3 * w), BF16),
                   jax.ShapeDtypeStruct((t // GDN_CHUNK, w, GDN_CHUNK), BF16),
                   jax.ShapeDtypeStruct((t, w), F32),
                   jax.ShapeDtypeStruct((t // GDN_CHUNK, 2 * GDN_HEADS, GDN_CHUNK), F32)],
        compiler_params=_params("arbitrary", "arbitrary"),
        name="gdn_prep",
    )(proj, proj, proj, conv_w, ab, alog_row, dtb_row, tril, triu, expand)


def _gdn_scan_body(q_ref, k_ref, v_ref, kt_ref, gh_ref, gct_ref, o_ref,
                   aqf, aqb, bsf, bsb, egf, egb, *, nch, nctx, batch):
    head = pl.program_id(1)
    cs = GDN_CHUNK
    ii = lax.broadcasted_iota(jnp.int32, (cs, cs), 0)
    jj = lax.broadcasted_iota(jnp.int32, (cs, cs), 1)
    masks = ((ii >= jj, ii > jj), (ii <= jj, ii < jj))
    same_block = {}
    width = GDN_SOLVE_BASE
    while width <= cs:
        shift = width.bit_length() - 1
        same_block[width] = jnp.right_shift(ii, shift) == jnp.right_shift(jj, shift)
        width *= 2
    hd = HEAD_DIM
    aqs, bss, egs = (aqf, aqb), (bsf, bsb), (egf, egb)

    def intra(first, count):
        chunks = [first + j for j in range(count)]
        r0s = [pl.multiple_of(c * cs, cs) for c in chunks]
        qs = [q_ref[pl.ds(r0, cs), :] for r0 in r0s]
        ks = [k_ref[pl.ds(r0, cs), :] for r0 in r0s]
        vs = [v_ref[pl.ds(r0, cs), :] for r0 in r0s]
        kts = [kt_ref[c] for c in chunks]
        raw_q = [_dot(q, kt) for q, kt in zip(qs, kts)]
        raw_k = [_dot(k, kt) for k, kt in zip(ks, kts)]
        pms, xs, atts, kps, qes, gends = [], [], [], [], [], []
        for j, c in enumerate(chunks):
            g4 = gh_ref[pl.ds(r0s[j], cs), :]
            qf, kf, vf = qs[j].astype(F32), ks[j].astype(F32), vs[j].astype(F32)
            ktf = kts[j].astype(F32)
            for d in (0, 1):
                incl, strict = masks[d]
                gcol = g4[:, d:d + 1]
                bcol = g4[:, 2 + d:3 + d]
                grow = gct_ref[c, pl.ds(d * GDN_HEADS + head, 1), :]
                decay = jnp.where(incl, jnp.exp(jnp.where(incl, gcol - grow, 0.0)), 0.0)
                eg = jnp.exp(gcol)
                gend = grow[:, cs - 1:cs] if d == 0 else grow[:, 0:1]
                pms.append(jnp.where(strict, -(raw_k[j] * bcol) * decay, 0.0))
                xs.append(jnp.concatenate([vf * bcol, kf * (bcol * eg)], axis=1))
                atts.append((raw_q[j] * decay).astype(BF16))
                kps.append((ktf * jnp.exp(gend - grow)).astype(BF16))
                qes.append(qf * eg)
                gends.append(gend)
        bfs = lambda vals: [a.astype(BF16) for a in vals]
        pds = bfs([jnp.where(same_block[GDN_SOLVE_BASE], pm, 0.0) for pm in pms])
        ns = [pd.astype(F32) for pd in pds]
        pw = pds
        span = 1
        while 2 * span < GDN_SOLVE_BASE:
            sq = [_dot(p, p) for p in pw]
            pw = bfs(sq)
            ns = [n + s + _dot(p, n.astype(BF16)) for n, s, p in zip(ns, sq, pw)]
            span *= 2
        width = GDN_SOLVE_BASE
        while width < cs:
            couple = same_block[2 * width] & ~same_block[width]
            qs_ = [jnp.where(couple, pm, 0.0) for pm in pms]
            nbs = bfs(ns)
            bs_ = [q + _dot(nb, q.astype(BF16)) for q, nb in zip(qs_, nbs)]
            ns = [n + b + _dot(b.astype(BF16), nb) for n, b, nb in zip(ns, bs_, nbs)]
            width *= 2
        xs = [x + _dot(n.astype(BF16), x.astype(BF16)) for n, x in zip(ns, xs)]
        xbs = [x.astype(BF16) for x in xs]
        aws = [_dot(att, xb) for att, xb in zip(atts, xbs)]
        kxs = [_dot(kp, xb) for kp, xb in zip(kps, xbs)]
        for j, c in enumerate(chunks):
            for d in (0, 1):
                n = 2 * j + d
                aqs[d][c, 0:hd, :] = (-kxs[n][:, hd:]).astype(BF16)
                aqs[d][c, hd:hd + cs, :] = (qes[n] - aws[n][:, hd:]).astype(BF16)
                bss[d][c] = kxs[n][:, :hd]
                egs[d][c] = jnp.broadcast_to(jnp.exp(gends[n]), (8, hd))
            o_ref[pl.ds(r0s[j], cs), :] = aws[2 * j][:, :hd] + aws[2 * j + 1][:, :hd]

    def inter(i, carry):
        cb = jnp.where(i < nctx, nctx - 1 - i, nch - 1 - (i - nctx))
        new = []
        for d, c in ((0, i), (1, cb)):
            s = carry[d]
            r0 = pl.multiple_of(c * cs, cs)
            r = _dot(aqs[d][c], s.astype(BF16))
            o_ref[pl.ds(r0, cs), :] += r[hd:]
            new.append(s * egs[d][c][0:1, :] + r[:hd] + bss[d][c])
        return tuple(new)

    full = nch // batch

    def intra_step(it, carry):
        intra(it * batch, batch)
        return carry

    lax.fori_loop(0, full, intra_step, 0)
    if nch > full * batch:
        intra(full * batch, nch - full * batch)
    zero = jnp.zeros((hd, hd), F32)
    lax.fori_loop(0, nch, inter, (zero, zero), unroll=2)


def _gdn_scan_call(qkvn, kt3, gh, gct3, nb, period, ctx):
    t = qkvn.shape[0]
    nch = period // GDN_CHUNK
    nctx = ctx // GDN_CHUNK
    hd = HEAD_DIM
    body = functools.partial(_gdn_scan_body, nch=nch, nctx=nctx, batch=min(nch, GDN_INTRA_BATCH))
    dirs2 = lambda shape, dt: [pltpu.VMEM(shape, dt), pltpu.VMEM(shape, dt)]
    return pl.pallas_call(
        body,
        grid=(nb, GDN_HEADS),
        in_specs=[pl.BlockSpec((period, hd), lambda b, h: (b, h)),
                  pl.BlockSpec((period, hd), lambda b, h: (b, GDN_HEADS + h)),
                  pl.BlockSpec((period, hd), lambda b, h: (b, 2 * GDN_HEADS + h)),
                  pl.BlockSpec((nch, hd, GDN_CHUNK), lambda b, h: (b, h, 0)),
                  pl.BlockSpec((period, hd), lambda b, h: (b, h)),
                  pl.BlockSpec((nch, 2 * GDN_HEADS, GDN_CHUNK), lambda b, h: (b, 0, 0))],
        out_specs=pl.BlockSpec((period, hd), lambda b, h: (b, h)),
        out_shape=jax.ShapeDtypeStruct((t, GDN_WIDTH), F32),
        scratch_shapes=(dirs2((nch, hd + GDN_CHUNK, hd), BF16) + dirs2((nch, hd, hd), F32)
                        + dirs2((nch, 8, hd), F32)),
        compiler_params=_params("arbitrary", "arbitrary"),
        name="gdn_scan",
    )(qkvn, qkvn, qkvn, kt3, gh, gct3)


def _s5_tables_body(lr_ref, li_ref, ldt_ref, btr_ref, bti_ref, cr_ref, ci_ref,
                    ar_ref, ai_ref, cmr_ref, cmi_ref, k_ref, ll_ref):
    lr = lr_ref[...]
    li = li_ref[...]
    dt = jnp.exp(ldt_ref[...])
    kk = lax.broadcasted_iota(jnp.int32, (24, 1), 0).astype(F32)
    mag = jnp.exp(kk * (lr * dt))
    ang = kk * (li * dt)
    er = mag * jnp.cos(ang)
    ei = mag * jnp.sin(ang)
    nr = er[1:2] - 1.0
    ni = ei[1:2]
    den = lr * lr + li * li
    cfr = (nr * lr + ni * li) / den
    cfi = (ni * lr - nr * li) / den
    btr = btr_ref[...]
    bti = bti_ref[...]
    bbr = btr * cfr - bti * cfi
    bbi = btr * cfi + bti * cfr
    cr = cr_ref[...]
    ci = ci_ref[...]
    a_r, a_i, cm_r, cm_i, c0_r, c0_i = [], [], [], [], [], []
    for k in range(S5_CHUNK):
        e_r, e_i = er[k:k + 1], ei[k:k + 1]
        a_r.append(bbr * e_r - bbi * e_i)
        a_i.append(bbr * e_i + bbi * e_r)
        c0_r.append(cr * e_r - ci * e_i)
        c0_i.append(cr * e_i + ci * e_r)
        f_r, f_i = er[k + 1:k + 2], ei[k + 1:k + 2]
        cm_r.append(cr * f_r - ci * f_i)
        cm_i.append(-(cr * f_i + ci * f_r))
    ar_ref[...] = jnp.concatenate(a_r, axis=0)
    ai_ref[...] = jnp.concatenate(a_i, axis=0)
    cmr_ref[...] = jnp.concatenate(cm_r, axis=0)
    cmi_ref[...] = jnp.concatenate(cm_i, axis=0)
    c0r = jnp.concatenate(c0_r, axis=0)
    c0i = jnp.concatenate(c0_i, axis=0)

    def dot3_nt(a, b):
        a0 = a.astype(BF16)
        a1 = (a - a0.astype(F32)).astype(BF16)
        b0 = b.astype(BF16)
        b1 = (b - b0.astype(F32)).astype(BF16)
        return _dot_nt(a0, b0) + _dot_nt(a0, b1) + _dot_nt(a1, b0)

    k_ref[...] = dot3_nt(c0r, bbr) - dot3_nt(c0i, bbi)
    ll_ref[...] = jnp.concatenate([er[S5_CHUNK:S5_CHUNK + 1], ei[S5_CHUNK:S5_CHUNK + 1]], axis=0)


def _s5_tables_call(lam_re, lam_im, log_dt, b_re, b_im, c_re, c_im):
    _, g, p = lam_re.shape
    cg = S5_GROUP
    rows = S5_CHUNK * cg
    row4 = lambda a: a.reshape(2, g, 1, p)
    spec = lambda r, c: pl.BlockSpec((None, None, r, c), lambda d, gi: (d, gi, 0, 0))
    ldt = jnp.broadcast_to(log_dt[:, :, None, None], (2, g, 1, p))
    return pl.pallas_call(
        _s5_tables_body,
        grid=(2, g),
        in_specs=[spec(1, p), spec(1, p), spec(1, p), spec(cg, p), spec(cg, p), spec(cg, p), spec(cg, p)],
        out_specs=[spec(rows, p), spec(rows, p), spec(rows, p), spec(rows, p), spec(rows, cg), spec(2, p)],
        out_shape=[jax.ShapeDtypeStruct((2, g, rows, p), F32)] * 4
                  + [jax.ShapeDtypeStruct((2, g, rows, cg), F32), jax.ShapeDtypeStruct((2, g, 2, p), F32)],
        compiler_params=_params("arbitrary", "arbitrary"),
        name="s5_tables",
    )(row4(lam_re), row4(lam_im), ldt, jnp.swapaxes(b_re, -1, -2), jnp.swapaxes(b_im, -1, -2), c_re, c_im)


def _s5_operators(tables):
    a_r, a_i, cm_r, cm_i, kk, ll = tables
    g = a_r.shape[1]
    n, cg, p = S5_CHUNK, S5_GROUP, S5_STATE

    def win(a):
        a = a.reshape(2, g, n, cg, p)
        return jnp.concatenate([a[0][:, ::-1].reshape(g, n * cg, p), a[1].reshape(g, n * cg, p)], -1).astype(BF16)

    def wout(c):
        c = c.reshape(2, g, n, cg, p)
        full = jnp.concatenate([c[0].reshape(g, n * cg, p), c[1][:, ::-1].reshape(g, n * cg, p)], -1)
        return jnp.swapaxes(full, 1, 2).astype(BF16)

    k5 = kk.reshape(2, g, n, cg, cg)
    s_i = jnp.arange(n)[:, None]
    t_i = jnp.arange(n)[None, :]

    def toeplitz(k4, lag, valid):
        blk = k4[:, jnp.clip(lag, 0, n - 1)]
        blk = jnp.where(valid[None, :, :, None, None], blk, 0.0)
        return blk.transpose(0, 1, 4, 2, 3).reshape(g, n * cg, n * cg).astype(BF16)

    ktf = toeplitz(k5[0], t_i - s_i, t_i >= s_i)
    ktb = toeplitz(k5[1], s_i - t_i, s_i >= t_i)
    gb = S5_GROUPS_PER_STEP
    lam_r = jnp.concatenate([ll[0, :, 0], ll[1, :, 0]], -1).reshape(g // gb, 1, gb * 2 * p)
    lam_i = jnp.concatenate([ll[0, :, 1], ll[1, :, 1]], -1).reshape(g // gb, 1, gb * 2 * p)
    return ktf, ktb, win(a_r), win(a_i), wout(cm_r), wout(cm_i), lam_r, lam_i


def _s5_body(u_ref, ktf_ref, ktb_ref, wir_ref, wii_ref, wor_ref, woi_ref, lr_ref, li_ref, y_ref,
             *scratch, nb, nch, nctx):
    gb = S5_GROUPS_PER_STEP
    sw = 2 * S5_STATE
    xr, xi, sfr, sfi, sbr, sbi = (scratch[k * gb:(k + 1) * gb] for k in range(6))
    for g in range(gb):
        ug = u_ref[:, g * S5_TILE:(g + 1) * S5_TILE].astype(BF16)
        xr[g][...] = _dot(ug, wir_ref[g])
        xi[g][...] = _dot(ug, wii_ref[g])
    is_f = lax.broadcasted_iota(jnp.int32, (1, sw), 1) < S5_STATE
    ar = [lr_ref[:, g * sw:(g + 1) * sw] for g in range(gb)]
    ai = [li_ref[:, g * sw:(g + 1) * sw] for g in range(gb)]

    def step(i, carry):
        cb = jnp.where(i < nctx, nctx - 1 - i, nch - 1 - (i - nctx))
        at_f = pl.ds(i, nb, stride=nch)
        at_b = pl.ds(cb, nb, stride=nch)
        new = []
        for g in range(gb):
            sr, si = carry[2 * g], carry[2 * g + 1]
            sfr[g][at_f, :] = sr
            sfi[g][at_f, :] = si
            sbr[g][at_b, :] = sr
            sbi[g][at_b, :] = si
            inr = jnp.where(is_f, xr[g][at_f, :], xr[g][at_b, :])
            ini = jnp.where(is_f, xi[g][at_f, :], xi[g][at_b, :])
            new.append(ar[g] * sr - ai[g] * si + inr)
            new.append(ar[g] * si + ai[g] * sr + ini)
        return tuple(new)

    zero = jnp.zeros((nb, sw), F32)
    lax.fori_loop(0, nch, step, (zero,) * (2 * gb))
    for g in range(gb):
        s_r = jnp.where(is_f, sfr[g][...], sbr[g][...]).astype(BF16)
        s_i = jnp.where(is_f, sfi[g][...], sbi[g][...]).astype(BF16)
        ug = u_ref[:, g * S5_TILE:(g + 1) * S5_TILE].astype(BF16)
        y_ref[:, g * S5_TILE:(g + 1) * S5_TILE] = (_dot(ug, ktf_ref[g]) + _dot(ug, ktb_ref[g])
                                                   + _dot(s_r, wor_ref[g]) + _dot(s_i, woi_ref[g]))


def _s5_call(u_blk, ops, nb, period, ctx):
    ktf, ktb, wir, wii, wor, woi, lam_r, lam_i = ops
    rows, width = u_blk.shape
    g = width // S5_TILE
    gb = S5_GROUPS_PER_STEP
    sw = 2 * S5_STATE
    nch = period // S5_CHUNK
    nctx = ctx // S5_CHUNK
    body = functools.partial(_s5_body, nb=nb, nch=nch, nctx=nctx)
    wspec = lambda r, c: pl.BlockSpec((gb, r, c), lambda i: (i, 0, 0))
    return pl.pallas_call(
        body,
        grid=(g // gb,),
        in_specs=[pl.BlockSpec((rows, gb * S5_TILE), lambda i: (0, i)),
                  wspec(S5_TILE, S5_TILE), wspec(S5_TILE, S5_TILE),
                  wspec(S5_TILE, sw), wspec(S5_TILE, sw), wspec(sw, S5_TILE), wspec(sw, S5_TILE),
                  pl.BlockSpec((None, 1, gb * sw), lambda i: (i, 0, 0)),
                  pl.BlockSpec((None, 1, gb * sw), lambda i: (i, 0, 0))],
        out_specs=pl.BlockSpec((rows, gb * S5_TILE), lambda i: (0, i)),
        out_shape=jax.ShapeDtypeStruct((rows, width), F32),
        scratch_shapes=[pltpu.VMEM((rows, sw), F32) for _ in range(6 * gb)],
        compiler_params=_params("arbitrary"),
        name="s5_scan",
    )(u_blk, ktf, ktb, wir, wii, wor, woi, lam_r, lam_i)


def _merge_body(o_ref, z_ref, u_ref, y_ref, h_ref, mb_ref, mc_ref, gn_ref, ds_ref, wglu_ref, bglu_ref,
                wout_ref, lng_ref, lnb_ref, out_ref, cat_ref, *, tm, tpb, ctx):
    rib = _rows_in_batch(pl.program_id(0), tm, tpb)
    gn = gn_ref[...]
    for hh in range(GDN_HEADS):
        sl = slice(hh * HEAD_DIM, (hh + 1) * HEAD_DIM)
        oh = o_ref[:, sl]
        r = lax.rsqrt(jnp.mean(oh * oh, -1, keepdims=True) + RMS_EPS)
        cat_ref[:, sl] = (oh * r * gn * _silu(z_ref[:, sl].astype(F32))).astype(BF16)
    s = y_ref[...] + ds_ref[...] * u_ref[...].astype(F32)
    s = 0.5 * s * (1.0 + jnp.tanh(math.sqrt(2.0 / math.pi) * (s + 0.044715 * (s * s * s))))
    s = s * jax.nn.sigmoid(_dot(s.astype(BF16), wglu_ref[...]) + bglu_ref[...])
    cat_ref[:, GDN_WIDTH:] = s.astype(BF16)
    mix = _dot(cat_ref[...], wout_ref[...])
    gate = jnp.where(rib < ctx, mc_ref[2:3, :], mb_ref[2:3, :])
    out_ref[...] = _layernorm(ALPHA * h_ref[...] + gate * mix, lng_ref[...], lnb_ref[...])


def _merge_call(o, proj, y, h, mod, gn, d_skip, w_glu, b_glu, w_out, ln_g, ln_b, nb, period, ctx):
    t, d = h.shape
    w = GDN_WIDTH
    sw = d - w
    tm = _pick(period, 272, 16)
    tpb = period // tm
    body = functools.partial(_merge_body, tm=tm, tpb=tpb, ctx=ctx)
    const = lambda r, c: pl.BlockSpec((r, c), lambda i: (0, 0))
    return pl.pallas_call(
        body,
        grid=(t // tm,),
        in_specs=[pl.BlockSpec((tm, w), lambda i: (i, 0)),
                  pl.BlockSpec((tm, w), lambda i: (i, 3)),
                  pl.BlockSpec((tm, sw), lambda i: (i, 4 * w // sw)),
                  pl.BlockSpec((tm, sw), lambda i: (i, 0)),
                  pl.BlockSpec((tm, d), lambda i: (i, 0)),
                  pl.BlockSpec((None, 6, d), lambda i: (i // tpb, 0, 0)),
                  pl.BlockSpec((None, 6, d), lambda i: (nb, 0, 0)),
                  const(1, HEAD_DIM), const(1, sw), const(sw, sw), const(1, sw), const(d, d),
                  const(1, d), const(1, d)],
        out_specs=pl.BlockSpec((tm, d), lambda i: (i, 0)),
        out_shape=jax.ShapeDtypeStruct((t, d), F32),
        scratch_shapes=[pltpu.VMEM((tm, d), BF16)],
        compiler_params=_params("arbitrary"),
        name="even_merge",
    )(o, proj, proj, y, h, mod, mod, gn, d_skip, w_glu, b_glu, w_out, ln_g, ln_b)


def _ffn_body(x_ref, xh_ref, mb_ref, mc_ref, wv_ref, wg_ref, cwv_ref, cwg_ref, cbv_ref, cbg_ref, wd_ref,
              lng_ref, lnb_ref, o_ref, xb_ref, *, tm, tpb, ctx, period):
    i = pl.program_id(0)
    j = pl.program_id(1)
    rib = _rows_in_batch(i, tm, tpb)
    n = tm + 32

    @pl.when(j == 0)
    def _():
        xb_ref[0:tm, :] = _modulate(x_ref[...], rib < ctx, mb_ref, mc_ref, 3, 4).astype(BF16)
        base = (i % tpb) * tm
        off = lax.broadcasted_iota(jnp.int32, (16, 1), 0)
        xh = xh_ref[...]
        xb_ref[tm:tm + 16, :] = _modulate(xh[0:16], base + tm + off < ctx, mb_ref, mc_ref, 3, 4).astype(BF16)
        xb_ref[tm + 16:n, :] = _modulate(xh[16:32], base - 16 + off < ctx, mb_ref, mc_ref, 3, 4).astype(BF16)
        o_ref[...] = jnp.zeros_like(o_ref)

    has_prev, has_next = _seq_edges(rib, ctx, period)
    xb = xb_ref[...]

    def conv(u, cw_ref, cb_ref, sl):
        cw = cw_ref[:, sl]
        up = pltpu.roll(u, 1, 0)[:tm]
        un = pltpu.roll(u, n - 1, 0)[:tm]
        return (jnp.where(has_prev, up, 0.0) * cw[0:1] + u[:tm] * cw[1:2]
                + jnp.where(has_next, un, 0.0) * cw[2:3] + cb_ref[:, sl])

    fc = wv_ref.shape[1]
    slabs = [slice(a, a + FFN_SLAB) for a in range(0, fc, FFN_SLAB)]
    ups = [(_dot(xb, wv_ref[:, sl]), _dot(xb, wg_ref[:, sl])) for sl in slabs]
    for sl, (uv, ug) in zip(slabs, ups):
        act = conv(uv, cwv_ref, cbv_ref, sl) * _silu(conv(ug, cwg_ref, cbg_ref, sl))
        o_ref[...] += _dot(act.astype(BF16), wd_ref[sl, :])

    @pl.when(j == pl.num_programs(1) - 1)
    def _():
        g = jnp.where(rib < ctx, mc_ref[5:6, :], mb_ref[5:6, :])
        o_ref[...] = _layernorm(ALPHA * x_ref[...] + g * o_ref[...], lng_ref[...], lnb_ref[...])


def _ffn_call(h, mod, w_up, conv_w, conv_b, w_down, ln_g, ln_b, nb, period, ctx):
    t, d = h.shape
    f = w_down.shape[0]
    tm = _pick(period, 544, 16)
    tpb = period // tm
    nt = t // tm
    fc = _pick(f, FFN_CHUNK, FFN_SLAB)
    nf = f // fc
    h3 = h.reshape(nt, tm, d)
    pad = jnp.zeros((1, 16, d), h.dtype)
    halo = jnp.concatenate([jnp.concatenate([h3[1:, :16], pad], 0),
                            jnp.concatenate([pad, h3[:-1, tm - 16:]], 0)], axis=1)
    body = functools.partial(_ffn_body, tm=tm, tpb=tpb, ctx=ctx, period=period)
    return pl.pallas_call(
        body,
        grid=(nt, nf),
        in_specs=[pl.BlockSpec((tm, d), lambda i, j: (i, 0)),
                  pl.BlockSpec((None, 32, d), lambda i, j: (i, 0, 0)),
                  pl.BlockSpec((None, 6, d), lambda i, j: (i // tpb, 0, 0)),
                  pl.BlockSpec((None, 6, d), lambda i, j: (nb, 0, 0)),
                  pl.BlockSpec((d, fc), lambda i, j: (0, j)),
                  pl.BlockSpec((d, fc), lambda i, j: (0, nf + j)),
                  pl.BlockSpec((3, fc), lambda i, j: (0, j)),
                  pl.BlockSpec((3, fc), lambda i, j: (0, nf + j)),
                  pl.BlockSpec((1, fc), lambda i, j: (0, j)),
                  pl.BlockSpec((1, fc), lambda i, j: (0, nf + j)),
                  pl.BlockSpec((fc, d), lambda i, j: (j, 0)),
                  pl.BlockSpec((1, d), lambda i, j: (0, 0)),
                  pl.BlockSpec((1, d), lambda i, j: (0, 0))],
        out_specs=pl.BlockSpec((tm, d), lambda i, j: (i, 0)),
        out_shape=jax.ShapeDtypeStruct((t, d), F32),
        scratch_shapes=[pltpu.VMEM((tm + 32, d), BF16)],
        compiler_params=_params("arbitrary", "arbitrary"),
        name="conv_ffn",
    )(h, halo, mod, mod, w_up, w_up, conv_w, conv_w, conv_b, conv_b, w_down, ln_g, ln_b)


def _qkv_body(x_ref, mb_ref, mc_ref, w_ref, cos_ref, sin_ref, qn_ref, kn_ref, o_ref, xb_ref,
              *, tm, tpb, ctx, nq_tiles):
    i = pl.program_id(0)
    j = pl.program_id(1)

    @pl.when(j == 0)
    def _():
        rib = _rows_in_batch(i, tm, tpb)
        xb_ref[...] = _modulate(x_ref[...], rib < ctx, mb_ref, mc_ref, 0, 1).astype(BF16)

    acc = _dot(xb_ref[...], w_ref[...])
    cos = cos_ref[...]
    sin = sin_ref[...]
    lane = lax.broadcasted_iota(jnp.int32, (1, HEAD_DIM), 1)
    first = (lane % 64) < 32
    heads = acc.shape[1] // HEAD_DIM

    def normrope(xh, wn, scale):
        xn = xh * lax.rsqrt(jnp.mean(xh * xh, -1, keepdims=True) + RMS_EPS) * wn
        partner = jnp.where(first, pltpu.roll(xn, HEAD_DIM - 32, 1), pltpu.roll(xn, 32, 1))
        return (xn * cos + partner * sin) * scale

    @pl.when(j < nq_tiles)
    def _():
        for hh in range(heads):
            sl = slice(hh * HEAD_DIM, (hh + 1) * HEAD_DIM)
            o_ref[:, sl] = normrope(acc[:, sl], qn_ref[...], ATT_SCALE).astype(BF16)

    @pl.when(j == nq_tiles)
    def _():
        for hh in range(heads):
            sl = slice(hh * HEAD_DIM, (hh + 1) * HEAD_DIM)
            if hh < ATT_KV_HEADS:
                o_ref[:, sl] = normrope(acc[:, sl], kn_ref[...], 1.0).astype(BF16)
            else:
                o_ref[:, sl] = acc[:, sl].astype(BF16)


def _qkv_call(h, mod, w_in, cos, sin, qn, kn, nb, period, ctx):
    t, d = h.shape
    n = w_in.shape[1]
    tn = 2 * ATT_KV_HEADS * HEAD_DIM
    nq_tiles = (n - tn) // tn
    tm = _pick(period, 544, 16)
    tpb = period // tm
    body = functools.partial(_qkv_body, tm=tm, tpb=tpb, ctx=ctx, nq_tiles=nq_tiles)
    return pl.pallas_call(
        body,
        grid=(t // tm, n // tn),
        in_specs=[pl.BlockSpec((tm, d), lambda i, j: (i, 0)),
                  pl.BlockSpec((None, 6, d), lambda i, j: (i // tpb, 0, 0)),
                  pl.BlockSpec((None, 6, d), lambda i, j: (nb, 0, 0)),
                  pl.BlockSpec((d, tn), lambda i, j: (0, j)),
                  pl.BlockSpec((tm, HEAD_DIM), lambda i, j: (i % tpb, 0)),
                  pl.BlockSpec((tm, HEAD_DIM), lambda i, j: (i % tpb, 0)),
                  pl.BlockSpec((1, HEAD_DIM), lambda i, j: (0, 0)),
                  pl.BlockSpec((1, HEAD_DIM), lambda i, j: (0, 0))],
        out_specs=pl.BlockSpec((tm, tn), lambda i, j: (i, j)),
        out_shape=jax.ShapeDtypeStruct((t, n), BF16),
        scratch_shapes=[pltpu.VMEM((tm, d), BF16)],
        compiler_params=_params("arbitrary", "arbitrary"),
        name="odd_qkv",
    )(h, mod, mod, w_in, cos, sin, qn, kn)


def _rope_tables(seq, ctx):
    rows = seq // GRID_W
    row = jnp.repeat(jnp.arange(rows, dtype=F32), GRID_W)
    col = jnp.tile(jnp.arange(GRID_W, dtype=F32), rows)
    half = HEAD_DIM // 4
    inv = ROPE_THETA ** (-jnp.arange(half, dtype=F32) / half)
    ar = row[:, None] * inv
    ac = col[:, None] * inv
    cos = jnp.concatenate([jnp.cos(ar), jnp.cos(ar), jnp.cos(ac), jnp.cos(ac)], -1)
    sin = jnp.concatenate([-jnp.sin(ar), jnp.sin(ar), -jnp.sin(ac), jnp.sin(ac)], -1)
    cos = jnp.concatenate([jnp.ones((ctx, HEAD_DIM), F32), cos], 0)
    sin = jnp.concatenate([jnp.zeros((ctx, HEAD_DIM), F32), sin], 0)
    return cos, sin


def _attn_body(q_ref, k_ref, v_ref, o_ref, *, tq):
    k = k_ref[...]
    v = v_ref[...]
    hd = HEAD_DIM
    stages = [tuple(range(g, g + 2)) for g in range(0, ATT_GROUP, 2)]
    scores = [_dot_nt(jnp.concatenate([q_ref[:, g * hd:(g + 1) * hd] for g in heads], axis=0), k)
              for heads in stages]
    for heads, s in zip(stages, scores):
        p = jnp.exp(s - jnp.max(s, -1, keepdims=True))
        out = _dot(p.astype(BF16), v) / jnp.sum(p, -1, keepdims=True)
        for n, g in enumerate(heads):
            o_ref[:, g * hd:(g + 1) * hd] = out[n * tq:(n + 1) * tq].astype(o_ref.dtype)


def _attn_call(qkv, nb, seq, ctx):
    period = seq + ctx
    tq = _pick(math.gcd(seq, ctx), ATT_Q_TILE, 16)
    gw = ATT_GROUP * HEAD_DIM
    qpb = seq // tq
    rpb = period // tq
    skip = ctx // tq
    kcol = ATT_HEADS
    vcol = ATT_HEADS + ATT_KV_HEADS
    body = functools.partial(_attn_body, tq=tq)
    return pl.pallas_call(
        body,
        grid=(nb, ATT_KV_HEADS, qpb),
        in_specs=[pl.BlockSpec((tq, gw), lambda b, kv, qi: (b * rpb + skip + qi, kv)),
                  pl.BlockSpec((period, HEAD_DIM), lambda b, kv, qi: (b, kcol + kv)),
                  pl.BlockSpec((period, HEAD_DIM), lambda b, kv, qi: (b, vcol + kv))],
        out_specs=pl.BlockSpec((tq, gw), lambda b, kv, qi: (b * qpb + qi, kv)),
        out_shape=jax.ShapeDtypeStruct((nb * seq, ATT_HEADS * HEAD_DIM), BF16),
        compiler_params=_params("arbitrary", "arbitrary", "arbitrary"),
        name="gqa_attention",
    )(qkv, qkv, qkv)


def _outproj_body(a_ref, h_ref, mb_ref, w_ref, lng_ref, lnb_ref, o_ref):
    mix = _dot(a_ref[...], w_ref[...])
    o_ref[...] = _layernorm(ALPHA * h_ref[...] + mb_ref[2:3, :] * mix, lng_ref[...], lnb_ref[...])


def _outproj_call(att, h, mod, w_out, ln_g, ln_b, nb, seq, ctx):
    d = h.shape[1]
    period = seq + ctx
    tm = _pick(math.gcd(seq, ctx), 256, 16)
    qpb = seq // tm
    rpb = period // tm
    skip = ctx // tm
    return pl.pallas_call(
        _outproj_body,
        grid=(nb * qpb,),
        in_specs=[pl.BlockSpec((tm, d), lambda i: (i, 0)),
                  pl.BlockSpec((tm, d), lambda i: ((i // qpb) * rpb + skip + i % qpb, 0)),
                  pl.BlockSpec((None, 6, d), lambda i: (i // qpb, 0, 0)),
                  pl.BlockSpec((d, d), lambda i: (0, 0)),
                  pl.BlockSpec((1, d), lambda i: (0, 0)),
                  pl.BlockSpec((1, d), lambda i: (0, 0))],
        out_specs=pl.BlockSpec((tm, d), lambda i: (i, 0)),
        out_shape=jax.ShapeDtypeStruct((nb * seq, d), F32),
        compiler_params=_params("arbitrary"),
        name="odd_outproj",
    )(att, h, mod, w_out, ln_g, ln_b)


def kernel(x, c, ctx, c_ctx, mod_w, mod_b, ln1_g, ln1_b, ln2_g, ln2_b, ffn_w_up, ffn_conv_w, ffn_conv_b, ffn_w_down, e_w_in, e_conv_qkv, e_a_log, e_dt_bias, e_gdn_norm, e_lam_re, e_lam_im, e_log_dt, e_b_re, e_b_im, e_c_re, e_c_im, e_d_skip, e_w_glu, e_b_glu, e_w_out, o_w_in, o_q_norm, o_k_norm, o_w_out):
    nb, seq, d = x.shape
    nctx = ctx.shape[1]
    period = nctx + seq
    t = nb * period
    w = GDN_WIDTH
    assert mod_w.shape[0] == DEPTH == 2 and nb < 8

    h = jnp.concatenate([ctx, x], axis=1).reshape(t, d)
    cs = jnp.zeros((8, d), F32).at[:nb].set(c).at[nb].set(c_ctx)
    mod = _mod_call(cs, mod_w, mod_b).reshape(DEPTH, 8, 6, d)
    row = lambda a: a.reshape(1, -1)

    w_in = e_w_in[0]
    gates_at = 4 * w
    w_main = jnp.concatenate([w_in[:, :gates_at], w_in[:, gates_at + 4 * GDN_HEADS:]], axis=1).astype(BF16)
    w_ab = jnp.pad(w_in[:, gates_at:gates_at + 4 * GDN_HEADS], ((0, 0), (0, 128 - 4 * GDN_HEADS))).astype(BF16)
    proj, ab, u32 = _inproj_even_call(h, mod[0], w_main, w_ab, 4 * w, nb, period, nctx)
    pad_row = lambda a: jnp.pad(a.reshape(1, -1), ((0, 0), (0, 128 - 2 * GDN_HEADS)))
    qkvn, kt3, gh, gct3 = _gdn_prep_call(proj, ab, e_conv_qkv[0], pad_row(e_a_log[0]), pad_row(e_dt_bias[0]),
                                         period, nctx)
    o = _gdn_scan_call(qkvn, kt3, gh, gct3, nb, period, nctx)

    ops = _s5_operators(_s5_tables_call(e_lam_re[0], e_lam_im[0], e_log_dt[0], e_b_re[0], e_b_im[0],
                                        e_c_re[0], e_c_im[0]))
    sgroups = (d - w) // S5_GROUP
    blk = lambda a: a.reshape(t // S5_CHUNK, S5_CHUNK, sgroups, S5_GROUP).transpose(0, 2, 1, 3)
    u_blk = blk(u32).reshape(t // S5_CHUNK, sgroups * S5_TILE)
    y_blk = _s5_call(u_blk, ops, nb, period, nctx)
    y = y_blk.reshape(t // S5_CHUNK, sgroups, S5_CHUNK, S5_GROUP).transpose(0, 2, 1, 3).reshape(t, d - w)

    h = _merge_call(o, proj, y, h, mod[0], row(e_gdn_norm[0]), row(e_d_skip[0]), e_w_glu[0].astype(BF16),
                    row(e_b_glu[0]), e_w_out[0].astype(BF16), row(ln1_g[0]), row(ln1_b[0]), nb, period, nctx)
    h = _ffn_call(h, mod[0], ffn_w_up[0].astype(BF16), ffn_conv_w[0], row(ffn_conv_b[0]),
                  ffn_w_down[0].astype(BF16), row(ln2_g[0]), row(ln2_b[0]), nb, period, nctx)

    cos, sin = _rope_tables(seq, nctx)
    qkv = _qkv_call(h, mod[1], o_w_in[0].astype(BF16), cos, sin, row(o_q_norm[0]), row(o_k_norm[0]),
                    nb, period, nctx)
    att = _attn_call(qkv, nb, seq, nctx)
    hl = _outproj_call(att, h, mod[1], o_w_out[0].astype(BF16), row(ln1_g[1]), row(ln1_b[1]), nb, seq, nctx)
    out = _ffn_call(hl, mod[1], ffn_w_up[1].astype(BF16), ffn_conv_w[1], row(ffn_conv_b[1]),
                    ffn_w_down[1].astype(BF16), row(ln2_g[1]), row(ln2_b[1]), nb, seq, 0)
    return out.reshape(nb, seq, d)
```

```python
import functools
import math

import jax
import jax.numpy as jnp
from jax import lax
from jax.experimental import pallas as pl
from jax.experimental.pallas import tpu as pltpu

F32 = jnp.float32
BF16 = jnp.bfloat16

DEPTH = 2
GDN_HEADS = 8
HEAD_DIM = 128
GDN_WIDTH = GDN_HEADS * HEAD_DIM
GDN_CHUNK = 64
GDN_INTRA_BATCH = 17
GDN_SOLVE_BASE = 8
S5_GROUP = 16
S5_STATE = 64
S5_CHUNK = 16
S5_TILE = S5_CHUNK * S5_GROUP
S5_GROUPS_PER_STEP = 4
ATT_HEADS = 16
ATT_KV_HEADS = 4
ATT_GROUP = ATT_HEADS // ATT_KV_HEADS
ATT_SCALE = HEAD_DIM ** -0.5
ATT_Q_TILE = 256
ATT_KEY_BLOCKS = 4
MXU_TILE = 256
GRID_W = 64
ROPE_THETA = 10000.0
FFN_CHUNK = 512
FFN_SLAB = 256
ALPHA = (2 * DEPTH) ** 0.25
LN_EPS = 1e-6
RMS_EPS = 1e-6
L2_EPS = 1e-6
VMEM_LIMIT_BYTES = 56 * 1024 * 1024


def _pick(n, target, mult):
    best = None
    for d in range(mult, min(n, target) + 1, mult):
        if n % d == 0:
            best = d
    assert best is not None, (n, target, mult)
    return best


def _params(*sem):
    return pltpu.CompilerParams(dimension_semantics=sem, vmem_limit_bytes=VMEM_LIMIT_BYTES)


def _dot(a, b):
    return jnp.dot(a, b, preferred_element_type=F32)


def _dot_nt(a, b):
    return lax.dot_general(a, b, (((1,), (1,)), ((), ())), preferred_element_type=F32)


def _split3(a):
    p0 = a.astype(BF16)
    r = a - p0.astype(F32)
    p1 = r.astype(BF16)
    p2 = (r - p1.astype(F32)).astype(BF16)
    return p0, p1, p2


def _rows_in_batch(i, tm, tiles_per_batch):
    return (i % tiles_per_batch) * tm + lax.broadcasted_iota(jnp.int32, (tm, 1), 0)


def _modulate(x, is_ctx, mb_ref, mc_ref, shift_i, scale_i):
    scale = jnp.where(is_ctx, mc_ref[scale_i:scale_i + 1, :], mb_ref[scale_i:scale_i + 1, :])
    shift = jnp.where(is_ctx, mc_ref[shift_i:shift_i + 1, :], mb_ref[shift_i:shift_i + 1, :])
    return x * (1.0 + scale) + shift


def _layernorm(r, g, b):
    xc = r - jnp.mean(r, -1, keepdims=True)
    var = jnp.mean(xc * xc, -1, keepdims=True)
    return xc * lax.rsqrt(var + LN_EPS) * g + b


def _silu(x):
    return x * jax.nn.sigmoid(x)


def _seq_edges(rib, ctx, period):
    has_prev = (rib != 0) & (rib != ctx)
    has_next = (rib != ctx - 1) & (rib != period - 1)
    return has_prev, has_next


def _mod_body(c_ref, w_ref, b_ref, o_ref):
    s = _silu(c_ref[...])
    o_ref[...] = _dot(s.astype(BF16), w_ref[...].astype(BF16)) + b_ref[...]


def _mod_call(cs, mod_w, mod_b):
    depth, d, n = mod_w.shape
    tn = _pick(n, 1024, 128)
    return pl.pallas_call(
        _mod_body,
        grid=(depth, n // tn),
        in_specs=[pl.BlockSpec((8, d), lambda l, j: (0, 0)),
                  pl.BlockSpec((None, d, tn), lambda l, j: (l, 0, j)),
                  pl.BlockSpec((None, 1, tn), lambda l, j: (l, 0, j))],
        out_specs=pl.BlockSpec((None, 8, tn), lambda l, j: (l, 0, j)),
        out_shape=jax.ShapeDtypeStruct((depth, 8, n), F32),
        compiler_params=_params("arbitrary", "arbitrary"),
        name="adaln_mod",
    )(cs, mod_w, mod_b.reshape(depth, 1, n))


def _inproj_even_body(x_ref, mb_ref, mc_ref, w_ref, wab_ref, o_ref, ab_ref, xb_ref, *, tm, tpb, ctx):
    i = pl.program_id(0)

    @pl.when(pl.program_id(1) == 0)
    def _():
        rib = _rows_in_batch(i, tm, tpb)
        xb = _modulate(x_ref[...], rib < ctx, mb_ref, mc_ref, 0, 1).astype(BF16)
        xb_ref[...] = xb
        ab_ref[...] = _dot(xb, wab_ref[...])

    o_ref[...] = _dot(xb_ref[...], w_ref[...]).astype(o_ref.dtype)


def _inproj_even_call(h, mod, w_main, w_ab, nb, period, ctx):
    t, d = h.shape
    n = w_main.shape[1]
    tm = _pick(period, 544, 16)
    tn = _pick(n, 1280, 256)
    tpb = period // tm
    body = functools.partial(_inproj_even_body, tm=tm, tpb=tpb, ctx=ctx)
    return pl.pallas_call(
        body,
        grid=(t // tm, n // tn),
        in_specs=[pl.BlockSpec((tm, d), lambda i, j: (i, 0)),
                  pl.BlockSpec((None, 6, d), lambda i, j: (i // tpb, 0, 0)),
                  pl.BlockSpec((None, 6, d), lambda i, j: (nb, 0, 0)),
                  pl.BlockSpec((d, tn), lambda i, j: (0, j)),
                  pl.BlockSpec((d, 128), lambda i, j: (0, 0))],
        out_specs=[pl.BlockSpec((tm, tn), lambda i, j: (i, j)),
                   pl.BlockSpec((tm, 128), lambda i, j: (i, 0))],
        out_shape=[jax.ShapeDtypeStruct((t, n), BF16), jax.ShapeDtypeStruct((t, 128), F32)],
        scratch_shapes=[pltpu.VMEM((tm, d), BF16)],
        compiler_params=_params("arbitrary", "arbitrary"),
        name="even_inproj",
    )(h, mod, mod, w_main, w_ab)


def _gdn_prep_body(x_ref, xn_ref, xp_ref, cw_ref, ab_ref, alog_ref, dtb_ref, tril_ref, triu_ref, e_ref,
                   qkv_ref, kt_ref, gh_ref, gct_ref, *, tr, tpb, ctx, period):
    i = pl.program_id(0)
    sec = pl.program_id(1)
    rib = _rows_in_batch(i, tr, tpb)
    has_prev, has_next = _seq_edges(rib, ctx, period)
    ext = jnp.concatenate([x_ref[...].astype(F32), xn_ref[...].astype(F32), xp_ref[...].astype(F32)], axis=0)
    n = tr + 32
    xprev = pltpu.roll(ext, 1, 0)[:tr]
    xnext = pltpu.roll(ext, n - 1, 0)[:tr]
    cw = cw_ref[...]
    y = (jnp.where(has_prev, xprev, 0.0) * cw[0:1] + ext[:tr] * cw[1:2]
         + jnp.where(has_next, xnext, 0.0) * cw[2:3])
    y = _silu(y)

    def l2n(scale):
        parts = []
        for hh in range(GDN_HEADS):
            yh = y[:, hh * HEAD_DIM:(hh + 1) * HEAD_DIM]
            parts.append(yh * (lax.rsqrt(jnp.sum(yh * yh, -1, keepdims=True) + L2_EPS) * scale))
        return jnp.concatenate(parts, axis=1)

    @pl.when(sec == 0)
    def _():
        qkv_ref[...] = l2n(HEAD_DIM ** -0.5).astype(BF16)
        ab = ab_ref[...]
        lane = lax.broadcasted_iota(jnp.int32, (1, 128), 1)
        xg = ab + dtb_ref[...]
        softplus = jnp.maximum(xg, 0.0) + jnp.log1p(jnp.exp(-jnp.abs(xg)))
        g = -jnp.exp(alog_ref[...]) * softplus
        g0, g1, g2 = _split3(g)
        tril = tril_ref[...]
        triu = triu_ref[...]
        pre = _dot(tril, g0) + _dot(tril, g1) + _dot(tril, g2)
        suf = _dot(triu, g0) + _dot(triu, g1) + _dot(triu, g2)
        gc = jnp.where(lane < GDN_HEADS, pre, suf)
        vals = jnp.where(lane < 2 * GDN_HEADS, gc, jax.nn.sigmoid(ab))
        v0, v1, v2 = _split3(vals)
        e = e_ref[...]
        gh_ref[...] = _dot(v0, e) + _dot(v1, e) + _dot(v2, e)
        gct = gc.T
        for c in range(tr // GDN_CHUNK):
            gct_ref[c] = gct[:2 * GDN_HEADS, c * GDN_CHUNK:(c + 1) * GDN_CHUNK]

    @pl.when(sec == 1)
    def _():
        kn = l2n(1.0)
        qkv_ref[...] = kn.astype(BF16)
        knt = kn.T
        for c in range(tr // GDN_CHUNK):
            kt_ref[c] = knt[:, c * GDN_CHUNK:(c + 1) * GDN_CHUNK].astype(BF16)

    @pl.when(sec == 2)
    def _():
        qkv_ref[...] = y.astype(BF16)


def _gdn_prep_call(proj, ab, conv_w, alog_row, dtb_row, period, ctx):
    t = proj.shape[0]
    w = GDN_WIDTH
    tr = _pick(period, 256, GDN_CHUNK)
    tpb = period // tr
    hb = tr // 16
    nh = t // 16
    nck = tr // GDN_CHUNK
    r = jnp.arange(tr)
    same = (r[:, None] // GDN_CHUNK) == (r[None, :] // GDN_CHUNK)
    tril = (same & (r[:, None] >= r[None, :])).astype(BF16)
    triu = (same & (r[:, None] <= r[None, :])).astype(BF16)
    src = jnp.arange(128)[:, None]
    dst = jnp.arange(w)[None, :]
    expand = ((dst % HEAD_DIM < 4) & (src == (dst % HEAD_DIM) * GDN_HEADS + dst // HEAD_DIM)).astype(BF16)
    body = functools.partial(_gdn_prep_body, tr=tr, tpb=tpb, ctx=ctx, period=period)
    return pl.pallas_call(
        body,
        grid=(t // tr, 3),
        in_specs=[pl.BlockSpec((tr, w), lambda i, s: (i, s)),
                  pl.BlockSpec((16, w), lambda i, s: (jnp.minimum((i + 1) * hb, nh - 1), s)),
                  pl.BlockSpec((16, w), lambda i, s: (jnp.maximum(i * hb - 1, 0), s)),
                  pl.BlockSpec((3, w), lambda i, s: (0, s)),
                  pl.BlockSpec((tr, 128), lambda i, s: (i, 0)),
                  pl.BlockSpec((1, 128), lambda i, s: (0, 0)),
                  pl.BlockSpec((1, 128), lambda i, s: (0, 0)),
                  pl.BlockSpec((tr, tr), lambda i, s: (0, 0)),
                  pl.BlockSpec((tr, tr), lambda i, s: (0, 0)),
                  pl.BlockSpec((128, w), lambda i, s: (0, 0))],
        out_specs=[pl.BlockSpec((tr, w), lambda i, s: (i, s)),
                   pl.BlockSpec((nck, w, GDN_CHUNK), lambda i, s: (i, 0, 0)),
                   pl.BlockSpec((tr, w), lambda i, s: (i, 0)),
                   pl.BlockSpec((nck, 2 * GDN_HEADS, GDN_CHUNK), lambda i, s: (i, 0, 0))],
        out_shape=[jax.ShapeDtypeStruct((t, 3 * w), BF16),
                   jax.ShapeDtypeStruct((t // GDN_CHUNK, w, GDN_CHUNK), BF16),
                   jax.ShapeDtypeStruct((t, w), F32),
                   jax.ShapeDtypeStruct((t // GDN_CHUNK, 2 * GDN_HEADS, GDN_CHUNK), F32)],
        compiler_params=_params("arbitrary", "arbitrary"),
        name="gdn_prep",
    )(proj, proj, proj, conv_w, ab, alog_row, dtb_row, tril, triu, expand)


def _gdn_scan_body(q_ref, k_ref, v_ref, kt_ref, gh_ref, gct_ref, o_ref,
                   aqf, aqb, bsf, bsb, egf, egb, *, nch, nctx, batch):
    head = pl.program_id(1)
    cs = GDN_CHUNK
    ii = lax.broadcasted_iota(jnp.int32, (cs, cs), 0)
    jj = lax.broadcasted_iota(jnp.int32, (cs, cs), 1)
    masks = ((ii >= jj, ii > jj), (ii <= jj, ii < jj))
    same_block = {}
    width = GDN_SOLVE_BASE
    while width <= cs:
        shift = width.bit_length() - 1
        same_block[width] = jnp.right_shift(ii, shift) == jnp.right_shift(jj, shift)
        width *= 2
    hd = HEAD_DIM
    aqs, bss, egs = (aqf, aqb), (bsf, bsb), (egf, egb)

    def intra(first, count):
        chunks = [first + j for j in range(count)]
        r0s = [pl.multiple_of(c * cs, cs) for c in chunks]
        qs = [q_ref[pl.ds(r0, cs), :] for r0 in r0s]
        ks = [k_ref[pl.ds(r0, cs), :] for r0 in r0s]
        vs = [v_ref[pl.ds(r0, cs), :] for r0 in r0s]
        kts = [kt_ref[c] for c in chunks]
        raw_q = [_dot(q, kt) for q, kt in zip(qs, kts)]
        raw_k = [_dot(k, kt) for k, kt in zip(ks, kts)]
        pms, xs, atts, kps, qes, gends = [], [], [], [], [], []
        for j, c in enumerate(chunks):
            g4 = gh_ref[pl.ds(r0s[j], cs), :]
            qf, kf, vf = qs[j].astype(F32), ks[j].astype(F32), vs[j].astype(F32)
            ktf = kts[j].astype(F32)
            for d in (0, 1):
                incl, strict = masks[d]
                gcol = g4[:, d:d + 1]
                bcol = g4[:, 2 + d:3 + d]
                grow = gct_ref[c, pl.ds(d * GDN_HEADS + head, 1), :]
                decay = jnp.where(incl, jnp.exp(jnp.where(incl, gcol - grow, 0.0)), 0.0)
                eg = jnp.exp(gcol)
                gend = grow[:, cs - 1:cs] if d == 0 else grow[:, 0:1]
                pms.append(jnp.where(strict, -(raw_k[j] * bcol) * decay, 0.0))
                xs.append(jnp.concatenate([vf * bcol, kf * (bcol * eg)], axis=1))
                atts.append((raw_q[j] * decay).astype(BF16))
                kps.append((ktf * jnp.exp(gend - grow)).astype(BF16))
                qes.append(qf * eg)
                gends.append(gend)
        bfs = lambda vals: [a.astype(BF16) for a in vals]
        pds = bfs([jnp.where(same_block[GDN_SOLVE_BASE], pm, 0.0) for pm in pms])
        ns = [pd.astype(F32) for pd in pds]
        pw = pds
        span = 1
        while 2 * span < GDN_SOLVE_BASE:
            sq = [_dot(p, p) for p in pw]
            pw = bfs(sq)
            ns = [n + s + _dot(p, n.astype(BF16)) for n, s, p in zip(ns, sq, pw)]
            span *= 2
        width = GDN_SOLVE_BASE
        while width < cs:
            couple = same_block[2 * width] & ~same_block[width]
            qs_ = [jnp.where(couple, pm, 0.0) for pm in pms]
            nbs = bfs(ns)
            bs_ = [q + _dot(nb, q.astype(BF16)) for q, nb in zip(qs_, nbs)]
            ns = [n + b + _dot(b.astype(BF16), nb) for n, b, nb in zip(ns, bs_, nbs)]
            width *= 2
        xs = [x + _dot(n.astype(BF16), x.astype(BF16)) for n, x in zip(ns, xs)]
        xbs = [x.astype(BF16) for x in xs]
        aws = [_dot(att, xb) for att, xb in zip(atts, xbs)]
        kxs = [_dot(kp, xb) for kp, xb in zip(kps, xbs)]
        for j, c in enumerate(chunks):
            for d in (0, 1):
                n = 2 * j + d
                aqs[d][c, 0:hd, :] = (-kxs[n][:, hd:]).astype(BF16)
                aqs[d][c, hd:hd + cs, :] = (qes[n] - aws[n][:, hd:]).astype(BF16)
                bss[d][c] = kxs[n][:, :hd]
                egs[d][c] = jnp.broadcast_to(jnp.exp(gends[n]), (8, hd))
            o_ref[pl.ds(r0s[j], cs), :] = aws[2 * j][:, :hd] + aws[2 * j + 1][:, :hd]

    def inter(i, carry):
        cb = jnp.where(i < nctx, nctx - 1 - i, nch - 1 - (i - nctx))
        new = []
        for d, c in ((0, i), (1, cb)):
            s = carry[d]
            r0 = pl.multiple_of(c * cs, cs)
            r = _dot(aqs[d][c], s.astype(BF16))
            o_ref[pl.ds(r0, cs), :] += r[hd:]
            new.append(s * egs[d][c][0:1, :] + r[:hd] + bss[d][c])
        return tuple(new)

    full = nch // batch

    def intra_step(it, carry):
        intra(it * batch, batch)
        return carry

    lax.fori_loop(0, full, intra_step, 0)
    if nch > full * batch:
        intra(full * batch, nch - full * batch)
    zero = jnp.zeros((hd, hd), F32)
    lax.fori_loop(0, nch, inter, (zero, zero), unroll=2)


def _gdn_scan_call(qkvn, kt3, gh, gct3, nb, period, ctx):
    t = qkvn.shape[0]
    nch = period // GDN_CHUNK
    nctx = ctx // GDN_CHUNK
    hd = HEAD_DIM
    body = functools.partial(_gdn_scan_body, nch=nch, nctx=nctx, batch=min(nch, GDN_INTRA_BATCH))
    dirs2 = lambda shape, dt: [pltpu.VMEM(shape, dt), pltpu.VMEM(shape, dt)]
    return pl.pallas_call(
        body,
        grid=(nb, GDN_HEADS),
        in_specs=[pl.BlockSpec((period, hd), lambda b, h: (b, h)),
                  pl.BlockSpec((period, hd), lambda b, h: (b, GDN_HEADS + h)),
                  pl.BlockSpec((period, hd), lambda b, h: (b, 2 * GDN_HEADS + h)),
                  pl.BlockSpec((nch, hd, GDN_CHUNK), lambda b, h: (b, h, 0)),
                  pl.BlockSpec((period, hd), lambda b, h: (b, h)),
                  pl.BlockSpec((nch, 2 * GDN_HEADS, GDN_CHUNK), lambda b, h: (b, 0, 0))],
        out_specs=pl.BlockSpec((period, hd), lambda b, h: (b, h)),
        out_shape=jax.ShapeDtypeStruct((t, GDN_WIDTH), F32),
        scratch_shapes=(dirs2((nch, hd + GDN_CHUNK, hd), BF16) + dirs2((nch, hd, hd), F32)
                        + dirs2((nch, 8, hd), F32)),
        compiler_params=_params("arbitrary", "arbitrary"),
        name="gdn_scan",
    )(qkvn, qkvn, qkvn, kt3, gh, gct3)


def _s5_tables_body(lr_ref, li_ref, ldt_ref, btr_ref, bti_ref, cr_ref, ci_ref,
                    ar_ref, ai_ref, cmr_ref, cmi_ref, k_ref, ll_ref):
    lr = lr_ref[...]
    li = li_ref[...]
    dt = jnp.exp(ldt_ref[...])
    kk = lax.broadcasted_iota(jnp.int32, (24, 1), 0).astype(F32)
    mag = jnp.exp(kk * (lr * dt))
    ang = kk * (li * dt)
    er = mag * jnp.cos(ang)
    ei = mag * jnp.sin(ang)
    nr = er[1:2] - 1.0
    ni = ei[1:2]
    den = lr * lr + li * li
    cfr = (nr * lr + ni * li) / den
    cfi = (ni * lr - nr * li) / den
    btr = btr_ref[...]
    bti = bti_ref[...]
    bbr = btr * cfr - bti * cfi
    bbi = btr * cfi + bti * cfr
    cr = cr_ref[...]
    ci = ci_ref[...]
    a_r, a_i, cm_r, cm_i, c0_r, c0_i = [], [], [], [], [], []
    for k in range(S5_CHUNK):
        e_r, e_i = er[k:k + 1], ei[k:k + 1]
        a_r.append(bbr * e_r - bbi * e_i)
        a_i.append(bbr * e_i + bbi * e_r)
        c0_r.append(cr * e_r - ci * e_i)
        c0_i.append(cr * e_i + ci * e_r)
        f_r, f_i = er[k + 1:k + 2], ei[k + 1:k + 2]
        cm_r.append(cr * f_r - ci * f_i)
        cm_i.append(-(cr * f_i + ci * f_r))
    ar_ref[...] = jnp.concatenate(a_r, axis=0)
    ai_ref[...] = jnp.concatenate(a_i, axis=0)
    cmr_ref[...] = jnp.concatenate(cm_r, axis=0)
    cmi_ref[...] = jnp.concatenate(cm_i, axis=0)
    c0r = jnp.concatenate(c0_r, axis=0)
    c0i = jnp.concatenate(c0_i, axis=0)

    def dot3_nt(a, b):
        a0 = a.astype(BF16)
        a1 = (a - a0.astype(F32)).astype(BF16)
        b0 = b.astype(BF16)
        b1 = (b - b0.astype(F32)).astype(BF16)
        return _dot_nt(a0, b0) + _dot_nt(a0, b1) + _dot_nt(a1, b0)

    k_ref[...] = dot3_nt(c0r, bbr) - dot3_nt(c0i, bbi)
    ll_ref[...] = jnp.concatenate([er[S5_CHUNK:S5_CHUNK + 1], ei[S5_CHUNK:S5_CHUNK + 1]], axis=0)


def _s5_tables_call(lam_re, lam_im, log_dt, b_re, b_im, c_re, c_im):
    _, g, p = lam_re.shape
    cg = S5_GROUP
    rows = S5_CHUNK * cg
    row4 = lambda a: a.reshape(2, g, 1, p)
    spec = lambda r, c: pl.BlockSpec((None, None, r, c), lambda d, gi: (d, gi, 0, 0))
    ldt = jnp.broadcast_to(log_dt[:, :, None, None], (2, g, 1, p))
    return pl.pallas_call(
        _s5_tables_body,
        grid=(2, g),
        in_specs=[spec(1, p), spec(1, p), spec(1, p), spec(cg, p), spec(cg, p), spec(cg, p), spec(cg, p)],
        out_specs=[spec(rows, p), spec(rows, p), spec(rows, p), spec(rows, p), spec(rows, cg), spec(2, p)],
        out_shape=[jax.ShapeDtypeStruct((2, g, rows, p), F32)] * 4
                  + [jax.ShapeDtypeStruct((2, g, rows, cg), F32), jax.ShapeDtypeStruct((2, g, 2, p), F32)],
        compiler_params=_params("arbitrary", "arbitrary"),
        name="s5_tables",
    )(row4(lam_re), row4(lam_im), ldt, jnp.swapaxes(b_re, -1, -2), jnp.swapaxes(b_im, -1, -2), c_re, c_im)


def _s5_operators(tables):
    a_r, a_i, cm_r, cm_i, kk, ll = tables
    g = a_r.shape[1]
    n, cg, p = S5_CHUNK, S5_GROUP, S5_STATE

    def win(a):
        a = a.reshape(2, g, n, cg, p)
        return jnp.concatenate([a[0][:, ::-1].reshape(g, n * cg, p), a[1].reshape(g, n * cg, p)], -1).astype(BF16)

    def wout(c):
        c = c.reshape(2, g, n, cg, p)
        full = jnp.concatenate([c[0].reshape(g, n * cg, p), c[1][:, ::-1].reshape(g, n * cg, p)], -1)
        return jnp.swapaxes(full, 1, 2).astype(BF16)

    k5 = kk.reshape(2, g, n, cg, cg)
    s_i = jnp.arange(n)[:, None]
    t_i = jnp.arange(n)[None, :]

    def toeplitz(k4, lag, valid):
        blk = k4[:, jnp.clip(lag, 0, n - 1)]
        blk = jnp.where(valid[None, :, :, None, None], blk, 0.0)
        return blk.transpose(0, 1, 4, 2, 3).reshape(g, n * cg, n * cg).astype(BF16)

    ktf = toeplitz(k5[0], t_i - s_i, t_i >= s_i)
    ktb = toeplitz(k5[1], s_i - t_i, s_i >= t_i)
    gb = S5_GROUPS_PER_STEP
    lam_r = jnp.concatenate([ll[0, :, 0], ll[1, :, 0]], -1).reshape(g // gb, 1, gb * 2 * p)
    lam_i = jnp.concatenate([ll[0, :, 1], ll[1, :, 1]], -1).reshape(g // gb, 1, gb * 2 * p)
    return ktf, ktb, win(a_r), win(a_i), wout(cm_r), wout(cm_i), lam_r, lam_i


def _s5_body(u_ref, ktf_ref, ktb_ref, wir_ref, wii_ref, wor_ref, woi_ref, lr_ref, li_ref, y_ref,
             *scratch, nb, nch, nctx):
    gb = S5_GROUPS_PER_STEP
    sw = 2 * S5_STATE
    xr, xi, sfr, sfi, sbr, sbi = (scratch[k * gb:(k + 1) * gb] for k in range(6))
    for g in range(gb):
        ug = u_ref[:, g * S5_TILE:(g + 1) * S5_TILE]
        xr[g][...] = _dot(ug, wir_ref[g])
        xi[g][...] = _dot(ug, wii_ref[g])
    is_f = lax.broadcasted_iota(jnp.int32, (1, sw), 1) < S5_STATE
    ar = [lr_ref[:, g * sw:(g + 1) * sw] for g in range(gb)]
    ai = [li_ref[:, g * sw:(g + 1) * sw] for g in range(gb)]

    def step(i, carry):
        cb = jnp.where(i < nctx, nctx - 1 - i, nch - 1 - (i - nctx))
        at_f = pl.ds(i, nb, stride=nch)
        at_b = pl.ds(cb, nb, stride=nch)
        new = []
        for g in range(gb):
            sr, si = carry[2 * g], carry[2 * g + 1]
            sfr[g][at_f, :] = sr
            sfi[g][at_f, :] = si
            sbr[g][at_b, :] = sr
            sbi[g][at_b, :] = si
            inr = jnp.where(is_f, xr[g][at_f, :], xr[g][at_b, :])
            ini = jnp.where(is_f, xi[g][at_f, :], xi[g][at_b, :])
            new.append(ar[g] * sr - ai[g] * si + inr)
            new.append(ar[g] * si + ai[g] * sr + ini)
        return tuple(new)

    zero = jnp.zeros((nb, sw), F32)
    lax.fori_loop(0, nch, step, (zero,) * (2 * gb))
    for g in range(gb):
        s_r = jnp.where(is_f, sfr[g][...], sbr[g][...]).astype(BF16)
        s_i = jnp.where(is_f, sfi[g][...], sbi[g][...]).astype(BF16)
        ug = u_ref[:, g * S5_TILE:(g + 1) * S5_TILE]
        y = _dot(ug, ktf_ref[g]) + _dot(ug, ktb_ref[g]) + _dot(s_r, wor_ref[g]) + _dot(s_i, woi_ref[g])
        y_ref[:, g * S5_TILE:(g + 1) * S5_TILE] = y.astype(y_ref.dtype)


def _s5_call(u_blk, ops, nb, period, ctx):
    ktf, ktb, wir, wii, wor, woi, lam_r, lam_i = ops
    rows, width = u_blk.shape
    g = width // S5_TILE
    gb = S5_GROUPS_PER_STEP
    sw = 2 * S5_STATE
    nch = period // S5_CHUNK
    nctx = ctx // S5_CHUNK
    body = functools.partial(_s5_body, nb=nb, nch=nch, nctx=nctx)
    wspec = lambda r, c: pl.BlockSpec((gb, r, c), lambda i: (i, 0, 0))
    return pl.pallas_call(
        body,
        grid=(g // gb,),
        in_specs=[pl.BlockSpec((rows, gb * S5_TILE), lambda i: (0, i)),
                  wspec(S5_TILE, S5_TILE), wspec(S5_TILE, S5_TILE),
                  wspec(S5_TILE, sw), wspec(S5_TILE, sw), wspec(sw, S5_TILE), wspec(sw, S5_TILE),
                  pl.BlockSpec((None, 1, gb * sw), lambda i: (i, 0, 0)),
                  pl.BlockSpec((None, 1, gb * sw), lambda i: (i, 0, 0))],
        out_specs=pl.BlockSpec((rows, gb * S5_TILE), lambda i: (0, i)),
        out_shape=jax.ShapeDtypeStruct((rows, width), BF16),
        scratch_shapes=[pltpu.VMEM((rows, sw), F32) for _ in range(6 * gb)],
        compiler_params=_params("arbitrary"),
        name="s5_scan",
    )(u_blk, ktf, ktb, wir, wii, wor, woi, lam_r, lam_i)


def _merge_body(o_ref, z_ref, u_ref, y_ref, h_ref, mb_ref, mc_ref, gn_ref, ds_ref, wglu_ref, bglu_ref,
                wout_ref, lng_ref, lnb_ref, out_ref, cat_ref, *, tm, tpb, ctx):
    rib = _rows_in_batch(pl.program_id(0), tm, tpb)
    gn = gn_ref[...]
    for hh in range(GDN_HEADS):
        sl = slice(hh * HEAD_DIM, (hh + 1) * HEAD_DIM)
        oh = o_ref[:, sl]
        r = lax.rsqrt(jnp.mean(oh * oh, -1, keepdims=True) + RMS_EPS)
        cat_ref[:, sl] = (oh * r * gn * _silu(z_ref[:, sl].astype(F32))).astype(BF16)
    s = y_ref[...].astype(F32) + ds_ref[...] * u_ref[...].astype(F32)
    s = 0.5 * s * (1.0 + jnp.tanh(math.sqrt(2.0 / math.pi) * (s + 0.044715 * (s * s * s))))
    s = s * jax.nn.sigmoid(_dot(s.astype(BF16), wglu_ref[...]) + bglu_ref[...])
    cat_ref[:, GDN_WIDTH:] = s.astype(BF16)
    mix = _dot(cat_ref[...], wout_ref[...])
    gate = jnp.where(rib < ctx, mc_ref[2:3, :], mb_ref[2:3, :])
    out_ref[...] = _layernorm(ALPHA * h_ref[...] + gate * mix, lng_ref[...], lnb_ref[...])


def _merge_call(o, proj, y, h, mod, gn, d_skip, w_glu, b_glu, w_out, ln_g, ln_b, nb, period, ctx):
    t, d = h.shape
    w = GDN_WIDTH
    sw = d - w
    tm = _pick(period, 272, 16)
    tpb = period // tm
    body = functools.partial(_merge_body, tm=tm, tpb=tpb, ctx=ctx)
    const = lambda r, c: pl.BlockSpec((r, c), lambda i: (0, 0))
    return pl.pallas_call(
        body,
        grid=(t // tm,),
        in_specs=[pl.BlockSpec((tm, w), lambda i: (i, 0)),
                  pl.BlockSpec((tm, w), lambda i: (i, 3)),
                  pl.BlockSpec((tm, sw), lambda i: (i, 4 * w // sw)),
                  pl.BlockSpec((tm, sw), lambda i: (i, 0)),
                  pl.BlockSpec((tm, d), lambda i: (i, 0)),
                  pl.BlockSpec((None, 6, d), lambda i: (i // tpb, 0, 0)),
                  pl.BlockSpec((None, 6, d), lambda i: (nb, 0, 0)),
                  const(1, HEAD_DIM), const(1, sw), const(sw, sw), const(1, sw), const(d, d),
                  const(1, d), const(1, d)],
        out_specs=pl.BlockSpec((tm, d), lambda i: (i, 0)),
        out_shape=jax.ShapeDtypeStruct((t, d), F32),
        scratch_shapes=[pltpu.VMEM((tm, d), BF16)],
        compiler_params=_params("arbitrary"),
        name="even_merge",
    )(o, proj, proj, y, h, mod, mod, gn, d_skip, w_glu, b_glu, w_out, ln_g, ln_b)


def _ffn_body(x_ref, xh_ref, mb_ref, mc_ref, wv_ref, wg_ref, cwv_ref, cwg_ref, cbv_ref, cbg_ref, wd_ref,
              lng_ref, lnb_ref, o_ref, xb_ref, *up_refs, tm, tpb, ctx, period):
    i = pl.program_id(0)
    j = pl.program_id(1)
    rib = _rows_in_batch(i, tm, tpb)
    n = tm + 32

    @pl.when(j == 0)
    def _():
        xb_ref[16:tm + 16, :] = _modulate(x_ref[...], rib < ctx, mb_ref, mc_ref, 3, 4).astype(BF16)
        base = (i % tpb) * tm
        off = lax.broadcasted_iota(jnp.int32, (16, 1), 0)
        xh = xh_ref[...]
        xb_ref[tm + 16:n, :] = _modulate(xh[0:16], base + tm + off < ctx, mb_ref, mc_ref, 3, 4).astype(BF16)
        xb_ref[0:16, :] = _modulate(xh[16:32], base - 16 + off < ctx, mb_ref, mc_ref, 3, 4).astype(BF16)
        o_ref[...] = jnp.zeros_like(o_ref)

    has_prev, has_next = _seq_edges(rib, ctx, period)
    xb = xb_ref[...]

    def conv(u_ref, cw_ref, cb_ref, sl):
        cw = cw_ref[:, sl]
        return (jnp.where(has_prev, u_ref[15:tm + 15, :], 0.0) * cw[0:1] + u_ref[16:tm + 16, :] * cw[1:2]
                + jnp.where(has_next, u_ref[17:tm + 17, :], 0.0) * cw[2:3] + cb_ref[:, sl])

    fc = wv_ref.shape[1]
    slabs = [slice(a, a + FFN_SLAB) for a in range(0, fc, FFN_SLAB)]
    for si, sl in enumerate(slabs):
        up_refs[2 * si][...] = _dot(xb, wv_ref[:, sl])
        up_refs[2 * si + 1][...] = _dot(xb, wg_ref[:, sl])
    for si, sl in enumerate(slabs):
        act = conv(up_refs[2 * si], cwv_ref, cbv_ref, sl) * _silu(conv(up_refs[2 * si + 1], cwg_ref, cbg_ref, sl))
        o_ref[...] += _dot(act.astype(BF16), wd_ref[sl, :])

    @pl.when(j == pl.num_programs(1) - 1)
    def _():
        g = jnp.where(rib < ctx, mc_ref[5:6, :], mb_ref[5:6, :])
        o_ref[...] = _layernorm(ALPHA * x_ref[...] + g * o_ref[...], lng_ref[...], lnb_ref[...])


def _ffn_call(h, mod, w_up, conv_w, conv_b, w_down, ln_g, ln_b, nb, period, ctx):
    t, d = h.shape
    f = w_down.shape[0]
    tm = _pick(period, 544, 16)
    tpb = period // tm
    nt = t // tm
    fc = _pick(f, FFN_CHUNK, FFN_SLAB)
    nf = f // fc
    h3 = h.reshape(nt, tm, d)
    pad = jnp.zeros((1, 16, d), h.dtype)
    halo = jnp.concatenate([jnp.concatenate([h3[1:, :16], pad], 0),
                            jnp.concatenate([pad, h3[:-1, tm - 16:]], 0)], axis=1)
    body = functools.partial(_ffn_body, tm=tm, tpb=tpb, ctx=ctx, period=period)
    return pl.pallas_call(
        body,
        grid=(nt, nf),
        in_specs=[pl.BlockSpec((tm, d), lambda i, j: (i, 0)),
                  pl.BlockSpec((None, 32, d), lambda i, j: (i, 0, 0)),
                  pl.BlockSpec((None, 6, d), lambda i, j: (i // tpb, 0, 0)),
                  pl.BlockSpec((None, 6, d), lambda i, j: (nb, 0, 0)),
                  pl.BlockSpec((d, fc), lambda i, j: (0, j)),
                  pl.BlockSpec((d, fc), lambda i, j: (0, nf + j)),
                  pl.BlockSpec((3, fc), lambda i, j: (0, j)),
                  pl.BlockSpec((3, fc), lambda i, j: (0, nf + j)),
                  pl.BlockSpec((1, fc), lambda i, j: (0, j)),
                  pl.BlockSpec((1, fc), lambda i, j: (0, nf + j)),
                  pl.BlockSpec((fc, d), lambda i, j: (j, 0)),
                  pl.BlockSpec((1, d), lambda i, j: (0, 0)),
                  pl.BlockSpec((1, d), lambda i, j: (0, 0))],
        out_specs=pl.BlockSpec((tm, d), lambda i, j: (i, 0)),
        out_shape=jax.ShapeDtypeStruct((t, d), F32),
        scratch_shapes=([pltpu.VMEM((tm + 32, d), BF16)]
                        + [pltpu.VMEM((tm + 32, FFN_SLAB), F32) for _ in range(2 * fc // FFN_SLAB)]),
        compiler_params=_params("arbitrary", "arbitrary"),
        name="conv_ffn",
    )(h, halo, mod, mod, w_up, w_up, conv_w, conv_w, conv_b, conv_b, w_down, ln_g, ln_b)


def _qkv_body(x_ref, mb_ref, mc_ref, w_ref, cos_ref, sin_ref, qn_ref, kn_ref, o_ref, xb_ref,
              *, tm, tpb, ctx, nq_tiles):
    i = pl.program_id(0)
    j = pl.program_id(1)

    @pl.when(j == 0)
    def _():
        rib = _rows_in_batch(i, tm, tpb)
        xb_ref[...] = _modulate(x_ref[...], rib < ctx, mb_ref, mc_ref, 0, 1).astype(BF16)

    acc = _dot(xb_ref[...], w_ref[...])
    cos = cos_ref[...]
    sin = sin_ref[...]
    lane = lax.broadcasted_iota(jnp.int32, (1, HEAD_DIM), 1)
    first = (lane % 64) < 32
    heads = acc.shape[1] // HEAD_DIM

    def normrope(xh, wn, scale):
        xn = xh * lax.rsqrt(jnp.mean(xh * xh, -1, keepdims=True) + RMS_EPS) * wn
        partner = jnp.where(first, pltpu.roll(xn, HEAD_DIM - 32, 1), pltpu.roll(xn, 32, 1))
        return (xn * cos + partner * sin) * scale

    @pl.when(j < nq_tiles)
    def _():
        for hh in range(heads):
            sl = slice(hh * HEAD_DIM, (hh + 1) * HEAD_DIM)
            o_ref[:, sl] = normrope(acc[:, sl], qn_ref[...], ATT_SCALE).astype(BF16)

    @pl.when(j == nq_tiles)
    def _():
        for hh in range(heads):
            sl = slice(hh * HEAD_DIM, (hh + 1) * HEAD_DIM)
            if hh < ATT_KV_HEADS:
                o_ref[:, sl] = normrope(acc[:, sl], kn_ref[...], 1.0).astype(BF16)
            else:
                o_ref[:, sl] = acc[:, sl].astype(BF16)


def _qkv_call(h, mod, w_in, cos, sin, qn, kn, nb, period, ctx):
    t, d = h.shape
    n = w_in.shape[1]
    tn = 2 * ATT_KV_HEADS * HEAD_DIM
    nq_tiles = (n - tn) // tn
    tm = _pick(period, 544, 16)
    tpb = period // tm
    body = functools.partial(_qkv_body, tm=tm, tpb=tpb, ctx=ctx, nq_tiles=nq_tiles)
    return pl.pallas_call(
        body,
        grid=(t // tm, n // tn),
        in_specs=[pl.BlockSpec((tm, d), lambda i, j: (i, 0)),
                  pl.BlockSpec((None, 6, d), lambda i, j: (i // tpb, 0, 0)),
                  pl.BlockSpec((None, 6, d), lambda i, j: (nb, 0, 0)),
                  pl.BlockSpec((d, tn), lambda i, j: (0, j)),
                  pl.BlockSpec((tm, HEAD_DIM), lambda i, j: (i % tpb, 0)),
                  pl.BlockSpec((tm, HEAD_DIM), lambda i, j: (i % tpb, 0)),
                  pl.BlockSpec((1, HEAD_DIM), lambda i, j: (0, 0)),
                  pl.BlockSpec((1, HEAD_DIM), lambda i, j: (0, 0))],
        out_specs=pl.BlockSpec((tm, tn), lambda i, j: (i, j)),
        out_shape=jax.ShapeDtypeStruct((t, n), BF16),
        scratch_shapes=[pltpu.VMEM((tm, d), BF16)],
        compiler_params=_params("arbitrary", "arbitrary"),
        name="odd_qkv",
    )(h, mod, mod, w_in, cos, sin, qn, kn)


def _rope_tables(seq, ctx):
    rows = seq // GRID_W
    row = jnp.repeat(jnp.arange(rows, dtype=F32), GRID_W)
    col = jnp.tile(jnp.arange(GRID_W, dtype=F32), rows)
    half = HEAD_DIM // 4
    inv = ROPE_THETA ** (-jnp.arange(half, dtype=F32) / half)
    ar = row[:, None] * inv
    ac = col[:, None] * inv
    cos = jnp.concatenate([jnp.cos(ar), jnp.cos(ar), jnp.cos(ac), jnp.cos(ac)], -1)
    sin = jnp.concatenate([-jnp.sin(ar), jnp.sin(ar), -jnp.sin(ac), jnp.sin(ac)], -1)
    cos = jnp.concatenate([jnp.ones((ctx, HEAD_DIM), F32), cos], 0)
    sin = jnp.concatenate([jnp.zeros((ctx, HEAD_DIM), F32), sin], 0)
    return cos, sin


def _attn_body(q_ref, k_ref, v_ref, o_ref, *, tq, key_blocks):
    hd = HEAD_DIM
    q = jnp.concatenate([q_ref[:, g * hd:(g + 1) * hd] for g in range(ATT_GROUP)], axis=0)
    rows = ATT_GROUP * tq
    m = jnp.full((rows, 1), -1e30, F32)
    l = jnp.zeros((rows, 1), F32)
    acc = jnp.zeros((rows, hd), F32)
    s_next = _dot_nt(q, k_ref[key_blocks[0][0]:key_blocks[0][1], :])
    for n, (k0, k1) in enumerate(key_blocks):
        s = s_next
        if n + 1 < len(key_blocks):
            s_next = _dot_nt(q, k_ref[key_blocks[n + 1][0]:key_blocks[n + 1][1], :])
        m_new = jnp.maximum(m, jnp.max(s, -1, keepdims=True))
        a = jnp.exp(m - m_new)
        p = jnp.exp(s - m_new)
        l = a * l + jnp.sum(p, -1, keepdims=True)
        acc = a * acc + _dot(p.astype(BF16), v_ref[k0:k1, :])
        m = m_new
    out = acc / l
    for g in range(ATT_GROUP):
        o_ref[:, g * hd:(g + 1) * hd] = out[g * tq:(g + 1) * tq].astype(o_ref.dtype)


def _key_blocks(period, count, align):
    tiles = period // align
    assert tiles * align == period and tiles >= count
    sizes = [tiles // count + (1 if n >= count - tiles % count else 0) for n in range(count)]
    edges = [0]
    for sz in sizes:
        edges.append(edges[-1] + sz * align)
    return tuple(zip(edges[:-1], edges[1:]))


def _attn_call(qkv, nb, seq, ctx):
    period = seq + ctx
    tq = _pick(math.gcd(seq, ctx), ATT_Q_TILE, 16)
    gw = ATT_GROUP * HEAD_DIM
    qpb = seq // tq
    rpb = period // tq
    skip = ctx // tq
    kcol = ATT_HEADS
    vcol = ATT_HEADS + ATT_KV_HEADS
    align = math.gcd(period, MXU_TILE)
    blocks = _key_blocks(period, min(ATT_KEY_BLOCKS, period // align), align)
    body = functools.partial(_attn_body, tq=tq, key_blocks=blocks)
    return pl.pallas_call(
        body,
        grid=(nb, ATT_KV_HEADS, qpb),
        in_specs=[pl.BlockSpec((tq, gw), lambda b, kv, qi: (b * rpb + skip + qi, kv)),
                  pl.BlockSpec((period, HEAD_DIM), lambda b, kv, qi: (b, kcol + kv)),
                  pl.BlockSpec((period, HEAD_DIM), lambda b, kv, qi: (b, vcol + kv))],
        out_specs=pl.BlockSpec((tq, gw), lambda b, kv, qi: (b * qpb + qi, kv)),
        out_shape=jax.ShapeDtypeStruct((nb * seq, ATT_HEADS * HEAD_DIM), BF16),
        compiler_params=_params("arbitrary", "arbitrary", "arbitrary"),
        name="gqa_attention",
    )(qkv, qkv, qkv)


def _outproj_body(a_ref, h_ref, mb_ref, w_ref, lng_ref, lnb_ref, o_ref):
    mix = _dot(a_ref[...], w_ref[...])
    o_ref[...] = _layernorm(ALPHA * h_ref[...] + mb_ref[2:3, :] * mix, lng_ref[...], lnb_ref[...])


def _outproj_call(att, h, mod, w_out, ln_g, ln_b, nb, seq, ctx):
    d = h.shape[1]
    period = seq + ctx
    tm = _pick(math.gcd(seq, ctx), 256, 16)
    qpb = seq // tm
    rpb = period // tm
    skip = ctx // tm
    return pl.pallas_call(
        _outproj_body,
        grid=(nb * qpb,),
        in_specs=[pl.BlockSpec((tm, d), lambda i: (i, 0)),
                  pl.BlockSpec((tm, d), lambda i: ((i // qpb) * rpb + skip + i % qpb, 0)),
                  pl.BlockSpec((None, 6, d), lambda i: (i // qpb, 0, 0)),
                  pl.BlockSpec((d, d), lambda i: (0, 0)),
                  pl.BlockSpec((1, d), lambda i: (0, 0)),
                  pl.BlockSpec((1, d), lambda i: (0, 0))],
        out_specs=pl.BlockSpec((tm, d), lambda i: (i, 0)),
        out_shape=jax.ShapeDtypeStruct((nb * seq, d), F32),
        compiler_params=_params("arbitrary"),
        name="odd_outproj",
    )(att, h, mod, w_out, ln_g, ln_b)


def kernel(x, c, ctx, c_ctx, mod_w, mod_b, ln1_g, ln1_b, ln2_g, ln2_b, ffn_w_up, ffn_conv_w, ffn_conv_b, ffn_w_down, e_w_in, e_conv_qkv, e_a_log, e_dt_bias, e_gdn_norm, e_lam_re, e_lam_im, e_log_dt, e_b_re, e_b_im, e_c_re, e_c_im, e_d_skip, e_w_glu, e_b_glu, e_w_out, o_w_in, o_q_norm, o_k_norm, o_w_out):
    nb, seq, d = x.shape
    nctx = ctx.shape[1]
    period = nctx + seq
    t = nb * period
    w = GDN_WIDTH
    assert mod_w.shape[0] == DEPTH == 2 and nb < 8

    h = jnp.concatenate([ctx, x], axis=1).reshape(t, d)
    cs = jnp.zeros((8, d), F32).at[:nb].set(c).at[nb].set(c_ctx)
    mod = _mod_call(cs, mod_w, mod_b).reshape(DEPTH, 8, 6, d)
    row = lambda a: a.reshape(1, -1)

    w_in = e_w_in[0]
    gates_at = 4 * w
    w_main = jnp.concatenate([w_in[:, :gates_at], w_in[:, gates_at + 4 * GDN_HEADS:]], axis=1).astype(BF16)
    w_ab = jnp.pad(w_in[:, gates_at:gates_at + 4 * GDN_HEADS], ((0, 0), (0, 128 - 4 * GDN_HEADS))).astype(BF16)
    proj, ab = _inproj_even_call(h, mod[0], w_main, w_ab, nb, period, nctx)
    pad_row = lambda a: jnp.pad(a.reshape(1, -1), ((0, 0), (0, 128 - 2 * GDN_HEADS)))
    qkvn, kt3, gh, gct3 = _gdn_prep_call(proj, ab, e_conv_qkv[0], pad_row(e_a_log[0]), pad_row(e_dt_bias[0]),
                                         period, nctx)
    o = _gdn_scan_call(qkvn, kt3, gh, gct3, nb, period, nctx)

    ops = _s5_operators(_s5_tables_call(e_lam_re[0], e_lam_im[0], e_log_dt[0], e_b_re[0], e_b_im[0],
                                        e_c_re[0], e_c_im[0]))
    sgroups = (d - w) // S5_GROUP
    blk = lambda a: a.reshape(t // S5_CHUNK, S5_CHUNK, sgroups, S5_GROUP).transpose(0, 2, 1, 3)
    u_blk = blk(proj[:, 4 * w:]).reshape(t // S5_CHUNK, sgroups * S5_TILE)
    y_blk = _s5_call(u_blk, ops, nb, period, nctx)
    y = y_blk.reshape(t // S5_CHUNK, sgroups, S5_CHUNK, S5_GROUP).transpose(0, 2, 1, 3).reshape(t, d - w)

    h = _merge_call(o, proj, y, h, mod[0], row(e_gdn_norm[0]), row(e_d_skip[0]), e_w_glu[0].astype(BF16),
                    row(e_b_glu[0]), e_w_out[0].astype(BF16), row(ln1_g[0]), row(ln1_b[0]), nb, period, nctx)
    h = _ffn_call(h, mod[0], ffn_w_up[0].astype(BF16), ffn_conv_w[0], row(ffn_conv_b[0]),
                  ffn_w_down[0].astype(BF16), row(ln2_g[0]), row(ln2_b[0]), nb, period, nctx)

    cos, sin = _rope_tables(seq, nctx)
    qkv = _qkv_call(h, mod[1], o_w_in[0].astype(BF16), cos, sin, row(o_q_norm[0]), row(o_k_norm[0]),
                    nb, period, nctx)
    att = _attn_call(qkv, nb, seq, nctx)
    hl = _outproj_call(att, h, mod[1], o_w_out[0].astype(BF16), row(ln1_g[1]), row(ln1_b[1]), nb, seq, nctx)
    out = _ffn_call(hl, mod[1], ffn_w_up[1].astype(BF16), ffn_conv_w[1], row(ffn_conv_b[1]),
                    ffn_w_down[1].astype(BF16), row(ln2_g[1]), row(ln2_b[1]), nb, seq, 0)
    return out.reshape(nb, seq, d)
```

```python
import functools
import math

import jax
import jax.numpy as jnp
from jax import lax
from jax.experimental import pallas as pl
from jax.experimental.pallas import tpu as pltpu

F32 = jnp.float32
BF16 = jnp.bfloat16

DEPTH = 2
GDN_HEADS = 8
HEAD_DIM = 128
GDN_WIDTH = GDN_HEADS * HEAD_DIM
GDN_CHUNK = 64
GDN_INTRA_BATCH = 17
GDN_SOLVE_BASE = 8
S5_GROUP = 16
S5_STATE = 64
S5_CHUNK = 16
S5_TILE = S5_CHUNK * S5_GROUP
S5_GROUPS_PER_STEP = 4
ATT_HEADS = 16
ATT_KV_HEADS = 4
ATT_GROUP = ATT_HEADS // ATT_KV_HEADS
ATT_SCALE = HEAD_DIM ** -0.5
ATT_Q_TILE = 256
ATT_KEY_BLOCKS = 4
MXU_TILE = 256
GRID_W = 64
ROPE_THETA = 10000.0
FFN_CHUNK = 512
FFN_SLAB = 256
ALPHA = (2 * DEPTH) ** 0.25
LN_EPS = 1e-6
RMS_EPS = 1e-6
L2_EPS = 1e-6
VMEM_LIMIT_BYTES = 56 * 1024 * 1024


def _pick(n, target, mult):
    best = None
    for d in range(mult, min(n, target) + 1, mult):
        if n % d == 0:
            best = d
    assert best is not None, (n, target, mult)
    return best


def _params(*sem):
    return pltpu.CompilerParams(dimension_semantics=sem, vmem_limit_bytes=VMEM_LIMIT_BYTES)


def _dot(a, b):
    return jnp.dot(a, b, preferred_element_type=F32)


def _dot_nt(a, b):
    return lax.dot_general(a, b, (((1,), (1,)), ((), ())), preferred_element_type=F32)


def _split3(a):
    p0 = a.astype(BF16)
    r = a - p0.astype(F32)
    p1 = r.astype(BF16)
    p2 = (r - p1.astype(F32)).astype(BF16)
    return p0, p1, p2


def _rows_in_batch(i, tm, tiles_per_batch):
    return (i % tiles_per_batch) * tm + lax.broadcasted_iota(jnp.int32, (tm, 1), 0)


def _modulate(x, is_ctx, mb_ref, mc_ref, shift_i, scale_i):
    scale = jnp.where(is_ctx, mc_ref[scale_i:scale_i + 1, :], mb_ref[scale_i:scale_i + 1, :])
    shift = jnp.where(is_ctx, mc_ref[shift_i:shift_i + 1, :], mb_ref[shift_i:shift_i + 1, :])
    return x * (1.0 + scale) + shift


def _layernorm(r, g, b):
    xc = r - jnp.mean(r, -1, keepdims=True)
    var = jnp.mean(xc * xc, -1, keepdims=True)
    return xc * lax.rsqrt(var + LN_EPS) * g + b


def _silu(x):
    return x * jax.nn.sigmoid(x)


def _seq_edges(rib, ctx, period):
    has_prev = (rib != 0) & (rib != ctx)
    has_next = (rib != ctx - 1) & (rib != period - 1)
    return has_prev, has_next


def _mod_body(c_ref, w_ref, b_ref, o_ref):
    s = _silu(c_ref[...])
    o_ref[...] = _dot(s.astype(BF16), w_ref[...].astype(BF16)) + b_ref[...]


def _mod_call(cs, mod_w, mod_b):
    depth, d, n = mod_w.shape
    tn = _pick(n, 1024, 128)
    return pl.pallas_call(
        _mod_body,
        grid=(depth, n // tn),
        in_specs=[pl.BlockSpec((8, d), lambda l, j: (0, 0)),
                  pl.BlockSpec((None, d, tn), lambda l, j: (l, 0, j)),
                  pl.BlockSpec((None, 1, tn), lambda l, j: (l, 0, j))],
        out_specs=pl.BlockSpec((None, 8, tn), lambda l, j: (l, 0, j)),
        out_shape=jax.ShapeDtypeStruct((depth, 8, n), F32),
        compiler_params=_params("arbitrary", "arbitrary"),
        name="adaln_mod",
    )(cs, mod_w, mod_b.reshape(depth, 1, n))


def _inproj_even_body(x_ref, mb_ref, mc_ref, w_ref, wab_ref, o_ref, ab_ref, xb_ref, *, tm, tpb, ctx):
    i = pl.program_id(0)

    @pl.when(pl.program_id(1) == 0)
    def _():
        rib = _rows_in_batch(i, tm, tpb)
        xb = _modulate(x_ref[...], rib < ctx, mb_ref, mc_ref, 0, 1).astype(BF16)
        xb_ref[...] = xb
        ab_ref[...] = _dot(xb, wab_ref[...])

    o_ref[...] = _dot(xb_ref[...], w_ref[...]).astype(o_ref.dtype)


def _inproj_even_call(h, mod, w_main, w_ab, nb, period, ctx):
    t, d = h.shape
    n = w_main.shape[1]
    tm = _pick(period, 544, 16)
    tn = _pick(n, 1280, 256)
    tpb = period // tm
    body = functools.partial(_inproj_even_body, tm=tm, tpb=tpb, ctx=ctx)
    return pl.pallas_call(
        body,
        grid=(t // tm, n // tn),
        in_specs=[pl.BlockSpec((tm, d), lambda i, j: (i, 0)),
                  pl.BlockSpec((None, 6, d), lambda i, j: (i // tpb, 0, 0)),
                  pl.BlockSpec((None, 6, d), lambda i, j: (nb, 0, 0)),
                  pl.BlockSpec((d, tn), lambda i, j: (0, j)),
                  pl.BlockSpec((d, 128), lambda i, j: (0, 0))],
        out_specs=[pl.BlockSpec((tm, tn), lambda i, j: (i, j)),
                   pl.BlockSpec((tm, 128), lambda i, j: (i, 0))],
        out_shape=[jax.ShapeDtypeStruct((t, n), BF16), jax.ShapeDtypeStruct((t, 128), F32)],
        scratch_shapes=[pltpu.VMEM((tm, d), BF16)],
        compiler_params=_params("arbitrary", "arbitrary"),
        name="even_inproj",
    )(h, mod, mod, w_main, w_ab)


def _gdn_prep_body(x_ref, xn_ref, xp_ref, cw_ref, ab_ref, alog_ref, dtb_ref, tril_ref, triu_ref, e_ref,
                   qkv_ref, kt_ref, gh_ref, gct_ref, *, tr, tpb, ctx, period):
    i = pl.program_id(0)
    sec = pl.program_id(1)
    rib = _rows_in_batch(i, tr, tpb)
    has_prev, has_next = _seq_edges(rib, ctx, period)
    ext = jnp.concatenate([x_ref[...].astype(F32), xn_ref[...].astype(F32), xp_ref[...].astype(F32)], axis=0)
    n = tr + 32
    xprev = pltpu.roll(ext, 1, 0)[:tr]
    xnext = pltpu.roll(ext, n - 1, 0)[:tr]
    cw = cw_ref[...]
    y = (jnp.where(has_prev, xprev, 0.0) * cw[0:1] + ext[:tr] * cw[1:2]
         + jnp.where(has_next, xnext, 0.0) * cw[2:3])
    y = _silu(y)

    def l2n(scale):
        parts = []
        for hh in range(GDN_HEADS):
            yh = y[:, hh * HEAD_DIM:(hh + 1) * HEAD_DIM]
            parts.append(yh * (lax.rsqrt(jnp.sum(yh * yh, -1, keepdims=True) + L2_EPS) * scale))
        return jnp.concatenate(parts, axis=1)

    @pl.when(sec == 0)
    def _():
        qkv_ref[...] = l2n(HEAD_DIM ** -0.5).astype(BF16)
        ab = ab_ref[...]
        lane = lax.broadcasted_iota(jnp.int32, (1, 128), 1)
        xg = ab + dtb_ref[...]
        softplus = jnp.maximum(xg, 0.0) + jnp.log1p(jnp.exp(-jnp.abs(xg)))
        g = -jnp.exp(alog_ref[...]) * softplus
        g0, g1, g2 = _split3(g)
        tril = tril_ref[...]
        triu = triu_ref[...]
        pre = _dot(tril, g0) + _dot(tril, g1) + _dot(tril, g2)
        suf = _dot(triu, g0) + _dot(triu, g1) + _dot(triu, g2)
        gc = jnp.where(lane < GDN_HEADS, pre, suf)
        vals = jnp.where(lane < 2 * GDN_HEADS, gc, jax.nn.sigmoid(ab))
        v0, v1, v2 = _split3(vals)
        e = e_ref[...]
        gh_ref[...] = _dot(v0, e) + _dot(v1, e) + _dot(v2, e)
        gct = gc.T
        for c in range(tr // GDN_CHUNK):
            gct_ref[c] = gct[:2 * GDN_HEADS, c * GDN_CHUNK:(c + 1) * GDN_CHUNK]

    @pl.when(sec == 1)
    def _():
        kn = l2n(1.0)
        qkv_ref[...] = kn.astype(BF16)
        knt = kn.T
        for c in range(tr // GDN_CHUNK):
            kt_ref[c] = knt[:, c * GDN_CHUNK:(c + 1) * GDN_CHUNK].astype(BF16)

    @pl.when(sec == 2)
    def _():
        qkv_ref[...] = y.astype(BF16)


def _gdn_prep_call(proj, ab, conv_w, alog_row, dtb_row, period, ctx):
    t = proj.shape[0]
    w = GDN_WIDTH
    tr = _pick(period, 256, GDN_CHUNK)
    tpb = period // tr
    hb = tr // 16
    nh = t // 16
    nck = tr // GDN_CHUNK
    r = jnp.arange(tr)
    same = (r[:, None] // GDN_CHUNK) == (r[None, :] // GDN_CHUNK)
    tril = (same & (r[:, None] >= r[None, :])).astype(BF16)
    triu = (same & (r[:, None] <= r[None, :])).astype(BF16)
    src = jnp.arange(128)[:, None]
    dst = jnp.arange(w)[None, :]
    expand = ((dst % HEAD_DIM < 4) & (src == (dst % HEAD_DIM) * GDN_HEADS + dst // HEAD_DIM)).astype(BF16)
    body = functools.partial(_gdn_prep_body, tr=tr, tpb=tpb, ctx=ctx, period=period)
    return pl.pallas_call(
        body,
        grid=(t // tr, 3),
        in_specs=[pl.BlockSpec((tr, w), lambda i, s: (i, s)),
                  pl.BlockSpec((16, w), lambda i, s: (jnp.minimum((i + 1) * hb, nh - 1), s)),
                  pl.BlockSpec((16, w), lambda i, s: (jnp.maximum(i * hb - 1, 0), s)),
                  pl.BlockSpec((3, w), lambda i, s: (0, s)),
                  pl.BlockSpec((tr, 128), lambda i, s: (i, 0)),
                  pl.BlockSpec((1, 128), lambda i, s: (0, 0)),
                  pl.BlockSpec((1, 128), lambda i, s: (0, 0)),
                  pl.BlockSpec((tr, tr), lambda i, s: (0, 0)),
                  pl.BlockSpec((tr, tr), lambda i, s: (0, 0)),
                  pl.BlockSpec((128, w), lambda i, s: (0, 0))],
        out_specs=[pl.BlockSpec((tr, w), lambda i, s: (i, s)),
                   pl.BlockSpec((nck, w, GDN_CHUNK), lambda i, s: (i, 0, 0)),
                   pl.BlockSpec((tr, w), lambda i, s: (i, 0)),
                   pl.BlockSpec((nck, 2 * GDN_HEADS, GDN_CHUNK), lambda i, s: (i, 0, 0))],
        out_shape=[jax.ShapeDtypeStruct((t, 3 * w), BF16),
                   jax.ShapeDtypeStruct((t // GDN_CHUNK, w, GDN_CHUNK), BF16),
                   jax.ShapeDtypeStruct((t, w), F32),
                   jax.ShapeDtypeStruct((t // GDN_CHUNK, 2 * GDN_HEADS, GDN_CHUNK), F32)],
        compiler_params=_params("arbitrary", "arbitrary"),
        name="gdn_prep",
    )(proj, proj, proj, conv_w, ab, alog_row, dtb_row, tril, triu, expand)


def _gdn_scan_body(q_ref, k_ref, v_ref, kt_ref, gh_ref, gct_ref, o_ref,
                   aqf, aqb, bsf, bsb, egf, egb, *, nch, nctx, batch):
    head = pl.program_id(1)
    cs = GDN_CHUNK
    ii = lax.broadcasted_iota(jnp.int32, (cs, cs), 0)
    jj = lax.broadcasted_iota(jnp.int32, (cs, cs), 1)
    masks = ((ii >= jj, ii > jj), (ii <= jj, ii < jj))
    same_block = {}
    width = GDN_SOLVE_BASE
    while width <= cs:
        shift = width.bit_length() - 1
        same_block[width] = jnp.right_shift(ii, shift) == jnp.right_shift(jj, shift)
        width *= 2
    hd = HEAD_DIM
    aqs, bss, egs = (aqf, aqb), (bsf, bsb), (egf, egb)

    def intra(first, count):
        chunks = [first + j for j in range(count)]
        r0s = [pl.multiple_of(c * cs, cs) for c in chunks]
        qs = [q_ref[pl.ds(r0, cs), :] for r0 in r0s]
        ks = [k_ref[pl.ds(r0, cs), :] for r0 in r0s]
        vs = [v_ref[pl.ds(r0, cs), :] for r0 in r0s]
        kts = [kt_ref[c] for c in chunks]
        raw_q = [_dot(q, kt) for q, kt in zip(qs, kts)]
        raw_k = [_dot(k, kt) for k, kt in zip(ks, kts)]
        pms, xs, atts, kps, qes, gends = [], [], [], [], [], []
        for j, c in enumerate(chunks):
            g4 = gh_ref[pl.ds(r0s[j], cs), :]
            qf, kf, vf = qs[j].astype(F32), ks[j].astype(F32), vs[j].astype(F32)
            ktf = kts[j].astype(F32)
            for d in (0, 1):
                incl, strict = masks[d]
                gcol = g4[:, d:d + 1]
                bcol = g4[:, 2 + d:3 + d]
                grow = gct_ref[c, pl.ds(d * GDN_HEADS + head, 1), :]
                decay = jnp.where(incl, jnp.exp(jnp.where(incl, gcol - grow, 0.0)), 0.0)
                eg = jnp.exp(gcol)
                gend = grow[:, cs - 1:cs] if d == 0 else grow[:, 0:1]
                pms.append(jnp.where(strict, -(raw_k[j] * bcol) * decay, 0.0))
                xs.append(jnp.concatenate([vf * bcol, kf * (bcol * eg)], axis=1))
                atts.append((raw_q[j] * decay).astype(BF16))
                kps.append((ktf * jnp.exp(gend - grow)).astype(BF16))
                qes.append(qf * eg)
                gends.append(gend)
        bfs = lambda vals: [a.astype(BF16) for a in vals]
        pds = bfs([jnp.where(same_block[GDN_SOLVE_BASE], pm, 0.0) for pm in pms])
        ns = [pd.astype(F32) for pd in pds]
        pw = pds
        span = 1
        while 2 * span < GDN_SOLVE_BASE:
            sq = [_dot(p, p) for p in pw]
            pw = bfs(sq)
            ns = [n + s + _dot(p, n.astype(BF16)) for n, s, p in zip(ns, sq, pw)]
            span *= 2
        width = GDN_SOLVE_BASE
        while width < cs:
            couple = same_block[2 * width] & ~same_block[width]
            qs_ = [jnp.where(couple, pm, 0.0) for pm in pms]
            nbs = bfs(ns)
            bs_ = [q + _dot(nb, q.astype(BF16)) for q, nb in zip(qs_, nbs)]
            ns = [n + b + _dot(b.astype(BF16), nb) for n, b, nb in zip(ns, bs_, nbs)]
            width *= 2
        xs = [x + _dot(n.astype(BF16), x.astype(BF16)) for n, x in zip(ns, xs)]
        xbs = [x.astype(BF16) for x in xs]
        aws = [_dot(att, xb) for att, xb in zip(atts, xbs)]
        kxs = [_dot(kp, xb) for kp, xb in zip(kps, xbs)]
        for j, c in enumerate(chunks):
            for d in (0, 1):
                n = 2 * j + d
                aqs[d][c, 0:hd, :] = (-kxs[n][:, hd:]).astype(BF16)
                aqs[d][c, hd:hd + cs, :] = (qes[n] - aws[n][:, hd:]).astype(BF16)
                bss[d][c] = kxs[n][:, :hd]
                egs[d][c] = jnp.broadcast_to(jnp.exp(gends[n]), (8, hd))
            o_ref[pl.ds(r0s[j], cs), :] = aws[2 * j][:, :hd] + aws[2 * j + 1][:, :hd]

    def inter(i, carry):
        cb = jnp.where(i < nctx, nctx - 1 - i, nch - 1 - (i - nctx))
        new = []
        for d, c in ((0, i), (1, cb)):
            s = carry[d]
            r0 = pl.multiple_of(c * cs, cs)
            r = _dot(aqs[d][c], s.astype(BF16))
            o_ref[pl.ds(r0, cs), :] += r[hd:]
            new.append(s * egs[d][c][0:1, :] + r[:hd] + bss[d][c])
        return tuple(new)

    full = nch // batch

    def intra_step(it, carry):
        intra(it * batch, batch)
        return carry

    lax.fori_loop(0, full, intra_step, 0)
    if nch > full * batch:
        intra(full * batch, nch - full * batch)
    zero = jnp.zeros((hd, hd), F32)
    lax.fori_loop(0, nch, inter, (zero, zero), unroll=2)


def _gdn_scan_call(qkvn, kt3, gh, gct3, nb, period, ctx):
    t = qkvn.shape[0]
    nch = period // GDN_CHUNK
    nctx = ctx // GDN_CHUNK
    hd = HEAD_DIM
    body = functools.partial(_gdn_scan_body, nch=nch, nctx=nctx, batch=min(nch, GDN_INTRA_BATCH))
    dirs2 = lambda shape, dt: [pltpu.VMEM(shape, dt), pltpu.VMEM(shape, dt)]
    return pl.pallas_call(
        body,
        grid=(nb, GDN_HEADS),
        in_specs=[pl.BlockSpec((period, hd), lambda b, h: (b, h)),
                  pl.BlockSpec((period, hd), lambda b, h: (b, GDN_HEADS + h)),
                  pl.BlockSpec((period, hd), lambda b, h: (b, 2 * GDN_HEADS + h)),
                  pl.BlockSpec((nch, hd, GDN_CHUNK), lambda b, h: (b, h, 0)),
                  pl.BlockSpec((period, hd), lambda b, h: (b, h)),
                  pl.BlockSpec((nch, 2 * GDN_HEADS, GDN_CHUNK), lambda b, h: (b, 0, 0))],
        out_specs=pl.BlockSpec((period, hd), lambda b, h: (b, h)),
        out_shape=jax.ShapeDtypeStruct((t, GDN_WIDTH), F32),
        scratch_shapes=(dirs2((nch, hd + GDN_CHUNK, hd), BF16) + dirs2((nch, hd, hd), F32)
                        + dirs2((nch, 8, hd), F32)),
        compiler_params=_params("arbitrary", "arbitrary"),
        name="gdn_scan",
    )(qkvn, qkvn, qkvn, kt3, gh, gct3)


def _s5_tables_body(lr_ref, li_ref, ldt_ref, btr_ref, bti_ref, cr_ref, ci_ref,
                    ar_ref, ai_ref, cmr_ref, cmi_ref, k_ref, ll_ref):
    lr = lr_ref[...]
    li = li_ref[...]
    dt = jnp.exp(ldt_ref[...])
    kk = lax.broadcasted_iota(jnp.int32, (24, 1), 0).astype(F32)
    mag = jnp.exp(kk * (lr * dt))
    ang = kk * (li * dt)
    er = mag * jnp.cos(ang)
    ei = mag * jnp.sin(ang)
    nr = er[1:2] - 1.0
    ni = ei[1:2]
    den = lr * lr + li * li
    cfr = (nr * lr + ni * li) / den
    cfi = (ni * lr - nr * li) / den
    btr = btr_ref[...]
    bti = bti_ref[...]
    bbr = btr * cfr - bti * cfi
    bbi = btr * cfi + bti * cfr
    cr = cr_ref[...]
    ci = ci_ref[...]
    a_r, a_i, cm_r, cm_i, c0_r, c0_i = [], [], [], [], [], []
    for k in range(S5_CHUNK):
        e_r, e_i = er[k:k + 1], ei[k:k + 1]
        a_r.append(bbr * e_r - bbi * e_i)
        a_i.append(bbr * e_i + bbi * e_r)
        c0_r.append(cr * e_r - ci * e_i)
        c0_i.append(cr * e_i + ci * e_r)
        f_r, f_i = er[k + 1:k + 2], ei[k + 1:k + 2]
        cm_r.append(cr * f_r - ci * f_i)
        cm_i.append(-(cr * f_i + ci * f_r))
    ar_ref[...] = jnp.concatenate(a_r, axis=0)
    ai_ref[...] = jnp.concatenate(a_i, axis=0)
    cmr_ref[...] = jnp.concatenate(cm_r, axis=0)
    cmi_ref[...] = jnp.concatenate(cm_i, axis=0)
    c0r = jnp.concatenate(c0_r, axis=0)
    c0i = jnp.concatenate(c0_i, axis=0)

    def dot3_nt(a, b):
        a0 = a.astype(BF16)
        a1 = (a - a0.astype(F32)).astype(BF16)
        b0 = b.astype(BF16)
        b1 = (b - b0.astype(F32)).astype(BF16)
        return _dot_nt(a0, b0) + _dot_nt(a0, b1) + _dot_nt(a1, b0)

    k_ref[...] = dot3_nt(c0r, bbr) - dot3_nt(c0i, bbi)
    ll_ref[...] = jnp.concatenate([er[S5_CHUNK:S5_CHUNK + 1], ei[S5_CHUNK:S5_CHUNK + 1]], axis=0)


def _s5_tables_call(lam_re, lam_im, log_dt, b_re, b_im, c_re, c_im):
    _, g, p = lam_re.shape
    cg = S5_GROUP
    rows = S5_CHUNK * cg
    row4 = lambda a: a.reshape(2, g, 1, p)
    spec = lambda r, c: pl.BlockSpec((None, None, r, c), lambda d, gi: (d, gi, 0, 0))
    ldt = jnp.broadcast_to(log_dt[:, :, None, None], (2, g, 1, p))
    return pl.pallas_call(
        _s5_tables_body,
        grid=(2, g),
        in_specs=[spec(1, p), spec(1, p), spec(1, p), spec(cg, p), spec(cg, p), spec(cg, p), spec(cg, p)],
        out_specs=[spec(rows, p), spec(rows, p), spec(rows, p), spec(rows, p), spec(rows, cg), spec(2, p)],
        out_shape=[jax.ShapeDtypeStruct((2, g, rows, p), F32)] * 4
                  + [jax.ShapeDtypeStruct((2, g, rows, cg), F32), jax.ShapeDtypeStruct((2, g, 2, p), F32)],
        compiler_params=_params("arbitrary", "arbitrary"),
        name="s5_tables",
    )(row4(lam_re), row4(lam_im), ldt, jnp.swapaxes(b_re, -1, -2), jnp.swapaxes(b_im, -1, -2), c_re, c_im)


def _s5_operators(tables):
    a_r, a_i, cm_r, cm_i, kk, ll = tables
    g = a_r.shape[1]
    n, cg, p = S5_CHUNK, S5_GROUP, S5_STATE

    def win(a):
        a = a.reshape(2, g, n, cg, p)
        return jnp.concatenate([a[0][:, ::-1].reshape(g, n * cg, p), a[1].reshape(g, n * cg, p)], -1).astype(BF16)

    def wout(c):
        c = c.reshape(2, g, n, cg, p)
        full = jnp.concatenate([c[0].reshape(g, n * cg, p), c[1][:, ::-1].reshape(g, n * cg, p)], -1)
        return jnp.swapaxes(full, 1, 2).astype(BF16)

    k5 = kk.reshape(2, g, n, cg, cg)
    s_i = jnp.arange(n)[:, None]
    t_i = jnp.arange(n)[None, :]

    def toeplitz(k4, lag, valid):
        blk = k4[:, jnp.clip(lag, 0, n - 1)]
        blk = jnp.where(valid[None, :, :, None, None], blk, 0.0)
        return blk.transpose(0, 1, 4, 2, 3).reshape(g, n * cg, n * cg).astype(BF16)

    ktf = toeplitz(k5[0], t_i - s_i, t_i >= s_i)
    ktb = toeplitz(k5[1], s_i - t_i, s_i >= t_i)
    gb = S5_GROUPS_PER_STEP
    lam_r = jnp.concatenate([ll[0, :, 0], ll[1, :, 0]], -1).reshape(g // gb, 1, gb * 2 * p)
    lam_i = jnp.concatenate([ll[0, :, 1], ll[1, :, 1]], -1).reshape(g // gb, 1, gb * 2 * p)
    return ktf, ktb, win(a_r), win(a_i), wout(cm_r), wout(cm_i), lam_r, lam_i


def _s5_gather_chunks(u_ref, l_ref):
    lanes = u_ref.shape[2]
    for s in range(S5_CHUNK):
        l_ref[:, s * lanes:(s + 1) * lanes] = u_ref[s]


def _s5_in_body(u_ref, pm_ref, wr_ref, wi_ref, ub_ref, x_ref, l_ref):
    @pl.when(pl.program_id(1) == 0)
    def _():
        _s5_gather_chunks(u_ref, l_ref)

    ub = _dot(l_ref[...], pm_ref[...]).astype(BF16)
    ub_ref[...] = ub
    sw = 2 * S5_STATE
    for q in range(ub.shape[1] // S5_TILE):
        ug = ub[:, q * S5_TILE:(q + 1) * S5_TILE]
        x_ref[:, 2 * q * sw:(2 * q + 1) * sw] = _dot(ug, wr_ref[q])
        x_ref[:, (2 * q + 1) * sw:(2 * q + 2) * sw] = _dot(ug, wi_ref[q])


def _s5_out_body(ub_ref, s_ref, ktf_ref, ktb_ref, wor_ref, woi_ref, pmt_ref, y_ref, y8_ref, *, col_tiles):
    n = pl.program_id(1)
    sw = 2 * S5_STATE

    @pl.when(n == 0)
    def _():
        for g in range(ub_ref.shape[1] // S5_TILE):
            ug = ub_ref[:, g * S5_TILE:(g + 1) * S5_TILE]
            s_r = s_ref[:, 2 * g * sw:(2 * g + 1) * sw]
            s_i = s_ref[:, (2 * g + 1) * sw:(2 * g + 2) * sw]
            y = _dot(ug, ktf_ref[g]) + _dot(ug, ktb_ref[g]) + _dot(s_r, wor_ref[g]) + _dot(s_i, woi_ref[g])
            y8_ref[:, g * S5_TILE:(g + 1) * S5_TILE] = y.astype(BF16)

    yp = _dot(y8_ref[...], pmt_ref[...])
    lanes = y_ref.shape[2]
    per = yp.shape[1] // lanes
    for nn in range(col_tiles):
        @pl.when(n == nn)
        def _(nn=nn):
            for q in range(per):
                y_ref[nn * per + q] = yp[:, q * lanes:(q + 1) * lanes].astype(y_ref.dtype)


def _s5_scan_body(x_ref, lr_ref, li_ref, s_ref, *scratch, nb, nch, nctx):
    gb = S5_GROUPS_PER_STEP
    sw = 2 * S5_STATE
    xr, xi, sfr, sfi, sbr, sbi = (scratch[k * gb:(k + 1) * gb] for k in range(6))
    for g in range(gb):
        xr[g][...] = x_ref[:, 2 * g * sw:(2 * g + 1) * sw]
        xi[g][...] = x_ref[:, (2 * g + 1) * sw:(2 * g + 2) * sw]
    is_f = lax.broadcasted_iota(jnp.int32, (1, sw), 1) < S5_STATE
    ar = [lr_ref[:, g * sw:(g + 1) * sw] for g in range(gb)]
    ai = [li_ref[:, g * sw:(g + 1) * sw] for g in range(gb)]

    def step(i, carry):
        cb = jnp.where(i < nctx, nctx - 1 - i, nch - 1 - (i - nctx))
        at_f = pl.ds(i, nb, stride=nch)
        at_b = pl.ds(cb, nb, stride=nch)
        new = []
        for g in range(gb):
            sr, si = carry[2 * g], carry[2 * g + 1]
            sfr[g][at_f, :] = sr
            sfi[g][at_f, :] = si
            sbr[g][at_b, :] = sr
            sbi[g][at_b, :] = si
            inr = jnp.where(is_f, xr[g][at_f, :], xr[g][at_b, :])
            ini = jnp.where(is_f, xi[g][at_f, :], xi[g][at_b, :])
            new.append(ar[g] * sr - ai[g] * si + inr)
            new.append(ar[g] * si + ai[g] * sr + ini)
        return tuple(new)

    zero = jnp.zeros((nb, sw), F32)
    lax.fori_loop(0, nch, step, (zero,) * (2 * gb))
    for g in range(gb):
        s_ref[:, 2 * g * sw:(2 * g + 1) * sw] = jnp.where(is_f, sfr[g][...], sbr[g][...]).astype(BF16)
        s_ref[:, (2 * g + 1) * sw:(2 * g + 2) * sw] = jnp.where(is_f, sfi[g][...], sbi[g][...]).astype(BF16)


def _s5_call(u_slabs, ops, nb, period, ctx):
    ktf, ktb, wir, wii, wor, woi, lam_r, lam_i = ops
    _, rows, width = u_slabs.shape
    groups = width // S5_GROUP
    lanes = 128
    lg = lanes // S5_GROUP
    nblk = width // lanes
    blk_w = S5_CHUNK * lanes
    col = 2 * S5_TILE
    col_tiles = blk_w // col
    sw = 2 * S5_STATE
    gb = S5_GROUPS_PER_STEP
    nch = period // S5_CHUNK
    nctx = ctx // S5_CHUNK
    src = jnp.arange(blk_w)
    dst = (src % lanes // S5_GROUP) * S5_TILE + (src // lanes) * S5_GROUP + src % S5_GROUP
    perm = (dst[:, None] == jnp.arange(blk_w)[None, :]).astype(BF16)
    perm_t = perm.T

    u_perm, xin = pl.pallas_call(
        _s5_in_body,
        grid=(nblk, col_tiles),
        in_specs=[pl.BlockSpec((S5_CHUNK, rows, lanes), lambda j, n: (0, 0, j)),
                  pl.BlockSpec((blk_w, col), lambda j, n: (0, n)),
                  pl.BlockSpec((col // S5_TILE, S5_TILE, sw), lambda j, n: (j * col_tiles + n, 0, 0)),
                  pl.BlockSpec((col // S5_TILE, S5_TILE, sw), lambda j, n: (j * col_tiles + n, 0, 0))],
        out_specs=[pl.BlockSpec((rows, col), lambda j, n: (0, j * col_tiles + n)),
                   pl.BlockSpec((rows, col), lambda j, n: (0, j * col_tiles + n))],
        out_shape=[jax.ShapeDtypeStruct((rows, groups * S5_TILE), BF16),
                   jax.ShapeDtypeStruct((rows, groups * 2 * sw), F32)],
        scratch_shapes=[pltpu.VMEM((rows, blk_w), BF16)],
        compiler_params=_params("arbitrary", "arbitrary"),
        name="s5_in",
    )(u_slabs, perm, wir, wii)

    states = pl.pallas_call(
        functools.partial(_s5_scan_body, nb=nb, nch=nch, nctx=nctx),
        grid=(groups // gb,),
        in_specs=[pl.BlockSpec((rows, gb * 2 * sw), lambda i: (0, i)),
                  pl.BlockSpec((None, 1, gb * sw), lambda i: (i, 0, 0)),
                  pl.BlockSpec((None, 1, gb * sw), lambda i: (i, 0, 0))],
        out_specs=pl.BlockSpec((rows, gb * 2 * sw), lambda i: (0, i)),
        out_shape=jax.ShapeDtypeStruct((rows, groups * 2 * sw), BF16),
        scratch_shapes=[pltpu.VMEM((rows, sw), F32) for _ in range(6 * gb)],
        compiler_params=_params("arbitrary"),
        name="s5_scan",
    )(xin, lam_r, lam_i)

    wspec = lambda r, c: pl.BlockSpec((lg, r, c), lambda j, n: (j, 0, 0))
    return pl.pallas_call(
        functools.partial(_s5_out_body, col_tiles=col_tiles),
        grid=(nblk, col_tiles),
        in_specs=[pl.BlockSpec((rows, lg * S5_TILE), lambda j, n: (0, j)),
                  pl.BlockSpec((rows, lg * 2 * sw), lambda j, n: (0, j)),
                  wspec(S5_TILE, S5_TILE), wspec(S5_TILE, S5_TILE), wspec(sw, S5_TILE), wspec(sw, S5_TILE),
                  pl.BlockSpec((blk_w, col), lambda j, n: (0, n))],
        out_specs=pl.BlockSpec((S5_CHUNK, rows, lanes), lambda j, n: (0, 0, j)),
        out_shape=jax.ShapeDtypeStruct((S5_CHUNK, rows, width), BF16),
        scratch_shapes=[pltpu.VMEM((rows, lg * S5_TILE), BF16)],
        compiler_params=_params("arbitrary", "arbitrary"),
        name="s5_out",
    )(u_perm, states, ktf, ktb, wor, woi, perm_t)


def _merge_body(o_ref, z_ref, u_ref, y_ref, h_ref, mb_ref, mc_ref, gn_ref, ds_ref, wglu_ref, bglu_ref,
                wout_ref, lng_ref, lnb_ref, out_ref, cat_ref, *, tm, tpb, ctx):
    rib = _rows_in_batch(pl.program_id(0), tm, tpb)
    gn = gn_ref[...]
    for hh in range(GDN_HEADS):
        sl = slice(hh * HEAD_DIM, (hh + 1) * HEAD_DIM)
        oh = o_ref[:, sl]
        r = lax.rsqrt(jnp.mean(oh * oh, -1, keepdims=True) + RMS_EPS)
        cat_ref[:, sl] = (oh * r * gn * _silu(z_ref[:, sl].astype(F32))).astype(BF16)
    s = y_ref[...].astype(F32) + ds_ref[...] * u_ref[...].astype(F32)
    s = 0.5 * s * (1.0 + jnp.tanh(math.sqrt(2.0 / math.pi) * (s + 0.044715 * (s * s * s))))
    s = s * jax.nn.sigmoid(_dot(s.astype(BF16), wglu_ref[...]) + bglu_ref[...])
    cat_ref[:, GDN_WIDTH:] = s.astype(BF16)
    mix = _dot(cat_ref[...], wout_ref[...])
    gate = jnp.where(rib < ctx, mc_ref[2:3, :], mb_ref[2:3, :])
    out_ref[...] = _layernorm(ALPHA * h_ref[...] + gate * mix, lng_ref[...], lnb_ref[...])


def _merge_call(o, proj, y, h, mod, gn, d_skip, w_glu, b_glu, w_out, ln_g, ln_b, nb, period, ctx):
    t, d = h.shape
    w = GDN_WIDTH
    sw = d - w
    tm = _pick(period, 272, 16)
    tpb = period // tm
    body = functools.partial(_merge_body, tm=tm, tpb=tpb, ctx=ctx)
    const = lambda r, c: pl.BlockSpec((r, c), lambda i: (0, 0))
    return pl.pallas_call(
        body,
        grid=(t // tm,),
        in_specs=[pl.BlockSpec((tm, w), lambda i: (i, 0)),
                  pl.BlockSpec((tm, w), lambda i: (i, 3)),
                  pl.BlockSpec((tm, sw), lambda i: (i, 4 * w // sw)),
                  pl.BlockSpec((tm, sw), lambda i: (i, 0)),
                  pl.BlockSpec((tm, d), lambda i: (i, 0)),
                  pl.BlockSpec((None, 6, d), lambda i: (i // tpb, 0, 0)),
                  pl.BlockSpec((None, 6, d), lambda i: (nb, 0, 0)),
                  const(1, HEAD_DIM), const(1, sw), const(sw, sw), const(1, sw), const(d, d),
                  const(1, d), const(1, d)],
        out_specs=pl.BlockSpec((tm, d), lambda i: (i, 0)),
        out_shape=jax.ShapeDtypeStruct((t, d), F32),
        scratch_shapes=[pltpu.VMEM((tm, d), BF16)],
        compiler_params=_params("arbitrary"),
        name="even_merge",
    )(o, proj, proj, y, h, mod, mod, gn, d_skip, w_glu, b_glu, w_out, ln_g, ln_b)


def _ffn_body(x_ref, xh_ref, mb_ref, mc_ref, wv_ref, wg_ref, cwv_ref, cwg_ref, cbv_ref, cbg_ref, wd_ref,
              lng_ref, lnb_ref, o_ref, xb_ref, *up_refs, tm, tpb, ctx, period):
    i = pl.program_id(0)
    j = pl.program_id(1)
    rib = _rows_in_batch(i, tm, tpb)
    n = tm + 32

    @pl.when(j == 0)
    def _():
        xb_ref[16:tm + 16, :] = _modulate(x_ref[...], rib < ctx, mb_ref, mc_ref, 3, 4).astype(BF16)
        base = (i % tpb) * tm
        off = lax.broadcasted_iota(jnp.int32, (16, 1), 0)
        xh = xh_ref[...]
        xb_ref[tm + 16:n, :] = _modulate(xh[0:16], base + tm + off < ctx, mb_ref, mc_ref, 3, 4).astype(BF16)
        xb_ref[0:16, :] = _modulate(xh[16:32], base - 16 + off < ctx, mb_ref, mc_ref, 3, 4).astype(BF16)
        o_ref[...] = jnp.zeros_like(o_ref)

    has_prev, has_next = _seq_edges(rib, ctx, period)
    xb = xb_ref[...]

    def conv(u_ref, cw_ref, cb_ref, sl):
        cw = cw_ref[:, sl]
        return (jnp.where(has_prev, u_ref[15:tm + 15, :], 0.0) * cw[0:1] + u_ref[16:tm + 16, :] * cw[1:2]
                + jnp.where(has_next, u_ref[17:tm + 17, :], 0.0) * cw[2:3] + cb_ref[:, sl])

    fc = wv_ref.shape[1]
    slabs = [slice(a, a + FFN_SLAB) for a in range(0, fc, FFN_SLAB)]
    for si, sl in enumerate(slabs):
        up_refs[2 * si][...] = _dot(xb, wv_ref[:, sl])
        up_refs[2 * si + 1][...] = _dot(xb, wg_ref[:, sl])
    for si, sl in enumerate(slabs):
        act = conv(up_refs[2 * si], cwv_ref, cbv_ref, sl) * _silu(conv(up_refs[2 * si + 1], cwg_ref, cbg_ref, sl))
        o_ref[...] += _dot(act.astype(BF16), wd_ref[sl, :])

    @pl.when(j == pl.num_programs(1) - 1)
    def _():
        g = jnp.where(rib < ctx, mc_ref[5:6, :], mb_ref[5:6, :])
        o_ref[...] = _layernorm(ALPHA * x_ref[...] + g * o_ref[...], lng_ref[...], lnb_ref[...])


def _ffn_call(h, mod, layer, w_up, conv_w, conv_b, w_down, ln_g, ln_b, nb, period, ctx):
    t, d = h.shape
    f = w_down.shape[1]
    tm = _pick(period, 544, 16)
    tpb = period // tm
    nt = t // tm
    fc = _pick(f, FFN_CHUNK, FFN_SLAB)
    nf = f // fc
    h3 = h.reshape(nt, tm, d)
    pad = jnp.zeros((1, 16, d), h.dtype)
    halo = jnp.concatenate([jnp.concatenate([h3[1:, :16], pad], 0),
                            jnp.concatenate([pad, h3[:-1, tm - 16:]], 0)], axis=1)
    body = functools.partial(_ffn_body, tm=tm, tpb=tpb, ctx=ctx, period=period)
    return pl.pallas_call(
        body,
        grid=(nt, nf),
        in_specs=[pl.BlockSpec((tm, d), lambda i, j: (i, 0)),
                  pl.BlockSpec((None, 32, d), lambda i, j: (i, 0, 0)),
                  pl.BlockSpec((None, 6, d), lambda i, j: (i // tpb, 0, 0)),
                  pl.BlockSpec((None, 6, d), lambda i, j: (nb, 0, 0)),
                  pl.BlockSpec((None, d, fc), lambda i, j: (layer, 0, j)),
                  pl.BlockSpec((None, d, fc), lambda i, j: (layer, 0, nf + j)),
                  pl.BlockSpec((None, 3, fc), lambda i, j: (layer, 0, j)),
                  pl.BlockSpec((None, 3, fc), lambda i, j: (layer, 0, nf + j)),
                  pl.BlockSpec((None, 1, fc), lambda i, j: (layer, 0, j)),
                  pl.BlockSpec((None, 1, fc), lambda i, j: (layer, 0, nf + j)),
                  pl.BlockSpec((None, fc, d), lambda i, j: (layer, j, 0)),
                  pl.BlockSpec((1, d), lambda i, j: (0, 0)),
                  pl.BlockSpec((1, d), lambda i, j: (0, 0))],
        out_specs=pl.BlockSpec((tm, d), lambda i, j: (i, 0)),
        out_shape=jax.ShapeDtypeStruct((t, d), F32),
        scratch_shapes=([pltpu.VMEM((tm + 32, d), BF16)]
                        + [pltpu.VMEM((tm + 32, FFN_SLAB), F32) for _ in range(2 * fc // FFN_SLAB)]),
        compiler_params=_params("arbitrary", "arbitrary"),
        name="conv_ffn",
    )(h, halo, mod, mod, w_up, w_up, conv_w, conv_w, conv_b, conv_b, w_down, ln_g, ln_b)


def _qkv_body(x_ref, mb_ref, mc_ref, w_ref, cos_ref, sin_ref, qn_ref, kn_ref, o_ref, xb_ref,
              *, tm, tpb, ctx, nq_tiles):
    i = pl.program_id(0)
    j = pl.program_id(1)

    @pl.when(j == 0)
    def _():
        rib = _rows_in_batch(i, tm, tpb)
        xb_ref[...] = _modulate(x_ref[...], rib < ctx, mb_ref, mc_ref, 0, 1).astype(BF16)

    acc = _dot(xb_ref[...], w_ref[...])
    cos = cos_ref[...]
    sin = sin_ref[...]
    lane = lax.broadcasted_iota(jnp.int32, (1, HEAD_DIM), 1)
    first = (lane % 64) < 32
    heads = acc.shape[1] // HEAD_DIM

    def normrope(xh, wn, scale):
        xn = xh * lax.rsqrt(jnp.mean(xh * xh, -1, keepdims=True) + RMS_EPS) * wn
        partner = jnp.where(first, pltpu.roll(xn, HEAD_DIM - 32, 1), pltpu.roll(xn, 32, 1))
        return (xn * cos + partner * sin) * scale

    @pl.when(j < nq_tiles)
    def _():
        for hh in range(heads):
            sl = slice(hh * HEAD_DIM, (hh + 1) * HEAD_DIM)
            o_ref[:, sl] = normrope(acc[:, sl], qn_ref[...], ATT_SCALE).astype(BF16)

    @pl.when(j == nq_tiles)
    def _():
        for hh in range(heads):
            sl = slice(hh * HEAD_DIM, (hh + 1) * HEAD_DIM)
            if hh < ATT_KV_HEADS:
                o_ref[:, sl] = normrope(acc[:, sl], kn_ref[...], 1.0).astype(BF16)
            else:
                o_ref[:, sl] = acc[:, sl].astype(BF16)


def _qkv_call(h, mod, w_in, cos, sin, qn, kn, nb, period, ctx):
    t, d = h.shape
    n = w_in.shape[1]
    tn = 2 * ATT_KV_HEADS * HEAD_DIM
    nq_tiles = (n - tn) // tn
    tm = _pick(period, 544, 16)
    tpb = period // tm
    body = functools.partial(_qkv_body, tm=tm, tpb=tpb, ctx=ctx, nq_tiles=nq_tiles)
    return pl.pallas_call(
        body,
        grid=(t // tm, n // tn),
        in_specs=[pl.BlockSpec((tm, d), lambda i, j: (i, 0)),
                  pl.BlockSpec((None, 6, d), lambda i, j: (i // tpb, 0, 0)),
                  pl.BlockSpec((None, 6, d), lambda i, j: (nb, 0, 0)),
                  pl.BlockSpec((d, tn), lambda i, j: (0, j)),
                  pl.BlockSpec((tm, HEAD_DIM), lambda i, j: (i % tpb, 0)),
                  pl.BlockSpec((tm, HEAD_DIM), lambda i, j: (i % tpb, 0)),
                  pl.BlockSpec((1, HEAD_DIM), lambda i, j: (0, 0)),
                  pl.BlockSpec((1, HEAD_DIM), lambda i, j: (0, 0))],
        out_specs=pl.BlockSpec((tm, tn), lambda i, j: (i, j)),
        out_shape=jax.ShapeDtypeStruct((t, n), BF16),
        scratch_shapes=[pltpu.VMEM((tm, d), BF16)],
        compiler_params=_params("arbitrary", "arbitrary"),
        name="odd_qkv",
    )(h, mod, mod, w_in, cos, sin, qn, kn)


def _rope_tables(seq, ctx):
    rows = seq // GRID_W
    row = jnp.repeat(jnp.arange(rows, dtype=F32), GRID_W)
    col = jnp.tile(jnp.arange(GRID_W, dtype=F32), rows)
    half = HEAD_DIM // 4
    inv = ROPE_THETA ** (-jnp.arange(half, dtype=F32) / half)
    ar = row[:, None] * inv
    ac = col[:, None] * inv
    cos = jnp.concatenate([jnp.cos(ar), jnp.cos(ar), jnp.cos(ac), jnp.cos(ac)], -1)
    sin = jnp.concatenate([-jnp.sin(ar), jnp.sin(ar), -jnp.sin(ac), jnp.sin(ac)], -1)
    cos = jnp.concatenate([jnp.ones((ctx, HEAD_DIM), F32), cos], 0)
    sin = jnp.concatenate([jnp.zeros((ctx, HEAD_DIM), F32), sin], 0)
    return cos, sin


def _attn_body(q_ref, k_ref, v_ref, o_ref, *, tq, key_blocks):
    hd = HEAD_DIM
    q = jnp.concatenate([q_ref[:, g * hd:(g + 1) * hd] for g in range(ATT_GROUP)], axis=0)
    rows = ATT_GROUP * tq
    m = jnp.full((rows, 1), -1e30, F32)
    l = jnp.zeros((rows, 1), F32)
    acc = jnp.zeros((rows, hd), F32)
    s_next = _dot_nt(q, k_ref[key_blocks[0][0]:key_blocks[0][1], :])
    for n, (k0, k1) in enumerate(key_blocks):
        s = s_next
        if n + 1 < len(key_blocks):
            s_next = _dot_nt(q, k_ref[key_blocks[n + 1][0]:key_blocks[n + 1][1], :])
        m_new = jnp.maximum(m, jnp.max(s, -1, keepdims=True))
        a = jnp.exp(m - m_new)
        p = jnp.exp(s - m_new)
        l = a * l + jnp.sum(p, -1, keepdims=True)
        acc = a * acc + _dot(p.astype(BF16), v_ref[k0:k1, :])
        m = m_new
    out = acc / l
    for g in range(ATT_GROUP):
        o_ref[:, g * hd:(g + 1) * hd] = out[g * tq:(g + 1) * tq].astype(o_ref.dtype)


def _key_blocks(period, count, align):
    tiles = period // align
    assert tiles * align == period and tiles >= count
    sizes = [tiles // count + (1 if n >= count - tiles % count else 0) for n in range(count)]
    edges = [0]
    for sz in sizes:
        edges.append(edges[-1] + sz * align)
    return tuple(zip(edges[:-1], edges[1:]))


def _attn_call(qkv, nb, seq, ctx):
    period = seq + ctx
    tq = _pick(math.gcd(seq, ctx), ATT_Q_TILE, 16)
    gw = ATT_GROUP * HEAD_DIM
    qpb = seq // tq
    rpb = period // tq
    skip = ctx // tq
    kcol = ATT_HEADS
    vcol = ATT_HEADS + ATT_KV_HEADS
    align = math.gcd(period, MXU_TILE)
    blocks = _key_blocks(period, min(ATT_KEY_BLOCKS, period // align), align)
    body = functools.partial(_attn_body, tq=tq, key_blocks=blocks)
    return pl.pallas_call(
        body,
        grid=(nb, ATT_KV_HEADS, qpb),
        in_specs=[pl.BlockSpec((tq, gw), lambda b, kv, qi: (b * rpb + skip + qi, kv)),
                  pl.BlockSpec((period, HEAD_DIM), lambda b, kv, qi: (b, kcol + kv)),
                  pl.BlockSpec((period, HEAD_DIM), lambda b, kv, qi: (b, vcol + kv))],
        out_specs=pl.BlockSpec((tq, gw), lambda b, kv, qi: (b * qpb + qi, kv)),
        out_shape=jax.ShapeDtypeStruct((nb * seq, ATT_HEADS * HEAD_DIM), BF16),
        compiler_params=_params("arbitrary", "arbitrary", "arbitrary"),
        name="gqa_attention",
    )(qkv, qkv, qkv)


def _outproj_body(a_ref, h_ref, mb_ref, w_ref, lng_ref, lnb_ref, o_ref):
    mix = _dot(a_ref[...], w_ref[...])
    o_ref[...] = _layernorm(ALPHA * h_ref[...] + mb_ref[2:3, :] * mix, lng_ref[...], lnb_ref[...])


def _outproj_call(att, h, mod, w_out, ln_g, ln_b, nb, seq, ctx):
    d = h.shape[1]
    period = seq + ctx
    tm = _pick(math.gcd(seq, ctx), 256, 16)
    qpb = seq // tm
    rpb = period // tm
    skip = ctx // tm
    return pl.pallas_call(
        _outproj_body,
        grid=(nb * qpb,),
        in_specs=[pl.BlockSpec((tm, d), lambda i: (i, 0)),
                  pl.BlockSpec((tm, d), lambda i: ((i // qpb) * rpb + skip + i % qpb, 0)),
                  pl.BlockSpec((None, 6, d), lambda i: (i // qpb, 0, 0)),
                  pl.BlockSpec((d, d), lambda i: (0, 0)),
                  pl.BlockSpec((1, d), lambda i: (0, 0)),
                  pl.BlockSpec((1, d), lambda i: (0, 0))],
        out_specs=pl.BlockSpec((tm, d), lambda i: (i, 0)),
        out_shape=jax.ShapeDtypeStruct((nb * seq, d), F32),
        compiler_params=_params("arbitrary"),
        name="odd_outproj",
    )(att, h, mod, w_out, ln_g, ln_b)


def kernel(x, c, ctx, c_ctx, mod_w, mod_b, ln1_g, ln1_b, ln2_g, ln2_b, ffn_w_up, ffn_conv_w, ffn_conv_b, ffn_w_down, e_w_in, e_conv_qkv, e_a_log, e_dt_bias, e_gdn_norm, e_lam_re, e_lam_im, e_log_dt, e_b_re, e_b_im, e_c_re, e_c_im, e_d_skip, e_w_glu, e_b_glu, e_w_out, o_w_in, o_q_norm, o_k_norm, o_w_out):
    nb, seq, d = x.shape
    nctx = ctx.shape[1]
    period = nctx + seq
    t = nb * period
    w = GDN_WIDTH
    assert mod_w.shape[0] == DEPTH == 2 and nb < 8

    h = jnp.concatenate([ctx, x], axis=1).reshape(t, d)
    cs = jnp.zeros((8, d), F32).at[:nb].set(c).at[nb].set(c_ctx)
    mod = _mod_call(cs, mod_w, mod_b).reshape(DEPTH, 8, 6, d)
    row = lambda a: a.reshape(1, -1)

    w_in = e_w_in[0]
    gates_at = 4 * w
    w_main = jnp.concatenate([w_in[:, :gates_at], w_in[:, gates_at + 4 * GDN_HEADS:]], axis=1).astype(BF16)
    w_ab = jnp.pad(w_in[:, gates_at:gates_at + 4 * GDN_HEADS], ((0, 0), (0, 128 - 4 * GDN_HEADS))).astype(BF16)
    proj, ab = _inproj_even_call(h, mod[0], w_main, w_ab, nb, period, nctx)
    pad_row = lambda a: jnp.pad(a.reshape(1, -1), ((0, 0), (0, 128 - 2 * GDN_HEADS)))
    qkvn, kt3, gh, gct3 = _gdn_prep_call(proj, ab, e_conv_qkv[0], pad_row(e_a_log[0]), pad_row(e_dt_bias[0]),
                                         period, nctx)
    o = _gdn_scan_call(qkvn, kt3, gh, gct3, nb, period, nctx)

    ops = _s5_operators(_s5_tables_call(e_lam_re[0], e_lam_im[0], e_log_dt[0], e_b_re[0], e_b_im[0],
                                        e_c_re[0], e_c_im[0]))
    u_slabs = proj[:, 4 * w:].reshape(t // S5_CHUNK, S5_CHUNK, d - w).transpose(1, 0, 2)
    y = _s5_call(u_slabs, ops, nb, period, nctx).transpose(1, 0, 2).reshape(t, d - w)

    h = _merge_call(o, proj, y, h, mod[0], row(e_gdn_norm[0]), row(e_d_skip[0]), e_w_glu[0].astype(BF16),
                    row(e_b_glu[0]), e_w_out[0].astype(BF16), row(ln1_g[0]), row(ln1_b[0]), nb, period, nctx)
    w_up, w_down = ffn_w_up.astype(BF16), ffn_w_down.astype(BF16)
    conv_b = ffn_conv_b[:, None, :]
    h = _ffn_call(h, mod[0], 0, w_up, ffn_conv_w, conv_b, w_down, row(ln2_g[0]), row(ln2_b[0]), nb, period, nctx)

    cos, sin = _rope_tables(seq, nctx)
    qkv = _qkv_call(h, mod[1], o_w_in[0].astype(BF16), cos, sin, row(o_q_norm[0]), row(o_k_norm[0]),
                    nb, period, nctx)
    att = _attn_call(qkv, nb, seq, nctx)
    hl = _outproj_call(att, h, mod[1], o_w_out[0].astype(BF16), row(ln1_g[1]), row(ln1_b[1]), nb, seq, nctx)
    out = _ffn_call(hl, mod[1], 1, w_up, ffn_conv_w, conv_b, w_down, row(ln2_g[1]), row(ln2_b[1]), nb, seq, 0)
    return out.reshape(nb, seq, d)
```

```python
import functools
import math

import jax
import jax.numpy as jnp
from jax import lax
from jax.experimental import pallas as pl
from jax.experimental.pallas import tpu as pltpu

F32 = jnp.float32
BF16 = jnp.bfloat16

DEPTH = 2
GDN_HEADS = 8
HEAD_DIM = 128
GDN_WIDTH = GDN_HEADS * HEAD_DIM
GDN_CHUNK = 64
GDN_INTRA_BATCH = 17
GDN_SOLVE_BASE = 8
S5_GROUP = 16
S5_STATE = 64
S5_CHUNK = 16
S5_TILE = S5_CHUNK * S5_GROUP
S5_GROUPS_PER_STEP = 4
S5_TABLE_GROUPS = 8
ATT_HEADS = 16
ATT_KV_HEADS = 4
ATT_GROUP = ATT_HEADS // ATT_KV_HEADS
ATT_SCALE = HEAD_DIM ** -0.5
Q_SCALE = ATT_SCALE * math.log2(math.e)
ATT_Q_TILE = 256
ATT_KEY_BLOCKS = 4
MXU_TILE = 256
GRID_W = 64
ROPE_THETA = 10000.0
PROJ_ROW_TILE = 1088
FFN_CHUNK = 512
FFN_SLAB = 256
ALPHA = (2 * DEPTH) ** 0.25
LN_EPS = 1e-6
RMS_EPS = 1e-6
L2_EPS = 1e-6
VMEM_LIMIT_BYTES = 56 * 1024 * 1024


def _pick(n, target, mult):
    best = None
    for d in range(mult, min(n, target) + 1, mult):
        if n % d == 0:
            best = d
    assert best is not None, (n, target, mult)
    return best


def _params(*sem):
    return pltpu.CompilerParams(dimension_semantics=sem, vmem_limit_bytes=VMEM_LIMIT_BYTES)


def _dot(a, b):
    return jnp.dot(a, b, preferred_element_type=F32)


def _dot_nt(a, b):
    return lax.dot_general(a, b, (((1,), (1,)), ((), ())), preferred_element_type=F32)


def _split3(a):
    p0 = a.astype(BF16)
    r = a - p0.astype(F32)
    p1 = r.astype(BF16)
    p2 = (r - p1.astype(F32)).astype(BF16)
    return p0, p1, p2


def _rows_in_batch(i, tm, tiles_per_batch):
    return (i % tiles_per_batch) * tm + lax.broadcasted_iota(jnp.int32, (tm, 1), 0)


def _modulate(x, is_ctx, mb_ref, mc_ref, shift_i, scale_i):
    scale = jnp.where(is_ctx, mc_ref[scale_i:scale_i + 1, :], mb_ref[scale_i:scale_i + 1, :])
    shift = jnp.where(is_ctx, mc_ref[shift_i:shift_i + 1, :], mb_ref[shift_i:shift_i + 1, :])
    return x * (1.0 + scale) + shift


def _layernorm(r, g, b):
    xc = r - jnp.mean(r, -1, keepdims=True)
    var = jnp.mean(xc * xc, -1, keepdims=True)
    return xc * lax.rsqrt(var + LN_EPS) * g + b


def _silu(x):
    return x * jax.nn.sigmoid(x)


def _seq_edges(rib, ctx, period):
    has_prev = (rib != 0) & (rib != ctx)
    has_next = (rib != ctx - 1) & (rib != period - 1)
    return has_prev, has_next


def _mod_body(c_ref, w_ref, b_ref, o_ref):
    s = _silu(c_ref[...])
    o_ref[...] = _dot(s.astype(BF16), w_ref[...].astype(BF16)) + b_ref[...]


def _mod_call(cs, mod_w, mod_b):
    depth, d, n = mod_w.shape
    tn = _pick(n, 1024, 128)
    return pl.pallas_call(
        _mod_body,
        grid=(depth, n // tn),
        in_specs=[pl.BlockSpec((8, d), lambda l, j: (0, 0)),
                  pl.BlockSpec((None, d, tn), lambda l, j: (l, 0, j)),
                  pl.BlockSpec((None, 1, tn), lambda l, j: (l, 0, j))],
        out_specs=pl.BlockSpec((None, 8, tn), lambda l, j: (l, 0, j)),
        out_shape=jax.ShapeDtypeStruct((depth, 8, n), F32),
        compiler_params=_params("arbitrary", "arbitrary"),
        name="adaln_mod",
    )(cs, mod_w, mod_b.reshape(depth, 1, n))


def _inproj_even_body(x_ref, mb_ref, mc_ref, w_ref, wab_ref, o_ref, ab_ref, xb_ref, *, tm, tpb, ctx):
    i = pl.program_id(0)

    @pl.when(pl.program_id(1) == 0)
    def _():
        rib = _rows_in_batch(i, tm, tpb)
        xb = _modulate(x_ref[...], rib < ctx, mb_ref, mc_ref, 0, 1).astype(BF16)
        xb_ref[...] = xb
        ab_ref[...] = _dot(xb, wab_ref[...])

    o_ref[...] = _dot(xb_ref[...], w_ref[...]).astype(o_ref.dtype)


def _inproj_even_call(h, mod, w_main, w_ab, nb, period, ctx):
    t, d = h.shape
    n = w_main.shape[1]
    tm = _pick(period, PROJ_ROW_TILE, 16)
    tn = _pick(n, 1280, 256)
    tpb = period // tm
    body = functools.partial(_inproj_even_body, tm=tm, tpb=tpb, ctx=ctx)
    return pl.pallas_call(
        body,
        grid=(t // tm, n // tn),
        in_specs=[pl.BlockSpec((tm, d), lambda i, j: (i, 0)),
                  pl.BlockSpec((None, 6, d), lambda i, j: (i // tpb, 0, 0)),
                  pl.BlockSpec((None, 6, d), lambda i, j: (nb, 0, 0)),
                  pl.BlockSpec((d, tn), lambda i, j: (0, j)),
                  pl.BlockSpec((d, 128), lambda i, j: (0, 0))],
        out_specs=[pl.BlockSpec((tm, tn), lambda i, j: (i, j)),
                   pl.BlockSpec((tm, 128), lambda i, j: (i, 0))],
        out_shape=[jax.ShapeDtypeStruct((t, n), BF16), jax.ShapeDtypeStruct((t, 128), F32)],
        scratch_shapes=[pltpu.VMEM((tm, d), BF16)],
        compiler_params=_params("arbitrary", "arbitrary"),
        name="even_inproj",
    )(h, mod, mod, w_main, w_ab)


def _gdn_prep_body(x_ref, xn_ref, xp_ref, cw_ref, ab_ref, alog_ref, dtb_ref, tril_ref, triu_ref, e_ref,
                   qkv_ref, kt_ref, gh_ref, gct_ref, *, tr, tpb, ctx, period):
    i = pl.program_id(0)
    sec = pl.program_id(1)
    rib = _rows_in_batch(i, tr, tpb)
    has_prev, has_next = _seq_edges(rib, ctx, period)
    ext = jnp.concatenate([x_ref[...].astype(F32), xn_ref[...].astype(F32), xp_ref[...].astype(F32)], axis=0)
    n = tr + 32
    xprev = pltpu.roll(ext, 1, 0)[:tr]
    xnext = pltpu.roll(ext, n - 1, 0)[:tr]
    cw = cw_ref[...]
    y = (jnp.where(has_prev, xprev, 0.0) * cw[0:1] + ext[:tr] * cw[1:2]
         + jnp.where(has_next, xnext, 0.0) * cw[2:3])
    y = _silu(y)

    def l2n(scale):
        parts = []
        for hh in range(GDN_HEADS):
            yh = y[:, hh * HEAD_DIM:(hh + 1) * HEAD_DIM]
            parts.append(yh * (lax.rsqrt(jnp.sum(yh * yh, -1, keepdims=True) + L2_EPS) * scale))
        return jnp.concatenate(parts, axis=1)

    @pl.when(sec == 0)
    def _():
        qkv_ref[...] = l2n(HEAD_DIM ** -0.5).astype(BF16)
        ab = ab_ref[...]
        lane = lax.broadcasted_iota(jnp.int32, (1, 128), 1)
        xg = ab + dtb_ref[...]
        softplus = jnp.maximum(xg, 0.0) + jnp.log1p(jnp.exp(-jnp.abs(xg)))
        g = -jnp.exp(alog_ref[...]) * softplus
        g0, g1, g2 = _split3(g)
        tril = tril_ref[...]
        triu = triu_ref[...]
        pre = _dot(tril, g0) + _dot(tril, g1) + _dot(tril, g2)
        suf = _dot(triu, g0) + _dot(triu, g1) + _dot(triu, g2)
        gc = jnp.where(lane < GDN_HEADS, pre, suf)
        vals = jnp.where(lane < 2 * GDN_HEADS, gc, jax.nn.sigmoid(ab))
        v0, v1, v2 = _split3(vals)
        e = e_ref[...]
        gh_ref[...] = _dot(v0, e) + _dot(v1, e) + _dot(v2, e)
        gct = gc.T
        for c in range(tr // GDN_CHUNK):
            gct_ref[c] = gct[:2 * GDN_HEADS, c * GDN_CHUNK:(c + 1) * GDN_CHUNK]

    @pl.when(sec == 1)
    def _():
        kn = l2n(1.0)
        qkv_ref[...] = kn.astype(BF16)
        knt = kn.T
        for c in range(tr // GDN_CHUNK):
            kt_ref[c] = knt[:, c * GDN_CHUNK:(c + 1) * GDN_CHUNK].astype(BF16)

    @pl.when(sec == 2)
    def _():
        qkv_ref[...] = y.astype(BF16)


def _gdn_prep_call(proj, ab, conv_w, alog_row, dtb_row, period, ctx):
    t = proj.shape[0]
    w = GDN_WIDTH
    tr = _pick(period, 256, GDN_CHUNK)
    tpb = period // tr
    hb = tr // 16
    nh = t // 16
    nck = tr // GDN_CHUNK
    r = jnp.arange(tr)
    same = (r[:, None] // GDN_CHUNK) == (r[None, :] // GDN_CHUNK)
    tril = (same & (r[:, None] >= r[None, :])).astype(BF16)
    triu = (same & (r[:, None] <= r[None, :])).astype(BF16)
    src = jnp.arange(128)[:, None]
    dst = jnp.arange(w)[None, :]
    expand = ((dst % HEAD_DIM < 4) & (src == (dst % HEAD_DIM) * GDN_HEADS + dst // HEAD_DIM)).astype(BF16)
    body = functools.partial(_gdn_prep_body, tr=tr, tpb=tpb, ctx=ctx, period=period)
    return pl.pallas_call(
        body,
        grid=(t // tr, 3),
        in_specs=[pl.BlockSpec((tr, w), lambda i, s: (i, s)),
                  pl.BlockSpec((16, w), lambda i, s: (jnp.minimum((i + 1) * hb, nh - 1), s)),
                  pl.BlockSpec((16, w), lambda i, s: (jnp.maximum(i * hb - 1, 0), s)),
                  pl.BlockSpec((3, w), lambda i, s: (0, s)),
                  pl.BlockSpec((tr, 128), lambda i, s: (i, 0)),
                  pl.BlockSpec((1, 128), lambda i, s: (0, 0)),
                  pl.BlockSpec((1, 128), lambda i, s: (0, 0)),
                  pl.BlockSpec((tr, tr), lambda i, s: (0, 0)),
                  pl.BlockSpec((tr, tr), lambda i, s: (0, 0)),
                  pl.BlockSpec((128, w), lambda i, s: (0, 0))],
        out_specs=[pl.BlockSpec((tr, w), lambda i, s: (i, s)),
                   pl.BlockSpec((nck, w, GDN_CHUNK), lambda i, s: (i, 0, 0)),
                   pl.BlockSpec((tr, w), lambda i, s: (i, 0)),
                   pl.BlockSpec((nck, 2 * GDN_HEADS, GDN_CHUNK), lambda i, s: (i, 0, 0))],
        out_shape=[jax.ShapeDtypeStruct((t, 3 * w), BF16),
                   jax.ShapeDtypeStruct((t // GDN_CHUNK, w, GDN_CHUNK), BF16),
                   jax.ShapeDtypeStruct((t, w), F32),
                   jax.ShapeDtypeStruct((t // GDN_CHUNK, 2 * GDN_HEADS, GDN_CHUNK), F32)],
        compiler_params=_params("arbitrary", "arbitrary"),
        name="gdn_prep",
    )(proj, proj, proj, conv_w, ab, alog_row, dtb_row, tril, triu, expand)


def _gdn_scan_body(q_ref, k_ref, v_ref, kt_ref, gh_ref, gct_ref, o_ref,
                   aqf, aqb, bsf, bsb, egf, egb, *, nch, nctx, batch):
    head = pl.program_id(1)
    cs = GDN_CHUNK
    ii = lax.broadcasted_iota(jnp.int32, (cs, cs), 0)
    jj = lax.broadcasted_iota(jnp.int32, (cs, cs), 1)
    masks = ((ii >= jj, ii > jj), (ii <= jj, ii < jj))
    same_block = {}
    width = GDN_SOLVE_BASE
    while width <= cs:
        shift = width.bit_length() - 1
        same_block[width] = jnp.right_shift(ii, shift) == jnp.right_shift(jj, shift)
        width *= 2
    hd = HEAD_DIM
    aqs, bss, egs = (aqf, aqb), (bsf, bsb), (egf, egb)

    def intra(first, count):
        chunks = [first + j for j in range(count)]
        r0s = [pl.multiple_of(c * cs, cs) for c in chunks]
        qs = [q_ref[pl.ds(r0, cs), :] for r0 in r0s]
        ks = [k_ref[pl.ds(r0, cs), :] for r0 in r0s]
        vs = [v_ref[pl.ds(r0, cs), :] for r0 in r0s]
        kts = [kt_ref[c] for c in chunks]
        raw_q = [_dot(q, kt) for q, kt in zip(qs, kts)]
        raw_k = [_dot(k, kt) for k, kt in zip(ks, kts)]
        pms, xs, atts, kps, qes, gends = [], [], [], [], [], []
        for j, c in enumerate(chunks):
            g4 = gh_ref[pl.ds(r0s[j], cs), :]
            qf, kf, vf = qs[j].astype(F32), ks[j].astype(F32), vs[j].astype(F32)
            ktf = kts[j].astype(F32)
            for d in (0, 1):
                incl, strict = masks[d]
                gcol = g4[:, d:d + 1]
                bcol = g4[:, 2 + d:3 + d]
                grow = gct_ref[c, pl.ds(d * GDN_HEADS + head, 1), :]
                decay = jnp.where(incl, jnp.exp(jnp.where(incl, gcol - grow, 0.0)), 0.0)
                eg = jnp.exp(gcol)
                gend = grow[:, cs - 1:cs] if d == 0 else grow[:, 0:1]
                pms.append(jnp.where(strict, -(raw_k[j] * bcol) * decay, 0.0))
                xs.append(jnp.concatenate([vf * bcol, kf * (bcol * eg)], axis=1))
                atts.append((raw_q[j] * decay).astype(BF16))
                kps.append((ktf * jnp.exp(gend - grow)).astype(BF16))
                qes.append(qf * eg)
                gends.append(gend)
        bfs = lambda vals: [a.astype(BF16) for a in vals]
        pds = bfs([jnp.where(same_block[GDN_SOLVE_BASE], pm, 0.0) for pm in pms])
        ns = [pd.astype(F32) for pd in pds]
        pw = pds
        span = 1
        while 2 * span < GDN_SOLVE_BASE:
            sq = [_dot(p, p) for p in pw]
            pw = bfs(sq)
            ns = [n + s + _dot(p, n.astype(BF16)) for n, s, p in zip(ns, sq, pw)]
            span *= 2
        width = GDN_SOLVE_BASE
        while width < cs:
            couple = same_block[2 * width] & ~same_block[width]
            qs_ = [jnp.where(couple, pm, 0.0) for pm in pms]
            nbs = bfs(ns)
            bs_ = [q + _dot(nb, q.astype(BF16)) for q, nb in zip(qs_, nbs)]
            ns = [n + b + _dot(b.astype(BF16), nb) for n, b, nb in zip(ns, bs_, nbs)]
            width *= 2
        xs = [x + _dot(n.astype(BF16), x.astype(BF16)) for n, x in zip(ns, xs)]
        xbs = [x.astype(BF16) for x in xs]
        aws = [_dot(att, xb) for att, xb in zip(atts, xbs)]
        kxs = [_dot(kp, xb) for kp, xb in zip(kps, xbs)]
        for j, c in enumerate(chunks):
            for d in (0, 1):
                n = 2 * j + d
                aqs[d][c, 0:hd, :] = (-kxs[n][:, hd:]).astype(BF16)
                aqs[d][c, hd:hd + cs, :] = (qes[n] - aws[n][:, hd:]).astype(BF16)
                bss[d][c] = kxs[n][:, :hd]
                egs[d][c] = jnp.broadcast_to(jnp.exp(gends[n]), (8, hd))
            o_ref[pl.ds(r0s[j], cs), :] = aws[2 * j][:, :hd] + aws[2 * j + 1][:, :hd]

    def inter(i, carry):
        cb = jnp.where(i < nctx, nctx - 1 - i, nch - 1 - (i - nctx))
        new = []
        for d, c in ((0, i), (1, cb)):
            s = carry[d]
            r0 = pl.multiple_of(c * cs, cs)
            r = _dot(aqs[d][c], s.astype(BF16))
            o_ref[pl.ds(r0, cs), :] += r[hd:]
            new.append(s * egs[d][c][0:1, :] + r[:hd] + bss[d][c])
        return tuple(new)

    full = nch // batch

    def intra_step(it, carry):
        intra(it * batch, batch)
        return carry

    lax.fori_loop(0, full, intra_step, 0)
    if nch > full * batch:
        intra(full * batch, nch - full * batch)
    zero = jnp.zeros((hd, hd), F32)
    lax.fori_loop(0, nch, inter, (zero, zero), unroll=2)


def _gdn_scan_call(qkvn, kt3, gh, gct3, nb, period, ctx):
    t = qkvn.shape[0]
    nch = period // GDN_CHUNK
    nctx = ctx // GDN_CHUNK
    hd = HEAD_DIM
    body = functools.partial(_gdn_scan_body, nch=nch, nctx=nctx, batch=min(nch, GDN_INTRA_BATCH))
    dirs2 = lambda shape, dt: [pltpu.VMEM(shape, dt), pltpu.VMEM(shape, dt)]
    return pl.pallas_call(
        body,
        grid=(nb, GDN_HEADS),
        in_specs=[pl.BlockSpec((period, hd), lambda b, h: (b, h)),
                  pl.BlockSpec((period, hd), lambda b, h: (b, GDN_HEADS + h)),
                  pl.BlockSpec((period, hd), lambda b, h: (b, 2 * GDN_HEADS + h)),
                  pl.BlockSpec((nch, hd, GDN_CHUNK), lambda b, h: (b, h, 0)),
                  pl.BlockSpec((period, hd), lambda b, h: (b, h)),
                  pl.BlockSpec((nch, 2 * GDN_HEADS, GDN_CHUNK), lambda b, h: (b, 0, 0))],
        out_specs=pl.BlockSpec((period, hd), lambda b, h: (b, h)),
        out_shape=jax.ShapeDtypeStruct((t, GDN_WIDTH), F32),
        scratch_shapes=(dirs2((nch, hd + GDN_CHUNK, hd), BF16) + dirs2((nch, hd, hd), F32)
                        + dirs2((nch, 8, hd), F32)),
        compiler_params=_params("arbitrary", "arbitrary"),
        name="gdn_scan",
    )(qkvn, qkvn, qkvn, kt3, gh, gct3)


def _s5_tables_body(lr_ref, li_ref, ldt_ref, btr_ref, bti_ref, cr_ref, ci_ref,
                    ar_ref, ai_ref, cmr_ref, cmi_ref, k_ref, ll_ref):
    refs = (lr_ref, li_ref, ldt_ref, btr_ref, bti_ref, cr_ref, ci_ref,
            ar_ref, ai_ref, cmr_ref, cmi_ref, k_ref, ll_ref)
    lax.fori_loop(0, lr_ref.shape[0], functools.partial(_s5_tables_group, refs), 0)


def _s5_tables_group(refs, q, carry):
    lr_ref, li_ref, ldt_ref, btr_ref, bti_ref, cr_ref, ci_ref, ar_ref, ai_ref, cmr_ref, cmi_ref, k_ref, ll_ref = refs
    lr = lr_ref[q]
    li = li_ref[q]
    dt = jnp.exp(ldt_ref[q])
    kk = lax.broadcasted_iota(jnp.int32, (24, 1), 0).astype(F32)
    mag = jnp.exp(kk * (lr * dt))
    ang = kk * (li * dt)
    er = mag * jnp.cos(ang)
    ei = mag * jnp.sin(ang)
    nr = er[1:2] - 1.0
    ni = ei[1:2]
    den = lr * lr + li * li
    cfr = (nr * lr + ni * li) / den
    cfi = (ni * lr - nr * li) / den
    btr = btr_ref[q]
    bti = bti_ref[q]
    bbr = btr * cfr - bti * cfi
    bbi = btr * cfi + bti * cfr
    cr = cr_ref[q]
    ci = ci_ref[q]
    a_r, a_i, cm_r, cm_i, c0_r, c0_i = [], [], [], [], [], []
    for k in range(S5_CHUNK):
        e_r, e_i = er[k:k + 1], ei[k:k + 1]
        a_r.append(bbr * e_r - bbi * e_i)
        a_i.append(bbr * e_i + bbi * e_r)
        c0_r.append(cr * e_r - ci * e_i)
        c0_i.append(cr * e_i + ci * e_r)
        f_r, f_i = er[k + 1:k + 2], ei[k + 1:k + 2]
        cm_r.append(cr * f_r - ci * f_i)
        cm_i.append(-(cr * f_i + ci * f_r))
    ar_ref[q] = jnp.concatenate(a_r, axis=0)
    ai_ref[q] = jnp.concatenate(a_i, axis=0)
    cmr_ref[q] = jnp.concatenate(cm_r, axis=0)
    cmi_ref[q] = jnp.concatenate(cm_i, axis=0)
    c0r = jnp.concatenate(c0_r, axis=0)
    c0i = jnp.concatenate(c0_i, axis=0)

    def dot3_nt(a, b):
        a0 = a.astype(BF16)
        a1 = (a - a0.astype(F32)).astype(BF16)
        b0 = b.astype(BF16)
        b1 = (b - b0.astype(F32)).astype(BF16)
        return _dot_nt(a0, b0) + _dot_nt(a0, b1) + _dot_nt(a1, b0)

    k_ref[q] = dot3_nt(c0r, bbr) - dot3_nt(c0i, bbi)
    ll_ref[q] = jnp.concatenate([er[S5_CHUNK:S5_CHUNK + 1], ei[S5_CHUNK:S5_CHUNK + 1]], axis=0)
    return carry


def _s5_tables_call(lam_re, lam_im, log_dt, b_re, b_im, c_re, c_im):
    _, g, p = lam_re.shape
    cg = S5_GROUP
    rows = S5_CHUNK * cg
    row4 = lambda a: a.reshape(2, g, 1, p)
    per = _pick(g, S5_TABLE_GROUPS, 1)
    spec = lambda r, c: pl.BlockSpec((None, per, r, c), lambda d, gi: (d, gi, 0, 0))
    ldt = jnp.broadcast_to(log_dt[:, :, None, None], (2, g, 1, p))
    return pl.pallas_call(
        _s5_tables_body,
        grid=(2, g // per),
        in_specs=[spec(1, p), spec(1, p), spec(1, p), spec(cg, p), spec(cg, p), spec(cg, p), spec(cg, p)],
        out_specs=[spec(rows, p), spec(rows, p), spec(rows, p), spec(rows, p), spec(rows, cg), spec(2, p)],
        out_shape=[jax.ShapeDtypeStruct((2, g, rows, p), F32)] * 4
                  + [jax.ShapeDtypeStruct((2, g, rows, cg), F32), jax.ShapeDtypeStruct((2, g, 2, p), F32)],
        compiler_params=_params("arbitrary", "arbitrary"),
        name="s5_tables",
    )(row4(lam_re), row4(lam_im), ldt, jnp.swapaxes(b_re, -1, -2), jnp.swapaxes(b_im, -1, -2), c_re, c_im)


def _s5_operators(tables):
    a_r, a_i, cm_r, cm_i, kk, ll = tables
    g = a_r.shape[1]
    n, cg, p = S5_CHUNK, S5_GROUP, S5_STATE

    def win(a):
        a = a.reshape(2, g, n, cg, p)
        return jnp.concatenate([a[0][:, ::-1].reshape(g, n * cg, p), a[1].reshape(g, n * cg, p)], -1).astype(BF16)

    def wout(c):
        c = c.reshape(2, g, n, cg, p)
        full = jnp.concatenate([c[0].reshape(g, n * cg, p), c[1][:, ::-1].reshape(g, n * cg, p)], -1)
        return jnp.swapaxes(full, 1, 2).astype(BF16)

    k5 = kk.reshape(2, g, n, cg, cg)
    s_i = jnp.arange(n)[:, None]
    t_i = jnp.arange(n)[None, :]

    def toeplitz(k4, lag, valid):
        blk = k4[:, jnp.clip(lag, 0, n - 1)]
        blk = jnp.where(valid[None, :, :, None, None], blk, 0.0)
        return blk.transpose(0, 1, 4, 2, 3).reshape(g, n * cg, n * cg).astype(BF16)

    ktf = toeplitz(k5[0], t_i - s_i, t_i >= s_i)
    ktb = toeplitz(k5[1], s_i - t_i, s_i >= t_i)
    gb = S5_GROUPS_PER_STEP
    lam_r = jnp.concatenate([ll[0, :, 0], ll[1, :, 0]], -1).reshape(g // gb, 1, gb * 2 * p)
    lam_i = jnp.concatenate([ll[0, :, 1], ll[1, :, 1]], -1).reshape(g // gb, 1, gb * 2 * p)
    return ktf, ktb, win(a_r), win(a_i), wout(cm_r), wout(cm_i), lam_r, lam_i


def _s5_gather_chunks(u_ref, l_ref):
    lanes = u_ref.shape[2]
    for s in range(S5_CHUNK):
        l_ref[:, s * lanes:(s + 1) * lanes] = u_ref[s]


def _s5_in_body(u_ref, pm_ref, wr_ref, wi_ref, ub_ref, x_ref, l_ref):
    @pl.when(pl.program_id(1) == 0)
    def _():
        _s5_gather_chunks(u_ref, l_ref)

    ub = _dot(l_ref[...], pm_ref[...]).astype(BF16)
    ub_ref[...] = ub
    sw = 2 * S5_STATE
    for q in range(ub.shape[1] // S5_TILE):
        ug = ub[:, q * S5_TILE:(q + 1) * S5_TILE]
        x_ref[:, 2 * q * sw:(2 * q + 1) * sw] = _dot(ug, wr_ref[q])
        x_ref[:, (2 * q + 1) * sw:(2 * q + 2) * sw] = _dot(ug, wi_ref[q])


def _s5_out_body(ub_ref, s_ref, ktf_ref, ktb_ref, wor_ref, woi_ref, pmt_ref, y_ref, y8_ref, *, col_tiles):
    n = pl.program_id(1)
    sw = 2 * S5_STATE

    @pl.when(n == 0)
    def _():
        for g in range(ub_ref.shape[1] // S5_TILE):
            ug = ub_ref[:, g * S5_TILE:(g + 1) * S5_TILE]
            s_r = s_ref[:, 2 * g * sw:(2 * g + 1) * sw]
            s_i = s_ref[:, (2 * g + 1) * sw:(2 * g + 2) * sw]
            y = _dot(ug, ktf_ref[g]) + _dot(ug, ktb_ref[g]) + _dot(s_r, wor_ref[g]) + _dot(s_i, woi_ref[g])
            y8_ref[:, g * S5_TILE:(g + 1) * S5_TILE] = y.astype(BF16)

    yp = _dot(y8_ref[...], pmt_ref[...])
    lanes = y_ref.shape[2]
    per = yp.shape[1] // lanes
    for nn in range(col_tiles):
        @pl.when(n == nn)
        def _(nn=nn):
            for q in range(per):
                y_ref[nn * per + q] = yp[:, q * lanes:(q + 1) * lanes].astype(y_ref.dtype)


def _s5_scan_body(x_ref, lr_ref, li_ref, s_ref, *scratch, nb, nch, nctx):
    gb = S5_GROUPS_PER_STEP
    sw = 2 * S5_STATE
    xr, xi, sfr, sfi, sbr, sbi = (scratch[k * gb:(k + 1) * gb] for k in range(6))
    for g in range(gb):
        xr[g][...] = x_ref[:, 2 * g * sw:(2 * g + 1) * sw]
        xi[g][...] = x_ref[:, (2 * g + 1) * sw:(2 * g + 2) * sw]
    is_f = lax.broadcasted_iota(jnp.int32, (1, sw), 1) < S5_STATE
    ar = [lr_ref[:, g * sw:(g + 1) * sw] for g in range(gb)]
    ai = [li_ref[:, g * sw:(g + 1) * sw] for g in range(gb)]

    def step(i, carry):
        cb = jnp.where(i < nctx, nctx - 1 - i, nch - 1 - (i - nctx))
        at_f = pl.ds(i, nb, stride=nch)
        at_b = pl.ds(cb, nb, stride=nch)
        new = []
        for g in range(gb):
            sr, si = carry[2 * g], carry[2 * g + 1]
            sfr[g][at_f, :] = sr
            sfi[g][at_f, :] = si
            sbr[g][at_b, :] = sr
            sbi[g][at_b, :] = si
            inr = jnp.where(is_f, xr[g][at_f, :], xr[g][at_b, :])
            ini = jnp.where(is_f, xi[g][at_f, :], xi[g][at_b, :])
            new.append(ar[g] * sr - ai[g] * si + inr)
            new.append(ar[g] * si + ai[g] * sr + ini)
        return tuple(new)

    zero = jnp.zeros((nb, sw), F32)
    lax.fori_loop(0, nch, step, (zero,) * (2 * gb))
    for g in range(gb):
        s_ref[:, 2 * g * sw:(2 * g + 1) * sw] = jnp.where(is_f, sfr[g][...], sbr[g][...]).astype(BF16)
        s_ref[:, (2 * g + 1) * sw:(2 * g + 2) * sw] = jnp.where(is_f, sfi[g][...], sbi[g][...]).astype(BF16)


def _s5_call(u_slabs, ops, nb, period, ctx):
    ktf, ktb, wir, wii, wor, woi, lam_r, lam_i = ops
    _, rows, width = u_slabs.shape
    groups = width // S5_GROUP
    lanes = 128
    lg = lanes // S5_GROUP
    nblk = width // lanes
    blk_w = S5_CHUNK * lanes
    col = 2 * S5_TILE
    col_tiles = blk_w // col
    sw = 2 * S5_STATE
    gb = S5_GROUPS_PER_STEP
    nch = period // S5_CHUNK
    nctx = ctx // S5_CHUNK
    src = jnp.arange(blk_w)
    dst = (src % lanes // S5_GROUP) * S5_TILE + (src // lanes) * S5_GROUP + src % S5_GROUP
    perm = (dst[:, None] == jnp.arange(blk_w)[None, :]).astype(BF16)
    perm_t = perm.T

    u_perm, xin = pl.pallas_call(
        _s5_in_body,
        grid=(nblk, col_tiles),
        in_specs=[pl.BlockSpec((S5_CHUNK, rows, lanes), lambda j, n: (0, 0, j)),
                  pl.BlockSpec((blk_w, col), lambda j, n: (0, n)),
                  pl.BlockSpec((col // S5_TILE, S5_TILE, sw), lambda j, n: (j * col_tiles + n, 0, 0)),
                  pl.BlockSpec((col // S5_TILE, S5_TILE, sw), lambda j, n: (j * col_tiles + n, 0, 0))],
        out_specs=[pl.BlockSpec((rows, col), lambda j, n: (0, j * col_tiles + n)),
                   pl.BlockSpec((rows, col), lambda j, n: (0, j * col_tiles + n))],
        out_shape=[jax.ShapeDtypeStruct((rows, groups * S5_TILE), BF16),
                   jax.ShapeDtypeStruct((rows, groups * 2 * sw), F32)],
        scratch_shapes=[pltpu.VMEM((rows, blk_w), BF16)],
        compiler_params=_params("arbitrary", "arbitrary"),
        name="s5_in",
    )(u_slabs, perm, wir, wii)

    states = pl.pallas_call(
        functools.partial(_s5_scan_body, nb=nb, nch=nch, nctx=nctx),
        grid=(groups // gb,),
        in_specs=[pl.BlockSpec((rows, gb * 2 * sw), lambda i: (0, i)),
                  pl.BlockSpec((None, 1, gb * sw), lambda i: (i, 0, 0)),
                  pl.BlockSpec((None, 1, gb * sw), lambda i: (i, 0, 0))],
        out_specs=pl.BlockSpec((rows, gb * 2 * sw), lambda i: (0, i)),
        out_shape=jax.ShapeDtypeStruct((rows, groups * 2 * sw), BF16),
        scratch_shapes=[pltpu.VMEM((rows, sw), F32) for _ in range(6 * gb)],
        compiler_params=_params("arbitrary"),
        name="s5_scan",
    )(xin, lam_r, lam_i)

    wspec = lambda r, c: pl.BlockSpec((lg, r, c), lambda j, n: (j, 0, 0))
    return pl.pallas_call(
        functools.partial(_s5_out_body, col_tiles=col_tiles),
        grid=(nblk, col_tiles),
        in_specs=[pl.BlockSpec((rows, lg * S5_TILE), lambda j, n: (0, j)),
                  pl.BlockSpec((rows, lg * 2 * sw), lambda j, n: (0, j)),
                  wspec(S5_TILE, S5_TILE), wspec(S5_TILE, S5_TILE), wspec(sw, S5_TILE), wspec(sw, S5_TILE),
                  pl.BlockSpec((blk_w, col), lambda j, n: (0, n))],
        out_specs=pl.BlockSpec((S5_CHUNK, rows, lanes), lambda j, n: (0, 0, j)),
        out_shape=jax.ShapeDtypeStruct((S5_CHUNK, rows, width), BF16),
        scratch_shapes=[pltpu.VMEM((rows, lg * S5_TILE), BF16)],
        compiler_params=_params("arbitrary", "arbitrary"),
        name="s5_out",
    )(u_perm, states, ktf, ktb, wor, woi, perm_t)


def _merge_body(o_ref, z_ref, u_ref, y_ref, h_ref, mb_ref, mc_ref, gn_ref, ds_ref, wglu_ref, bglu_ref,
                wout_ref, lng_ref, lnb_ref, out_ref, cat_ref, *, tm, tpb, ctx):
    rib = _rows_in_batch(pl.program_id(0), tm, tpb)
    gn = gn_ref[...]
    for hh in range(GDN_HEADS):
        sl = slice(hh * HEAD_DIM, (hh + 1) * HEAD_DIM)
        oh = o_ref[:, sl]
        r = lax.rsqrt(jnp.mean(oh * oh, -1, keepdims=True) + RMS_EPS)
        cat_ref[:, sl] = (oh * r * gn * _silu(z_ref[:, sl].astype(F32))).astype(BF16)
    s = y_ref[...].astype(F32) + ds_ref[...] * u_ref[...].astype(F32)
    s = 0.5 * s * (1.0 + jnp.tanh(math.sqrt(2.0 / math.pi) * (s + 0.044715 * (s * s * s))))
    s = s * jax.nn.sigmoid(_dot(s.astype(BF16), wglu_ref[...]) + bglu_ref[...])
    cat_ref[:, GDN_WIDTH:] = s.astype(BF16)
    mix = _dot(cat_ref[...], wout_ref[...])
    gate = jnp.where(rib < ctx, mc_ref[2:3, :], mb_ref[2:3, :])
    out_ref[...] = _layernorm(ALPHA * h_ref[...] + gate * mix, lng_ref[...], lnb_ref[...])


def _merge_call(o, proj, y, h, mod, gn, d_skip, w_glu, b_glu, w_out, ln_g, ln_b, nb, period, ctx):
    t, d = h.shape
    w = GDN_WIDTH
    sw = d - w
    tm = _pick(period, 272, 16)
    tpb = period // tm
    body = functools.partial(_merge_body, tm=tm, tpb=tpb, ctx=ctx)
    const = lambda r, c: pl.BlockSpec((r, c), lambda i: (0, 0))
    return pl.pallas_call(
        body,
        grid=(t // tm,),
        in_specs=[pl.BlockSpec((tm, w), lambda i: (i, 0)),
                  pl.BlockSpec((tm, w), lambda i: (i, 3)),
                  pl.BlockSpec((tm, sw), lambda i: (i, 4 * w // sw)),
                  pl.BlockSpec((tm, sw), lambda i: (i, 0)),
                  pl.BlockSpec((tm, d), lambda i: (i, 0)),
                  pl.BlockSpec((None, 6, d), lambda i: (i // tpb, 0, 0)),
                  pl.BlockSpec((None, 6, d), lambda i: (nb, 0, 0)),
                  const(1, HEAD_DIM), const(1, sw), const(sw, sw), const(1, sw), const(d, d),
                  const(1, d), const(1, d)],
        out_specs=pl.BlockSpec((tm, d), lambda i: (i, 0)),
        out_shape=jax.ShapeDtypeStruct((t, d), F32),
        scratch_shapes=[pltpu.VMEM((tm, d), BF16)],
        compiler_params=_params("arbitrary"),
        name="even_merge",
    )(o, proj, proj, y, h, mod, mod, gn, d_skip, w_glu, b_glu, w_out, ln_g, ln_b)


def _ffn_body(x_ref, xh_ref, mb_ref, mc_ref, wv_ref, wg_ref, cwv_ref, cwg_ref, cbv_ref, cbg_ref, wd_ref,
              lng_ref, lnb_ref, o_ref, xb_ref, *up_refs, tm, tpb, ctx, period):
    i = pl.program_id(0)
    j = pl.program_id(1)
    rib = _rows_in_batch(i, tm, tpb)
    n = tm + 32

    @pl.when(j == 0)
    def _():
        xb_ref[16:tm + 16, :] = _modulate(x_ref[...], rib < ctx, mb_ref, mc_ref, 3, 4).astype(BF16)
        base = (i % tpb) * tm
        off = lax.broadcasted_iota(jnp.int32, (16, 1), 0)
        xh = xh_ref[...]
        next_ok = (base + tm != period) & (base + tm != ctx)
        prev_ok = (base != 0) & (base != ctx)
        nxt = _modulate(xh[0:16], base + tm + off < ctx, mb_ref, mc_ref, 3, 4)
        prv = _modulate(xh[16:32], base - 16 + off < ctx, mb_ref, mc_ref, 3, 4)
        xb_ref[tm + 16:n, :] = jnp.where(next_ok, nxt, 0.0).astype(BF16)
        xb_ref[0:16, :] = jnp.where(prev_ok, prv, 0.0).astype(BF16)
        o_ref[...] = jnp.zeros_like(o_ref)

    xb = xb_ref[...]
    edge_inside_tile = ctx % tm != 0

    def conv(u_ref, cw_ref, cb_ref, sl):
        cw = cw_ref[:, sl]
        up = u_ref[15:tm + 15, :]
        un = u_ref[17:tm + 17, :]
        if edge_inside_tile:
            up = jnp.where(rib != ctx, up, 0.0)
            un = jnp.where(rib != ctx - 1, un, 0.0)
        return up * cw[0:1] + u_ref[16:tm + 16, :] * cw[1:2] + un * cw[2:3] + cb_ref[:, sl]

    fc = wv_ref.shape[1]
    slabs = [slice(a, a + FFN_SLAB) for a in range(0, fc, FFN_SLAB)]
    for si, sl in enumerate(slabs):
        up_refs[2 * si][...] = _dot(xb, wv_ref[:, sl])
        up_refs[2 * si + 1][...] = _dot(xb, wg_ref[:, sl])
    for si, sl in enumerate(slabs):
        act = conv(up_refs[2 * si], cwv_ref, cbv_ref, sl) * _silu(conv(up_refs[2 * si + 1], cwg_ref, cbg_ref, sl))
        o_ref[...] += _dot(act.astype(BF16), wd_ref[sl, :])

    @pl.when(j == pl.num_programs(1) - 1)
    def _():
        g = jnp.where(rib < ctx, mc_ref[5:6, :], mb_ref[5:6, :])
        o_ref[...] = _layernorm(ALPHA * x_ref[...] + g * o_ref[...], lng_ref[...], lnb_ref[...])


def _ffn_call(h, mod, layer, w_up, conv_w, conv_b, w_down, ln_g, ln_b, nb, period, ctx):
    t, d = h.shape
    f = w_down.shape[1]
    tm = _pick(period, 544, 16)
    tpb = period // tm
    nt = t // tm
    fc = _pick(f, FFN_CHUNK, FFN_SLAB)
    nf = f // fc
    h3 = h.reshape(nt, tm, d)
    pad = jnp.zeros((1, 16, d), h.dtype)
    halo = jnp.concatenate([jnp.concatenate([h3[1:, :16], pad], 0),
                            jnp.concatenate([pad, h3[:-1, tm - 16:]], 0)], axis=1)
    body = functools.partial(_ffn_body, tm=tm, tpb=tpb, ctx=ctx, period=period)
    return pl.pallas_call(
        body,
        grid=(nt, nf),
        in_specs=[pl.BlockSpec((tm, d), lambda i, j: (i, 0)),
                  pl.BlockSpec((None, 32, d), lambda i, j: (i, 0, 0)),
                  pl.BlockSpec((None, 6, d), lambda i, j: (i // tpb, 0, 0)),
                  pl.BlockSpec((None, 6, d), lambda i, j: (nb, 0, 0)),
                  pl.BlockSpec((None, d, fc), lambda i, j: (layer, 0, j)),
                  pl.BlockSpec((None, d, fc), lambda i, j: (layer, 0, nf + j)),
                  pl.BlockSpec((None, 3, fc), lambda i, j: (layer, 0, j)),
                  pl.BlockSpec((None, 3, fc), lambda i, j: (layer, 0, nf + j)),
                  pl.BlockSpec((None, 1, fc), lambda i, j: (layer, 0, j)),
                  pl.BlockSpec((None, 1, fc), lambda i, j: (layer, 0, nf + j)),
                  pl.BlockSpec((None, fc, d), lambda i, j: (layer, j, 0)),
                  pl.BlockSpec((1, d), lambda i, j: (0, 0)),
                  pl.BlockSpec((1, d), lambda i, j: (0, 0))],
        out_specs=pl.BlockSpec((tm, d), lambda i, j: (i, 0)),
        out_shape=jax.ShapeDtypeStruct((t, d), F32),
        scratch_shapes=([pltpu.VMEM((tm + 32, d), BF16)]
                        + [pltpu.VMEM((tm + 32, FFN_SLAB), F32) for _ in range(2 * fc // FFN_SLAB)]),
        compiler_params=_params("arbitrary", "arbitrary"),
        name="conv_ffn",
    )(h, halo, mod, mod, w_up, w_up, conv_w, conv_w, conv_b, conv_b, w_down, ln_g, ln_b)


def _qkv_body(x_ref, mb_ref, mc_ref, w_ref, cos_ref, sin_ref, qn_ref, kn_ref, o_ref, xb_ref,
              *, tm, tpb, ctx, nq_tiles):
    i = pl.program_id(0)
    j = pl.program_id(1)

    @pl.when(j == 0)
    def _():
        rib = _rows_in_batch(i, tm, tpb)
        xb_ref[...] = _modulate(x_ref[...], rib < ctx, mb_ref, mc_ref, 0, 1).astype(BF16)

    cos = cos_ref[...]
    sin = sin_ref[...]
    lane = lax.broadcasted_iota(jnp.int32, (1, HEAD_DIM), 1)
    first = (lane % 64) < 32
    acc = _dot(xb_ref[...], w_ref[...])
    heads = acc.shape[1] // HEAD_DIM

    def normrope(xh, wn, scale):
        xn = xh * lax.rsqrt(jnp.mean(xh * xh, -1, keepdims=True) + RMS_EPS) * wn
        partner = jnp.where(first, pltpu.roll(xn, HEAD_DIM - 32, 1), pltpu.roll(xn, 32, 1))
        return (xn * cos + partner * sin) * scale

    @pl.when(j < nq_tiles)
    def _():
        for hh in range(heads):
            sl = slice(hh * HEAD_DIM, (hh + 1) * HEAD_DIM)
            o_ref[:, sl] = normrope(acc[:, sl], qn_ref[...], Q_SCALE).astype(BF16)

    @pl.when(j == nq_tiles)
    def _():
        for hh in range(heads):
            sl = slice(hh * HEAD_DIM, (hh + 1) * HEAD_DIM)
            if hh < ATT_KV_HEADS:
                o_ref[:, sl] = normrope(acc[:, sl], kn_ref[...], 1.0).astype(BF16)
            else:
                o_ref[:, sl] = acc[:, sl].astype(BF16)


def _qkv_call(h, mod, w_in, cos, sin, qn, kn, nb, period, ctx):
    t, d = h.shape
    n = w_in.shape[1]
    tn = 2 * ATT_KV_HEADS * HEAD_DIM
    nq_tiles = (n - tn) // tn
    tm = _pick(period, PROJ_ROW_TILE, 16)
    tpb = period // tm
    body = functools.partial(_qkv_body, tm=tm, tpb=tpb, ctx=ctx, nq_tiles=nq_tiles)
    return pl.pallas_call(
        body,
        grid=(t // tm, n // tn),
        in_specs=[pl.BlockSpec((tm, d), lambda i, j: (i, 0)),
                  pl.BlockSpec((None, 6, d), lambda i, j: (i // tpb, 0, 0)),
                  pl.BlockSpec((None, 6, d), lambda i, j: (nb, 0, 0)),
                  pl.BlockSpec((d, tn), lambda i, j: (0, j)),
                  pl.BlockSpec((tm, HEAD_DIM), lambda i, j: (i % tpb, 0)),
                  pl.BlockSpec((tm, HEAD_DIM), lambda i, j: (i % tpb, 0)),
                  pl.BlockSpec((1, HEAD_DIM), lambda i, j: (0, 0)),
                  pl.BlockSpec((1, HEAD_DIM), lambda i, j: (0, 0))],
        out_specs=pl.BlockSpec((tm, tn), lambda i, j: (i, j)),
        out_shape=jax.ShapeDtypeStruct((t, n), BF16),
        scratch_shapes=[pltpu.VMEM((tm, d), BF16)],
        compiler_params=_params("arbitrary", "arbitrary"),
        name="odd_qkv",
    )(h, mod, mod, w_in, cos, sin, qn, kn)


def _rope_tables(seq, ctx):
    rows = seq // GRID_W
    row = jnp.repeat(jnp.arange(rows, dtype=F32), GRID_W)
    col = jnp.tile(jnp.arange(GRID_W, dtype=F32), rows)
    half = HEAD_DIM // 4
    inv = ROPE_THETA ** (-jnp.arange(half, dtype=F32) / half)
    ar = row[:, None] * inv
    ac = col[:, None] * inv
    cos = jnp.concatenate([jnp.cos(ar), jnp.cos(ar), jnp.cos(ac), jnp.cos(ac)], -1)
    sin = jnp.concatenate([-jnp.sin(ar), jnp.sin(ar), -jnp.sin(ac), jnp.sin(ac)], -1)
    cos = jnp.concatenate([jnp.ones((ctx, HEAD_DIM), F32), cos], 0)
    sin = jnp.concatenate([jnp.zeros((ctx, HEAD_DIM), F32), sin], 0)
    return cos, sin


def _attn_body(q_ref, k_ref, v_ref, o_ref, *, tq, key_blocks):
    hd = HEAD_DIM
    q = jnp.concatenate([q_ref[:, g * hd:(g + 1) * hd] for g in range(ATT_GROUP)], axis=0)
    rows = ATT_GROUP * tq
    m = jnp.full((rows, 1), -1e30, F32)
    l = jnp.zeros((rows, 1), F32)
    acc = jnp.zeros((rows, hd), F32)
    s_next = _dot_nt(q, k_ref[key_blocks[0][0]:key_blocks[0][1], :])
    for n, (k0, k1) in enumerate(key_blocks):
        s = s_next
        if n + 1 < len(key_blocks):
            s_next = _dot_nt(q, k_ref[key_blocks[n + 1][0]:key_blocks[n + 1][1], :])
        m_new = jnp.maximum(m, jnp.max(s, -1, keepdims=True))
        a = jnp.exp2(m - m_new)
        p = jnp.exp2(s - m_new)
        l = a * l + jnp.sum(p, -1, keepdims=True)
        acc = a * acc + _dot(p.astype(BF16), v_ref[k0:k1, :])
        m = m_new
    out = acc / l
    for g in range(ATT_GROUP):
        o_ref[:, g * hd:(g + 1) * hd] = out[g * tq:(g + 1) * tq].astype(o_ref.dtype)


def _key_blocks(period, count, align):
    tiles = period // align
    assert tiles * align == period and tiles >= count
    sizes = [tiles // count + (1 if n >= count - tiles % count else 0) for n in range(count)]
    edges = [0]
    for sz in sizes:
        edges.append(edges[-1] + sz * align)
    return tuple(zip(edges[:-1], edges[1:]))


def _attn_call(qkv, nb, seq, ctx):
    period = seq + ctx
    tq = _pick(math.gcd(seq, ctx), ATT_Q_TILE, 16)
    gw = ATT_GROUP * HEAD_DIM
    qpb = seq // tq
    rpb = period // tq
    skip = ctx // tq
    kcol = ATT_HEADS
    vcol = ATT_HEADS + ATT_KV_HEADS
    align = math.gcd(period, MXU_TILE)
    blocks = _key_blocks(period, min(ATT_KEY_BLOCKS, period // align), align)
    body = functools.partial(_attn_body, tq=tq, key_blocks=blocks)
    return pl.pallas_call(
        body,
        grid=(nb, ATT_KV_HEADS, qpb),
        in_specs=[pl.BlockSpec((tq, gw), lambda b, kv, qi: (b * rpb + skip + qi, kv)),
                  pl.BlockSpec((period, HEAD_DIM), lambda b, kv, qi: (b, kcol + kv)),
                  pl.BlockSpec((period, HEAD_DIM), lambda b, kv, qi: (b, vcol + kv))],
        out_specs=pl.BlockSpec((tq, gw), lambda b, kv, qi: (b * qpb + qi, kv)),
        out_shape=jax.ShapeDtypeStruct((nb * seq, ATT_HEADS * HEAD_DIM), BF16),
        compiler_params=_params("arbitrary", "arbitrary", "arbitrary"),
        name="gqa_attention",
    )(qkv, qkv, qkv)


def _outproj_body(a_ref, h_ref, mb_ref, w_ref, lng_ref, lnb_ref, o_ref):
    mix = _dot(a_ref[...], w_ref[...])
    o_ref[...] = _layernorm(ALPHA * h_ref[...] + mb_ref[2:3, :] * mix, lng_ref[...], lnb_ref[...])


def _outproj_call(att, h, mod, w_out, ln_g, ln_b, nb, seq, ctx):
    d = h.shape[1]
    period = seq + ctx
    tm = _pick(math.gcd(seq, ctx), 256, 16)
    qpb = seq // tm
    rpb = period // tm
    skip = ctx // tm
    return pl.pallas_call(
        _outproj_body,
        grid=(nb * qpb,),
        in_specs=[pl.BlockSpec((tm, d), lambda i: (i, 0)),
                  pl.BlockSpec((tm, d), lambda i: ((i // qpb) * rpb + skip + i % qpb, 0)),
                  pl.BlockSpec((None, 6, d), lambda i: (i // qpb, 0, 0)),
                  pl.BlockSpec((d, d), lambda i: (0, 0)),
                  pl.BlockSpec((1, d), lambda i: (0, 0)),
                  pl.BlockSpec((1, d), lambda i: (0, 0))],
        out_specs=pl.BlockSpec((tm, d), lambda i: (i, 0)),
        out_shape=jax.ShapeDtypeStruct((nb * seq, d), F32),
        compiler_params=_params("arbitrary"),
        name="odd_outproj",
    )(att, h, mod, w_out, ln_g, ln_b)


def kernel(x, c, ctx, c_ctx, mod_w, mod_b, ln1_g, ln1_b, ln2_g, ln2_b, ffn_w_up, ffn_conv_w, ffn_conv_b, ffn_w_down, e_w_in, e_conv_qkv, e_a_log, e_dt_bias, e_gdn_norm, e_lam_re, e_lam_im, e_log_dt, e_b_re, e_b_im, e_c_re, e_c_im, e_d_skip, e_w_glu, e_b_glu, e_w_out, o_w_in, o_q_norm, o_k_norm, o_w_out):
    nb, seq, d = x.shape
    nctx = ctx.shape[1]
    period = nctx + seq
    t = nb * period
    w = GDN_WIDTH
    assert mod_w.shape[0] == DEPTH == 2 and nb < 8

    h = jnp.concatenate([ctx, x], axis=1).reshape(t, d)
    cs = jnp.zeros((8, d), F32).at[:nb].set(c).at[nb].set(c_ctx)
    mod = _mod_call(cs, mod_w, mod_b).reshape(DEPTH, 8, 6, d)
    row = lambda a: a.reshape(1, -1)

    w_in = e_w_in[0]
    gates_at = 4 * w
    w_main = jnp.concatenate([w_in[:, :gates_at], w_in[:, gates_at + 4 * GDN_HEADS:]], axis=1).astype(BF16)
    w_ab = jnp.pad(w_in[:, gates_at:gates_at + 4 * GDN_HEADS], ((0, 0), (0, 128 - 4 * GDN_HEADS))).astype(BF16)
    proj, ab = _inproj_even_call(h, mod[0], w_main, w_ab, nb, period, nctx)
    pad_row = lambda a: jnp.pad(a.reshape(1, -1), ((0, 0), (0, 128 - 2 * GDN_HEADS)))
    qkvn, kt3, gh, gct3 = _gdn_prep_call(proj, ab, e_conv_qkv[0], pad_row(e_a_log[0]), pad_row(e_dt_bias[0]),
                                         period, nctx)
    o = _gdn_scan_call(qkvn, kt3, gh, gct3, nb, period, nctx)

    ops = _s5_operators(_s5_tables_call(e_lam_re[0], e_lam_im[0], e_log_dt[0], e_b_re[0], e_b_im[0],
                                        e_c_re[0], e_c_im[0]))
    u_slabs = proj[:, 4 * w:].reshape(t // S5_CHUNK, S5_CHUNK, d - w).transpose(1, 0, 2)
    y = _s5_call(u_slabs, ops, nb, period, nctx).transpose(1, 0, 2).reshape(t, d - w)

    h = _merge_call(o, proj, y, h, mod[0], row(e_gdn_norm[0]), row(e_d_skip[0]), e_w_glu[0].astype(BF16),
                    row(e_b_glu[0]), e_w_out[0].astype(BF16), row(ln1_g[0]), row(ln1_b[0]), nb, period, nctx)
    w_up, w_down = ffn_w_up.astype(BF16), ffn_w_down.astype(BF16)
    conv_b = ffn_conv_b[:, None, :]
    h = _ffn_call(h, mod[0], 0, w_up, ffn_conv_w, conv_b, w_down, row(ln2_g[0]), row(ln2_b[0]), nb, period, nctx)

    cos, sin = _rope_tables(seq, nctx)
    qkv = _qkv_call(h, mod[1], o_w_in[0].astype(BF16), cos, sin, row(o_q_norm[0]), row(o_k_norm[0]),
                    nb, period, nctx)
    att = _attn_call(qkv, nb, seq, nctx)
    hl = _outproj_call(att, h, mod[1], o_w_out[0].astype(BF16), row(ln1_g[1]), row(ln1_b[1]), nb, seq, nctx)
    out = _ffn_call(hl, mod[1], 1, w_up, ffn_conv_w, conv_b, w_down, row(ln2_g[1]), row(ln2_b[1]), nb, seq, 0)
    return out.reshape(nb, seq, d)
```

```python
import functools
import math

import jax
import jax.numpy as jnp
from jax import lax
from jax.experimental import pallas as pl
from jax.experimental.pallas import tpu as pltpu

F32 = jnp.float32
BF16 = jnp.bfloat16

DEPTH = 2
GDN_HEADS = 8
HEAD_DIM = 128
GDN_WIDTH = GDN_HEADS * HEAD_DIM
GDN_CHUNK = 64
GDN_INTRA_BATCH = 17
GDN_SOLVE_BASE = 8
S5_GROUP = 16
S5_STATE = 64
S5_CHUNK = 16
S5_TILE = S5_CHUNK * S5_GROUP
S5_GROUPS_PER_STEP = 4
S5_TABLE_GROUPS = 8
ATT_HEADS = 16
ATT_KV_HEADS = 4
ATT_GROUP = ATT_HEADS // ATT_KV_HEADS
ATT_SCALE = HEAD_DIM ** -0.5
Q_SCALE = ATT_SCALE * math.log2(math.e)
ATT_Q_TILE = 256
ATT_KEY_BLOCKS = 4
MXU_TILE = 256
GRID_W = 64
ROPE_THETA = 10000.0
PROJ_ROW_TILE = 1088
FFN_CHUNK = 512
FFN_SLAB = 256
ALPHA = (2 * DEPTH) ** 0.25
LN_EPS = 1e-6
RMS_EPS = 1e-6
L2_EPS = 1e-6
VMEM_LIMIT_BYTES = 56 * 1024 * 1024


def _pick(n, target, mult):
    best = None
    for d in range(mult, min(n, target) + 1, mult):
        if n % d == 0:
            best = d
    assert best is not None, (n, target, mult)
    return best


def _params(*sem):
    return pltpu.CompilerParams(dimension_semantics=sem, vmem_limit_bytes=VMEM_LIMIT_BYTES)


def _dot(a, b):
    return jnp.dot(a, b, preferred_element_type=F32)


def _dot_nt(a, b):
    return lax.dot_general(a, b, (((1,), (1,)), ((), ())), preferred_element_type=F32)


def _split3(a):
    p0 = a.astype(BF16)
    r = a - p0.astype(F32)
    p1 = r.astype(BF16)
    p2 = (r - p1.astype(F32)).astype(BF16)
    return p0, p1, p2


def _rows_in_batch(i, tm, tiles_per_batch):
    return (i % tiles_per_batch) * tm + lax.broadcasted_iota(jnp.int32, (tm, 1), 0)


def _modulate(x, is_ctx, mb_ref, mc_ref, shift_i, scale_i):
    scale = jnp.where(is_ctx, mc_ref[scale_i:scale_i + 1, :], mb_ref[scale_i:scale_i + 1, :])
    shift = jnp.where(is_ctx, mc_ref[shift_i:shift_i + 1, :], mb_ref[shift_i:shift_i + 1, :])
    return x * (1.0 + scale) + shift


def _layernorm(r, g, b):
    xc = r - jnp.mean(r, -1, keepdims=True)
    var = jnp.mean(xc * xc, -1, keepdims=True)
    return xc * lax.rsqrt(var + LN_EPS) * g + b


def _silu(x):
    return x * jax.nn.sigmoid(x)


def _seq_edges(rib, ctx, period):
    has_prev = (rib != 0) & (rib != ctx)
    has_next = (rib != ctx - 1) & (rib != period - 1)
    return has_prev, has_next


def _mod_body(c_ref, w_ref, b_ref, o_ref):
    s = _silu(c_ref[...])
    o_ref[...] = _dot(s.astype(BF16), w_ref[...].astype(BF16)) + b_ref[...]


def _mod_call(cs, mod_w, mod_b):
    depth, d, n = mod_w.shape
    tn = _pick(n, 1024, 128)
    return pl.pallas_call(
        _mod_body,
        grid=(depth, n // tn),
        in_specs=[pl.BlockSpec((8, d), lambda l, j: (0, 0)),
                  pl.BlockSpec((None, d, tn), lambda l, j: (l, 0, j)),
                  pl.BlockSpec((None, 1, tn), lambda l, j: (l, 0, j))],
        out_specs=pl.BlockSpec((None, 8, tn), lambda l, j: (l, 0, j)),
        out_shape=jax.ShapeDtypeStruct((depth, 8, n), F32),
        compiler_params=_params("arbitrary", "arbitrary"),
        name="adaln_mod",
    )(cs, mod_w, mod_b.reshape(depth, 1, n))


def _inproj_even_body(x_ref, mb_ref, mc_ref, w_ref, wab_ref, o_ref, ab_ref, xb_ref, *, tm, tpb, ctx):
    i = pl.program_id(0)

    @pl.when(pl.program_id(1) == 0)
    def _():
        rib = _rows_in_batch(i, tm, tpb)
        xb = _modulate(x_ref[...], rib < ctx, mb_ref, mc_ref, 0, 1).astype(BF16)
        xb_ref[...] = xb
        ab_ref[...] = _dot(xb, wab_ref[...])

    o_ref[...] = _dot(xb_ref[...], w_ref[...]).astype(o_ref.dtype)


def _inproj_even_call(h, mod, w_main, w_ab, nb, period, ctx):
    t, d = h.shape
    n = w_main.shape[1]
    tm = _pick(period, PROJ_ROW_TILE, 16)
    tn = _pick(n, 1280, 256)
    tpb = period // tm
    body = functools.partial(_inproj_even_body, tm=tm, tpb=tpb, ctx=ctx)
    return pl.pallas_call(
        body,
        grid=(t // tm, n // tn),
        in_specs=[pl.BlockSpec((tm, d), lambda i, j: (i, 0)),
                  pl.BlockSpec((None, 6, d), lambda i, j: (i // tpb, 0, 0)),
                  pl.BlockSpec((None, 6, d), lambda i, j: (nb, 0, 0)),
                  pl.BlockSpec((d, tn), lambda i, j: (0, j)),
                  pl.BlockSpec((d, 128), lambda i, j: (0, 0))],
        out_specs=[pl.BlockSpec((tm, tn), lambda i, j: (i, j)),
                   pl.BlockSpec((tm, 128), lambda i, j: (i, 0))],
        out_shape=[jax.ShapeDtypeStruct((t, n), BF16), jax.ShapeDtypeStruct((t, 128), F32)],
        scratch_shapes=[pltpu.VMEM((tm, d), BF16)],
        compiler_params=_params("arbitrary", "arbitrary"),
        name="even_inproj",
    )(h, mod, mod, w_main, w_ab)


def _gdn_prep_body(x_ref, xn_ref, xp_ref, cw_ref, ab_ref, alog_ref, dtb_ref, tril_ref, triu_ref, e_ref,
                   qkv_ref, kt_ref, gh_ref, gct_ref, ext_ref, *, tr, tpb, ctx, period):
    i = pl.program_id(0)
    sec = pl.program_id(1)
    rib = _rows_in_batch(i, tr, tpb)
    has_prev, has_next = _seq_edges(rib, ctx, period)
    ext_ref[0:16, :] = xp_ref[...].astype(F32)
    ext_ref[16:tr + 16, :] = x_ref[...].astype(F32)
    ext_ref[tr + 16:tr + 32, :] = xn_ref[...].astype(F32)
    cw = cw_ref[...]
    y = (jnp.where(has_prev, ext_ref[15:tr + 15, :], 0.0) * cw[0:1] + ext_ref[16:tr + 16, :] * cw[1:2]
         + jnp.where(has_next, ext_ref[17:tr + 17, :], 0.0) * cw[2:3])
    y = _silu(y)

    def l2n(scale):
        parts = []
        for hh in range(GDN_HEADS):
            yh = y[:, hh * HEAD_DIM:(hh + 1) * HEAD_DIM]
            parts.append(yh * (lax.rsqrt(jnp.sum(yh * yh, -1, keepdims=True) + L2_EPS) * scale))
        return jnp.concatenate(parts, axis=1)

    @pl.when(sec == 0)
    def _():
        qkv_ref[...] = l2n(HEAD_DIM ** -0.5).astype(BF16)
        ab = ab_ref[...]
        lane = lax.broadcasted_iota(jnp.int32, (1, 128), 1)
        xg = ab + dtb_ref[...]
        softplus = jnp.maximum(xg, 0.0) + jnp.log1p(jnp.exp(-jnp.abs(xg)))
        g = -jnp.exp(alog_ref[...]) * softplus
        g0, g1, g2 = _split3(g)
        tril = tril_ref[...]
        triu = triu_ref[...]
        pre = _dot(tril, g0) + _dot(tril, g1) + _dot(tril, g2)
        suf = _dot(triu, g0) + _dot(triu, g1) + _dot(triu, g2)
        gc = jnp.where(lane < GDN_HEADS, pre, suf)
        vals = jnp.where(lane < 2 * GDN_HEADS, gc, jax.nn.sigmoid(ab))
        v0, v1, v2 = _split3(vals)
        e = e_ref[...]
        gh_ref[...] = _dot(v0, e) + _dot(v1, e) + _dot(v2, e)
        gct = gc.T
        for c in range(tr // GDN_CHUNK):
            gct_ref[c] = gct[:2 * GDN_HEADS, c * GDN_CHUNK:(c + 1) * GDN_CHUNK]

    @pl.when(sec == 1)
    def _():
        kn = l2n(1.0)
        qkv_ref[...] = kn.astype(BF16)
        knt = kn.T
        for c in range(tr // GDN_CHUNK):
            kt_ref[c] = knt[:, c * GDN_CHUNK:(c + 1) * GDN_CHUNK].astype(BF16)

    @pl.when(sec == 2)
    def _():
        qkv_ref[...] = y.astype(BF16)


def _gdn_prep_call(proj, ab, conv_w, alog_row, dtb_row, period, ctx):
    t = proj.shape[0]
    w = GDN_WIDTH
    tr = _pick(period, 256, GDN_CHUNK)
    tpb = period // tr
    hb = tr // 16
    nh = t // 16
    nck = tr // GDN_CHUNK
    r = jnp.arange(tr)
    same = (r[:, None] // GDN_CHUNK) == (r[None, :] // GDN_CHUNK)
    tril = (same & (r[:, None] >= r[None, :])).astype(BF16)
    triu = (same & (r[:, None] <= r[None, :])).astype(BF16)
    src = jnp.arange(128)[:, None]
    dst = jnp.arange(w)[None, :]
    expand = ((dst % HEAD_DIM < 4) & (src == (dst % HEAD_DIM) * GDN_HEADS + dst // HEAD_DIM)).astype(BF16)
    body = functools.partial(_gdn_prep_body, tr=tr, tpb=tpb, ctx=ctx, period=period)
    return pl.pallas_call(
        body,
        grid=(t // tr, 3),
        in_specs=[pl.BlockSpec((tr, w), lambda i, s: (i, s)),
                  pl.BlockSpec((16, w), lambda i, s: (jnp.minimum((i + 1) * hb, nh - 1), s)),
                  pl.BlockSpec((16, w), lambda i, s: (jnp.maximum(i * hb - 1, 0), s)),
                  pl.BlockSpec((3, w), lambda i, s: (0, s)),
                  pl.BlockSpec((tr, 128), lambda i, s: (i, 0)),
                  pl.BlockSpec((1, 128), lambda i, s: (0, 0)),
                  pl.BlockSpec((1, 128), lambda i, s: (0, 0)),
                  pl.BlockSpec((tr, tr), lambda i, s: (0, 0)),
                  pl.BlockSpec((tr, tr), lambda i, s: (0, 0)),
                  pl.BlockSpec((128, w), lambda i, s: (0, 0))],
        out_specs=[pl.BlockSpec((tr, w), lambda i, s: (i, s)),
                   pl.BlockSpec((nck, w, GDN_CHUNK), lambda i, s: (i, 0, 0)),
                   pl.BlockSpec((tr, w), lambda i, s: (i, 0)),
                   pl.BlockSpec((nck, 2 * GDN_HEADS, GDN_CHUNK), lambda i, s: (i, 0, 0))],
        out_shape=[jax.ShapeDtypeStruct((t, 3 * w), BF16),
                   jax.ShapeDtypeStruct((t // GDN_CHUNK, w, GDN_CHUNK), BF16),
                   jax.ShapeDtypeStruct((t, w), F32),
                   jax.ShapeDtypeStruct((t // GDN_CHUNK, 2 * GDN_HEADS, GDN_CHUNK), F32)],
        scratch_shapes=[pltpu.VMEM((tr + 32, w), F32)],
        compiler_params=_params("arbitrary", "arbitrary"),
        name="gdn_prep",
    )(proj, proj, proj, conv_w, ab, alog_row, dtb_row, tril, triu, expand)


def _gdn_scan_body(q_ref, k_ref, v_ref, kt_ref, gh_ref, gct_ref, o_ref,
                   aqf, aqb, bsf, bsb, egf, egb, *, nch, nctx, batch):
    head = pl.program_id(1)
    cs = GDN_CHUNK
    ii = lax.broadcasted_iota(jnp.int32, (cs, cs), 0)
    jj = lax.broadcasted_iota(jnp.int32, (cs, cs), 1)
    masks = ((ii >= jj, ii > jj), (ii <= jj, ii < jj))
    same_block = {}
    width = GDN_SOLVE_BASE
    while width <= cs:
        shift = width.bit_length() - 1
        same_block[width] = jnp.right_shift(ii, shift) == jnp.right_shift(jj, shift)
        width *= 2
    hd = HEAD_DIM
    aqs, bss, egs = (aqf, aqb), (bsf, bsb), (egf, egb)

    def intra(first, count):
        chunks = [first + j for j in range(count)]
        r0s = [pl.multiple_of(c * cs, cs) for c in chunks]
        qs = [q_ref[pl.ds(r0, cs), :] for r0 in r0s]
        ks = [k_ref[pl.ds(r0, cs), :] for r0 in r0s]
        vs = [v_ref[pl.ds(r0, cs), :] for r0 in r0s]
        kts = [kt_ref[c] for c in chunks]
        raws = [_dot(jnp.concatenate([q, k], axis=0), kt) for q, k, kt in zip(qs, ks, kts)]
        raw_q = [r[:cs] for r in raws]
        raw_k = [r[cs:] for r in raws]
        pms, xs, atts, kps, qes, gends = [], [], [], [], [], []
        for j, c in enumerate(chunks):
            g4 = gh_ref[pl.ds(r0s[j], cs), :]
            qf, kf, vf = qs[j].astype(F32), ks[j].astype(F32), vs[j].astype(F32)
            ktf = kts[j].astype(F32)
            for d in (0, 1):
                incl, strict = masks[d]
                gcol = g4[:, d:d + 1]
                bcol = g4[:, 2 + d:3 + d]
                grow = gct_ref[c, pl.ds(d * GDN_HEADS + head, 1), :]
                decay = jnp.where(incl, jnp.exp(jnp.where(incl, gcol - grow, 0.0)), 0.0)
                eg = jnp.exp(gcol)
                gend = grow[:, cs - 1:cs] if d == 0 else grow[:, 0:1]
                pms.append(jnp.where(strict, -(raw_k[j] * bcol) * decay, 0.0))
                xs.append(jnp.concatenate([vf * bcol, kf * (bcol * eg)], axis=1))
                atts.append((raw_q[j] * decay).astype(BF16))
                kps.append((ktf * jnp.exp(gend - grow)).astype(BF16))
                qes.append(qf * eg)
                gends.append(gend)
        bfs = lambda vals: [a.astype(BF16) for a in vals]
        pds = bfs([jnp.where(same_block[GDN_SOLVE_BASE], pm, 0.0) for pm in pms])
        ns = [pd.astype(F32) for pd in pds]
        pw = pds
        span = 1
        while 2 * span < GDN_SOLVE_BASE:
            sq = [_dot(p, p) for p in pw]
            pw = bfs(sq)
            ns = [n + s + _dot(p, n.astype(BF16)) for n, s, p in zip(ns, sq, pw)]
            span *= 2
        width = GDN_SOLVE_BASE
        while width < cs:
            couple = same_block[2 * width] & ~same_block[width]
            qs_ = [jnp.where(couple, pm, 0.0) for pm in pms]
            nbs = bfs(ns)
            bs_ = [q + _dot(nb, q.astype(BF16)) for q, nb in zip(qs_, nbs)]
            ns = [n + b + _dot(b.astype(BF16), nb) for n, b, nb in zip(ns, bs_, nbs)]
            width *= 2
        xs = [x + _dot(n.astype(BF16), x.astype(BF16)) for n, x in zip(ns, xs)]
        xbs = [x.astype(BF16) for x in xs]
        both = [_dot(jnp.concatenate([att, kp], axis=0), xb) for att, kp, xb in zip(atts, kps, xbs)]
        aws = [r[:cs] for r in both]
        kxs = [r[cs:] for r in both]
        for j, c in enumerate(chunks):
            for d in (0, 1):
                n = 2 * j + d
                aqs[d][c, 0:hd, :] = (-kxs[n][:, hd:]).astype(BF16)
                aqs[d][c, hd:hd + cs, :] = (qes[n] - aws[n][:, hd:]).astype(BF16)
                bss[d][c] = kxs[n][:, :hd]
                egs[d][c] = jnp.broadcast_to(jnp.exp(gends[n]), (8, hd))
            o_ref[pl.ds(r0s[j], cs), :] = aws[2 * j][:, :hd] + aws[2 * j + 1][:, :hd]

    def inter(i, carry):
        cb = jnp.where(i < nctx, nctx - 1 - i, nch - 1 - (i - nctx))
        new = []
        for d, c in ((0, i), (1, cb)):
            s = carry[d]
            r0 = pl.multiple_of(c * cs, cs)
            r = _dot(aqs[d][c], s.astype(BF16))
            o_ref[pl.ds(r0, cs), :] += r[hd:]
            new.append(s * egs[d][c][0:1, :] + r[:hd] + bss[d][c])
        return tuple(new)

    full = nch // batch

    def intra_step(it, carry):
        intra(it * batch, batch)
        return carry

    lax.fori_loop(0, full, intra_step, 0)
    if nch > full * batch:
        intra(full * batch, nch - full * batch)
    zero = jnp.zeros((hd, hd), F32)
    lax.fori_loop(0, nch, inter, (zero, zero), unroll=2)


def _gdn_scan_call(qkvn, kt3, gh, gct3, nb, period, ctx):
    t = qkvn.shape[0]
    nch = period // GDN_CHUNK
    nctx = ctx // GDN_CHUNK
    hd = HEAD_DIM
    body = functools.partial(_gdn_scan_body, nch=nch, nctx=nctx, batch=min(nch, GDN_INTRA_BATCH))
    dirs2 = lambda shape, dt: [pltpu.VMEM(shape, dt), pltpu.VMEM(shape, dt)]
    return pl.pallas_call(
        body,
        grid=(nb, GDN_HEADS),
        in_specs=[pl.BlockSpec((period, hd), lambda b, h: (b, h)),
                  pl.BlockSpec((period, hd), lambda b, h: (b, GDN_HEADS + h)),
                  pl.BlockSpec((period, hd), lambda b, h: (b, 2 * GDN_HEADS + h)),
                  pl.BlockSpec((nch, hd, GDN_CHUNK), lambda b, h: (b, h, 0)),
                  pl.BlockSpec((period, hd), lambda b, h: (b, h)),
                  pl.BlockSpec((nch, 2 * GDN_HEADS, GDN_CHUNK), lambda b, h: (b, 0, 0))],
        out_specs=pl.BlockSpec((period, hd), lambda b, h: (b, h)),
        out_shape=jax.ShapeDtypeStruct((t, GDN_WIDTH), F32),
        scratch_shapes=(dirs2((nch, hd + GDN_CHUNK, hd), BF16) + dirs2((nch, hd, hd), F32)
                        + dirs2((nch, 8, hd), F32)),
        compiler_params=_params("arbitrary", "arbitrary"),
        name="gdn_scan",
    )(qkvn, qkvn, qkvn, kt3, gh, gct3)


def _s5_tables_body(*refs):
    lax.fori_loop(0, refs[0].shape[1], functools.partial(_s5_tables_group, refs), 0)


def _s5_tables_group(refs, q, carry):
    (lr_ref, li_ref, ldt_ref, btr_ref, bti_ref, cr_ref, ci_ref,
     wir_ref, wii_ref, wor_ref, woi_ref, ktf_ref, ktb_ref, lam_ref) = refs
    n, cg = S5_CHUNK, S5_GROUP
    kk = lax.broadcasted_iota(jnp.int32, (24, 1), 0).astype(F32)
    lane = lax.broadcasted_iota(jnp.int32, (1, S5_TILE), 1)

    def dot3_nt(a, b):
        a0 = a.astype(BF16)
        a1 = (a - a0.astype(F32)).astype(BF16)
        b0 = b.astype(BF16)
        b1 = (b - b0.astype(F32)).astype(BF16)
        return _dot_nt(a0, b0) + _dot_nt(a0, b1) + _dot_nt(a1, b0)

    win_r, win_i, wout_r, wout_i, lam_r, lam_i = [], [], [], [], [], []
    for d in (0, 1):
        lr = lr_ref[d, q]
        li = li_ref[d, q]
        dt = jnp.exp(ldt_ref[d, q])
        mag = jnp.exp(kk * (lr * dt))
        ang = kk * (li * dt)
        er = mag * jnp.cos(ang)
        ei = mag * jnp.sin(ang)
        nr = er[1:2] - 1.0
        ni = ei[1:2]
        den = lr * lr + li * li
        cfr = (nr * lr + ni * li) / den
        cfi = (ni * lr - nr * li) / den
        btr = btr_ref[d, q]
        bti = bti_ref[d, q]
        bbr = btr * cfr - bti * cfi
        bbi = btr * cfi + bti * cfr
        cr = cr_ref[d, q]
        ci = ci_ref[d, q]
        a_r, a_i, cm_r, cm_i, c0_r, c0_i = [], [], [], [], [], []
        for k in range(n):
            e_r, e_i = er[k:k + 1], ei[k:k + 1]
            a_r.append(bbr * e_r - bbi * e_i)
            a_i.append(bbr * e_i + bbi * e_r)
            c0_r.append(cr * e_r - ci * e_i)
            c0_i.append(cr * e_i + ci * e_r)
            f_r, f_i = er[k + 1:k + 2], ei[k + 1:k + 2]
            cm_r.append(cr * f_r - ci * f_i)
            cm_i.append(-(cr * f_i + ci * f_r))
        fwd = d == 0
        down = list(range(n - 1, -1, -1))
        up = list(range(n))
        pick = lambda parts, order: jnp.concatenate([parts[k] for k in order], axis=0)
        win_r.append(pick(a_r, down if fwd else up))
        win_i.append(pick(a_i, down if fwd else up))
        wout_r.append(pick(cm_r, up if fwd else down))
        wout_i.append(pick(cm_i, up if fwd else down))
        lam_r.append(er[n:n + 1])
        lam_i.append(ei[n:n + 1])
        order = up if fwd else down
        strip = dot3_nt(bbr, pick(c0_r, order)) - dot3_nt(bbi, pick(c0_i, order))
        for s in range(n):
            rows = slice(s * cg, (s + 1) * cg)
            if fwd:
                blk = strip if s == 0 else jnp.where(lane >= s * cg, pltpu.roll(strip, s * cg, 1), 0.0)
                ktf_ref[q, rows, :] = blk.astype(BF16)
            else:
                sh = (n - 1 - s) * cg
                blk = strip if sh == 0 else jnp.where(lane < (s + 1) * cg, pltpu.roll(strip, S5_TILE - sh, 1), 0.0)
                ktb_ref[q, rows, :] = blk.astype(BF16)
    wir_ref[q] = jnp.concatenate(win_r, axis=1).astype(BF16)
    wii_ref[q] = jnp.concatenate(win_i, axis=1).astype(BF16)
    wor_ref[q] = jnp.concatenate(wout_r, axis=1).T.astype(BF16)
    woi_ref[q] = jnp.concatenate(wout_i, axis=1).T.astype(BF16)
    lam_ref[q] = jnp.concatenate([jnp.concatenate(lam_r, axis=1), jnp.concatenate(lam_i, axis=1)], axis=0)
    return carry


def _s5_tables_call(lam_re, lam_im, log_dt, b_re, b_im, c_re, c_im):
    _, g, p = lam_re.shape
    cg = S5_GROUP
    sw = 2 * p
    row4 = lambda a: a.reshape(2, g, 1, p)
    per = _pick(g, S5_TABLE_GROUPS, 1)
    ispec = lambda r: pl.BlockSpec((2, per, r, p), lambda gi: (0, gi, 0, 0))
    ospec = lambda r, c: pl.BlockSpec((per, r, c), lambda gi: (gi, 0, 0))
    ldt = jnp.broadcast_to(log_dt[:, :, None, None], (2, g, 1, p))
    wir, wii, wor, woi, ktf, ktb, lam = pl.pallas_call(
        _s5_tables_body,
        grid=(g // per,),
        in_specs=[ispec(1), ispec(1), ispec(1), ispec(cg), ispec(cg), ispec(cg), ispec(cg)],
        out_specs=[ospec(S5_TILE, sw), ospec(S5_TILE, sw), ospec(sw, S5_TILE), ospec(sw, S5_TILE),
                   ospec(S5_TILE, S5_TILE), ospec(S5_TILE, S5_TILE), ospec(2, sw)],
        out_shape=[jax.ShapeDtypeStruct((g, S5_TILE, sw), BF16)] * 2 + [jax.ShapeDtypeStruct((g, sw, S5_TILE), BF16)] * 2
                  + [jax.ShapeDtypeStruct((g, S5_TILE, S5_TILE), BF16)] * 2 + [jax.ShapeDtypeStruct((g, 2, sw), F32)],
        compiler_params=_params("arbitrary"),
        name="s5_tables",
    )(row4(lam_re), row4(lam_im), ldt, jnp.swapaxes(b_re, -1, -2), jnp.swapaxes(b_im, -1, -2), c_re, c_im)
    gb = S5_GROUPS_PER_STEP
    lam_r = lam[:, 0, :].reshape(g // gb, 1, gb * sw)
    lam_i = lam[:, 1, :].reshape(g // gb, 1, gb * sw)
    return ktf, ktb, wir, wii, wor, woi, lam_r, lam_i


def _s5_gather_chunks(u_ref, l_ref):
    lanes = u_ref.shape[2]
    for s in range(S5_CHUNK):
        l_ref[:, s * lanes:(s + 1) * lanes] = u_ref[s]


def _s5_in_body(u_ref, pm_ref, wr_ref, wi_ref, ub_ref, x_ref, l_ref):
    @pl.when(pl.program_id(1) == 0)
    def _():
        _s5_gather_chunks(u_ref, l_ref)

    ub = _dot(l_ref[...], pm_ref[...]).astype(BF16)
    ub_ref[...] = ub
    sw = 2 * S5_STATE
    for q in range(ub.shape[1] // S5_TILE):
        ug = ub[:, q * S5_TILE:(q + 1) * S5_TILE]
        x_ref[:, 2 * q * sw:(2 * q + 1) * sw] = _dot(ug, wr_ref[q])
        x_ref[:, (2 * q + 1) * sw:(2 * q + 2) * sw] = _dot(ug, wi_ref[q])


def _s5_out_body(ub_ref, s_ref, ktf_ref, ktb_ref, wor_ref, woi_ref, pmt_ref, y_ref, y8_ref, *, col_tiles):
    n = pl.program_id(1)
    sw = 2 * S5_STATE

    @pl.when(n == 0)
    def _():
        for g in range(ub_ref.shape[1] // S5_TILE):
            ug = ub_ref[:, g * S5_TILE:(g + 1) * S5_TILE]
            s_r = s_ref[:, 2 * g * sw:(2 * g + 1) * sw]
            s_i = s_ref[:, (2 * g + 1) * sw:(2 * g + 2) * sw]
            y = _dot(ug, ktf_ref[g]) + _dot(ug, ktb_ref[g]) + _dot(s_r, wor_ref[g]) + _dot(s_i, woi_ref[g])
            y8_ref[:, g * S5_TILE:(g + 1) * S5_TILE] = y.astype(BF16)

    yp = _dot(y8_ref[...], pmt_ref[...])
    lanes = y_ref.shape[2]
    per = yp.shape[1] // lanes
    for nn in range(col_tiles):
        @pl.when(n == nn)
        def _(nn=nn):
            for q in range(per):
                y_ref[nn * per + q] = yp[:, q * lanes:(q + 1) * lanes].astype(y_ref.dtype)


def _s5_scan_body(x_ref, lr_ref, li_ref, s_ref, *scratch, nb, nch, nctx):
    gb = S5_GROUPS_PER_STEP
    sw = 2 * S5_STATE
    xr, xi, sfr, sfi, sbr, sbi = (scratch[k * gb:(k + 1) * gb] for k in range(6))
    for g in range(gb):
        xr[g][...] = x_ref[:, 2 * g * sw:(2 * g + 1) * sw]
        xi[g][...] = x_ref[:, (2 * g + 1) * sw:(2 * g + 2) * sw]
    is_f = lax.broadcasted_iota(jnp.int32, (1, sw), 1) < S5_STATE
    ar = [lr_ref[:, g * sw:(g + 1) * sw] for g in range(gb)]
    ai = [li_ref[:, g * sw:(g + 1) * sw] for g in range(gb)]

    def step(i, carry):
        cb = jnp.where(i < nctx, nctx - 1 - i, nch - 1 - (i - nctx))
        at_f = pl.ds(i, nb, stride=nch)
        at_b = pl.ds(cb, nb, stride=nch)
        new = []
        for g in range(gb):
            sr, si = carry[2 * g], carry[2 * g + 1]
            sfr[g][at_f, :] = sr
            sfi[g][at_f, :] = si
            sbr[g][at_b, :] = sr
            sbi[g][at_b, :] = si
            inr = jnp.where(is_f, xr[g][at_f, :], xr[g][at_b, :])
            ini = jnp.where(is_f, xi[g][at_f, :], xi[g][at_b, :])
            new.append(ar[g] * sr - ai[g] * si + inr)
            new.append(ar[g] * si + ai[g] * sr + ini)
        return tuple(new)

    zero = jnp.zeros((nb, sw), F32)
    lax.fori_loop(0, nch, step, (zero,) * (2 * gb))
    for g in range(gb):
        s_ref[:, 2 * g * sw:(2 * g + 1) * sw] = jnp.where(is_f, sfr[g][...], sbr[g][...]).astype(BF16)
        s_ref[:, (2 * g + 1) * sw:(2 * g + 2) * sw] = jnp.where(is_f, sfi[g][...], sbi[g][...]).astype(BF16)


def _s5_call(u_slabs, ops, nb, period, ctx):
    ktf, ktb, wir, wii, wor, woi, lam_r, lam_i = ops
    _, rows, width = u_slabs.shape
    groups = width // S5_GROUP
    lanes = 128
    lg = lanes // S5_GROUP
    nblk = width // lanes
    blk_w = S5_CHUNK * lanes
    col = 2 * S5_TILE
    col_tiles = blk_w // col
    sw = 2 * S5_STATE
    gb = S5_GROUPS_PER_STEP
    nch = period // S5_CHUNK
    nctx = ctx // S5_CHUNK
    src = jnp.arange(blk_w)
    dst = (src % lanes // S5_GROUP) * S5_TILE + (src // lanes) * S5_GROUP + src % S5_GROUP
    perm = (dst[:, None] == jnp.arange(blk_w)[None, :]).astype(BF16)
    perm_t = perm.T

    u_perm, xin = pl.pallas_call(
        _s5_in_body,
        grid=(nblk, col_tiles),
        in_specs=[pl.BlockSpec((S5_CHUNK, rows, lanes), lambda j, n: (0, 0, j)),
                  pl.BlockSpec((blk_w, col), lambda j, n: (0, n)),
                  pl.BlockSpec((col // S5_TILE, S5_TILE, sw), lambda j, n: (j * col_tiles + n, 0, 0)),
                  pl.BlockSpec((col // S5_TILE, S5_TILE, sw), lambda j, n: (j * col_tiles + n, 0, 0))],
        out_specs=[pl.BlockSpec((rows, col), lambda j, n: (0, j * col_tiles + n)),
                   pl.BlockSpec((rows, col), lambda j, n: (0, j * col_tiles + n))],
        out_shape=[jax.ShapeDtypeStruct((rows, groups * S5_TILE), BF16),
                   jax.ShapeDtypeStruct((rows, groups * 2 * sw), F32)],
        scratch_shapes=[pltpu.VMEM((rows, blk_w), BF16)],
        compiler_params=_params("arbitrary", "arbitrary"),
        name="s5_in",
    )(u_slabs, perm, wir, wii)

    states = pl.pallas_call(
        functools.partial(_s5_scan_body, nb=nb, nch=nch, nctx=nctx),
        grid=(groups // gb,),
        in_specs=[pl.BlockSpec((rows, gb * 2 * sw), lambda i: (0, i)),
                  pl.BlockSpec((None, 1, gb * sw), lambda i: (i, 0, 0)),
                  pl.BlockSpec((None, 1, gb * sw), lambda i: (i, 0, 0))],
        out_specs=pl.BlockSpec((rows, gb * 2 * sw), lambda i: (0, i)),
        out_shape=jax.ShapeDtypeStruct((rows, groups * 2 * sw), BF16),
        scratch_shapes=[pltpu.VMEM((rows, sw), F32) for _ in range(6 * gb)],
        compiler_params=_params("arbitrary"),
        name="s5_scan",
    )(xin, lam_r, lam_i)

    wspec = lambda r, c: pl.BlockSpec((lg, r, c), lambda j, n: (j, 0, 0))
    return pl.pallas_call(
        functools.partial(_s5_out_body, col_tiles=col_tiles),
        grid=(nblk, col_tiles),
        in_specs=[pl.BlockSpec((rows, lg * S5_TILE), lambda j, n: (0, j)),
                  pl.BlockSpec((rows, lg * 2 * sw), lambda j, n: (0, j)),
                  wspec(S5_TILE, S5_TILE), wspec(S5_TILE, S5_TILE), wspec(sw, S5_TILE), wspec(sw, S5_TILE),
                  pl.BlockSpec((blk_w, col), lambda j, n: (0, n))],
        out_specs=pl.BlockSpec((S5_CHUNK, rows, lanes), lambda j, n: (0, 0, j)),
        out_shape=jax.ShapeDtypeStruct((S5_CHUNK, rows, width), BF16),
        scratch_shapes=[pltpu.VMEM((rows, lg * S5_TILE), BF16)],
        compiler_params=_params("arbitrary", "arbitrary"),
        name="s5_out",
    )(u_perm, states, ktf, ktb, wor, woi, perm_t)


def _merge_body(o_ref, z_ref, u_ref, y_ref, h_ref, mb_ref, mc_ref, gn_ref, ds_ref, wglu_ref, bglu_ref,
                wout_ref, lng_ref, lnb_ref, out_ref, cat_ref, *, tm, tpb, ctx):
    rib = _rows_in_batch(pl.program_id(0), tm, tpb)
    gn = gn_ref[...]
    for hh in range(GDN_HEADS):
        sl = slice(hh * HEAD_DIM, (hh + 1) * HEAD_DIM)
        oh = o_ref[:, sl]
        r = lax.rsqrt(jnp.mean(oh * oh, -1, keepdims=True) + RMS_EPS)
        cat_ref[:, sl] = (oh * r * gn * _silu(z_ref[:, sl].astype(F32))).astype(BF16)
    s = y_ref[...].astype(F32) + ds_ref[...] * u_ref[...].astype(F32)
    s = 0.5 * s * (1.0 + jnp.tanh(math.sqrt(2.0 / math.pi) * (s + 0.044715 * (s * s * s))))
    s = s * jax.nn.sigmoid(_dot(s.astype(BF16), wglu_ref[...]) + bglu_ref[...])
    cat_ref[:, GDN_WIDTH:] = s.astype(BF16)
    mix = _dot(cat_ref[...], wout_ref[...])
    gate = jnp.where(rib < ctx, mc_ref[2:3, :], mb_ref[2:3, :])
    out_ref[...] = _layernorm(ALPHA * h_ref[...] + gate * mix, lng_ref[...], lnb_ref[...])


def _merge_call(o, proj, y, h, mod, gn, d_skip, w_glu, b_glu, w_out, ln_g, ln_b, nb, period, ctx):
    t, d = h.shape
    w = GDN_WIDTH
    sw = d - w
    tm = _pick(period, 272, 16)
    tpb = period // tm
    body = functools.partial(_merge_body, tm=tm, tpb=tpb, ctx=ctx)
    const = lambda r, c: pl.BlockSpec((r, c), lambda i: (0, 0))
    return pl.pallas_call(
        body,
        grid=(t // tm,),
        in_specs=[pl.BlockSpec((tm, w), lambda i: (i, 0)),
                  pl.BlockSpec((tm, w), lambda i: (i, 3)),
                  pl.BlockSpec((tm, sw), lambda i: (i, 4 * w // sw)),
                  pl.BlockSpec((tm, sw), lambda i: (i, 0)),
                  pl.BlockSpec((tm, d), lambda i: (i, 0)),
                  pl.BlockSpec((None, 6, d), lambda i: (i // tpb, 0, 0)),
                  pl.BlockSpec((None, 6, d), lambda i: (nb, 0, 0)),
                  const(1, HEAD_DIM), const(1, sw), const(sw, sw), const(1, sw), const(d, d),
                  const(1, d), const(1, d)],
        out_specs=pl.BlockSpec((tm, d), lambda i: (i, 0)),
        out_shape=jax.ShapeDtypeStruct((t, d), F32),
        scratch_shapes=[pltpu.VMEM((tm, d), BF16)],
        compiler_params=_params("arbitrary"),
        name="even_merge",
    )(o, proj, proj, y, h, mod, mod, gn, d_skip, w_glu, b_glu, w_out, ln_g, ln_b)


def _ffn_body(x_ref, xh_ref, mb_ref, mc_ref, wv_ref, wg_ref, cwv_ref, cwg_ref, cbv_ref, cbg_ref, wd_ref,
              lng_ref, lnb_ref, o_ref, xb_ref, *up_refs, tm, tpb, ctx, period):
    i = pl.program_id(0)
    j = pl.program_id(1)
    rib = _rows_in_batch(i, tm, tpb)
    n = tm + 32

    @pl.when(j == 0)
    def _():
        xb_ref[16:tm + 16, :] = _modulate(x_ref[...], rib < ctx, mb_ref, mc_ref, 3, 4).astype(BF16)
        base = (i % tpb) * tm
        off = lax.broadcasted_iota(jnp.int32, (16, 1), 0)
        xh = xh_ref[...]
        next_ok = (base + tm != period) & (base + tm != ctx)
        prev_ok = (base != 0) & (base != ctx)
        nxt = _modulate(xh[0:16], base + tm + off < ctx, mb_ref, mc_ref, 3, 4)
        prv = _modulate(xh[16:32], base - 16 + off < ctx, mb_ref, mc_ref, 3, 4)
        xb_ref[tm + 16:n, :] = jnp.where(next_ok, nxt, 0.0).astype(BF16)
        xb_ref[0:16, :] = jnp.where(prev_ok, prv, 0.0).astype(BF16)
        o_ref[...] = jnp.zeros_like(o_ref)

    xb = xb_ref[...]
    edge_inside_tile = ctx % tm != 0

    def conv(u_ref, cw_ref, cb_ref, sl):
        cw = cw_ref[:, sl]
        up = u_ref[15:tm + 15, :]
        un = u_ref[17:tm + 17, :]
        if edge_inside_tile:
            up = jnp.where(rib != ctx, up, 0.0)
            un = jnp.where(rib != ctx - 1, un, 0.0)
        return up * cw[0:1] + u_ref[16:tm + 16, :] * cw[1:2] + un * cw[2:3] + cb_ref[:, sl]

    fc = wv_ref.shape[1]
    slabs = [slice(a, a + FFN_SLAB) for a in range(0, fc, FFN_SLAB)]
    for si, sl in enumerate(slabs):
        up_refs[2 * si][...] = _dot(xb, wv_ref[:, sl])
        up_refs[2 * si + 1][...] = _dot(xb, wg_ref[:, sl])
    for si, sl in enumerate(slabs):
        act = conv(up_refs[2 * si], cwv_ref, cbv_ref, sl) * _silu(conv(up_refs[2 * si + 1], cwg_ref, cbg_ref, sl))
        o_ref[...] += _dot(act.astype(BF16), wd_ref[sl, :])

    @pl.when(j == pl.num_programs(1) - 1)
    def _():
        g = jnp.where(rib < ctx, mc_ref[5:6, :], mb_ref[5:6, :])
        o_ref[...] = _layernorm(ALPHA * x_ref[...] + g * o_ref[...], lng_ref[...], lnb_ref[...])


def _ffn_call(h, mod, layer, w_up, conv_w, conv_b, w_down, ln_g, ln_b, nb, period, ctx):
    t, d = h.shape
    f = w_down.shape[1]
    tm = _pick(period, 544, 16)
    tpb = period // tm
    nt = t // tm
    fc = _pick(f, FFN_CHUNK, FFN_SLAB)
    nf = f // fc
    h3 = h.reshape(nt, tm, d)
    pad = jnp.zeros((1, 16, d), h.dtype)
    halo = jnp.concatenate([jnp.concatenate([h3[1:, :16], pad], 0),
                            jnp.concatenate([pad, h3[:-1, tm - 16:]], 0)], axis=1)
    body = functools.partial(_ffn_body, tm=tm, tpb=tpb, ctx=ctx, period=period)
    return pl.pallas_call(
        body,
        grid=(nt, nf),
        in_specs=[pl.BlockSpec((tm, d), lambda i, j: (i, 0)),
                  pl.BlockSpec((None, 32, d), lambda i, j: (i, 0, 0)),
                  pl.BlockSpec((None, 6, d), lambda i, j: (i // tpb, 0, 0)),
                  pl.BlockSpec((None, 6, d), lambda i, j: (nb, 0, 0)),
                  pl.BlockSpec((None, d, fc), lambda i, j: (layer, 0, j)),
                  pl.BlockSpec((None, d, fc), lambda i, j: (layer, 0, nf + j)),
                  pl.BlockSpec((None, 3, fc), lambda i, j: (layer, 0, j)),
                  pl.BlockSpec((None, 3, fc), lambda i, j: (layer, 0, nf + j)),
                  pl.BlockSpec((None, 1, fc), lambda i, j: (layer, 0, j)),
                  pl.BlockSpec((None, 1, fc), lambda i, j: (layer, 0, nf + j)),
                  pl.BlockSpec((None, fc, d), lambda i, j: (layer, j, 0)),
                  pl.BlockSpec((1, d), lambda i, j: (0, 0)),
                  pl.BlockSpec((1, d), lambda i, j: (0, 0))],
        out_specs=pl.BlockSpec((tm, d), lambda i, j: (i, 0)),
        out_shape=jax.ShapeDtypeStruct((t, d), F32),
        scratch_shapes=([pltpu.VMEM((tm + 32, d), BF16)]
                        + [pltpu.VMEM((tm + 32, FFN_SLAB), F32) for _ in range(2 * fc // FFN_SLAB)]),
        compiler_params=_params("arbitrary", "arbitrary"),
        name="conv_ffn",
    )(h, halo, mod, mod, w_up, w_up, conv_w, conv_w, conv_b, conv_b, w_down, ln_g, ln_b)


def _qkv_body(x_ref, mb_ref, mc_ref, w_ref, cos_ref, sin_ref, qn_ref, kn_ref, o_ref, xb_ref,
              *, tm, tpb, ctx, nq_tiles):
    i = pl.program_id(0)
    j = pl.program_id(1)

    @pl.when(j == 0)
    def _():
        rib = _rows_in_batch(i, tm, tpb)
        xb_ref[...] = _modulate(x_ref[...], rib < ctx, mb_ref, mc_ref, 0, 1).astype(BF16)

    cos = cos_ref[...]
    sin = sin_ref[...]
    lane = lax.broadcasted_iota(jnp.int32, (1, HEAD_DIM), 1)
    first = (lane % 64) < 32
    acc = _dot(xb_ref[...], w_ref[...])
    heads = acc.shape[1] // HEAD_DIM

    def normrope(xh, wn, scale):
        xn = xh * lax.rsqrt(jnp.mean(xh * xh, -1, keepdims=True) + RMS_EPS) * wn
        partner = jnp.where(first, pltpu.roll(xn, HEAD_DIM - 32, 1), pltpu.roll(xn, 32, 1))
        return (xn * cos + partner * sin) * scale

    @pl.when(j < nq_tiles)
    def _():
        for hh in range(heads):
            sl = slice(hh * HEAD_DIM, (hh + 1) * HEAD_DIM)
            o_ref[:, sl] = normrope(acc[:, sl], qn_ref[...], Q_SCALE).astype(BF16)

    @pl.when(j == nq_tiles)
    def _():
        for hh in range(heads):
            sl = slice(hh * HEAD_DIM, (hh + 1) * HEAD_DIM)
            if hh < ATT_KV_HEADS:
                o_ref[:, sl] = normrope(acc[:, sl], kn_ref[...], 1.0).astype(BF16)
            else:
                o_ref[:, sl] = acc[:, sl].astype(BF16)


def _qkv_call(h, mod, w_in, cos, sin, qn, kn, nb, period, ctx):
    t, d = h.shape
    n = w_in.shape[1]
    tn = 2 * ATT_KV_HEADS * HEAD_DIM
    nq_tiles = (n - tn) // tn
    tm = _pick(period, PROJ_ROW_TILE, 16)
    tpb = period // tm
    body = functools.partial(_qkv_body, tm=tm, tpb=tpb, ctx=ctx, nq_tiles=nq_tiles)
    return pl.pallas_call(
        body,
        grid=(t // tm, n // tn),
        in_specs=[pl.BlockSpec((tm, d), lambda i, j: (i, 0)),
                  pl.BlockSpec((None, 6, d), lambda i, j: (i // tpb, 0, 0)),
                  pl.BlockSpec((None, 6, d), lambda i, j: (nb, 0, 0)),
                  pl.BlockSpec((d, tn), lambda i, j: (0, j)),
                  pl.BlockSpec((tm, HEAD_DIM), lambda i, j: (i % tpb, 0)),
                  pl.BlockSpec((tm, HEAD_DIM), lambda i, j: (i % tpb, 0)),
                  pl.BlockSpec((1, HEAD_DIM), lambda i, j: (0, 0)),
                  pl.BlockSpec((1, HEAD_DIM), lambda i, j: (0, 0))],
        out_specs=pl.BlockSpec((tm, tn), lambda i, j: (i, j)),
        out_shape=jax.ShapeDtypeStruct((t, n), BF16),
        scratch_shapes=[pltpu.VMEM((tm, d), BF16)],
        compiler_params=_params("arbitrary", "arbitrary"),
        name="odd_qkv",
    )(h, mod, mod, w_in, cos, sin, qn, kn)


def _rope_tables(seq, ctx):
    rows = seq // GRID_W
    row = jnp.repeat(jnp.arange(rows, dtype=F32), GRID_W)
    col = jnp.tile(jnp.arange(GRID_W, dtype=F32), rows)
    half = HEAD_DIM // 4
    inv = ROPE_THETA ** (-jnp.arange(half, dtype=F32) / half)
    ar = row[:, None] * inv
    ac = col[:, None] * inv
    cos = jnp.concatenate([jnp.cos(ar), jnp.cos(ar), jnp.cos(ac), jnp.cos(ac)], -1)
    sin = jnp.concatenate([-jnp.sin(ar), jnp.sin(ar), -jnp.sin(ac), jnp.sin(ac)], -1)
    cos = jnp.concatenate([jnp.ones((ctx, HEAD_DIM), F32), cos], 0)
    sin = jnp.concatenate([jnp.zeros((ctx, HEAD_DIM), F32), sin], 0)
    return cos, sin


def _attn_body(q_ref, k_ref, v_ref, o_ref, *, tq, key_blocks):
    hd = HEAD_DIM
    q = jnp.concatenate([q_ref[:, g * hd:(g + 1) * hd] for g in range(ATT_GROUP)], axis=0)
    rows = ATT_GROUP * tq
    m = jnp.full((rows, 1), -1e30, F32)
    l = jnp.zeros((rows, 1), F32)
    acc = jnp.zeros((rows, hd), F32)
    s_next = _dot_nt(q, k_ref[key_blocks[0][0]:key_blocks[0][1], :])
    for n, (k0, k1) in enumerate(key_blocks):
        s = s_next
        if n + 1 < len(key_blocks):
            s_next = _dot_nt(q, k_ref[key_blocks[n + 1][0]:key_blocks[n + 1][1], :])
        m_new = jnp.maximum(m, jnp.max(s, -1, keepdims=True))
        a = jnp.exp2(m - m_new)
        p = jnp.exp2(s - m_new)
        l = a * l + jnp.sum(p, -1, keepdims=True)
        acc = a * acc + _dot(p.astype(BF16), v_ref[k0:k1, :])
        m = m_new
    out = acc / l
    for g in range(ATT_GROUP):
        o_ref[:, g * hd:(g + 1) * hd] = out[g * tq:(g + 1) * tq].astype(o_ref.dtype)


def _key_blocks(period, count, align):
    tiles = period // align
    assert tiles * align == period and tiles >= count
    sizes = [tiles // count + (1 if n >= count - tiles % count else 0) for n in range(count)]
    edges = [0]
    for sz in sizes:
        edges.append(edges[-1] + sz * align)
    return tuple(zip(edges[:-1], edges[1:]))


def _attn_call(qkv, nb, seq, ctx):
    period = seq + ctx
    tq = _pick(math.gcd(seq, ctx), ATT_Q_TILE, 16)
    gw = ATT_GROUP * HEAD_DIM
    qpb = seq // tq
    rpb = period // tq
    skip = ctx // tq
    kcol = ATT_HEADS
    vcol = ATT_HEADS + ATT_KV_HEADS
    align = math.gcd(period, MXU_TILE)
    blocks = _key_blocks(period, min(ATT_KEY_BLOCKS, period // align), align)
    body = functools.partial(_attn_body, tq=tq, key_blocks=blocks)
    return pl.pallas_call(
        body,
        grid=(nb, ATT_KV_HEADS, qpb),
        in_specs=[pl.BlockSpec((tq, gw), lambda b, kv, qi: (b * rpb + skip + qi, kv)),
                  pl.BlockSpec((period, HEAD_DIM), lambda b, kv, qi: (b, kcol + kv)),
                  pl.BlockSpec((period, HEAD_DIM), lambda b, kv, qi: (b, vcol + kv))],
        out_specs=pl.BlockSpec((tq, gw), lambda b, kv, qi: (b * qpb + qi, kv)),
        out_shape=jax.ShapeDtypeStruct((nb * seq, ATT_HEADS * HEAD_DIM), BF16),
        compiler_params=_params("arbitrary", "arbitrary", "arbitrary"),
        name="gqa_attention",
    )(qkv, qkv, qkv)


def _outproj_body(a_ref, h_ref, mb_ref, w_ref, lng_ref, lnb_ref, o_ref):
    mix = _dot(a_ref[...], w_ref[...])
    o_ref[...] = _layernorm(ALPHA * h_ref[...] + mb_ref[2:3, :] * mix, lng_ref[...], lnb_ref[...])


def _outproj_call(att, h, mod, w_out, ln_g, ln_b, nb, seq, ctx):
    d = h.shape[1]
    period = seq + ctx
    tm = _pick(math.gcd(seq, ctx), 256, 16)
    qpb = seq // tm
    rpb = period // tm
    skip = ctx // tm
    return pl.pallas_call(
        _outproj_body,
        grid=(nb * qpb,),
        in_specs=[pl.BlockSpec((tm, d), lambda i: (i, 0)),
                  pl.BlockSpec((tm, d), lambda i: ((i // qpb) * rpb + skip + i % qpb, 0)),
                  pl.BlockSpec((None, 6, d), lambda i: (i // qpb, 0, 0)),
                  pl.BlockSpec((d, d), lambda i: (0, 0)),
                  pl.BlockSpec((1, d), lambda i: (0, 0)),
                  pl.BlockSpec((1, d), lambda i: (0, 0))],
        out_specs=pl.BlockSpec((tm, d), lambda i: (i, 0)),
        out_shape=jax.ShapeDtypeStruct((nb * seq, d), F32),
        compiler_params=_params("arbitrary"),
        name="odd_outproj",
    )(att, h, mod, w_out, ln_g, ln_b)


def kernel(x, c, ctx, c_ctx, mod_w, mod_b, ln1_g, ln1_b, ln2_g, ln2_b, ffn_w_up, ffn_conv_w, ffn_conv_b, ffn_w_down, e_w_in, e_conv_qkv, e_a_log, e_dt_bias, e_gdn_norm, e_lam_re, e_lam_im, e_log_dt, e_b_re, e_b_im, e_c_re, e_c_im, e_d_skip, e_w_glu, e_b_glu, e_w_out, o_w_in, o_q_norm, o_k_norm, o_w_out):
    nb, seq, d = x.shape
    nctx = ctx.shape[1]
    period = nctx + seq
    t = nb * period
    w = GDN_WIDTH
    assert mod_w.shape[0] == DEPTH == 2 and nb < 8

    h = jnp.concatenate([ctx, x], axis=1).reshape(t, d)
    cs = jnp.zeros((8, d), F32).at[:nb].set(c).at[nb].set(c_ctx)
    mod = _mod_call(cs, mod_w, mod_b).reshape(DEPTH, 8, 6, d)
    row = lambda a: a.reshape(1, -1)

    w_in = e_w_in[0]
    gates_at = 4 * w
    w_main = jnp.concatenate([w_in[:, :gates_at], w_in[:, gates_at + 4 * GDN_HEADS:]], axis=1).astype(BF16)
    w_ab = jnp.pad(w_in[:, gates_at:gates_at + 4 * GDN_HEADS], ((0, 0), (0, 128 - 4 * GDN_HEADS))).astype(BF16)
    proj, ab = _inproj_even_call(h, mod[0], w_main, w_ab, nb, period, nctx)
    pad_row = lambda a: jnp.pad(a.reshape(1, -1), ((0, 0), (0, 128 - 2 * GDN_HEADS)))
    qkvn, kt3, gh, gct3 = _gdn_prep_call(proj, ab, e_conv_qkv[0], pad_row(e_a_log[0]), pad_row(e_dt_bias[0]),
                                         period, nctx)
    o = _gdn_scan_call(qkvn, kt3, gh, gct3, nb, period, nctx)

    ops = _s5_tables_call(e_lam_re[0], e_lam_im[0], e_log_dt[0], e_b_re[0], e_b_im[0], e_c_re[0], e_c_im[0])
    u_slabs = proj[:, 4 * w:].reshape(t // S5_CHUNK, S5_CHUNK, d - w).transpose(1, 0, 2)
    y = _s5_call(u_slabs, ops, nb, period, nctx).transpose(1, 0, 2).reshape(t, d - w)

    h = _merge_call(o, proj, y, h, mod[0], row(e_gdn_norm[0]), row(e_d_skip[0]), e_w_glu[0].astype(BF16),
                    row(e_b_glu[0]), e_w_out[0].astype(BF16), row(ln1_g[0]), row(ln1_b[0]), nb, period, nctx)
    w_up, w_down = ffn_w_up.astype(BF16), ffn_w_down.astype(BF16)
    conv_b = ffn_conv_b[:, None, :]
    h = _ffn_call(h, mod[0], 0, w_up, ffn_conv_w, conv_b, w_down, row(ln2_g[0]), row(ln2_b[0]), nb, period, nctx)

    cos, sin = _rope_tables(seq, nctx)
    qkv = _qkv_call(h, mod[1], o_w_in[0].astype(BF16), cos, sin, row(o_q_norm[0]), row(o_k_norm[0]),
                    nb, period, nctx)
    att = _attn_call(qkv, nb, seq, nctx)
    hl = _outproj_call(att, h, mod[1], o_w_out[0].astype(BF16), row(ln1_g[1]), row(ln1_b[1]), nb, seq, nctx)
    out = _ffn_call(hl, mod[1], 1, w_up, ffn_conv_w, conv_b, w_down, row(ln2_g[1]), row(ln2_b[1]), nb, seq, 0)
    return out.reshape(nb, seq, d)
```

```python
import functools
import math

import jax
import jax.numpy as jnp
from jax import lax
from jax.experimental import pallas as pl
from jax.experimental.pallas import tpu as pltpu

F32 = jnp.float32
BF16 = jnp.bfloat16

DEPTH = 2
GDN_HEADS = 8
HEAD_DIM = 128
GDN_WIDTH = GDN_HEADS * HEAD_DIM
GDN_CHUNK = 128
GDN_INTRA_BATCH = 17
GDN_SOLVE_BASE = 8
S5_GROUP = 16
S5_STATE = 64
S5_CHUNK = 16
S5_TILE = S5_CHUNK * S5_GROUP
S5_GROUPS_PER_STEP = 4
S5_TABLE_GROUPS = 8
ATT_HEADS = 16
ATT_KV_HEADS = 4
ATT_GROUP = ATT_HEADS // ATT_KV_HEADS
ATT_SCALE = HEAD_DIM ** -0.5
Q_SCALE = ATT_SCALE * math.log2(math.e)
ATT_Q_TILE = 256
ATT_KEY_BLOCKS = 4
MXU_TILE = 256
GRID_W = 64
ROPE_THETA = 10000.0
PROJ_ROW_TILE = 1088
FFN_CHUNK = 512
FFN_SLAB = 256
ALPHA = (2 * DEPTH) ** 0.25
LN_EPS = 1e-6
RMS_EPS = 1e-6
L2_EPS = 1e-6
VMEM_LIMIT_BYTES = 56 * 1024 * 1024


def _pick(n, target, mult):
    best = None
    for d in range(mult, min(n, target) + 1, mult):
        if n % d == 0:
            best = d
    assert best is not None, (n, target, mult)
    return best


def _params(*sem):
    return pltpu.CompilerParams(dimension_semantics=sem, vmem_limit_bytes=VMEM_LIMIT_BYTES)


def _dot(a, b):
    return jnp.dot(a, b, preferred_element_type=F32)


def _dot_nt(a, b):
    return lax.dot_general(a, b, (((1,), (1,)), ((), ())), preferred_element_type=F32)


def _split3(a):
    p0 = a.astype(BF16)
    r = a - p0.astype(F32)
    p1 = r.astype(BF16)
    p2 = (r - p1.astype(F32)).astype(BF16)
    return p0, p1, p2


def _rows_in_batch(i, tm, tiles_per_batch):
    return (i % tiles_per_batch) * tm + lax.broadcasted_iota(jnp.int32, (tm, 1), 0)


def _modulate(x, is_ctx, mb_ref, mc_ref, shift_i, scale_i):
    scale = jnp.where(is_ctx, mc_ref[scale_i:scale_i + 1, :], mb_ref[scale_i:scale_i + 1, :])
    shift = jnp.where(is_ctx, mc_ref[shift_i:shift_i + 1, :], mb_ref[shift_i:shift_i + 1, :])
    return x * (1.0 + scale) + shift


def _layernorm(r, g, b):
    xc = r - jnp.mean(r, -1, keepdims=True)
    var = jnp.mean(xc * xc, -1, keepdims=True)
    return xc * lax.rsqrt(var + LN_EPS) * g + b


def _silu(x):
    return x * jax.nn.sigmoid(x)


def _seq_edges(rib, ctx, period):
    has_prev = (rib != 0) & (rib != ctx)
    has_next = (rib != ctx - 1) & (rib != period - 1)
    return has_prev, has_next


def _mod_body(c_ref, w_ref, b_ref, o_ref):
    s = _silu(c_ref[...])
    o_ref[...] = _dot(s.astype(BF16), w_ref[...].astype(BF16)) + b_ref[...]


def _mod_call(cs, mod_w, mod_b):
    depth, d, n = mod_w.shape
    tn = _pick(n, 1024, 128)
    return pl.pallas_call(
        _mod_body,
        grid=(depth, n // tn),
        in_specs=[pl.BlockSpec((8, d), lambda l, j: (0, 0)),
                  pl.BlockSpec((None, d, tn), lambda l, j: (l, 0, j)),
                  pl.BlockSpec((None, 1, tn), lambda l, j: (l, 0, j))],
        out_specs=pl.BlockSpec((None, 8, tn), lambda l, j: (l, 0, j)),
        out_shape=jax.ShapeDtypeStruct((depth, 8, n), F32),
        compiler_params=_params("arbitrary", "arbitrary"),
        name="adaln_mod",
    )(cs, mod_w, mod_b.reshape(depth, 1, n))


def _inproj_even_body(x_ref, mb_ref, mc_ref, w_ref, wab_ref, o_ref, ab_ref, xb_ref, *, tm, tpb, ctx):
    i = pl.program_id(0)

    @pl.when(pl.program_id(1) == 0)
    def _():
        rib = _rows_in_batch(i, tm, tpb)
        xb = _modulate(x_ref[...], rib < ctx, mb_ref, mc_ref, 0, 1).astype(BF16)
        xb_ref[...] = xb
        ab_ref[...] = _dot(xb, wab_ref[...])

    o_ref[...] = _dot(xb_ref[...], w_ref[...]).astype(o_ref.dtype)


def _inproj_even_call(h, mod, w_main, w_ab, nb, period, ctx):
    t, d = h.shape
    n = w_main.shape[1]
    tm = _pick(period, PROJ_ROW_TILE, 16)
    tn = _pick(n, 1280, 256)
    tpb = period // tm
    body = functools.partial(_inproj_even_body, tm=tm, tpb=tpb, ctx=ctx)
    return pl.pallas_call(
        body,
        grid=(t // tm, n // tn),
        in_specs=[pl.BlockSpec((tm, d), lambda i, j: (i, 0)),
                  pl.BlockSpec((None, 6, d), lambda i, j: (i // tpb, 0, 0)),
                  pl.BlockSpec((None, 6, d), lambda i, j: (nb, 0, 0)),
                  pl.BlockSpec((d, tn), lambda i, j: (0, j)),
                  pl.BlockSpec((d, 128), lambda i, j: (0, 0))],
        out_specs=[pl.BlockSpec((tm, tn), lambda i, j: (i, j)),
                   pl.BlockSpec((tm, 128), lambda i, j: (i, 0))],
        out_shape=[jax.ShapeDtypeStruct((t, n), BF16), jax.ShapeDtypeStruct((t, 128), F32)],
        scratch_shapes=[pltpu.VMEM((tm, d), BF16)],
        compiler_params=_params("arbitrary", "arbitrary"),
        name="even_inproj",
    )(h, mod, mod, w_main, w_ab)


def _gdn_prep_body(x_ref, xn_ref, xp_ref, cw_ref, ab_ref, alog_ref, dtb_ref, tril_ref, triu_ref, e_ref,
                   qkv_ref, kt_ref, gh_ref, gct_ref, ext_ref, *, tr, tpb, ctx, period):
    i = pl.program_id(0)
    sec = pl.program_id(1)
    rib = _rows_in_batch(i, tr, tpb)
    has_prev, has_next = _seq_edges(rib, ctx, period)
    ext_ref[0:16, :] = xp_ref[...].astype(F32)
    ext_ref[16:tr + 16, :] = x_ref[...].astype(F32)
    ext_ref[tr + 16:tr + 32, :] = xn_ref[...].astype(F32)
    cw = cw_ref[...]
    y = (jnp.where(has_prev, ext_ref[15:tr + 15, :], 0.0) * cw[0:1] + ext_ref[16:tr + 16, :] * cw[1:2]
         + jnp.where(has_next, ext_ref[17:tr + 17, :], 0.0) * cw[2:3])
    y = _silu(y)

    def l2n(scale):
        parts = []
        for hh in range(GDN_HEADS):
            yh = y[:, hh * HEAD_DIM:(hh + 1) * HEAD_DIM]
            parts.append(yh * (lax.rsqrt(jnp.sum(yh * yh, -1, keepdims=True) + L2_EPS) * scale))
        return jnp.concatenate(parts, axis=1)

    @pl.when(sec == 0)
    def _():
        qkv_ref[...] = l2n(HEAD_DIM ** -0.5).astype(BF16)
        ab = ab_ref[...]
        lane = lax.broadcasted_iota(jnp.int32, (1, 128), 1)
        xg = ab + dtb_ref[...]
        softplus = jnp.maximum(xg, 0.0) + jnp.log1p(jnp.exp(-jnp.abs(xg)))
        g = -jnp.exp(alog_ref[...]) * softplus
        g0, g1, g2 = _split3(g)
        tril = tril_ref[...]
        triu = triu_ref[...]
        pre = _dot(tril, g0) + _dot(tril, g1) + _dot(tril, g2)
        suf = _dot(triu, g0) + _dot(triu, g1) + _dot(triu, g2)
        gc = jnp.where(lane < GDN_HEADS, pre, suf)
        vals = jnp.where(lane < 2 * GDN_HEADS, gc, jax.nn.sigmoid(ab))
        v0, v1, v2 = _split3(vals)
        e = e_ref[...]
        gh_ref[...] = _dot(v0, e) + _dot(v1, e) + _dot(v2, e)
        gct = gc.T
        for c in range(tr // GDN_CHUNK):
            gct_ref[c] = gct[:2 * GDN_HEADS, c * GDN_CHUNK:(c + 1) * GDN_CHUNK]

    @pl.when(sec == 1)
    def _():
        kn = l2n(1.0)
        qkv_ref[...] = kn.astype(BF16)
        knt = kn.T
        for c in range(tr // GDN_CHUNK):
            kt_ref[c] = knt[:, c * GDN_CHUNK:(c + 1) * GDN_CHUNK].astype(BF16)

    @pl.when(sec == 2)
    def _():
        qkv_ref[...] = y.astype(BF16)


def _gdn_prep_call(proj, ab, conv_w, alog_row, dtb_row, period, ctx):
    t = proj.shape[0]
    w = GDN_WIDTH
    tr = _pick(period, 256, GDN_CHUNK)
    tpb = period // tr
    hb = tr // 16
    nh = t // 16
    nck = tr // GDN_CHUNK
    r = jnp.arange(tr)
    same = (r[:, None] // GDN_CHUNK) == (r[None, :] // GDN_CHUNK)
    tril = (same & (r[:, None] >= r[None, :])).astype(BF16)
    triu = (same & (r[:, None] <= r[None, :])).astype(BF16)
    src = jnp.arange(128)[:, None]
    dst = jnp.arange(w)[None, :]
    expand = ((dst % HEAD_DIM < 4) & (src == (dst % HEAD_DIM) * GDN_HEADS + dst // HEAD_DIM)).astype(BF16)
    body = functools.partial(_gdn_prep_body, tr=tr, tpb=tpb, ctx=ctx, period=period)
    return pl.pallas_call(
        body,
        grid=(t // tr, 3),
        in_specs=[pl.BlockSpec((tr, w), lambda i, s: (i, s)),
                  pl.BlockSpec((16, w), lambda i, s: (jnp.minimum((i + 1) * hb, nh - 1), s)),
                  pl.BlockSpec((16, w), lambda i, s: (jnp.maximum(i * hb - 1, 0), s)),
                  pl.BlockSpec((3, w), lambda i, s: (0, s)),
                  pl.BlockSpec((tr, 128), lambda i, s: (i, 0)),
                  pl.BlockSpec((1, 128), lambda i, s: (0, 0)),
                  pl.BlockSpec((1, 128), lambda i, s: (0, 0)),
                  pl.BlockSpec((tr, tr), lambda i, s: (0, 0)),
                  pl.BlockSpec((tr, tr), lambda i, s: (0, 0)),
                  pl.BlockSpec((128, w), lambda i, s: (0, 0))],
        out_specs=[pl.BlockSpec((tr, w), lambda i, s: (i, s)),
                   pl.BlockSpec((nck, w, GDN_CHUNK), lambda i, s: (i, 0, 0)),
                   pl.BlockSpec((tr, w), lambda i, s: (i, 0)),
                   pl.BlockSpec((nck, 2 * GDN_HEADS, GDN_CHUNK), lambda i, s: (i, 0, 0))],
        out_shape=[jax.ShapeDtypeStruct((t, 3 * w), BF16),
                   jax.ShapeDtypeStruct((t // GDN_CHUNK, w, GDN_CHUNK), BF16),
                   jax.ShapeDtypeStruct((t, w), F32),
                   jax.ShapeDtypeStruct((t // GDN_CHUNK, 2 * GDN_HEADS, GDN_CHUNK), F32)],
        scratch_shapes=[pltpu.VMEM((tr + 32, w), F32)],
        compiler_params=_params("arbitrary", "arbitrary"),
        name="gdn_prep",
    )(proj, proj, proj, conv_w, ab, alog_row, dtb_row, tril, triu, expand)


def _gdn_scan_body(q_ref, k_ref, v_ref, kt_ref, gh_ref, gct_ref, o_ref,
                   aqf, aqb, bsf, bsb, egf, egb, *, nch, nctx, batch):
    head = pl.program_id(1)
    cs = GDN_CHUNK
    ii = lax.broadcasted_iota(jnp.int32, (cs, cs), 0)
    jj = lax.broadcasted_iota(jnp.int32, (cs, cs), 1)
    masks = ((ii >= jj, ii > jj), (ii <= jj, ii < jj))
    same_block = {}
    width = GDN_SOLVE_BASE
    while width <= cs:
        shift = width.bit_length() - 1
        same_block[width] = jnp.right_shift(ii, shift) == jnp.right_shift(jj, shift)
        width *= 2
    hd = HEAD_DIM
    aqs, bss, egs = (aqf, aqb), (bsf, bsb), (egf, egb)

    def intra(first, count):
        chunks = [first + j for j in range(count)]
        r0s = [pl.multiple_of(c * cs, cs) for c in chunks]
        qs = [q_ref[pl.ds(r0, cs), :] for r0 in r0s]
        ks = [k_ref[pl.ds(r0, cs), :] for r0 in r0s]
        vs = [v_ref[pl.ds(r0, cs), :] for r0 in r0s]
        kts = [kt_ref[c] for c in chunks]
        raws = [_dot(jnp.concatenate([q, k], axis=0), kt) for q, k, kt in zip(qs, ks, kts)]
        raw_q = [r[:cs] for r in raws]
        raw_k = [r[cs:] for r in raws]
        pms, xs, atts, kps, qes, gends = [], [], [], [], [], []
        for j, c in enumerate(chunks):
            g4 = gh_ref[pl.ds(r0s[j], cs), :]
            qf, kf, vf = qs[j].astype(F32), ks[j].astype(F32), vs[j].astype(F32)
            ktf = kts[j].astype(F32)
            for d in (0, 1):
                incl, strict = masks[d]
                gcol = g4[:, d:d + 1]
                bcol = g4[:, 2 + d:3 + d]
                grow = gct_ref[c, pl.ds(d * GDN_HEADS + head, 1), :]
                decay = jnp.where(incl, jnp.exp(jnp.where(incl, gcol - grow, 0.0)), 0.0)
                eg = jnp.exp(gcol)
                gend = grow[:, cs - 1:cs] if d == 0 else grow[:, 0:1]
                pms.append(jnp.where(strict, -(raw_k[j] * bcol) * decay, 0.0))
                xs.append(jnp.concatenate([vf * bcol, kf * (bcol * eg)], axis=1))
                atts.append((raw_q[j] * decay).astype(BF16))
                kps.append((ktf * jnp.exp(gend - grow)).astype(BF16))
                qes.append(qf * eg)
                gends.append(gend)
        bfs = lambda vals: [a.astype(BF16) for a in vals]
        pds = bfs([jnp.where(same_block[GDN_SOLVE_BASE], pm, 0.0) for pm in pms])
        ns = [pd.astype(F32) for pd in pds]
        pw = pds
        span = 1
        while 2 * span < GDN_SOLVE_BASE:
            sq = [_dot(p, p) for p in pw]
            pw = bfs(sq)
            ns = [n + s + _dot(p, n.astype(BF16)) for n, s, p in zip(ns, sq, pw)]
            span *= 2
        width = GDN_SOLVE_BASE
        while width < cs:
            couple = same_block[2 * width] & ~same_block[width]
            qs_ = [jnp.where(couple, pm, 0.0) for pm in pms]
            nbs = bfs(ns)
            bs_ = [q + _dot(nb, q.astype(BF16)) for q, nb in zip(qs_, nbs)]
            ns = [n + b + _dot(b.astype(BF16), nb) for n, b, nb in zip(ns, bs_, nbs)]
            width *= 2
        xs = [x + _dot(n.astype(BF16), x.astype(BF16)) for n, x in zip(ns, xs)]
        xbs = [x.astype(BF16) for x in xs]
        both = [_dot(jnp.concatenate([att, kp], axis=0), xb) for att, kp, xb in zip(atts, kps, xbs)]
        aws = [r[:cs] for r in both]
        kxs = [r[cs:] for r in both]
        for j, c in enumerate(chunks):
            for d in (0, 1):
                n = 2 * j + d
                aqs[d][c, 0:hd, :] = (-kxs[n][:, hd:]).astype(BF16)
                aqs[d][c, hd:hd + cs, :] = (qes[n] - aws[n][:, hd:]).astype(BF16)
                bss[d][c] = kxs[n][:, :hd]
                egs[d][c] = jnp.broadcast_to(jnp.exp(gends[n]), (8, hd))
            o_ref[pl.ds(r0s[j], cs), :] = aws[2 * j][:, :hd] + aws[2 * j + 1][:, :hd]

    def inter(i, carry):
        cb = jnp.where(i < nctx, nctx - 1 - i, nch - 1 - (i - nctx))
        new = []
        for d, c in ((0, i), (1, cb)):
            s = carry[d]
            r0 = pl.multiple_of(c * cs, cs)
            r = _dot(aqs[d][c], s.astype(BF16))
            o_ref[pl.ds(r0, cs), :] += r[hd:]
            new.append(s * egs[d][c][0:1, :] + r[:hd] + bss[d][c])
        return tuple(new)

    full = nch // batch

    def intra_step(it, carry):
        intra(it * batch, batch)
        return carry

    lax.fori_loop(0, full, intra_step, 0)
    if nch > full * batch:
        intra(full * batch, nch - full * batch)
    zero = jnp.zeros((hd, hd), F32)
    lax.fori_loop(0, nch, inter, (zero, zero), unroll=2)


def _gdn_scan_call(qkvn, kt3, gh, gct3, nb, period, ctx):
    t = qkvn.shape[0]
    nch = period // GDN_CHUNK
    nctx = ctx // GDN_CHUNK
    hd = HEAD_DIM
    body = functools.partial(_gdn_scan_body, nch=nch, nctx=nctx, batch=min(nch, GDN_INTRA_BATCH))
    dirs2 = lambda shape, dt: [pltpu.VMEM(shape, dt), pltpu.VMEM(shape, dt)]
    return pl.pallas_call(
        body,
        grid=(nb, GDN_HEADS),
        in_specs=[pl.BlockSpec((period, hd), lambda b, h: (b, h)),
                  pl.BlockSpec((period, hd), lambda b, h: (b, GDN_HEADS + h)),
                  pl.BlockSpec((period, hd), lambda b, h: (b, 2 * GDN_HEADS + h)),
                  pl.BlockSpec((nch, hd, GDN_CHUNK), lambda b, h: (b, h, 0)),
                  pl.BlockSpec((period, hd), lambda b, h: (b, h)),
                  pl.BlockSpec((nch, 2 * GDN_HEADS, GDN_CHUNK), lambda b, h: (b, 0, 0))],
        out_specs=pl.BlockSpec((period, hd), lambda b, h: (b, h)),
        out_shape=jax.ShapeDtypeStruct((t, GDN_WIDTH), F32),
        scratch_shapes=(dirs2((nch, hd + GDN_CHUNK, hd), BF16) + dirs2((nch, hd, hd), F32)
                        + dirs2((nch, 8, hd), F32)),
        compiler_params=_params("arbitrary", "arbitrary"),
        name="gdn_scan",
    )(qkvn, qkvn, qkvn, kt3, gh, gct3)


def _s5_tables_body(*refs):
    lax.fori_loop(0, refs[0].shape[1], functools.partial(_s5_tables_group, refs), 0)


def _s5_tables_group(refs, q, carry):
    (lr_ref, li_ref, ldt_ref, btr_ref, bti_ref, cr_ref, ci_ref,
     wir_ref, wii_ref, wor_ref, woi_ref, ktf_ref, ktb_ref, lam_ref) = refs
    n, cg = S5_CHUNK, S5_GROUP
    kk = lax.broadcasted_iota(jnp.int32, (24, 1), 0).astype(F32)
    lane = lax.broadcasted_iota(jnp.int32, (1, S5_TILE), 1)

    def dot3_nt(a, b):
        a0 = a.astype(BF16)
        a1 = (a - a0.astype(F32)).astype(BF16)
        b0 = b.astype(BF16)
        b1 = (b - b0.astype(F32)).astype(BF16)
        return _dot_nt(a0, b0) + _dot_nt(a0, b1) + _dot_nt(a1, b0)

    win_r, win_i, wout_r, wout_i, lam_r, lam_i = [], [], [], [], [], []
    for d in (0, 1):
        lr = lr_ref[d, q]
        li = li_ref[d, q]
        dt = jnp.exp(ldt_ref[d, q])
        mag = jnp.exp(kk * (lr * dt))
        ang = kk * (li * dt)
        er = mag * jnp.cos(ang)
        ei = mag * jnp.sin(ang)
        nr = er[1:2] - 1.0
        ni = ei[1:2]
        den = lr * lr + li * li
        cfr = (nr * lr + ni * li) / den
        cfi = (ni * lr - nr * li) / den
        btr = btr_ref[d, q]
        bti = bti_ref[d, q]
        bbr = btr * cfr - bti * cfi
        bbi = btr * cfi + bti * cfr
        cr = cr_ref[d, q]
        ci = ci_ref[d, q]
        a_r, a_i, cm_r, cm_i, c0_r, c0_i = [], [], [], [], [], []
        for k in range(n):
            e_r, e_i = er[k:k + 1], ei[k:k + 1]
            a_r.append(bbr * e_r - bbi * e_i)
            a_i.append(bbr * e_i + bbi * e_r)
            c0_r.append(cr * e_r - ci * e_i)
            c0_i.append(cr * e_i + ci * e_r)
            f_r, f_i = er[k + 1:k + 2], ei[k + 1:k + 2]
            cm_r.append(cr * f_r - ci * f_i)
            cm_i.append(-(cr * f_i + ci * f_r))
        fwd = d == 0
        down = list(range(n - 1, -1, -1))
        up = list(range(n))
        pick = lambda parts, order: jnp.concatenate([parts[k] for k in order], axis=0)
        win_r.append(pick(a_r, down if fwd else up))
        win_i.append(pick(a_i, down if fwd else up))
        wout_r.append(pick(cm_r, up if fwd else down))
        wout_i.append(pick(cm_i, up if fwd else down))
        lam_r.append(er[n:n + 1])
        lam_i.append(ei[n:n + 1])
        order = up if fwd else down
        strip = dot3_nt(bbr, pick(c0_r, order)) - dot3_nt(bbi, pick(c0_i, order))
        for s in range(n):
            rows = slice(s * cg, (s + 1) * cg)
            if fwd:
                blk = strip if s == 0 else jnp.where(lane >= s * cg, pltpu.roll(strip, s * cg, 1), 0.0)
                ktf_ref[q, rows, :] = blk.astype(BF16)
            else:
                sh = (n - 1 - s) * cg
                blk = strip if sh == 0 else jnp.where(lane < (s + 1) * cg, pltpu.roll(strip, S5_TILE - sh, 1), 0.0)
                ktb_ref[q, rows, :] = blk.astype(BF16)
    wir_ref[q] = jnp.concatenate(win_r, axis=1).astype(BF16)
    wii_ref[q] = jnp.concatenate(win_i, axis=1).astype(BF16)
    wor_ref[q] = jnp.concatenate(wout_r, axis=1).T.astype(BF16)
    woi_ref[q] = jnp.concatenate(wout_i, axis=1).T.astype(BF16)
    lam_ref[q] = jnp.concatenate([jnp.concatenate(lam_r, axis=1), jnp.concatenate(lam_i, axis=1)], axis=0)
    return carry


def _s5_tables_call(lam_re, lam_im, log_dt, b_re, b_im, c_re, c_im):
    _, g, p = lam_re.shape
    cg = S5_GROUP
    sw = 2 * p
    row4 = lambda a: a.reshape(2, g, 1, p)
    per = _pick(g, S5_TABLE_GROUPS, 1)
    ispec = lambda r: pl.BlockSpec((2, per, r, p), lambda gi: (0, gi, 0, 0))
    ospec = lambda r, c: pl.BlockSpec((per, r, c), lambda gi: (gi, 0, 0))
    ldt = jnp.broadcast_to(log_dt[:, :, None, None], (2, g, 1, p))
    wir, wii, wor, woi, ktf, ktb, lam = pl.pallas_call(
        _s5_tables_body,
        grid=(g // per,),
        in_specs=[ispec(1), ispec(1), ispec(1), ispec(cg), ispec(cg), ispec(cg), ispec(cg)],
        out_specs=[ospec(S5_TILE, sw), ospec(S5_TILE, sw), ospec(sw, S5_TILE), ospec(sw, S5_TILE),
                   ospec(S5_TILE, S5_TILE), ospec(S5_TILE, S5_TILE), ospec(2, sw)],
        out_shape=[jax.ShapeDtypeStruct((g, S5_TILE, sw), BF16)] * 2 + [jax.ShapeDtypeStruct((g, sw, S5_TILE), BF16)] * 2
                  + [jax.ShapeDtypeStruct((g, S5_TILE, S5_TILE), BF16)] * 2 + [jax.ShapeDtypeStruct((g, 2, sw), F32)],
        compiler_params=_params("arbitrary"),
        name="s5_tables",
    )(row4(lam_re), row4(lam_im), ldt, jnp.swapaxes(b_re, -1, -2), jnp.swapaxes(b_im, -1, -2), c_re, c_im)
    gb = S5_GROUPS_PER_STEP
    lam_r = lam[:, 0, :].reshape(g // gb, 1, gb * sw)
    lam_i = lam[:, 1, :].reshape(g // gb, 1, gb * sw)
    return ktf, ktb, wir, wii, wor, woi, lam_r, lam_i


def _s5_gather_chunks(u_ref, l_ref):
    lanes = u_ref.shape[2]
    for s in range(S5_CHUNK):
        l_ref[:, s * lanes:(s + 1) * lanes] = u_ref[s]


def _s5_in_body(u_ref, pm_ref, wr_ref, wi_ref, ub_ref, x_ref, l_ref):
    @pl.when(pl.program_id(1) == 0)
    def _():
        _s5_gather_chunks(u_ref, l_ref)

    ub = _dot(l_ref[...], pm_ref[...]).astype(BF16)
    ub_ref[...] = ub
    sw = 2 * S5_STATE
    for q in range(ub.shape[1] // S5_TILE):
        ug = ub[:, q * S5_TILE:(q + 1) * S5_TILE]
        x_ref[:, 2 * q * sw:(2 * q + 1) * sw] = _dot(ug, wr_ref[q])
        x_ref[:, (2 * q + 1) * sw:(2 * q + 2) * sw] = _dot(ug, wi_ref[q])


def _s5_out_body(ub_ref, s_ref, ktf_ref, ktb_ref, wor_ref, woi_ref, pmt_ref, y_ref, y8_ref, *, col_tiles):
    n = pl.program_id(1)
    sw = 2 * S5_STATE

    @pl.when(n == 0)
    def _():
        for g in range(ub_ref.shape[1] // S5_TILE):
            ug = ub_ref[:, g * S5_TILE:(g + 1) * S5_TILE]
            s_r = s_ref[:, 2 * g * sw:(2 * g + 1) * sw]
            s_i = s_ref[:, (2 * g + 1) * sw:(2 * g + 2) * sw]
            y = _dot(ug, ktf_ref[g]) + _dot(ug, ktb_ref[g]) + _dot(s_r, wor_ref[g]) + _dot(s_i, woi_ref[g])
            y8_ref[:, g * S5_TILE:(g + 1) * S5_TILE] = y.astype(BF16)

    yp = _dot(y8_ref[...], pmt_ref[...])
    lanes = y_ref.shape[2]
    per = yp.shape[1] // lanes
    for nn in range(col_tiles):
        @pl.when(n == nn)
        def _(nn=nn):
            for q in range(per):
                y_ref[nn * per + q] = yp[:, q * lanes:(q + 1) * lanes].astype(y_ref.dtype)


def _s5_scan_body(x_ref, lr_ref, li_ref, s_ref, *scratch, nb, nch, nctx):
    gb = S5_GROUPS_PER_STEP
    sw = 2 * S5_STATE
    xr, xi, sfr, sfi, sbr, sbi = (scratch[k * gb:(k + 1) * gb] for k in range(6))
    for g in range(gb):
        xr[g][...] = x_ref[:, 2 * g * sw:(2 * g + 1) * sw]
        xi[g][...] = x_ref[:, (2 * g + 1) * sw:(2 * g + 2) * sw]
    is_f = lax.broadcasted_iota(jnp.int32, (1, sw), 1) < S5_STATE
    ar = [lr_ref[:, g * sw:(g + 1) * sw] for g in range(gb)]
    ai = [li_ref[:, g * sw:(g + 1) * sw] for g in range(gb)]

    def step(i, carry):
        cb = jnp.where(i < nctx, nctx - 1 - i, nch - 1 - (i - nctx))
        at_f = pl.ds(i, nb, stride=nch)
        at_b = pl.ds(cb, nb, stride=nch)
        new = []
        for g in range(gb):
            sr, si = carry[2 * g], carry[2 * g + 1]
            sfr[g][at_f, :] = sr
            sfi[g][at_f, :] = si
            sbr[g][at_b, :] = sr
            sbi[g][at_b, :] = si
            inr = jnp.where(is_f, xr[g][at_f, :], xr[g][at_b, :])
            ini = jnp.where(is_f, xi[g][at_f, :], xi[g][at_b, :])
            new.append(ar[g] * sr - ai[g] * si + inr)
            new.append(ar[g] * si + ai[g] * sr + ini)
        return tuple(new)

    zero = jnp.zeros((nb, sw), F32)
    lax.fori_loop(0, nch, step, (zero,) * (2 * gb))
    for g in range(gb):
        s_ref[:, 2 * g * sw:(2 * g + 1) * sw] = jnp.where(is_f, sfr[g][...], sbr[g][...]).astype(BF16)
        s_ref[:, (2 * g + 1) * sw:(2 * g + 2) * sw] = jnp.where(is_f, sfi[g][...], sbi[g][...]).astype(BF16)


def _s5_call(u_slabs, ops, nb, period, ctx):
    ktf, ktb, wir, wii, wor, woi, lam_r, lam_i = ops
    _, rows, width = u_slabs.shape
    groups = width // S5_GROUP
    lanes = 128
    lg = lanes // S5_GROUP
    nblk = width // lanes
    blk_w = S5_CHUNK * lanes
    col = 2 * S5_TILE
    col_tiles = blk_w // col
    sw = 2 * S5_STATE
    gb = S5_GROUPS_PER_STEP
    nch = period // S5_CHUNK
    nctx = ctx // S5_CHUNK
    src = jnp.arange(blk_w)
    dst = (src % lanes // S5_GROUP) * S5_TILE + (src // lanes) * S5_GROUP + src % S5_GROUP
    perm = (dst[:, None] == jnp.arange(blk_w)[None, :]).astype(BF16)
    perm_t = perm.T

    u_perm, xin = pl.pallas_call(
        _s5_in_body,
        grid=(nblk, col_tiles),
        in_specs=[pl.BlockSpec((S5_CHUNK, rows, lanes), lambda j, n: (0, 0, j)),
                  pl.BlockSpec((blk_w, col), lambda j, n: (0, n)),
                  pl.BlockSpec((col // S5_TILE, S5_TILE, sw), lambda j, n: (j * col_tiles + n, 0, 0)),
                  pl.BlockSpec((col // S5_TILE, S5_TILE, sw), lambda j, n: (j * col_tiles + n, 0, 0))],
        out_specs=[pl.BlockSpec((rows, col), lambda j, n: (0, j * col_tiles + n)),
                   pl.BlockSpec((rows, col), lambda j, n: (0, j * col_tiles + n))],
        out_shape=[jax.ShapeDtypeStruct((rows, groups * S5_TILE), BF16),
                   jax.ShapeDtypeStruct((rows, groups * 2 * sw), F32)],
        scratch_shapes=[pltpu.VMEM((rows, blk_w), BF16)],
        compiler_params=_params("arbitrary", "arbitrary"),
        name="s5_in",
    )(u_slabs, perm, wir, wii)

    states = pl.pallas_call(
        functools.partial(_s5_scan_body, nb=nb, nch=nch, nctx=nctx),
        grid=(groups // gb,),
        in_specs=[pl.BlockSpec((rows, gb * 2 * sw), lambda i: (0, i)),
                  pl.BlockSpec((None, 1, gb * sw), lambda i: (i, 0, 0)),
                  pl.BlockSpec((None, 1, gb * sw), lambda i: (i, 0, 0))],
        out_specs=pl.BlockSpec((rows, gb * 2 * sw), lambda i: (0, i)),
        out_shape=jax.ShapeDtypeStruct((rows, groups * 2 * sw), BF16),
        scratch_shapes=[pltpu.VMEM((rows, sw), F32) for _ in range(6 * gb)],
        compiler_params=_params("arbitrary"),
        name="s5_scan",
    )(xin, lam_r, lam_i)

    wspec = lambda r, c: pl.BlockSpec((lg, r, c), lambda j, n: (j, 0, 0))
    return pl.pallas_call(
        functools.partial(_s5_out_body, col_tiles=col_tiles),
        grid=(nblk, col_tiles),
        in_specs=[pl.BlockSpec((rows, lg * S5_TILE), lambda j, n: (0, j)),
                  pl.BlockSpec((rows, lg * 2 * sw), lambda j, n: (0, j)),
                  wspec(S5_TILE, S5_TILE), wspec(S5_TILE, S5_TILE), wspec(sw, S5_TILE), wspec(sw, S5_TILE),
                  pl.BlockSpec((blk_w, col), lambda j, n: (0, n))],
        out_specs=pl.BlockSpec((S5_CHUNK, rows, lanes), lambda j, n: (0, 0, j)),
        out_shape=jax.ShapeDtypeStruct((S5_CHUNK, rows, width), BF16),
        scratch_shapes=[pltpu.VMEM((rows, lg * S5_TILE), BF16)],
        compiler_params=_params("arbitrary", "arbitrary"),
        name="s5_out",
    )(u_perm, states, ktf, ktb, wor, woi, perm_t)


def _merge_body(o_ref, z_ref, u_ref, y_ref, h_ref, mb_ref, mc_ref, gn_ref, ds_ref, wglu_ref, bglu_ref,
                wout_ref, lng_ref, lnb_ref, out_ref, cat_ref, *, tm, tpb, ctx):
    rib = _rows_in_batch(pl.program_id(0), tm, tpb)
    gn = gn_ref[...]
    half = tm // 2
    for rows in (slice(0, half), slice(half, tm)):
        for hh in range(GDN_HEADS):
            sl = slice(hh * HEAD_DIM, (hh + 1) * HEAD_DIM)
            oh = o_ref[rows, sl]
            r = lax.rsqrt(jnp.mean(oh * oh, -1, keepdims=True) + RMS_EPS)
            cat_ref[rows, sl] = (oh * r * gn * _silu(z_ref[rows, sl].astype(F32))).astype(BF16)
        s = y_ref[rows, :].astype(F32) + ds_ref[...] * u_ref[rows, :].astype(F32)
        s = 0.5 * s * (1.0 + jnp.tanh(math.sqrt(2.0 / math.pi) * (s + 0.044715 * (s * s * s))))
        s = s * jax.nn.sigmoid(_dot(s.astype(BF16), wglu_ref[...]) + bglu_ref[...])
        cat_ref[rows, GDN_WIDTH:] = s.astype(BF16)
        mix = _dot(cat_ref[rows, :], wout_ref[...])
        gate = jnp.where(rib[rows] < ctx, mc_ref[2:3, :], mb_ref[2:3, :])
        out_ref[rows, :] = _layernorm(ALPHA * h_ref[rows, :] + gate * mix, lng_ref[...], lnb_ref[...])


def _merge_call(o, proj, y, h, mod, gn, d_skip, w_glu, b_glu, w_out, ln_g, ln_b, nb, period, ctx):
    t, d = h.shape
    w = GDN_WIDTH
    sw = d - w
    tm = _pick(period, 544, 32)
    tpb = period // tm
    body = functools.partial(_merge_body, tm=tm, tpb=tpb, ctx=ctx)
    const = lambda r, c: pl.BlockSpec((r, c), lambda i: (0, 0), pipeline_mode=pl.Buffered(1))
    return pl.pallas_call(
        body,
        grid=(t // tm,),
        in_specs=[pl.BlockSpec((tm, w), lambda i: (i, 0)),
                  pl.BlockSpec((tm, w), lambda i: (i, 3)),
                  pl.BlockSpec((tm, sw), lambda i: (i, 4 * w // sw)),
                  pl.BlockSpec((tm, sw), lambda i: (i, 0)),
                  pl.BlockSpec((tm, d), lambda i: (i, 0)),
                  pl.BlockSpec((None, 6, d), lambda i: (i // tpb, 0, 0)),
                  pl.BlockSpec((None, 6, d), lambda i: (nb, 0, 0)),
                  const(1, HEAD_DIM), const(1, sw), const(sw, sw), const(1, sw), const(d, d),
                  const(1, d), const(1, d)],
        out_specs=pl.BlockSpec((tm, d), lambda i: (i, 0)),
        out_shape=jax.ShapeDtypeStruct((t, d), F32),
        scratch_shapes=[pltpu.VMEM((tm, d), BF16)],
        compiler_params=_params("arbitrary"),
        name="even_merge",
    )(o, proj, proj, y, h, mod, mod, gn, d_skip, w_glu, b_glu, w_out, ln_g, ln_b)


def _ffn_body(x_ref, xh_ref, mb_ref, mc_ref, wv_ref, wg_ref, cwv_ref, cwg_ref, cbv_ref, cbg_ref, wd_ref,
              lng_ref, lnb_ref, o_ref, xb_ref, *up_refs, tm, tpb, ctx, period):
    i = pl.program_id(0)
    j = pl.program_id(1)
    rib = _rows_in_batch(i, tm, tpb)
    n = tm + 32

    @pl.when(j == 0)
    def _():
        xb_ref[16:tm + 16, :] = _modulate(x_ref[...], rib < ctx, mb_ref, mc_ref, 3, 4).astype(BF16)
        base = (i % tpb) * tm
        off = lax.broadcasted_iota(jnp.int32, (16, 1), 0)
        xh = xh_ref[...]
        next_ok = (base + tm != period) & (base + tm != ctx)
        prev_ok = (base != 0) & (base != ctx)
        nxt = _modulate(xh[0:16], base + tm + off < ctx, mb_ref, mc_ref, 3, 4)
        prv = _modulate(xh[16:32], base - 16 + off < ctx, mb_ref, mc_ref, 3, 4)
        xb_ref[tm + 16:n, :] = jnp.where(next_ok, nxt, 0.0).astype(BF16)
        xb_ref[0:16, :] = jnp.where(prev_ok, prv, 0.0).astype(BF16)
        o_ref[...] = jnp.zeros_like(o_ref)

    xb = xb_ref[...]
    edge_inside_tile = ctx % tm != 0

    def conv(u_ref, cw_ref, cb_ref, sl):
        cw = cw_ref[:, sl]
        up = u_ref[15:tm + 15, :]
        un = u_ref[17:tm + 17, :]
        if edge_inside_tile:
            up = jnp.where(rib != ctx, up, 0.0)
            un = jnp.where(rib != ctx - 1, un, 0.0)
        return up * cw[0:1] + u_ref[16:tm + 16, :] * cw[1:2] + un * cw[2:3] + cb_ref[:, sl]

    fc = wv_ref.shape[1]
    slabs = [slice(a, a + FFN_SLAB) for a in range(0, fc, FFN_SLAB)]
    for si, sl in enumerate(slabs):
        up_refs[2 * si][...] = _dot(xb, wv_ref[:, sl])
        up_refs[2 * si + 1][...] = _dot(xb, wg_ref[:, sl])
    for si, sl in enumerate(slabs):
        act = conv(up_refs[2 * si], cwv_ref, cbv_ref, sl) * _silu(conv(up_refs[2 * si + 1], cwg_ref, cbg_ref, sl))
        o_ref[...] += _dot(act.astype(BF16), wd_ref[sl, :])

    @pl.when(j == pl.num_programs(1) - 1)
    def _():
        g = jnp.where(rib < ctx, mc_ref[5:6, :], mb_ref[5:6, :])
        o_ref[...] = _layernorm(ALPHA * x_ref[...] + g * o_ref[...], lng_ref[...], lnb_ref[...])


def _ffn_call(h, mod, layer, w_up, conv_w, conv_b, w_down, ln_g, ln_b, nb, period, ctx):
    t, d = h.shape
    f = w_down.shape[1]
    tm = _pick(period, 544, 16)
    tpb = period // tm
    nt = t // tm
    fc = _pick(f, FFN_CHUNK, FFN_SLAB)
    nf = f // fc
    h3 = h.reshape(nt, tm, d)
    pad = jnp.zeros((1, 16, d), h.dtype)
    halo = jnp.concatenate([jnp.concatenate([h3[1:, :16], pad], 0),
                            jnp.concatenate([pad, h3[:-1, tm - 16:]], 0)], axis=1)
    body = functools.partial(_ffn_body, tm=tm, tpb=tpb, ctx=ctx, period=period)
    return pl.pallas_call(
        body,
        grid=(nt, nf),
        in_specs=[pl.BlockSpec((tm, d), lambda i, j: (i, 0)),
                  pl.BlockSpec((None, 32, d), lambda i, j: (i, 0, 0)),
                  pl.BlockSpec((None, 6, d), lambda i, j: (i // tpb, 0, 0)),
                  pl.BlockSpec((None, 6, d), lambda i, j: (nb, 0, 0)),
                  pl.BlockSpec((None, d, fc), lambda i, j: (layer, 0, j)),
                  pl.BlockSpec((None, d, fc), lambda i, j: (layer, 0, nf + j)),
                  pl.BlockSpec((None, 3, fc), lambda i, j: (layer, 0, j)),
                  pl.BlockSpec((None, 3, fc), lambda i, j: (layer, 0, nf + j)),
                  pl.BlockSpec((None, 1, fc), lambda i, j: (layer, 0, j)),
                  pl.BlockSpec((None, 1, fc), lambda i, j: (layer, 0, nf + j)),
                  pl.BlockSpec((None, fc, d), lambda i, j: (layer, j, 0)),
                  pl.BlockSpec((1, d), lambda i, j: (0, 0)),
                  pl.BlockSpec((1, d), lambda i, j: (0, 0))],
        out_specs=pl.BlockSpec((tm, d), lambda i, j: (i, 0)),
        out_shape=jax.ShapeDtypeStruct((t, d), F32),
        scratch_shapes=([pltpu.VMEM((tm + 32, d), BF16)]
                        + [pltpu.VMEM((tm + 32, FFN_SLAB), F32) for _ in range(2 * fc // FFN_SLAB)]),
        compiler_params=_params("arbitrary", "arbitrary"),
        name="conv_ffn",
    )(h, halo, mod, mod, w_up, w_up, conv_w, conv_w, conv_b, conv_b, w_down, ln_g, ln_b)


def _qkv_body(x_ref, mb_ref, mc_ref, w_ref, cos_ref, sin_ref, qn_ref, kn_ref, o_ref, xb_ref,
              *, tm, tpb, ctx, nq_tiles):
    i = pl.program_id(0)
    j = pl.program_id(1)

    @pl.when(j == 0)
    def _():
        rib = _rows_in_batch(i, tm, tpb)
        xb_ref[...] = _modulate(x_ref[...], rib < ctx, mb_ref, mc_ref, 0, 1).astype(BF16)

    cos = cos_ref[...]
    sin = sin_ref[...]
    lane = lax.broadcasted_iota(jnp.int32, (1, HEAD_DIM), 1)
    first = (lane % 64) < 32
    acc = _dot(xb_ref[...], w_ref[...])
    heads = acc.shape[1] // HEAD_DIM

    def normrope(xh, wn, scale):
        xn = xh * lax.rsqrt(jnp.mean(xh * xh, -1, keepdims=True) + RMS_EPS) * wn
        partner = jnp.where(first, pltpu.roll(xn, HEAD_DIM - 32, 1), pltpu.roll(xn, 32, 1))
        return (xn * cos + partner * sin) * scale

    @pl.when(j < nq_tiles)
    def _():
        for hh in range(heads):
            sl = slice(hh * HEAD_DIM, (hh + 1) * HEAD_DIM)
            o_ref[:, sl] = normrope(acc[:, sl], qn_ref[...], Q_SCALE).astype(BF16)

    @pl.when(j == nq_tiles)
    def _():
        for hh in range(heads):
            sl = slice(hh * HEAD_DIM, (hh + 1) * HEAD_DIM)
            if hh < ATT_KV_HEADS:
                o_ref[:, sl] = normrope(acc[:, sl], kn_ref[...], 1.0).astype(BF16)
            else:
                o_ref[:, sl] = acc[:, sl].astype(BF16)


def _qkv_call(h, mod, w_in, cos, sin, qn, kn, nb, period, ctx):
    t, d = h.shape
    n = w_in.shape[1]
    tn = 2 * ATT_KV_HEADS * HEAD_DIM
    nq_tiles = (n - tn) // tn
    tm = _pick(period, PROJ_ROW_TILE, 16)
    tpb = period // tm
    body = functools.partial(_qkv_body, tm=tm, tpb=tpb, ctx=ctx, nq_tiles=nq_tiles)
    return pl.pallas_call(
        body,
        grid=(t // tm, n // tn),
        in_specs=[pl.BlockSpec((tm, d), lambda i, j: (i, 0)),
                  pl.BlockSpec((None, 6, d), lambda i, j: (i // tpb, 0, 0)),
                  pl.BlockSpec((None, 6, d), lambda i, j: (nb, 0, 0)),
                  pl.BlockSpec((d, tn), lambda i, j: (0, j)),
                  pl.BlockSpec((tm, HEAD_DIM), lambda i, j: (i % tpb, 0)),
                  pl.BlockSpec((tm, HEAD_DIM), lambda i, j: (i % tpb, 0)),
                  pl.BlockSpec((1, HEAD_DIM), lambda i, j: (0, 0)),
                  pl.BlockSpec((1, HEAD_DIM), lambda i, j: (0, 0))],
        out_specs=pl.BlockSpec((tm, tn), lambda i, j: (i, j)),
        out_shape=jax.ShapeDtypeStruct((t, n), BF16),
        scratch_shapes=[pltpu.VMEM((tm, d), BF16)],
        compiler_params=_params("arbitrary", "arbitrary"),
        name="odd_qkv",
    )(h, mod, mod, w_in, cos, sin, qn, kn)


def _rope_tables(seq, ctx):
    rows = seq // GRID_W
    row = jnp.repeat(jnp.arange(rows, dtype=F32), GRID_W)
    col = jnp.tile(jnp.arange(GRID_W, dtype=F32), rows)
    half = HEAD_DIM // 4
    inv = ROPE_THETA ** (-jnp.arange(half, dtype=F32) / half)
    ar = row[:, None] * inv
    ac = col[:, None] * inv
    cos = jnp.concatenate([jnp.cos(ar), jnp.cos(ar), jnp.cos(ac), jnp.cos(ac)], -1)
    sin = jnp.concatenate([-jnp.sin(ar), jnp.sin(ar), -jnp.sin(ac), jnp.sin(ac)], -1)
    cos = jnp.concatenate([jnp.ones((ctx, HEAD_DIM), F32), cos], 0)
    sin = jnp.concatenate([jnp.zeros((ctx, HEAD_DIM), F32), sin], 0)
    return cos, sin


def _attn_body(q_ref, k_ref, v_ref, o_ref, *, tq, key_blocks):
    hd = HEAD_DIM
    q = jnp.concatenate([q_ref[:, g * hd:(g + 1) * hd] for g in range(ATT_GROUP)], axis=0)
    rows = ATT_GROUP * tq
    m = jnp.full((rows, 1), -1e30, F32)
    l = jnp.zeros((rows, 1), F32)
    acc = jnp.zeros((rows, hd), F32)
    s_next = _dot_nt(q, k_ref[key_blocks[0][0]:key_blocks[0][1], :])
    for n, (k0, k1) in enumerate(key_blocks):
        s = s_next
        if n + 1 < len(key_blocks):
            s_next = _dot_nt(q, k_ref[key_blocks[n + 1][0]:key_blocks[n + 1][1], :])
        m_new = jnp.maximum(m, jnp.max(s, -1, keepdims=True))
        a = jnp.exp2(m - m_new)
        p = jnp.exp2(s - m_new)
        l = a * l + jnp.sum(p, -1, keepdims=True)
        acc = a * acc + _dot(p.astype(BF16), v_ref[k0:k1, :])
        m = m_new
    out = acc / l
    for g in range(ATT_GROUP):
        o_ref[:, g * hd:(g + 1) * hd] = out[g * tq:(g + 1) * tq].astype(o_ref.dtype)


def _key_blocks(period, count, align):
    tiles = period // align
    assert tiles * align == period and tiles >= count
    sizes = [tiles // count + (1 if n >= count - tiles % count else 0) for n in range(count)]
    edges = [0]
    for sz in sizes:
        edges.append(edges[-1] + sz * align)
    return tuple(zip(edges[:-1], edges[1:]))


def _attn_call(qkv, nb, seq, ctx):
    period = seq + ctx
    tq = _pick(math.gcd(seq, ctx), ATT_Q_TILE, 16)
    gw = ATT_GROUP * HEAD_DIM
    qpb = seq // tq
    rpb = period // tq
    skip = ctx // tq
    kcol = ATT_HEADS
    vcol = ATT_HEADS + ATT_KV_HEADS
    align = math.gcd(period, MXU_TILE)
    blocks = _key_blocks(period, min(ATT_KEY_BLOCKS, period // align), align)
    body = functools.partial(_attn_body, tq=tq, key_blocks=blocks)
    return pl.pallas_call(
        body,
        grid=(nb, ATT_KV_HEADS, qpb),
        in_specs=[pl.BlockSpec((tq, gw), lambda b, kv, qi: (b * rpb + skip + qi, kv)),
                  pl.BlockSpec((period, HEAD_DIM), lambda b, kv, qi: (b, kcol + kv)),
                  pl.BlockSpec((period, HEAD_DIM), lambda b, kv, qi: (b, vcol + kv))],
        out_specs=pl.BlockSpec((tq, gw), lambda b, kv, qi: (b * qpb + qi, kv)),
        out_shape=jax.ShapeDtypeStruct((nb * seq, ATT_HEADS * HEAD_DIM), BF16),
        compiler_params=_params("arbitrary", "arbitrary", "arbitrary"),
        name="gqa_attention",
    )(qkv, qkv, qkv)


def _outproj_body(a_ref, h_ref, mb_ref, w_ref, lng_ref, lnb_ref, o_ref):
    mix = _dot(a_ref[...], w_ref[...])
    o_ref[...] = _layernorm(ALPHA * h_ref[...] + mb_ref[2:3, :] * mix, lng_ref[...], lnb_ref[...])


def _outproj_call(att, h, mod, w_out, ln_g, ln_b, nb, seq, ctx):
    d = h.shape[1]
    period = seq + ctx
    tm = _pick(math.gcd(seq, ctx), 256, 16)
    qpb = seq // tm
    rpb = period // tm
    skip = ctx // tm
    return pl.pallas_call(
        _outproj_body,
        grid=(nb * qpb,),
        in_specs=[pl.BlockSpec((tm, d), lambda i: (i, 0)),
                  pl.BlockSpec((tm, d), lambda i: ((i // qpb) * rpb + skip + i % qpb, 0)),
                  pl.BlockSpec((None, 6, d), lambda i: (i // qpb, 0, 0)),
                  pl.BlockSpec((d, d), lambda i: (0, 0)),
                  pl.BlockSpec((1, d), lambda i: (0, 0)),
                  pl.BlockSpec((1, d), lambda i: (0, 0))],
        out_specs=pl.BlockSpec((tm, d), lambda i: (i, 0)),
        out_shape=jax.ShapeDtypeStruct((nb * seq, d), F32),
        compiler_params=_params("arbitrary"),
        name="odd_outproj",
    )(att, h, mod, w_out, ln_g, ln_b)


def kernel(x, c, ctx, c_ctx, mod_w, mod_b, ln1_g, ln1_b, ln2_g, ln2_b, ffn_w_up, ffn_conv_w, ffn_conv_b, ffn_w_down, e_w_in, e_conv_qkv, e_a_log, e_dt_bias, e_gdn_norm, e_lam_re, e_lam_im, e_log_dt, e_b_re, e_b_im, e_c_re, e_c_im, e_d_skip, e_w_glu, e_b_glu, e_w_out, o_w_in, o_q_norm, o_k_norm, o_w_out):
    nb, seq, d = x.shape
    nctx = ctx.shape[1]
    period = nctx + seq
    t = nb * period
    w = GDN_WIDTH
    assert mod_w.shape[0] == DEPTH == 2 and nb < 8

    h = jnp.concatenate([ctx, x], axis=1).reshape(t, d)
    cs = jnp.zeros((8, d), F32).at[:nb].set(c).at[nb].set(c_ctx)
    mod = _mod_call(cs, mod_w, mod_b).reshape(DEPTH, 8, 6, d)
    row = lambda a: a.reshape(1, -1)

    w_in = e_w_in[0]
    gates_at = 4 * w
    w_main = jnp.concatenate([w_in[:, :gates_at], w_in[:, gates_at + 4 * GDN_HEADS:]], axis=1).astype(BF16)
    w_ab = jnp.pad(w_in[:, gates_at:gates_at + 4 * GDN_HEADS], ((0, 0), (0, 128 - 4 * GDN_HEADS))).astype(BF16)
    proj, ab = _inproj_even_call(h, mod[0], w_main, w_ab, nb, period, nctx)
    pad_row = lambda a: jnp.pad(a.reshape(1, -1), ((0, 0), (0, 128 - 2 * GDN_HEADS)))
    qkvn, kt3, gh, gct3 = _gdn_prep_call(proj, ab, e_conv_qkv[0], pad_row(e_a_log[0]), pad_row(e_dt_bias[0]),
                                         period, nctx)
    o = _gdn_scan_call(qkvn, kt3, gh, gct3, nb, period, nctx)

    ops = _s5_tables_call(e_lam_re[0], e_lam_im[0], e_log_dt[0], e_b_re[0], e_b_im[0], e_c_re[0], e_c_im[0])
    u_slabs = proj[:, 4 * w:].reshape(t // S5_CHUNK, S5_CHUNK, d - w).transpose(1, 0, 2)
    y = _s5_call(u_slabs, ops, nb, period, nctx).transpose(1, 0, 2).reshape(t, d - w)

    h = _merge_call(o, proj, y, h, mod[0], row(e_gdn_norm[0]), row(e_d_skip[0]), e_w_glu[0].astype(BF16),
                    row(e_b_glu[0]), e_w_out[0].astype(BF16), row(ln1_g[0]), row(ln1_b[0]), nb, period, nctx)
    w_up, w_down = ffn_w_up.astype(BF16), ffn_w_down.astype(BF16)
    conv_b = ffn_conv_b[:, None, :]
    h = _ffn_call(h, mod[0], 0, w_up, ffn_conv_w, conv_b, w_down, row(ln2_g[0]), row(ln2_b[0]), nb, period, nctx)

    cos, sin = _rope_tables(seq, nctx)
    qkv = _qkv_call(h, mod[1], o_w_in[0].astype(BF16), cos, sin, row(o_q_norm[0]), row(o_k_norm[0]),
                    nb, period, nctx)
    att = _attn_call(qkv, nb, seq, nctx)
    hl = _outproj_call(att, h, mod[1], o_w_out[0].astype(BF16), row(ln1_g[1]), row(ln1_b[1]), nb, seq, nctx)
    out = _ffn_call(hl, mod[1], 1, w_up, ffn_conv_w, conv_b, w_down, row(ln2_g[1]), row(ln2_b[1]), nb, seq, 0)
    return out.reshape(nb, seq, d)
```

```python
import functools
import math

import jax
import jax.numpy as jnp
from jax import lax
from jax.experimental import pallas as pl
from jax.experimental.pallas import tpu as pltpu

F32 = jnp.float32
BF16 = jnp.bfloat16

DEPTH = 2
GDN_HEADS = 8
HEAD_DIM = 128
GDN_WIDTH = GDN_HEADS * HEAD_DIM
GDN_CHUNK = 128
GDN_INTRA_BATCH = 17
GDN_SOLVE_BASE = 8
S5_GROUP = 16
S5_STATE = 64
S5_CHUNK = 16
S5_TILE = S5_CHUNK * S5_GROUP
S5_GROUPS_PER_STEP = 4
S5_TABLE_GROUPS = 8
ATT_HEADS = 16
ATT_KV_HEADS = 4
ATT_GROUP = ATT_HEADS // ATT_KV_HEADS
ATT_SCALE = HEAD_DIM ** -0.5
Q_SCALE = ATT_SCALE * math.log2(math.e)
ATT_Q_TILE = 256
ATT_KEY_BLOCKS = 4
MXU_TILE = 256
GRID_W = 64
ROPE_THETA = 10000.0
CAST_BLOCK_BYTES = 8 * 1024 * 1024
PROJ_ROW_TILE = 1088
FFN_CHUNK = 512
FFN_SLAB = 256
ALPHA = (2 * DEPTH) ** 0.25
LN_EPS = 1e-6
RMS_EPS = 1e-6
L2_EPS = 1e-6
VMEM_LIMIT_BYTES = 56 * 1024 * 1024


def _pick(n, target, mult):
    best = None
    for d in range(mult, min(n, target) + 1, mult):
        if n % d == 0:
            best = d
    assert best is not None, (n, target, mult)
    return best


def _params(*sem):
    return pltpu.CompilerParams(dimension_semantics=sem, vmem_limit_bytes=VMEM_LIMIT_BYTES)


def _dot(a, b):
    return jnp.dot(a, b, preferred_element_type=F32)


def _dot_nt(a, b):
    return lax.dot_general(a, b, (((1,), (1,)), ((), ())), preferred_element_type=F32)


def _split3(a):
    p0 = a.astype(BF16)
    r = a - p0.astype(F32)
    p1 = r.astype(BF16)
    p2 = (r - p1.astype(F32)).astype(BF16)
    return p0, p1, p2


def _rows_in_batch(i, tm, tiles_per_batch):
    return (i % tiles_per_batch) * tm + lax.broadcasted_iota(jnp.int32, (tm, 1), 0)


def _modulate(x, is_ctx, mb_ref, mc_ref, shift_i, scale_i):
    scale = jnp.where(is_ctx, mc_ref[scale_i:scale_i + 1, :], mb_ref[scale_i:scale_i + 1, :])
    shift = jnp.where(is_ctx, mc_ref[shift_i:shift_i + 1, :], mb_ref[shift_i:shift_i + 1, :])
    return x * (1.0 + scale) + shift


def _layernorm(r, g, b):
    xc = r - jnp.mean(r, -1, keepdims=True)
    var = jnp.mean(xc * xc, -1, keepdims=True)
    return xc * lax.rsqrt(var + LN_EPS) * g + b


def _silu(x):
    return x * jax.nn.sigmoid(x)


def _seq_edges(rib, ctx, period):
    has_prev = (rib != 0) & (rib != ctx)
    has_next = (rib != ctx - 1) & (rib != period - 1)
    return has_prev, has_next


def _mod_body(c_ref, w_ref, b_ref, o_ref):
    s = _silu(c_ref[...])
    o_ref[...] = _dot(s.astype(BF16), w_ref[...].astype(BF16)) + b_ref[...]


def _mod_call(cs, mod_w, mod_b):
    depth, d, n = mod_w.shape
    tn = _pick(n, 1024, 128)
    return pl.pallas_call(
        _mod_body,
        grid=(depth, n // tn),
        in_specs=[pl.BlockSpec((8, d), lambda l, j: (0, 0)),
                  pl.BlockSpec((None, d, tn), lambda l, j: (l, 0, j)),
                  pl.BlockSpec((None, 1, tn), lambda l, j: (l, 0, j))],
        out_specs=pl.BlockSpec((None, 8, tn), lambda l, j: (l, 0, j)),
        out_shape=jax.ShapeDtypeStruct((depth, 8, n), F32),
        compiler_params=_params("arbitrary", "arbitrary"),
        name="adaln_mod",
    )(cs, mod_w, mod_b.reshape(depth, 1, n))


def _cast_body(w_ref, o_ref):
    o_ref[...] = w_ref[...].astype(o_ref.dtype)


def _cast_call(w):
    layers, rows, cols = w.shape
    tr = _pick(rows, max(8, CAST_BLOCK_BYTES // (4 * cols)), 8)
    return pl.pallas_call(
        _cast_body,
        grid=(layers, rows // tr),
        in_specs=[pl.BlockSpec((None, tr, cols), lambda l, i: (l, i, 0))],
        out_specs=pl.BlockSpec((None, tr, cols), lambda l, i: (l, i, 0)),
        out_shape=jax.ShapeDtypeStruct(w.shape, BF16),
        compiler_params=_params("arbitrary", "arbitrary"),
        name="weight_cast",
    )(w)


def _inproj_even_body(x_ref, mb_ref, mc_ref, w_ref, wab_ref, o_ref, ab_ref, xb_ref, *, tm, tpb, ctx):
    i = pl.program_id(0)

    @pl.when(pl.program_id(1) == 0)
    def _():
        rib = _rows_in_batch(i, tm, tpb)
        xb = _modulate(x_ref[...], rib < ctx, mb_ref, mc_ref, 0, 1).astype(BF16)
        xb_ref[...] = xb
        ab_ref[...] = _dot(xb, wab_ref[...])

    o_ref[...] = _dot(xb_ref[...], w_ref[...]).astype(o_ref.dtype)


def _inproj_even_call(h, mod, w_main, w_ab, nb, period, ctx):
    t, d = h.shape
    n = w_main.shape[1]
    tm = _pick(period, PROJ_ROW_TILE, 16)
    tn = _pick(n, 1280, 256)
    tpb = period // tm
    body = functools.partial(_inproj_even_body, tm=tm, tpb=tpb, ctx=ctx)
    return pl.pallas_call(
        body,
        grid=(t // tm, n // tn),
        in_specs=[pl.BlockSpec((tm, d), lambda i, j: (i, 0)),
                  pl.BlockSpec((None, 6, d), lambda i, j: (i // tpb, 0, 0)),
                  pl.BlockSpec((None, 6, d), lambda i, j: (nb, 0, 0)),
                  pl.BlockSpec((d, tn), lambda i, j: (0, j)),
                  pl.BlockSpec((d, 128), lambda i, j: (0, 0))],
        out_specs=[pl.BlockSpec((tm, tn), lambda i, j: (i, j)),
                   pl.BlockSpec((tm, 128), lambda i, j: (i, 0))],
        out_shape=[jax.ShapeDtypeStruct((t, n), BF16), jax.ShapeDtypeStruct((t, 128), F32)],
        scratch_shapes=[pltpu.VMEM((tm, d), BF16)],
        compiler_params=_params("arbitrary", "arbitrary"),
        name="even_inproj",
    )(h, mod, mod, w_main, w_ab)


def _gdn_prep_body(x_ref, xn_ref, xp_ref, cw_ref, ab_ref, alog_ref, dtb_ref, tril_ref, triu_ref, e_ref,
                   qkv_ref, kt_ref, gh_ref, gct_ref, ext_ref, *, tr, tpb, ctx, period):
    i = pl.program_id(0)
    sec = pl.program_id(1)
    rib = _rows_in_batch(i, tr, tpb)
    has_prev, has_next = _seq_edges(rib, ctx, period)
    ext_ref[0:16, :] = xp_ref[...].astype(F32)
    ext_ref[16:tr + 16, :] = x_ref[...].astype(F32)
    ext_ref[tr + 16:tr + 32, :] = xn_ref[...].astype(F32)
    cw = cw_ref[...]
    y = (jnp.where(has_prev, ext_ref[15:tr + 15, :], 0.0) * cw[0:1] + ext_ref[16:tr + 16, :] * cw[1:2]
         + jnp.where(has_next, ext_ref[17:tr + 17, :], 0.0) * cw[2:3])
    y = _silu(y)

    def l2n(scale):
        parts = []
        for hh in range(GDN_HEADS):
            yh = y[:, hh * HEAD_DIM:(hh + 1) * HEAD_DIM]
            parts.append(yh * (lax.rsqrt(jnp.sum(yh * yh, -1, keepdims=True) + L2_EPS) * scale))
        return jnp.concatenate(parts, axis=1)

    @pl.when(sec == 0)
    def _():
        qkv_ref[...] = l2n(HEAD_DIM ** -0.5).astype(BF16)
        ab = ab_ref[...]
        lane = lax.broadcasted_iota(jnp.int32, (1, 128), 1)
        xg = ab + dtb_ref[...]
        softplus = jnp.maximum(xg, 0.0) + jnp.log1p(jnp.exp(-jnp.abs(xg)))
        g = -jnp.exp(alog_ref[...]) * softplus
        g0, g1, g2 = _split3(g)
        tril = tril_ref[...]
        triu = triu_ref[...]
        pre = _dot(tril, g0) + _dot(tril, g1) + _dot(tril, g2)
        suf = _dot(triu, g0) + _dot(triu, g1) + _dot(triu, g2)
        gc = jnp.where(lane < GDN_HEADS, pre, suf)
        vals = jnp.where(lane < 2 * GDN_HEADS, gc, jax.nn.sigmoid(ab))
        v0, v1, v2 = _split3(vals)
        e = e_ref[...]
        gh_ref[...] = _dot(v0, e) + _dot(v1, e) + _dot(v2, e)
        gct = gc.T
        for c in range(tr // GDN_CHUNK):
            gct_ref[c] = gct[:2 * GDN_HEADS, c * GDN_CHUNK:(c + 1) * GDN_CHUNK]

    @pl.when(sec == 1)
    def _():
        kn = l2n(1.0)
        qkv_ref[...] = kn.astype(BF16)
        knt = kn.T
        for c in range(tr // GDN_CHUNK):
            kt_ref[c] = knt[:, c * GDN_CHUNK:(c + 1) * GDN_CHUNK].astype(BF16)

    @pl.when(sec == 2)
    def _():
        qkv_ref[...] = y.astype(BF16)


def _gdn_prep_call(proj, ab, conv_w, alog_row, dtb_row, period, ctx):
    t = proj.shape[0]
    w = GDN_WIDTH
    tr = _pick(period, 256, GDN_CHUNK)
    tpb = period // tr
    hb = tr // 16
    nh = t // 16
    nck = tr // GDN_CHUNK
    r = jnp.arange(tr)
    same = (r[:, None] // GDN_CHUNK) == (r[None, :] // GDN_CHUNK)
    tril = (same & (r[:, None] >= r[None, :])).astype(BF16)
    triu = (same & (r[:, None] <= r[None, :])).astype(BF16)
    src = jnp.arange(128)[:, None]
    dst = jnp.arange(w)[None, :]
    expand = ((dst % HEAD_DIM < 4) & (src == (dst % HEAD_DIM) * GDN_HEADS + dst // HEAD_DIM)).astype(BF16)
    body = functools.partial(_gdn_prep_body, tr=tr, tpb=tpb, ctx=ctx, period=period)
    return pl.pallas_call(
        body,
        grid=(t // tr, 3),
        in_specs=[pl.BlockSpec((tr, w), lambda i, s: (i, s)),
                  pl.BlockSpec((16, w), lambda i, s: (jnp.minimum((i + 1) * hb, nh - 1), s)),
                  pl.BlockSpec((16, w), lambda i, s: (jnp.maximum(i * hb - 1, 0), s)),
                  pl.BlockSpec((3, w), lambda i, s: (0, s)),
                  pl.BlockSpec((tr, 128), lambda i, s: (i, 0)),
                  pl.BlockSpec((1, 128), lambda i, s: (0, 0)),
                  pl.BlockSpec((1, 128), lambda i, s: (0, 0)),
                  pl.BlockSpec((tr, tr), lambda i, s: (0, 0)),
                  pl.BlockSpec((tr, tr), lambda i, s: (0, 0)),
                  pl.BlockSpec((128, w), lambda i, s: (0, 0))],
        out_specs=[pl.BlockSpec((tr, w), lambda i, s: (i, s)),
                   pl.BlockSpec((nck, w, GDN_CHUNK), lambda i, s: (i, 0, 0)),
                   pl.BlockSpec((tr, w), lambda i, s: (i, 0)),
                   pl.BlockSpec((nck, 2 * GDN_HEADS, GDN_CHUNK), lambda i, s: (i, 0, 0))],
        out_shape=[jax.ShapeDtypeStruct((t, 3 * w), BF16),
                   jax.ShapeDtypeStruct((t // GDN_CHUNK, w, GDN_CHUNK), BF16),
                   jax.ShapeDtypeStruct((t, w), F32),
                   jax.ShapeDtypeStruct((t // GDN_CHUNK, 2 * GDN_HEADS, GDN_CHUNK), F32)],
        scratch_shapes=[pltpu.VMEM((tr + 32, w), F32)],
        compiler_params=_params("arbitrary", "arbitrary"),
        name="gdn_prep",
    )(proj, proj, proj, conv_w, ab, alog_row, dtb_row, tril, triu, expand)


def _gdn_scan_body(q_ref, k_ref, v_ref, kt_ref, gh_ref, gct_ref, o_ref,
                   aqf, aqb, bsf, bsb, egf, egb, *, nch, nctx, batch):
    head = pl.program_id(1)
    cs = GDN_CHUNK
    ii = lax.broadcasted_iota(jnp.int32, (cs, cs), 0)
    jj = lax.broadcasted_iota(jnp.int32, (cs, cs), 1)
    masks = ((ii >= jj, ii > jj), (ii <= jj, ii < jj))
    same_block = {}
    width = GDN_SOLVE_BASE
    while width <= cs:
        shift = width.bit_length() - 1
        same_block[width] = jnp.right_shift(ii, shift) == jnp.right_shift(jj, shift)
        width *= 2
    hd = HEAD_DIM
    aqs, bss, egs = (aqf, aqb), (bsf, bsb), (egf, egb)

    def intra(first, count):
        chunks = [first + j for j in range(count)]
        r0s = [pl.multiple_of(c * cs, cs) for c in chunks]
        qs = [q_ref[pl.ds(r0, cs), :] for r0 in r0s]
        ks = [k_ref[pl.ds(r0, cs), :] for r0 in r0s]
        vs = [v_ref[pl.ds(r0, cs), :] for r0 in r0s]
        kts = [kt_ref[c] for c in chunks]
        raws = [_dot(jnp.concatenate([q, k], axis=0), kt) for q, k, kt in zip(qs, ks, kts)]
        raw_q = [r[:cs] for r in raws]
        raw_k = [r[cs:] for r in raws]
        pms, xs, atts, kps, qes, gends = [], [], [], [], [], []
        for j, c in enumerate(chunks):
            g4 = gh_ref[pl.ds(r0s[j], cs), :]
            qf, kf, vf = qs[j].astype(F32), ks[j].astype(F32), vs[j].astype(F32)
            ktf = kts[j].astype(F32)
            for d in (0, 1):
                incl, strict = masks[d]
                gcol = g4[:, d:d + 1]
                bcol = g4[:, 2 + d:3 + d]
                grow = gct_ref[c, pl.ds(d * GDN_HEADS + head, 1), :]
                decay = jnp.where(incl, jnp.exp(jnp.where(incl, gcol - grow, 0.0)), 0.0)
                eg = jnp.exp(gcol)
                gend = grow[:, cs - 1:cs] if d == 0 else grow[:, 0:1]
                pms.append(jnp.where(strict, -(raw_k[j] * bcol) * decay, 0.0))
                xs.append(jnp.concatenate([vf * bcol, kf * (bcol * eg)], axis=1))
                atts.append((raw_q[j] * decay).astype(BF16))
                kps.append((ktf * jnp.exp(gend - grow)).astype(BF16))
                qes.append(qf * eg)
                gends.append(gend)
        bfs = lambda vals: [a.astype(BF16) for a in vals]
        pds = bfs([jnp.where(same_block[GDN_SOLVE_BASE], pm, 0.0) for pm in pms])
        ns = [pd.astype(F32) for pd in pds]
        pw = pds
        span = 1
        while 2 * span < GDN_SOLVE_BASE:
            sq = [_dot(p, p) for p in pw]
            pw = bfs(sq)
            ns = [n + s + _dot(p, n.astype(BF16)) for n, s, p in zip(ns, sq, pw)]
            span *= 2
        width = GDN_SOLVE_BASE
        while width < cs:
            couple = same_block[2 * width] & ~same_block[width]
            qs_ = [jnp.where(couple, pm, 0.0) for pm in pms]
            nbs = bfs(ns)
            bs_ = [q + _dot(nb, q.astype(BF16)) for q, nb in zip(qs_, nbs)]
            ns = [n + b + _dot(b.astype(BF16), nb) for n, b, nb in zip(ns, bs_, nbs)]
            width *= 2
        xs = [x + _dot(n.astype(BF16), x.astype(BF16)) for n, x in zip(ns, xs)]
        xbs = [x.astype(BF16) for x in xs]
        both = [_dot(jnp.concatenate([att, kp], axis=0), xb) for att, kp, xb in zip(atts, kps, xbs)]
        aws = [r[:cs] for r in both]
        kxs = [r[cs:] for r in both]
        for j, c in enumerate(chunks):
            for d in (0, 1):
                n = 2 * j + d
                aqs[d][c, 0:hd, :] = (-kxs[n][:, hd:]).astype(BF16)
                aqs[d][c, hd:hd + cs, :] = (qes[n] - aws[n][:, hd:]).astype(BF16)
                bss[d][c] = kxs[n][:, :hd]
                egs[d][c] = jnp.broadcast_to(jnp.exp(gends[n]), (8, hd))
            o_ref[pl.ds(r0s[j], cs), :] = aws[2 * j][:, :hd] + aws[2 * j + 1][:, :hd]

    def inter(i, carry):
        cb = jnp.where(i < nctx, nctx - 1 - i, nch - 1 - (i - nctx))
        new = []
        for d, c in ((0, i), (1, cb)):
            s = carry[d]
            r0 = pl.multiple_of(c * cs, cs)
            r = _dot(aqs[d][c], s.astype(BF16))
            o_ref[pl.ds(r0, cs), :] += r[hd:]
            new.append(s * egs[d][c][0:1, :] + r[:hd] + bss[d][c])
        return tuple(new)

    full = nch // batch

    def intra_step(it, carry):
        intra(it * batch, batch)
        return carry

    lax.fori_loop(0, full, intra_step, 0)
    if nch > full * batch:
        intra(full * batch, nch - full * batch)
    zero = jnp.zeros((hd, hd), F32)
    lax.fori_loop(0, nch, inter, (zero, zero), unroll=2)


def _gdn_scan_call(qkvn, kt3, gh, gct3, nb, period, ctx):
    t = qkvn.shape[0]
    nch = period // GDN_CHUNK
    nctx = ctx // GDN_CHUNK
    hd = HEAD_DIM
    body = functools.partial(_gdn_scan_body, nch=nch, nctx=nctx, batch=min(nch, GDN_INTRA_BATCH))
    dirs2 = lambda shape, dt: [pltpu.VMEM(shape, dt), pltpu.VMEM(shape, dt)]
    return pl.pallas_call(
        body,
        grid=(nb, GDN_HEADS),
        in_specs=[pl.BlockSpec((period, hd), lambda b, h: (b, h)),
                  pl.BlockSpec((period, hd), lambda b, h: (b, GDN_HEADS + h)),
                  pl.BlockSpec((period, hd), lambda b, h: (b, 2 * GDN_HEADS + h)),
                  pl.BlockSpec((nch, hd, GDN_CHUNK), lambda b, h: (b, h, 0)),
                  pl.BlockSpec((period, hd), lambda b, h: (b, h)),
                  pl.BlockSpec((nch, 2 * GDN_HEADS, GDN_CHUNK), lambda b, h: (b, 0, 0))],
        out_specs=pl.BlockSpec((period, hd), lambda b, h: (b, h)),
        out_shape=jax.ShapeDtypeStruct((t, GDN_WIDTH), F32),
        scratch_shapes=(dirs2((nch, hd + GDN_CHUNK, hd), BF16) + dirs2((nch, hd, hd), F32)
                        + dirs2((nch, 8, hd), F32)),
        compiler_params=_params("arbitrary", "arbitrary"),
        name="gdn_scan",
    )(qkvn, qkvn, qkvn, kt3, gh, gct3)


def _s5_tables_body(*refs):
    lax.fori_loop(0, refs[0].shape[1], functools.partial(_s5_tables_group, refs), 0)


def _s5_tables_group(refs, q, carry):
    (lr_ref, li_ref, ldt_ref, btr_ref, bti_ref, cr_ref, ci_ref,
     wir_ref, wii_ref, wor_ref, woi_ref, ktf_ref, ktb_ref, lam_ref) = refs
    n, cg = S5_CHUNK, S5_GROUP
    kk = lax.broadcasted_iota(jnp.int32, (24, 1), 0).astype(F32)
    lane = lax.broadcasted_iota(jnp.int32, (1, S5_TILE), 1)

    def dot3_nt(a, b):
        a0 = a.astype(BF16)
        a1 = (a - a0.astype(F32)).astype(BF16)
        b0 = b.astype(BF16)
        b1 = (b - b0.astype(F32)).astype(BF16)
        return _dot_nt(a0, b0) + _dot_nt(a0, b1) + _dot_nt(a1, b0)

    win_r, win_i, wout_r, wout_i, lam_r, lam_i = [], [], [], [], [], []
    for d in (0, 1):
        lr = lr_ref[d, q]
        li = li_ref[d, q]
        dt = jnp.exp(ldt_ref[d, q])
        mag = jnp.exp(kk * (lr * dt))
        ang = kk * (li * dt)
        er = mag * jnp.cos(ang)
        ei = mag * jnp.sin(ang)
        nr = er[1:2] - 1.0
        ni = ei[1:2]
        den = lr * lr + li * li
        cfr = (nr * lr + ni * li) / den
        cfi = (ni * lr - nr * li) / den
        btr = btr_ref[d, q]
        bti = bti_ref[d, q]
        bbr = btr * cfr - bti * cfi
        bbi = btr * cfi + bti * cfr
        cr = cr_ref[d, q]
        ci = ci_ref[d, q]
        a_r, a_i, cm_r, cm_i, c0_r, c0_i = [], [], [], [], [], []
        for k in range(n):
            e_r, e_i = er[k:k + 1], ei[k:k + 1]
            a_r.append(bbr * e_r - bbi * e_i)
            a_i.append(bbr * e_i + bbi * e_r)
            c0_r.append(cr * e_r - ci * e_i)
            c0_i.append(cr * e_i + ci * e_r)
            f_r, f_i = er[k + 1:k + 2], ei[k + 1:k + 2]
            cm_r.append(cr * f_r - ci * f_i)
            cm_i.append(-(cr * f_i + ci * f_r))
        fwd = d == 0
        down = list(range(n - 1, -1, -1))
        up = list(range(n))
        pick = lambda parts, order: jnp.concatenate([parts[k] for k in order], axis=0)
        win_r.append(pick(a_r, down if fwd else up))
        win_i.append(pick(a_i, down if fwd else up))
        wout_r.append(pick(cm_r, up if fwd else down))
        wout_i.append(pick(cm_i, up if fwd else down))
        lam_r.append(er[n:n + 1])
        lam_i.append(ei[n:n + 1])
        order = up if fwd else down
        strip = dot3_nt(bbr, pick(c0_r, order)) - dot3_nt(bbi, pick(c0_i, order))
        for s in range(n):
            rows = slice(s * cg, (s + 1) * cg)
            if fwd:
                blk = strip if s == 0 else jnp.where(lane >= s * cg, pltpu.roll(strip, s * cg, 1), 0.0)
                ktf_ref[q, rows, :] = blk.astype(BF16)
            else:
                sh = (n - 1 - s) * cg
                blk = strip if sh == 0 else jnp.where(lane < (s + 1) * cg, pltpu.roll(strip, S5_TILE - sh, 1), 0.0)
                ktb_ref[q, rows, :] = blk.astype(BF16)
    wir_ref[q] = jnp.concatenate(win_r, axis=1).astype(BF16)
    wii_ref[q] = jnp.concatenate(win_i, axis=1).astype(BF16)
    wor_ref[q] = jnp.concatenate(wout_r, axis=1).T.astype(BF16)
    woi_ref[q] = jnp.concatenate(wout_i, axis=1).T.astype(BF16)
    lam_ref[q] = jnp.concatenate([jnp.concatenate(lam_r, axis=1), jnp.concatenate(lam_i, axis=1)], axis=0)
    return carry


def _s5_tables_call(lam_re, lam_im, log_dt, b_re, b_im, c_re, c_im):
    _, g, p = lam_re.shape
    cg = S5_GROUP
    sw = 2 * p
    row4 = lambda a: a.reshape(2, g, 1, p)
    per = _pick(g, S5_TABLE_GROUPS, 1)
    ispec = lambda r: pl.BlockSpec((2, per, r, p), lambda gi: (0, gi, 0, 0))
    ospec = lambda r, c: pl.BlockSpec((per, r, c), lambda gi: (gi, 0, 0))
    ldt = jnp.broadcast_to(log_dt[:, :, None, None], (2, g, 1, p))
    wir, wii, wor, woi, ktf, ktb, lam = pl.pallas_call(
        _s5_tables_body,
        grid=(g // per,),
        in_specs=[ispec(1), ispec(1), ispec(1), ispec(cg), ispec(cg), ispec(cg), ispec(cg)],
        out_specs=[ospec(S5_TILE, sw), ospec(S5_TILE, sw), ospec(sw, S5_TILE), ospec(sw, S5_TILE),
                   ospec(S5_TILE, S5_TILE), ospec(S5_TILE, S5_TILE), ospec(2, sw)],
        out_shape=[jax.ShapeDtypeStruct((g, S5_TILE, sw), BF16)] * 2 + [jax.ShapeDtypeStruct((g, sw, S5_TILE), BF16)] * 2
                  + [jax.ShapeDtypeStruct((g, S5_TILE, S5_TILE), BF16)] * 2 + [jax.ShapeDtypeStruct((g, 2, sw), F32)],
        compiler_params=_params("arbitrary"),
        name="s5_tables",
    )(row4(lam_re), row4(lam_im), ldt, jnp.swapaxes(b_re, -1, -2), jnp.swapaxes(b_im, -1, -2), c_re, c_im)
    gb = S5_GROUPS_PER_STEP
    lam_r = lam[:, 0, :].reshape(g // gb, 1, gb * sw)
    lam_i = lam[:, 1, :].reshape(g // gb, 1, gb * sw)
    return ktf, ktb, wir, wii, wor, woi, lam_r, lam_i


def _s5_gather_chunks(u_ref, l_ref):
    lanes = u_ref.shape[2]
    for s in range(S5_CHUNK):
        l_ref[:, s * lanes:(s + 1) * lanes] = u_ref[s]


def _s5_in_body(u_ref, pm_ref, wr_ref, wi_ref, ub_ref, x_ref, l_ref):
    @pl.when(pl.program_id(1) == 0)
    def _():
        _s5_gather_chunks(u_ref, l_ref)

    ub = _dot(l_ref[...], pm_ref[...]).astype(BF16)
    ub_ref[...] = ub
    sw = 2 * S5_STATE
    for q in range(ub.shape[1] // S5_TILE):
        ug = ub[:, q * S5_TILE:(q + 1) * S5_TILE]
        x_ref[:, 2 * q * sw:(2 * q + 1) * sw] = _dot(ug, wr_ref[q])
        x_ref[:, (2 * q + 1) * sw:(2 * q + 2) * sw] = _dot(ug, wi_ref[q])


def _s5_out_body(ub_ref, s_ref, ktf_ref, ktb_ref, wor_ref, woi_ref, pmt_ref, y_ref, y8_ref, *, col_tiles):
    n = pl.program_id(1)
    sw = 2 * S5_STATE

    @pl.when(n == 0)
    def _():
        for g in range(ub_ref.shape[1] // S5_TILE):
            ug = ub_ref[:, g * S5_TILE:(g + 1) * S5_TILE]
            s_r = s_ref[:, 2 * g * sw:(2 * g + 1) * sw]
            s_i = s_ref[:, (2 * g + 1) * sw:(2 * g + 2) * sw]
            y = _dot(ug, ktf_ref[g]) + _dot(ug, ktb_ref[g]) + _dot(s_r, wor_ref[g]) + _dot(s_i, woi_ref[g])
            y8_ref[:, g * S5_TILE:(g + 1) * S5_TILE] = y.astype(BF16)

    yp = _dot(y8_ref[...], pmt_ref[...])
    lanes = y_ref.shape[2]
    per = yp.shape[1] // lanes
    for nn in range(col_tiles):
        @pl.when(n == nn)
        def _(nn=nn):
            for q in range(per):
                y_ref[nn * per + q] = yp[:, q * lanes:(q + 1) * lanes].astype(y_ref.dtype)


def _s5_scan_body(x_ref, lr_ref, li_ref, s_ref, *scratch, nb, nch, nctx):
    gb = S5_GROUPS_PER_STEP
    sw = 2 * S5_STATE
    xr, xi, sfr, sfi, sbr, sbi = (scratch[k * gb:(k + 1) * gb] for k in range(6))
    for g in range(gb):
        xr[g][...] = x_ref[:, 2 * g * sw:(2 * g + 1) * sw]
        xi[g][...] = x_ref[:, (2 * g + 1) * sw:(2 * g + 2) * sw]
    is_f = lax.broadcasted_iota(jnp.int32, (1, sw), 1) < S5_STATE
    ar = [lr_ref[:, g * sw:(g + 1) * sw] for g in range(gb)]
    ai = [li_ref[:, g * sw:(g + 1) * sw] for g in range(gb)]

    def step(i, carry):
        cb = jnp.where(i < nctx, nctx - 1 - i, nch - 1 - (i - nctx))
        at_f = pl.ds(i, nb, stride=nch)
        at_b = pl.ds(cb, nb, stride=nch)
        new = []
        for g in range(gb):
            sr, si = carry[2 * g], carry[2 * g + 1]
            sfr[g][at_f, :] = sr
            sfi[g][at_f, :] = si
            sbr[g][at_b, :] = sr
            sbi[g][at_b, :] = si
            inr = jnp.where(is_f, xr[g][at_f, :], xr[g][at_b, :])
            ini = jnp.where(is_f, xi[g][at_f, :], xi[g][at_b, :])
            new.append(ar[g] * sr - ai[g] * si + inr)
            new.append(ar[g] * si + ai[g] * sr + ini)
        return tuple(new)

    zero = jnp.zeros((nb, sw), F32)
    lax.fori_loop(0, nch, step, (zero,) * (2 * gb))
    for g in range(gb):
        s_ref[:, 2 * g * sw:(2 * g + 1) * sw] = jnp.where(is_f, sfr[g][...], sbr[g][...]).astype(BF16)
        s_ref[:, (2 * g + 1) * sw:(2 * g + 2) * sw] = jnp.where(is_f, sfi[g][...], sbi[g][...]).astype(BF16)


def _s5_call(u_slabs, ops, nb, period, ctx):
    ktf, ktb, wir, wii, wor, woi, lam_r, lam_i = ops
    _, rows, width = u_slabs.shape
    groups = width // S5_GROUP
    lanes = 128
    lg = lanes // S5_GROUP
    nblk = width // lanes
    blk_w = S5_CHUNK * lanes
    col = 2 * S5_TILE
    col_tiles = blk_w // col
    sw = 2 * S5_STATE
    gb = S5_GROUPS_PER_STEP
    nch = period // S5_CHUNK
    nctx = ctx // S5_CHUNK
    src = jnp.arange(blk_w)
    dst = (src % lanes // S5_GROUP) * S5_TILE + (src // lanes) * S5_GROUP + src % S5_GROUP
    perm = (dst[:, None] == jnp.arange(blk_w)[None, :]).astype(BF16)
    perm_t = perm.T

    u_perm, xin = pl.pallas_call(
        _s5_in_body,
        grid=(nblk, col_tiles),
        in_specs=[pl.BlockSpec((S5_CHUNK, rows, lanes), lambda j, n: (0, 0, j)),
                  pl.BlockSpec((blk_w, col), lambda j, n: (0, n)),
                  pl.BlockSpec((col // S5_TILE, S5_TILE, sw), lambda j, n: (j * col_tiles + n, 0, 0)),
                  pl.BlockSpec((col // S5_TILE, S5_TILE, sw), lambda j, n: (j * col_tiles + n, 0, 0))],
        out_specs=[pl.BlockSpec((rows, col), lambda j, n: (0, j * col_tiles + n)),
                   pl.BlockSpec((rows, col), lambda j, n: (0, j * col_tiles + n))],
        out_shape=[jax.ShapeDtypeStruct((rows, groups * S5_TILE), BF16),
                   jax.ShapeDtypeStruct((rows, groups * 2 * sw), F32)],
        scratch_shapes=[pltpu.VMEM((rows, blk_w), BF16)],
        compiler_params=_params("arbitrary", "arbitrary"),
        name="s5_in",
    )(u_slabs, perm, wir, wii)

    states = pl.pallas_call(
        functools.partial(_s5_scan_body, nb=nb, nch=nch, nctx=nctx),
        grid=(groups // gb,),
        in_specs=[pl.BlockSpec((rows, gb * 2 * sw), lambda i: (0, i)),
                  pl.BlockSpec((None, 1, gb * sw), lambda i: (i, 0, 0)),
                  pl.BlockSpec((None, 1, gb * sw), lambda i: (i, 0, 0))],
        out_specs=pl.BlockSpec((rows, gb * 2 * sw), lambda i: (0, i)),
        out_shape=jax.ShapeDtypeStruct((rows, groups * 2 * sw), BF16),
        scratch_shapes=[pltpu.VMEM((rows, sw), F32) for _ in range(6 * gb)],
        compiler_params=_params("arbitrary"),
        name="s5_scan",
    )(xin, lam_r, lam_i)

    wspec = lambda r, c: pl.BlockSpec((lg, r, c), lambda j, n: (j, 0, 0))
    return pl.pallas_call(
        functools.partial(_s5_out_body, col_tiles=col_tiles),
        grid=(nblk, col_tiles),
        in_specs=[pl.BlockSpec((rows, lg * S5_TILE), lambda j, n: (0, j)),
                  pl.BlockSpec((rows, lg * 2 * sw), lambda j, n: (0, j)),
                  wspec(S5_TILE, S5_TILE), wspec(S5_TILE, S5_TILE), wspec(sw, S5_TILE), wspec(sw, S5_TILE),
                  pl.BlockSpec((blk_w, col), lambda j, n: (0, n))],
        out_specs=pl.BlockSpec((S5_CHUNK, rows, lanes), lambda j, n: (0, 0, j)),
        out_shape=jax.ShapeDtypeStruct((S5_CHUNK, rows, width), BF16),
        scratch_shapes=[pltpu.VMEM((rows, lg * S5_TILE), BF16)],
        compiler_params=_params("arbitrary", "arbitrary"),
        name="s5_out",
    )(u_perm, states, ktf, ktb, wor, woi, perm_t)


def _merge_body(o_ref, z_ref, u_ref, y_ref, h_ref, mb_ref, mc_ref, gn_ref, ds_ref, wglu_ref, bglu_ref,
                wout_ref, lng_ref, lnb_ref, out_ref, cat_ref, *, tm, tpb, ctx):
    rib = _rows_in_batch(pl.program_id(0), tm, tpb)
    gn = gn_ref[...]
    half = tm // 2
    for rows in (slice(0, half), slice(half, tm)):
        for hh in range(GDN_HEADS):
            sl = slice(hh * HEAD_DIM, (hh + 1) * HEAD_DIM)
            oh = o_ref[rows, sl]
            r = lax.rsqrt(jnp.mean(oh * oh, -1, keepdims=True) + RMS_EPS)
            cat_ref[rows, sl] = (oh * r * gn * _silu(z_ref[rows, sl].astype(F32))).astype(BF16)
        s = y_ref[rows, :].astype(F32) + ds_ref[...] * u_ref[rows, :].astype(F32)
        s = 0.5 * s * (1.0 + jnp.tanh(math.sqrt(2.0 / math.pi) * (s + 0.044715 * (s * s * s))))
        s = s * jax.nn.sigmoid(_dot(s.astype(BF16), wglu_ref[...]) + bglu_ref[...])
        cat_ref[rows, GDN_WIDTH:] = s.astype(BF16)
        mix = _dot(cat_ref[rows, :], wout_ref[...])
        gate = jnp.where(rib[rows] < ctx, mc_ref[2:3, :], mb_ref[2:3, :])
        out_ref[rows, :] = _layernorm(ALPHA * h_ref[rows, :] + gate * mix, lng_ref[...], lnb_ref[...])


def _merge_call(o, proj, y, h, mod, gn, d_skip, w_glu, b_glu, w_out, ln_g, ln_b, nb, period, ctx):
    t, d = h.shape
    w = GDN_WIDTH
    sw = d - w
    tm = _pick(period, 544, 32)
    tpb = period // tm
    body = functools.partial(_merge_body, tm=tm, tpb=tpb, ctx=ctx)
    const = lambda r, c: pl.BlockSpec((r, c), lambda i: (0, 0), pipeline_mode=pl.Buffered(1))
    return pl.pallas_call(
        body,
        grid=(t // tm,),
        in_specs=[pl.BlockSpec((tm, w), lambda i: (i, 0)),
                  pl.BlockSpec((tm, w), lambda i: (i, 3)),
                  pl.BlockSpec((tm, sw), lambda i: (i, 4 * w // sw)),
                  pl.BlockSpec((tm, sw), lambda i: (i, 0)),
                  pl.BlockSpec((tm, d), lambda i: (i, 0)),
                  pl.BlockSpec((None, 6, d), lambda i: (i // tpb, 0, 0)),
                  pl.BlockSpec((None, 6, d), lambda i: (nb, 0, 0)),
                  const(1, HEAD_DIM), const(1, sw), const(sw, sw), const(1, sw), const(d, d),
                  const(1, d), const(1, d)],
        out_specs=pl.BlockSpec((tm, d), lambda i: (i, 0)),
        out_shape=jax.ShapeDtypeStruct((t, d), F32),
        scratch_shapes=[pltpu.VMEM((tm, d), BF16)],
        compiler_params=_params("arbitrary"),
        name="even_merge",
    )(o, proj, proj, y, h, mod, mod, gn, d_skip, w_glu, b_glu, w_out, ln_g, ln_b)


def _ffn_body(x_ref, xh_ref, mb_ref, mc_ref, wv_ref, wg_ref, cwv_ref, cwg_ref, cbv_ref, cbg_ref, wd_ref,
              lng_ref, lnb_ref, o_ref, xb_ref, *up_refs, tm, tpb, ctx, period):
    i = pl.program_id(0)
    j = pl.program_id(1)
    rib = _rows_in_batch(i, tm, tpb)
    n = tm + 32

    @pl.when(j == 0)
    def _():
        xb_ref[16:tm + 16, :] = _modulate(x_ref[...], rib < ctx, mb_ref, mc_ref, 3, 4).astype(BF16)
        base = (i % tpb) * tm
        off = lax.broadcasted_iota(jnp.int32, (16, 1), 0)
        xh = xh_ref[...]
        next_ok = (base + tm != period) & (base + tm != ctx)
        prev_ok = (base != 0) & (base != ctx)
        nxt = _modulate(xh[0:16], base + tm + off < ctx, mb_ref, mc_ref, 3, 4)
        prv = _modulate(xh[16:32], base - 16 + off < ctx, mb_ref, mc_ref, 3, 4)
        xb_ref[tm + 16:n, :] = jnp.where(next_ok, nxt, 0.0).astype(BF16)
        xb_ref[0:16, :] = jnp.where(prev_ok, prv, 0.0).astype(BF16)
        o_ref[...] = jnp.zeros_like(o_ref)

    xb = xb_ref[...]
    edge_inside_tile = ctx % tm != 0

    def conv(u_ref, cw_ref, cb_ref, sl):
        cw = cw_ref[:, sl]
        up = u_ref[15:tm + 15, :]
        un = u_ref[17:tm + 17, :]
        if edge_inside_tile:
            up = jnp.where(rib != ctx, up, 0.0)
            un = jnp.where(rib != ctx - 1, un, 0.0)
        return up * cw[0:1] + u_ref[16:tm + 16, :] * cw[1:2] + un * cw[2:3] + cb_ref[:, sl]

    fc = wv_ref.shape[1]
    slabs = [slice(a, a + FFN_SLAB) for a in range(0, fc, FFN_SLAB)]
    for si, sl in enumerate(slabs):
        up_refs[2 * si][...] = _dot(xb, wv_ref[:, sl])
        up_refs[2 * si + 1][...] = _dot(xb, wg_ref[:, sl])
    for si, sl in enumerate(slabs):
        act = conv(up_refs[2 * si], cwv_ref, cbv_ref, sl) * _silu(conv(up_refs[2 * si + 1], cwg_ref, cbg_ref, sl))
        o_ref[...] += _dot(act.astype(BF16), wd_ref[sl, :])

    @pl.when(j == pl.num_programs(1) - 1)
    def _():
        g = jnp.where(rib < ctx, mc_ref[5:6, :], mb_ref[5:6, :])
        o_ref[...] = _layernorm(ALPHA * x_ref[...] + g * o_ref[...], lng_ref[...], lnb_ref[...])


def _ffn_call(h, mod, layer, w_up, conv_w, conv_b, w_down, ln_g, ln_b, nb, period, ctx):
    t, d = h.shape
    f = w_down.shape[1]
    tm = _pick(period, 544, 16)
    tpb = period // tm
    nt = t // tm
    fc = _pick(f, FFN_CHUNK, FFN_SLAB)
    nf = f // fc
    h3 = h.reshape(nt, tm, d)
    pad = jnp.zeros((1, 16, d), h.dtype)
    halo = jnp.concatenate([jnp.concatenate([h3[1:, :16], pad], 0),
                            jnp.concatenate([pad, h3[:-1, tm - 16:]], 0)], axis=1)
    body = functools.partial(_ffn_body, tm=tm, tpb=tpb, ctx=ctx, period=period)
    return pl.pallas_call(
        body,
        grid=(nt, nf),
        in_specs=[pl.BlockSpec((tm, d), lambda i, j: (i, 0)),
                  pl.BlockSpec((None, 32, d), lambda i, j: (i, 0, 0)),
                  pl.BlockSpec((None, 6, d), lambda i, j: (i // tpb, 0, 0)),
                  pl.BlockSpec((None, 6, d), lambda i, j: (nb, 0, 0)),
                  pl.BlockSpec((None, d, fc), lambda i, j: (layer, 0, j)),
                  pl.BlockSpec((None, d, fc), lambda i, j: (layer, 0, nf + j)),
                  pl.BlockSpec((None, 3, fc), lambda i, j: (layer, 0, j)),
                  pl.BlockSpec((None, 3, fc), lambda i, j: (layer, 0, nf + j)),
                  pl.BlockSpec((None, 1, fc), lambda i, j: (layer, 0, j)),
                  pl.BlockSpec((None, 1, fc), lambda i, j: (layer, 0, nf + j)),
                  pl.BlockSpec((None, fc, d), lambda i, j: (layer, j, 0)),
                  pl.BlockSpec((1, d), lambda i, j: (0, 0)),
                  pl.BlockSpec((1, d), lambda i, j: (0, 0))],
        out_specs=pl.BlockSpec((tm, d), lambda i, j: (i, 0)),
        out_shape=jax.ShapeDtypeStruct((t, d), F32),
        scratch_shapes=([pltpu.VMEM((tm + 32, d), BF16)]
                        + [pltpu.VMEM((tm + 32, FFN_SLAB), F32) for _ in range(2 * fc // FFN_SLAB)]),
        compiler_params=_params("arbitrary", "arbitrary"),
        name="conv_ffn",
    )(h, halo, mod, mod, w_up, w_up, conv_w, conv_w, conv_b, conv_b, w_down, ln_g, ln_b)


def _qkv_body(x_ref, mb_ref, mc_ref, w_ref, cos_ref, sin_ref, qn_ref, kn_ref, o_ref, xb_ref,
              *, tm, tpb, ctx, nq_tiles):
    i = pl.program_id(0)
    j = pl.program_id(1)

    @pl.when(j == 0)
    def _():
        rib = _rows_in_batch(i, tm, tpb)
        xb_ref[...] = _modulate(x_ref[...], rib < ctx, mb_ref, mc_ref, 0, 1).astype(BF16)

    cos = cos_ref[...]
    sin = sin_ref[...]
    lane = lax.broadcasted_iota(jnp.int32, (1, HEAD_DIM), 1)
    first = (lane % 64) < 32
    acc = _dot(xb_ref[...], w_ref[...])
    heads = acc.shape[1] // HEAD_DIM

    def normrope(xh, wn, scale):
        xn = xh * lax.rsqrt(jnp.mean(xh * xh, -1, keepdims=True) + RMS_EPS) * wn
        partner = jnp.where(first, pltpu.roll(xn, HEAD_DIM - 32, 1), pltpu.roll(xn, 32, 1))
        return (xn * cos + partner * sin) * scale

    @pl.when(j < nq_tiles)
    def _():
        for hh in range(heads):
            sl = slice(hh * HEAD_DIM, (hh + 1) * HEAD_DIM)
            o_ref[:, sl] = normrope(acc[:, sl], qn_ref[...], Q_SCALE).astype(BF16)

    @pl.when(j == nq_tiles)
    def _():
        for hh in range(heads):
            sl = slice(hh * HEAD_DIM, (hh + 1) * HEAD_DIM)
            if hh < ATT_KV_HEADS:
                o_ref[:, sl] = normrope(acc[:, sl], kn_ref[...], 1.0).astype(BF16)
            else:
                o_ref[:, sl] = acc[:, sl].astype(BF16)


def _qkv_call(h, mod, w_in, cos, sin, qn, kn, nb, period, ctx):
    t, d = h.shape
    n = w_in.shape[1]
    tn = 2 * ATT_KV_HEADS * HEAD_DIM
    nq_tiles = (n - tn) // tn
    tm = _pick(period, PROJ_ROW_TILE, 16)
    tpb = period // tm
    body = functools.partial(_qkv_body, tm=tm, tpb=tpb, ctx=ctx, nq_tiles=nq_tiles)
    return pl.pallas_call(
        body,
        grid=(t // tm, n // tn),
        in_specs=[pl.BlockSpec((tm, d), lambda i, j: (i, 0)),
                  pl.BlockSpec((None, 6, d), lambda i, j: (i // tpb, 0, 0)),
                  pl.BlockSpec((None, 6, d), lambda i, j: (nb, 0, 0)),
                  pl.BlockSpec((d, tn), lambda i, j: (0, j)),
                  pl.BlockSpec((tm, HEAD_DIM), lambda i, j: (i % tpb, 0)),
                  pl.BlockSpec((tm, HEAD_DIM), lambda i, j: (i % tpb, 0)),
                  pl.BlockSpec((1, HEAD_DIM), lambda i, j: (0, 0)),
                  pl.BlockSpec((1, HEAD_DIM), lambda i, j: (0, 0))],
        out_specs=pl.BlockSpec((tm, tn), lambda i, j: (i, j)),
        out_shape=jax.ShapeDtypeStruct((t, n), BF16),
        scratch_shapes=[pltpu.VMEM((tm, d), BF16)],
        compiler_params=_params("arbitrary", "arbitrary"),
        name="odd_qkv",
    )(h, mod, mod, w_in, cos, sin, qn, kn)


def _rope_tables(seq, ctx):
    rows = seq // GRID_W
    row = jnp.repeat(jnp.arange(rows, dtype=F32), GRID_W)
    col = jnp.tile(jnp.arange(GRID_W, dtype=F32), rows)
    half = HEAD_DIM // 4
    inv = ROPE_THETA ** (-jnp.arange(half, dtype=F32) / half)
    ar = row[:, None] * inv
    ac = col[:, None] * inv
    cos = jnp.concatenate([jnp.cos(ar), jnp.cos(ar), jnp.cos(ac), jnp.cos(ac)], -1)
    sin = jnp.concatenate([-jnp.sin(ar), jnp.sin(ar), -jnp.sin(ac), jnp.sin(ac)], -1)
    cos = jnp.concatenate([jnp.ones((ctx, HEAD_DIM), F32), cos], 0)
    sin = jnp.concatenate([jnp.zeros((ctx, HEAD_DIM), F32), sin], 0)
    return cos, sin


def _attn_body(q_ref, k_ref, v_ref, o_ref, *, tq, key_blocks):
    hd = HEAD_DIM
    q = jnp.concatenate([q_ref[:, g * hd:(g + 1) * hd] for g in range(ATT_GROUP)], axis=0)
    rows = ATT_GROUP * tq
    m = jnp.full((rows, 1), -1e30, F32)
    l = jnp.zeros((rows, 1), F32)
    acc = jnp.zeros((rows, hd), F32)
    s_next = _dot_nt(q, k_ref[key_blocks[0][0]:key_blocks[0][1], :])
    for n, (k0, k1) in enumerate(key_blocks):
        s = s_next
        if n + 1 < len(key_blocks):
            s_next = _dot_nt(q, k_ref[key_blocks[n + 1][0]:key_blocks[n + 1][1], :])
        m_new = jnp.maximum(m, jnp.max(s, -1, keepdims=True))
        a = jnp.exp2(m - m_new)
        p = jnp.exp2(s - m_new)
        l = a * l + jnp.sum(p, -1, keepdims=True)
        acc = a * acc + _dot(p.astype(BF16), v_ref[k0:k1, :])
        m = m_new
    out = acc / l
    for g in range(ATT_GROUP):
        o_ref[:, g * hd:(g + 1) * hd] = out[g * tq:(g + 1) * tq].astype(o_ref.dtype)


def _key_blocks(period, count, align):
    tiles = period // align
    assert tiles * align == period and tiles >= count
    sizes = [tiles // count + (1 if n >= count - tiles % count else 0) for n in range(count)]
    edges = [0]
    for sz in sizes:
        edges.append(edges[-1] + sz * align)
    return tuple(zip(edges[:-1], edges[1:]))


def _attn_call(qkv, nb, seq, ctx):
    period = seq + ctx
    tq = _pick(math.gcd(seq, ctx), ATT_Q_TILE, 16)
    gw = ATT_GROUP * HEAD_DIM
    qpb = seq // tq
    rpb = period // tq
    skip = ctx // tq
    kcol = ATT_HEADS
    vcol = ATT_HEADS + ATT_KV_HEADS
    align = math.gcd(period, MXU_TILE)
    blocks = _key_blocks(period, min(ATT_KEY_BLOCKS, period // align), align)
    body = functools.partial(_attn_body, tq=tq, key_blocks=blocks)
    return pl.pallas_call(
        body,
        grid=(nb, ATT_KV_HEADS, qpb),
        in_specs=[pl.BlockSpec((tq, gw), lambda b, kv, qi: (b * rpb + skip + qi, kv)),
                  pl.BlockSpec((period, HEAD_DIM), lambda b, kv, qi: (b, kcol + kv)),
                  pl.BlockSpec((period, HEAD_DIM), lambda b, kv, qi: (b, vcol + kv))],
        out_specs=pl.BlockSpec((tq, gw), lambda b, kv, qi: (b * qpb + qi, kv)),
        out_shape=jax.ShapeDtypeStruct((nb * seq, ATT_HEADS * HEAD_DIM), BF16),
        compiler_params=_params("arbitrary", "arbitrary", "arbitrary"),
        name="gqa_attention",
    )(qkv, qkv, qkv)


def _outproj_body(a_ref, h_ref, mb_ref, w_ref, lng_ref, lnb_ref, o_ref):
    half = a_ref.shape[0] // 2
    parts = (slice(0, half), slice(half, 2 * half))
    mixes = [_dot(a_ref[rows, :], w_ref[...]) for rows in parts]
    for rows, mix in zip(parts, mixes):
        o_ref[rows, :] = _layernorm(ALPHA * h_ref[rows, :] + mb_ref[2:3, :] * mix, lng_ref[...], lnb_ref[...])


def _outproj_call(att, h, mod, w_out, ln_g, ln_b, nb, seq, ctx):
    d = h.shape[1]
    period = seq + ctx
    tm = _pick(math.gcd(seq, ctx), 256, 16)
    qpb = seq // tm
    rpb = period // tm
    skip = ctx // tm
    return pl.pallas_call(
        _outproj_body,
        grid=(nb * qpb,),
        in_specs=[pl.BlockSpec((tm, d), lambda i: (i, 0)),
                  pl.BlockSpec((tm, d), lambda i: ((i // qpb) * rpb + skip + i % qpb, 0)),
                  pl.BlockSpec((None, 6, d), lambda i: (i // qpb, 0, 0)),
                  pl.BlockSpec((d, d), lambda i: (0, 0)),
                  pl.BlockSpec((1, d), lambda i: (0, 0)),
                  pl.BlockSpec((1, d), lambda i: (0, 0))],
        out_specs=pl.BlockSpec((tm, d), lambda i: (i, 0)),
        out_shape=jax.ShapeDtypeStruct((nb * seq, d), F32),
        compiler_params=_params("arbitrary"),
        name="odd_outproj",
    )(att, h, mod, w_out, ln_g, ln_b)


def kernel(x, c, ctx, c_ctx, mod_w, mod_b, ln1_g, ln1_b, ln2_g, ln2_b, ffn_w_up, ffn_conv_w, ffn_conv_b, ffn_w_down, e_w_in, e_conv_qkv, e_a_log, e_dt_bias, e_gdn_norm, e_lam_re, e_lam_im, e_log_dt, e_b_re, e_b_im, e_c_re, e_c_im, e_d_skip, e_w_glu, e_b_glu, e_w_out, o_w_in, o_q_norm, o_k_norm, o_w_out):
    nb, seq, d = x.shape
    nctx = ctx.shape[1]
    period = nctx + seq
    t = nb * period
    w = GDN_WIDTH
    assert mod_w.shape[0] == DEPTH == 2 and nb < 8

    h = jnp.concatenate([ctx, x], axis=1).reshape(t, d)
    cs = jnp.zeros((8, d), F32).at[:nb].set(c).at[nb].set(c_ctx)
    mod = _mod_call(cs, mod_w, mod_b).reshape(DEPTH, 8, 6, d)
    row = lambda a: a.reshape(1, -1)

    w_in = e_w_in[0]
    gates_at = 4 * w
    w_main = jnp.concatenate([w_in[:, :gates_at], w_in[:, gates_at + 4 * GDN_HEADS:]], axis=1).astype(BF16)
    w_ab = jnp.pad(w_in[:, gates_at:gates_at + 4 * GDN_HEADS], ((0, 0), (0, 128 - 4 * GDN_HEADS))).astype(BF16)
    proj, ab = _inproj_even_call(h, mod[0], w_main, w_ab, nb, period, nctx)
    pad_row = lambda a: jnp.pad(a.reshape(1, -1), ((0, 0), (0, 128 - 2 * GDN_HEADS)))
    qkvn, kt3, gh, gct3 = _gdn_prep_call(proj, ab, e_conv_qkv[0], pad_row(e_a_log[0]), pad_row(e_dt_bias[0]),
                                         period, nctx)
    o = _gdn_scan_call(qkvn, kt3, gh, gct3, nb, period, nctx)

    ops = _s5_tables_call(e_lam_re[0], e_lam_im[0], e_log_dt[0], e_b_re[0], e_b_im[0], e_c_re[0], e_c_im[0])
    u_slabs = proj[:, 4 * w:].reshape(t // S5_CHUNK, S5_CHUNK, d - w).transpose(1, 0, 2)
    y = _s5_call(u_slabs, ops, nb, period, nctx).transpose(1, 0, 2).reshape(t, d - w)

    h = _merge_call(o, proj, y, h, mod[0], row(e_gdn_norm[0]), row(e_d_skip[0]), e_w_glu[0].astype(BF16),
                    row(e_b_glu[0]), e_w_out[0].astype(BF16), row(ln1_g[0]), row(ln1_b[0]), nb, period, nctx)
    w_up, w_down = _cast_call(ffn_w_up), _cast_call(ffn_w_down)
    conv_b = ffn_conv_b[:, None, :]
    h = _ffn_call(h, mod[0], 0, w_up, ffn_conv_w, conv_b, w_down, row(ln2_g[0]), row(ln2_b[0]), nb, period, nctx)

    cos, sin = _rope_tables(seq, nctx)
    qkv = _qkv_call(h, mod[1], o_w_in[0].astype(BF16), cos, sin, row(o_q_norm[0]), row(o_k_norm[0]),
                    nb, period, nctx)
    att = _attn_call(qkv, nb, seq, nctx)
    hl = _outproj_call(att, h, mod[1], o_w_out[0].astype(BF16), row(ln1_g[1]), row(ln1_b[1]), nb, seq, nctx)
    out = _ffn_call(hl, mod[1], 1, w_up, ffn_conv_w, conv_b, w_down, row(ln2_g[1]), row(ln2_b[1]), nb, seq, 0)
    return out.reshape(nb, seq, d)
```

```python
import functools
import math

import jax
import jax.numpy as jnp
from jax import lax
from jax.experimental import pallas as pl
from jax.experimental.pallas import tpu as pltpu

F32 = jnp.float32
BF16 = jnp.bfloat16

DEPTH = 2
GDN_HEADS = 8
HEAD_DIM = 128
GDN_WIDTH = GDN_HEADS * HEAD_DIM
GDN_CHUNK = 128
GDN_INTRA_BATCH = 17
GDN_SOLVE_BASE = 8
S5_GROUP = 16
S5_STATE = 64
S5_CHUNK = 16
S5_TILE = S5_CHUNK * S5_GROUP
S5_GROUPS_PER_STEP = 4
S5_TABLE_GROUPS = 8
ATT_HEADS = 16
ATT_KV_HEADS = 4
ATT_GROUP = ATT_HEADS // ATT_KV_HEADS
ATT_SCALE = HEAD_DIM ** -0.5
Q_SCALE = ATT_SCALE * math.log2(math.e)
ATT_Q_TILE = 256
ATT_KEY_BLOCKS = 6
MXU_TILE = 256
GRID_W = 64
ROPE_THETA = 10000.0
PROJ_ROW_TILE = 1088
FFN_CHUNK = 512
FFN_SLAB = 256
ALPHA = (2 * DEPTH) ** 0.25
LN_EPS = 1e-6
RMS_EPS = 1e-6
L2_EPS = 1e-6
VMEM_LIMIT_BYTES = 56 * 1024 * 1024


def _pick(n, target, mult):
    best = None
    for d in range(mult, min(n, target) + 1, mult):
        if n % d == 0:
            best = d
    assert best is not None, (n, target, mult)
    return best


def _params(*sem):
    return pltpu.CompilerParams(dimension_semantics=sem, vmem_limit_bytes=VMEM_LIMIT_BYTES)


def _dot(a, b):
    return jnp.dot(a, b, preferred_element_type=F32)


def _dot_nt(a, b):
    return lax.dot_general(a, b, (((1,), (1,)), ((), ())), preferred_element_type=F32)


def _split3(a):
    p0 = a.astype(BF16)
    r = a - p0.astype(F32)
    p1 = r.astype(BF16)
    p2 = (r - p1.astype(F32)).astype(BF16)
    return p0, p1, p2


def _rows_in_batch(i, tm, tiles_per_batch):
    return (i % tiles_per_batch) * tm + lax.broadcasted_iota(jnp.int32, (tm, 1), 0)


def _modulate(x, is_ctx, mb_ref, mc_ref, shift_i, scale_i):
    scale = jnp.where(is_ctx, mc_ref[scale_i:scale_i + 1, :], mb_ref[scale_i:scale_i + 1, :])
    shift = jnp.where(is_ctx, mc_ref[shift_i:shift_i + 1, :], mb_ref[shift_i:shift_i + 1, :])
    return x * (1.0 + scale) + shift


def _layernorm(r, g, b):
    xc = r - jnp.mean(r, -1, keepdims=True)
    var = jnp.mean(xc * xc, -1, keepdims=True)
    return xc * lax.rsqrt(var + LN_EPS) * g + b


def _silu(x):
    return x * jax.nn.sigmoid(x)


def _seq_edges(rib, ctx, period):
    has_prev = (rib != 0) & (rib != ctx)
    has_next = (rib != ctx - 1) & (rib != period - 1)
    return has_prev, has_next


def _mod_body(c_ref, w_ref, b_ref, o_ref):
    s = _silu(c_ref[...])
    o_ref[...] = _dot(s.astype(BF16), w_ref[...].astype(BF16)) + b_ref[...]


def _mod_call(cs, mod_w, mod_b):
    depth, d, n = mod_w.shape
    tn = _pick(n, 1024, 128)
    return pl.pallas_call(
        _mod_body,
        grid=(depth, n // tn),
        in_specs=[pl.BlockSpec((8, d), lambda l, j: (0, 0)),
                  pl.BlockSpec((None, d, tn), lambda l, j: (l, 0, j)),
                  pl.BlockSpec((None, 1, tn), lambda l, j: (l, 0, j))],
        out_specs=pl.BlockSpec((None, 8, tn), lambda l, j: (l, 0, j)),
        out_shape=jax.ShapeDtypeStruct((depth, 8, n), F32),
        compiler_params=_params("arbitrary", "arbitrary"),
        name="adaln_mod",
    )(cs, mod_w, mod_b.reshape(depth, 1, n))


def _inproj_even_body(x_ref, mb_ref, mc_ref, w_ref, wab_ref, o_ref, ab_ref, xb_ref, *, tm, tpb, ctx):
    i = pl.program_id(0)

    @pl.when(pl.program_id(1) == 0)
    def _():
        rib = _rows_in_batch(i, tm, tpb)
        xb = _modulate(x_ref[...], rib < ctx, mb_ref, mc_ref, 0, 1).astype(BF16)
        xb_ref[...] = xb
        ab_ref[...] = _dot(xb, wab_ref[...])

    o_ref[...] = _dot(xb_ref[...], w_ref[...]).astype(o_ref.dtype)


def _inproj_even_call(h, mod, w_main, w_ab, nb, period, ctx):
    t, d = h.shape
    n = w_main.shape[1]
    tm = _pick(period, PROJ_ROW_TILE, 16)
    tn = _pick(n, 1280, 256)
    tpb = period // tm
    body = functools.partial(_inproj_even_body, tm=tm, tpb=tpb, ctx=ctx)
    return pl.pallas_call(
        body,
        grid=(t // tm, n // tn),
        in_specs=[pl.BlockSpec((tm, d), lambda i, j: (i, 0)),
                  pl.BlockSpec((None, 6, d), lambda i, j: (i // tpb, 0, 0)),
                  pl.BlockSpec((None, 6, d), lambda i, j: (nb, 0, 0)),
                  pl.BlockSpec((d, tn), lambda i, j: (0, j)),
                  pl.BlockSpec((d, 128), lambda i, j: (0, 0))],
        out_specs=[pl.BlockSpec((tm, tn), lambda i, j: (i, j)),
                   pl.BlockSpec((tm, 128), lambda i, j: (i, 0))],
        out_shape=[jax.ShapeDtypeStruct((t, n), BF16), jax.ShapeDtypeStruct((t, 128), F32)],
        scratch_shapes=[pltpu.VMEM((tm, d), BF16)],
        compiler_params=_params("arbitrary", "arbitrary"),
        name="even_inproj",
    )(h, mod, mod, w_main, w_ab)


def _gdn_prep_body(x_ref, xn_ref, xp_ref, cw_ref, ab_ref, alog_ref, dtb_ref, tril_ref, triu_ref, e_ref,
                   qkv_ref, kt_ref, gh_ref, gct_ref, ext_ref, *, tr, tpb, ctx, period):
    i = pl.program_id(0)
    sec = pl.program_id(1)
    rib = _rows_in_batch(i, tr, tpb)
    has_prev, has_next = _seq_edges(rib, ctx, period)
    ext_ref[0:16, :] = xp_ref[...].astype(F32)
    ext_ref[16:tr + 16, :] = x_ref[...].astype(F32)
    ext_ref[tr + 16:tr + 32, :] = xn_ref[...].astype(F32)
    cw = cw_ref[...]
    y = (jnp.where(has_prev, ext_ref[15:tr + 15, :], 0.0) * cw[0:1] + ext_ref[16:tr + 16, :] * cw[1:2]
         + jnp.where(has_next, ext_ref[17:tr + 17, :], 0.0) * cw[2:3])
    y = _silu(y)

    def l2n(scale):
        parts = []
        for hh in range(GDN_HEADS):
            yh = y[:, hh * HEAD_DIM:(hh + 1) * HEAD_DIM]
            parts.append(yh * (lax.rsqrt(jnp.sum(yh * yh, -1, keepdims=True) + L2_EPS) * scale))
        return jnp.concatenate(parts, axis=1)

    @pl.when(sec == 0)
    def _():
        qkv_ref[...] = l2n(HEAD_DIM ** -0.5).astype(BF16)
        ab = ab_ref[...]
        lane = lax.broadcasted_iota(jnp.int32, (1, 128), 1)
        xg = ab + dtb_ref[...]
        softplus = jnp.maximum(xg, 0.0) + jnp.log1p(jnp.exp(-jnp.abs(xg)))
        g = -jnp.exp(alog_ref[...]) * softplus
        g0, g1, g2 = _split3(g)
        tril = tril_ref[...]
        triu = triu_ref[...]
        pre = _dot(tril, g0) + _dot(tril, g1) + _dot(tril, g2)
        suf = _dot(triu, g0) + _dot(triu, g1) + _dot(triu, g2)
        gc = jnp.where(lane < GDN_HEADS, pre, suf)
        vals = jnp.where(lane < 2 * GDN_HEADS, gc, jax.nn.sigmoid(ab))
        v0, v1, v2 = _split3(vals)
        e = e_ref[...]
        gh_ref[...] = _dot(v0, e) + _dot(v1, e) + _dot(v2, e)
        gct = gc.T
        for c in range(tr // GDN_CHUNK):
            gct_ref[c] = gct[:2 * GDN_HEADS, c * GDN_CHUNK:(c + 1) * GDN_CHUNK]

    @pl.when(sec == 1)
    def _():
        kn = l2n(1.0)
        qkv_ref[...] = kn.astype(BF16)
        knt = kn.T
        for c in range(tr // GDN_CHUNK):
            kt_ref[c] = knt[:, c * GDN_CHUNK:(c + 1) * GDN_CHUNK].astype(BF16)

    @pl.when(sec == 2)
    def _():
        qkv_ref[...] = y.astype(BF16)


def _gdn_prep_call(proj, ab, conv_w, alog_row, dtb_row, period, ctx):
    t = proj.shape[0]
    w = GDN_WIDTH
    tr = _pick(period, 256, GDN_CHUNK)
    tpb = period // tr
    hb = tr // 16
    nh = t // 16
    nck = tr // GDN_CHUNK
    r = jnp.arange(tr)
    same = (r[:, None] // GDN_CHUNK) == (r[None, :] // GDN_CHUNK)
    tril = (same & (r[:, None] >= r[None, :])).astype(BF16)
    triu = (same & (r[:, None] <= r[None, :])).astype(BF16)
    src = jnp.arange(128)[:, None]
    dst = jnp.arange(w)[None, :]
    expand = ((dst % HEAD_DIM < 4) & (src == (dst % HEAD_DIM) * GDN_HEADS + dst // HEAD_DIM)).astype(BF16)
    body = functools.partial(_gdn_prep_body, tr=tr, tpb=tpb, ctx=ctx, period=period)
    return pl.pallas_call(
        body,
        grid=(t // tr, 3),
        in_specs=[pl.BlockSpec((tr, w), lambda i, s: (i, s)),
                  pl.BlockSpec((16, w), lambda i, s: (jnp.minimum((i + 1) * hb, nh - 1), s)),
                  pl.BlockSpec((16, w), lambda i, s: (jnp.maximum(i * hb - 1, 0), s)),
                  pl.BlockSpec((3, w), lambda i, s: (0, s)),
                  pl.BlockSpec((tr, 128), lambda i, s: (i, 0)),
                  pl.BlockSpec((1, 128), lambda i, s: (0, 0)),
                  pl.BlockSpec((1, 128), lambda i, s: (0, 0)),
                  pl.BlockSpec((tr, tr), lambda i, s: (0, 0)),
                  pl.BlockSpec((tr, tr), lambda i, s: (0, 0)),
                  pl.BlockSpec((128, w), lambda i, s: (0, 0))],
        out_specs=[pl.BlockSpec((tr, w), lambda i, s: (i, s)),
                   pl.BlockSpec((nck, w, GDN_CHUNK), lambda i, s: (i, 0, 0)),
                   pl.BlockSpec((tr, w), lambda i, s: (i, 0)),
                   pl.BlockSpec((nck, 2 * GDN_HEADS, GDN_CHUNK), lambda i, s: (i, 0, 0))],
        out_shape=[jax.ShapeDtypeStruct((t, 3 * w), BF16),
                   jax.ShapeDtypeStruct((t // GDN_CHUNK, w, GDN_CHUNK), BF16),
                   jax.ShapeDtypeStruct((t, w), F32),
                   jax.ShapeDtypeStruct((t // GDN_CHUNK, 2 * GDN_HEADS, GDN_CHUNK), F32)],
        scratch_shapes=[pltpu.VMEM((tr + 32, w), F32)],
        compiler_params=_params("arbitrary", "arbitrary"),
        name="gdn_prep",
    )(proj, proj, proj, conv_w, ab, alog_row, dtb_row, tril, triu, expand)


def _gdn_scan_body(q_ref, k_ref, v_ref, kt_ref, gh_ref, gct_ref, o_ref,
                   aqf, aqb, bsf, bsb, egf, egb, *, nch, nctx, batch):
    head = pl.program_id(1)
    cs = GDN_CHUNK
    ii = lax.broadcasted_iota(jnp.int32, (cs, cs), 0)
    jj = lax.broadcasted_iota(jnp.int32, (cs, cs), 1)
    masks = ((ii >= jj, ii > jj), (ii <= jj, ii < jj))
    same_block = {}
    width = GDN_SOLVE_BASE
    while width <= cs:
        shift = width.bit_length() - 1
        same_block[width] = jnp.right_shift(ii, shift) == jnp.right_shift(jj, shift)
        width *= 2
    hd = HEAD_DIM
    aqs, bss, egs = (aqf, aqb), (bsf, bsb), (egf, egb)

    def intra(first, count):
        chunks = [first + j for j in range(count)]
        r0s = [pl.multiple_of(c * cs, cs) for c in chunks]
        qs = [q_ref[pl.ds(r0, cs), :] for r0 in r0s]
        ks = [k_ref[pl.ds(r0, cs), :] for r0 in r0s]
        vs = [v_ref[pl.ds(r0, cs), :] for r0 in r0s]
        kts = [kt_ref[c] for c in chunks]
        raws = [_dot(jnp.concatenate([q, k], axis=0), kt) for q, k, kt in zip(qs, ks, kts)]
        raw_q = [r[:cs] for r in raws]
        raw_k = [r[cs:] for r in raws]
        pms, xs, atts, kps, qes, gends = [], [], [], [], [], []
        for j, c in enumerate(chunks):
            g4 = gh_ref[pl.ds(r0s[j], cs), :]
            qf, kf, vf = qs[j].astype(F32), ks[j].astype(F32), vs[j].astype(F32)
            ktf = kts[j].astype(F32)
            for d in (0, 1):
                incl, strict = masks[d]
                gcol = g4[:, d:d + 1]
                bcol = g4[:, 2 + d:3 + d]
                grow = gct_ref[c, pl.ds(d * GDN_HEADS + head, 1), :]
                decay = jnp.where(incl, jnp.exp(jnp.where(incl, gcol - grow, 0.0)), 0.0)
                eg = jnp.exp(gcol)
                gend = grow[:, cs - 1:cs] if d == 0 else grow[:, 0:1]
                pms.append(jnp.where(strict, -(raw_k[j] * bcol) * decay, 0.0))
                xs.append(jnp.concatenate([vf * bcol, kf * (bcol * eg)], axis=1))
                atts.append((raw_q[j] * decay).astype(BF16))
                kps.append((ktf * jnp.exp(gend - grow)).astype(BF16))
                qes.append(qf * eg)
                gends.append(gend)
        bfs = lambda vals: [a.astype(BF16) for a in vals]
        pds = bfs([jnp.where(same_block[GDN_SOLVE_BASE], pm, 0.0) for pm in pms])
        ns = [pd.astype(F32) for pd in pds]
        pw = pds
        span = 1
        while 2 * span < GDN_SOLVE_BASE:
            sq = [_dot(p, p) for p in pw]
            pw = bfs(sq)
            ns = [n + s + _dot(p, n.astype(BF16)) for n, s, p in zip(ns, sq, pw)]
            span *= 2
        width = GDN_SOLVE_BASE
        while width < cs:
            couple = same_block[2 * width] & ~same_block[width]
            qs_ = [jnp.where(couple, pm, 0.0) for pm in pms]
            nbs = bfs(ns)
            bs_ = [q + _dot(nb, q.astype(BF16)) for q, nb in zip(qs_, nbs)]
            ns = [n + b + _dot(b.astype(BF16), nb) for n, b, nb in zip(ns, bs_, nbs)]
            width *= 2
        xs = [x + _dot(n.astype(BF16), x.astype(BF16)) for n, x in zip(ns, xs)]
        xbs = [x.astype(BF16) for x in xs]
        both = [_dot(jnp.concatenate([att, kp], axis=0), xb) for att, kp, xb in zip(atts, kps, xbs)]
        aws = [r[:cs] for r in both]
        kxs = [r[cs:] for r in both]
        for j, c in enumerate(chunks):
            for d in (0, 1):
                n = 2 * j + d
                aqs[d][c, 0:hd, :] = (-kxs[n][:, hd:]).astype(BF16)
                aqs[d][c, hd:hd + cs, :] = (qes[n] - aws[n][:, hd:]).astype(BF16)
                bss[d][c] = kxs[n][:, :hd]
                egs[d][c] = jnp.broadcast_to(jnp.exp(gends[n]), (8, hd))
            o_ref[pl.ds(r0s[j], cs), :] = aws[2 * j][:, :hd] + aws[2 * j + 1][:, :hd]

    def inter(i, carry):
        cb = jnp.where(i < nctx, nctx - 1 - i, nch - 1 - (i - nctx))
        new = []
        for d, c in ((0, i), (1, cb)):
            s = carry[d]
            r0 = pl.multiple_of(c * cs, cs)
            r = _dot(aqs[d][c], s.astype(BF16))
            o_ref[pl.ds(r0, cs), :] += r[hd:]
            new.append(s * egs[d][c][0:1, :] + r[:hd] + bss[d][c])
        return tuple(new)

    full = nch // batch

    def intra_step(it, carry):
        intra(it * batch, batch)
        return carry

    lax.fori_loop(0, full, intra_step, 0)
    if nch > full * batch:
        intra(full * batch, nch - full * batch)
    zero = jnp.zeros((hd, hd), F32)
    lax.fori_loop(0, nch, inter, (zero, zero), unroll=2)


def _gdn_scan_call(qkvn, kt3, gh, gct3, nb, period, ctx):
    t = qkvn.shape[0]
    nch = period // GDN_CHUNK
    nctx = ctx // GDN_CHUNK
    hd = HEAD_DIM
    body = functools.partial(_gdn_scan_body, nch=nch, nctx=nctx, batch=min(nch, GDN_INTRA_BATCH))
    dirs2 = lambda shape, dt: [pltpu.VMEM(shape, dt), pltpu.VMEM(shape, dt)]
    return pl.pallas_call(
        body,
        grid=(nb, GDN_HEADS),
        in_specs=[pl.BlockSpec((period, hd), lambda b, h: (b, h)),
                  pl.BlockSpec((period, hd), lambda b, h: (b, GDN_HEADS + h)),
                  pl.BlockSpec((period, hd), lambda b, h: (b, 2 * GDN_HEADS + h)),
                  pl.BlockSpec((nch, hd, GDN_CHUNK), lambda b, h: (b, h, 0)),
                  pl.BlockSpec((period, hd), lambda b, h: (b, h)),
                  pl.BlockSpec((nch, 2 * GDN_HEADS, GDN_CHUNK), lambda b, h: (b, 0, 0))],
        out_specs=pl.BlockSpec((period, hd), lambda b, h: (b, h)),
        out_shape=jax.ShapeDtypeStruct((t, GDN_WIDTH), F32),
        scratch_shapes=(dirs2((nch, hd + GDN_CHUNK, hd), BF16) + dirs2((nch, hd, hd), F32)
                        + dirs2((nch, 8, hd), F32)),
        compiler_params=_params("arbitrary", "arbitrary"),
        name="gdn_scan",
    )(qkvn, qkvn, qkvn, kt3, gh, gct3)


def _s5_tables_body(*refs):
    lax.fori_loop(0, refs[0].shape[1], functools.partial(_s5_tables_group, refs), 0)


def _s5_tables_group(refs, q, carry):
    (lr_ref, li_ref, ldt_ref, btr_ref, bti_ref, cr_ref, ci_ref,
     wir_ref, wii_ref, wor_ref, woi_ref, ktf_ref, ktb_ref, lam_ref) = refs
    n, cg = S5_CHUNK, S5_GROUP
    kk = lax.broadcasted_iota(jnp.int32, (24, 1), 0).astype(F32)
    lane = lax.broadcasted_iota(jnp.int32, (1, S5_TILE), 1)

    def dot3_nt(a, b):
        a0 = a.astype(BF16)
        a1 = (a - a0.astype(F32)).astype(BF16)
        b0 = b.astype(BF16)
        b1 = (b - b0.astype(F32)).astype(BF16)
        return _dot_nt(a0, b0) + _dot_nt(a0, b1) + _dot_nt(a1, b0)

    win_r, win_i, wout_r, wout_i, lam_r, lam_i = [], [], [], [], [], []
    for d in (0, 1):
        lr = lr_ref[d, q]
        li = li_ref[d, q]
        dt = jnp.exp(ldt_ref[d, q])
        mag = jnp.exp(kk * (lr * dt))
        ang = kk * (li * dt)
        er = mag * jnp.cos(ang)
        ei = mag * jnp.sin(ang)
        nr = er[1:2] - 1.0
        ni = ei[1:2]
        den = lr * lr + li * li
        cfr = (nr * lr + ni * li) / den
        cfi = (ni * lr - nr * li) / den
        btr = btr_ref[d, q]
        bti = bti_ref[d, q]
        bbr = btr * cfr - bti * cfi
        bbi = btr * cfi + bti * cfr
        cr = cr_ref[d, q]
        ci = ci_ref[d, q]
        a_r, a_i, cm_r, cm_i, c0_r, c0_i = [], [], [], [], [], []
        for k in range(n):
            e_r, e_i = er[k:k + 1], ei[k:k + 1]
            a_r.append(bbr * e_r - bbi * e_i)
            a_i.append(bbr * e_i + bbi * e_r)
            c0_r.append(cr * e_r - ci * e_i)
            c0_i.append(cr * e_i + ci * e_r)
            f_r, f_i = er[k + 1:k + 2], ei[k + 1:k + 2]
            cm_r.append(cr * f_r - ci * f_i)
            cm_i.append(-(cr * f_i + ci * f_r))
        fwd = d == 0
        down = list(range(n - 1, -1, -1))
        up = list(range(n))
        pick = lambda parts, order: jnp.concatenate([parts[k] for k in order], axis=0)
        win_r.append(pick(a_r, down if fwd else up))
        win_i.append(pick(a_i, down if fwd else up))
        wout_r.append(pick(cm_r, up if fwd else down))
        wout_i.append(pick(cm_i, up if fwd else down))
        lam_r.append(er[n:n + 1])
        lam_i.append(ei[n:n + 1])
        order = up if fwd else down
        strip = dot3_nt(bbr, pick(c0_r, order)) - dot3_nt(bbi, pick(c0_i, order))
        for s in range(n):
            rows = slice(s * cg, (s + 1) * cg)
            if fwd:
                blk = strip if s == 0 else jnp.where(lane >= s * cg, pltpu.roll(strip, s * cg, 1), 0.0)
                ktf_ref[q, rows, :] = blk.astype(BF16)
            else:
                sh = (n - 1 - s) * cg
                blk = strip if sh == 0 else jnp.where(lane < (s + 1) * cg, pltpu.roll(strip, S5_TILE - sh, 1), 0.0)
                ktb_ref[q, rows, :] = blk.astype(BF16)
    wir_ref[q] = jnp.concatenate(win_r, axis=1).astype(BF16)
    wii_ref[q] = jnp.concatenate(win_i, axis=1).astype(BF16)
    wor_ref[q] = jnp.concatenate(wout_r, axis=1).T.astype(BF16)
    woi_ref[q] = jnp.concatenate(wout_i, axis=1).T.astype(BF16)
    lam_ref[q] = jnp.concatenate([jnp.concatenate(lam_r, axis=1), jnp.concatenate(lam_i, axis=1)], axis=0)
    return carry


def _s5_tables_call(lam_re, lam_im, log_dt, b_re, b_im, c_re, c_im):
    _, g, p = lam_re.shape
    cg = S5_GROUP
    sw = 2 * p
    row4 = lambda a: a.reshape(2, g, 1, p)
    per = _pick(g, S5_TABLE_GROUPS, 1)
    ispec = lambda r: pl.BlockSpec((2, per, r, p), lambda gi: (0, gi, 0, 0))
    ospec = lambda r, c: pl.BlockSpec((per, r, c), lambda gi: (gi, 0, 0))
    ldt = jnp.broadcast_to(log_dt[:, :, None, None], (2, g, 1, p))
    wir, wii, wor, woi, ktf, ktb, lam = pl.pallas_call(
        _s5_tables_body,
        grid=(g // per,),
        in_specs=[ispec(1), ispec(1), ispec(1), ispec(cg), ispec(cg), ispec(cg), ispec(cg)],
        out_specs=[ospec(S5_TILE, sw), ospec(S5_TILE, sw), ospec(sw, S5_TILE), ospec(sw, S5_TILE),
                   ospec(S5_TILE, S5_TILE), ospec(S5_TILE, S5_TILE), ospec(2, sw)],
        out_shape=[jax.ShapeDtypeStruct((g, S5_TILE, sw), BF16)] * 2 + [jax.ShapeDtypeStruct((g, sw, S5_TILE), BF16)] * 2
                  + [jax.ShapeDtypeStruct((g, S5_TILE, S5_TILE), BF16)] * 2 + [jax.ShapeDtypeStruct((g, 2, sw), F32)],
        compiler_params=_params("arbitrary"),
        name="s5_tables",
    )(row4(lam_re), row4(lam_im), ldt, jnp.swapaxes(b_re, -1, -2), jnp.swapaxes(b_im, -1, -2), c_re, c_im)
    gb = S5_GROUPS_PER_STEP
    lam_r = lam[:, 0, :].reshape(g // gb, 1, gb * sw)
    lam_i = lam[:, 1, :].reshape(g // gb, 1, gb * sw)
    return ktf, ktb, wir, wii, wor, woi, lam_r, lam_i


def _s5_gather_chunks(u_ref, l_ref):
    lanes = u_ref.shape[2]
    for s in range(S5_CHUNK):
        l_ref[:, s * lanes:(s + 1) * lanes] = u_ref[s]


def _s5_in_body(u_ref, pm_ref, wr_ref, wi_ref, ub_ref, x_ref, l_ref):
    @pl.when(pl.program_id(1) == 0)
    def _():
        _s5_gather_chunks(u_ref, l_ref)

    ub = _dot(l_ref[...], pm_ref[...]).astype(BF16)
    ub_ref[...] = ub
    sw = 2 * S5_STATE
    for q in range(ub.shape[1] // S5_TILE):
        ug = ub[:, q * S5_TILE:(q + 1) * S5_TILE]
        x_ref[:, 2 * q * sw:(2 * q + 1) * sw] = _dot(ug, wr_ref[q])
        x_ref[:, (2 * q + 1) * sw:(2 * q + 2) * sw] = _dot(ug, wi_ref[q])


def _s5_out_body(ub_ref, s_ref, ktf_ref, ktb_ref, wor_ref, woi_ref, pmt_ref, y_ref, y8_ref, *, col_tiles):
    n = pl.program_id(1)
    sw = 2 * S5_STATE

    @pl.when(n == 0)
    def _():
        for g in range(ub_ref.shape[1] // S5_TILE):
            ug = ub_ref[:, g * S5_TILE:(g + 1) * S5_TILE]
            s_r = s_ref[:, 2 * g * sw:(2 * g + 1) * sw]
            s_i = s_ref[:, (2 * g + 1) * sw:(2 * g + 2) * sw]
            y = _dot(ug, ktf_ref[g]) + _dot(ug, ktb_ref[g]) + _dot(s_r, wor_ref[g]) + _dot(s_i, woi_ref[g])
            y8_ref[:, g * S5_TILE:(g + 1) * S5_TILE] = y.astype(BF16)

    yp = _dot(y8_ref[...], pmt_ref[...])
    lanes = y_ref.shape[2]
    per = yp.shape[1] // lanes
    for nn in range(col_tiles):
        @pl.when(n == nn)
        def _(nn=nn):
            for q in range(per):
                y_ref[nn * per + q] = yp[:, q * lanes:(q + 1) * lanes].astype(y_ref.dtype)


def _s5_scan_body(x_ref, lr_ref, li_ref, s_ref, *scratch, nb, nch, nctx):
    gb = S5_GROUPS_PER_STEP
    sw = 2 * S5_STATE
    xr, xi, sfr, sfi, sbr, sbi = (scratch[k * gb:(k + 1) * gb] for k in range(6))
    for g in range(gb):
        xr[g][...] = x_ref[:, 2 * g * sw:(2 * g + 1) * sw]
        xi[g][...] = x_ref[:, (2 * g + 1) * sw:(2 * g + 2) * sw]
    is_f = lax.broadcasted_iota(jnp.int32, (1, sw), 1) < S5_STATE
    ar = [lr_ref[:, g * sw:(g + 1) * sw] for g in range(gb)]
    ai = [li_ref[:, g * sw:(g + 1) * sw] for g in range(gb)]

    def step(i, carry):
        cb = jnp.where(i < nctx, nctx - 1 - i, nch - 1 - (i - nctx))
        at_f = pl.ds(i, nb, stride=nch)
        at_b = pl.ds(cb, nb, stride=nch)
        new = []
        for g in range(gb):
            sr, si = carry[2 * g], carry[2 * g + 1]
            sfr[g][at_f, :] = sr
            sfi[g][at_f, :] = si
            sbr[g][at_b, :] = sr
            sbi[g][at_b, :] = si
            inr = jnp.where(is_f, xr[g][at_f, :], xr[g][at_b, :])
            ini = jnp.where(is_f, xi[g][at_f, :], xi[g][at_b, :])
            new.append(ar[g] * sr - ai[g] * si + inr)
            new.append(ar[g] * si + ai[g] * sr + ini)
        return tuple(new)

    zero = jnp.zeros((nb, sw), F32)
    lax.fori_loop(0, nch, step, (zero,) * (2 * gb))
    for g in range(gb):
        s_ref[:, 2 * g * sw:(2 * g + 1) * sw] = jnp.where(is_f, sfr[g][...], sbr[g][...]).astype(BF16)
        s_ref[:, (2 * g + 1) * sw:(2 * g + 2) * sw] = jnp.where(is_f, sfi[g][...], sbi[g][...]).astype(BF16)


def _s5_call(u_slabs, ops, nb, period, ctx):
    ktf, ktb, wir, wii, wor, woi, lam_r, lam_i = ops
    _, rows, width = u_slabs.shape
    groups = width // S5_GROUP
    lanes = 128
    lg = lanes // S5_GROUP
    nblk = width // lanes
    blk_w = S5_CHUNK * lanes
    col = 2 * S5_TILE
    col_tiles = blk_w // col
    sw = 2 * S5_STATE
    gb = S5_GROUPS_PER_STEP
    nch = period // S5_CHUNK
    nctx = ctx // S5_CHUNK
    src = jnp.arange(blk_w)
    dst = (src % lanes // S5_GROUP) * S5_TILE + (src // lanes) * S5_GROUP + src % S5_GROUP
    perm = (dst[:, None] == jnp.arange(blk_w)[None, :]).astype(BF16)
    perm_t = perm.T

    u_perm, xin = pl.pallas_call(
        _s5_in_body,
        grid=(nblk, col_tiles),
        in_specs=[pl.BlockSpec((S5_CHUNK, rows, lanes), lambda j, n: (0, 0, j)),
                  pl.BlockSpec((blk_w, col), lambda j, n: (0, n)),
                  pl.BlockSpec((col // S5_TILE, S5_TILE, sw), lambda j, n: (j * col_tiles + n, 0, 0)),
                  pl.BlockSpec((col // S5_TILE, S5_TILE, sw), lambda j, n: (j * col_tiles + n, 0, 0))],
        out_specs=[pl.BlockSpec((rows, col), lambda j, n: (0, j * col_tiles + n)),
                   pl.BlockSpec((rows, col), lambda j, n: (0, j * col_tiles + n))],
        out_shape=[jax.ShapeDtypeStruct((rows, groups * S5_TILE), BF16),
                   jax.ShapeDtypeStruct((rows, groups * 2 * sw), F32)],
        scratch_shapes=[pltpu.VMEM((rows, blk_w), BF16)],
        compiler_params=_params("arbitrary", "arbitrary"),
        name="s5_in",
    )(u_slabs, perm, wir, wii)

    states = pl.pallas_call(
        functools.partial(_s5_scan_body, nb=nb, nch=nch, nctx=nctx),
        grid=(groups // gb,),
        in_specs=[pl.BlockSpec((rows, gb * 2 * sw), lambda i: (0, i)),
                  pl.BlockSpec((None, 1, gb * sw), lambda i: (i, 0, 0)),
                  pl.BlockSpec((None, 1, gb * sw), lambda i: (i, 0, 0))],
        out_specs=pl.BlockSpec((rows, gb * 2 * sw), lambda i: (0, i)),
        out_shape=jax.ShapeDtypeStruct((rows, groups * 2 * sw), BF16),
        scratch_shapes=[pltpu.VMEM((rows, sw), F32) for _ in range(6 * gb)],
        compiler_params=_params("arbitrary"),
        name="s5_scan",
    )(xin, lam_r, lam_i)

    wspec = lambda r, c: pl.BlockSpec((lg, r, c), lambda j, n: (j, 0, 0))
    return pl.pallas_call(
        functools.partial(_s5_out_body, col_tiles=col_tiles),
        grid=(nblk, col_tiles),
        in_specs=[pl.BlockSpec((rows, lg * S5_TILE), lambda j, n: (0, j)),
                  pl.BlockSpec((rows, lg * 2 * sw), lambda j, n: (0, j)),
                  wspec(S5_TILE, S5_TILE), wspec(S5_TILE, S5_TILE), wspec(sw, S5_TILE), wspec(sw, S5_TILE),
                  pl.BlockSpec((blk_w, col), lambda j, n: (0, n))],
        out_specs=pl.BlockSpec((S5_CHUNK, rows, lanes), lambda j, n: (0, 0, j)),
        out_shape=jax.ShapeDtypeStruct((S5_CHUNK, rows, width), BF16),
        scratch_shapes=[pltpu.VMEM((rows, lg * S5_TILE), BF16)],
        compiler_params=_params("arbitrary", "arbitrary"),
        name="s5_out",
    )(u_perm, states, ktf, ktb, wor, woi, perm_t)


def _merge_body(o_ref, z_ref, u_ref, y_ref, h_ref, mb_ref, mc_ref, gn_ref, ds_ref, wglu_ref, bglu_ref,
                wout_ref, lng_ref, lnb_ref, out_ref, cat_ref, *, tm, tpb, ctx):
    rib = _rows_in_batch(pl.program_id(0), tm, tpb)
    gn = gn_ref[...]
    half = tm // 2
    for rows in (slice(0, half), slice(half, tm)):
        for hh in range(GDN_HEADS):
            sl = slice(hh * HEAD_DIM, (hh + 1) * HEAD_DIM)
            oh = o_ref[rows, sl]
            r = lax.rsqrt(jnp.mean(oh * oh, -1, keepdims=True) + RMS_EPS)
            cat_ref[rows, sl] = (oh * r * gn * _silu(z_ref[rows, sl].astype(F32))).astype(BF16)
        s = y_ref[rows, :].astype(F32) + ds_ref[...] * u_ref[rows, :].astype(F32)
        s = 0.5 * s * (1.0 + jnp.tanh(math.sqrt(2.0 / math.pi) * (s + 0.044715 * (s * s * s))))
        s = s * jax.nn.sigmoid(_dot(s.astype(BF16), wglu_ref[...]) + bglu_ref[...])
        cat_ref[rows, GDN_WIDTH:] = s.astype(BF16)
        mix = _dot(cat_ref[rows, :], wout_ref[...])
        gate = jnp.where(rib[rows] < ctx, mc_ref[2:3, :], mb_ref[2:3, :])
        out_ref[rows, :] = _layernorm(ALPHA * h_ref[rows, :] + gate * mix, lng_ref[...], lnb_ref[...])


def _merge_call(o, proj, y, h, mod, gn, d_skip, w_glu, b_glu, w_out, ln_g, ln_b, nb, period, ctx):
    t, d = h.shape
    w = GDN_WIDTH
    sw = d - w
    tm = _pick(period, 544, 32)
    tpb = period // tm
    body = functools.partial(_merge_body, tm=tm, tpb=tpb, ctx=ctx)
    const = lambda r, c: pl.BlockSpec((r, c), lambda i: (0, 0), pipeline_mode=pl.Buffered(1))
    return pl.pallas_call(
        body,
        grid=(t // tm,),
        in_specs=[pl.BlockSpec((tm, w), lambda i: (i, 0)),
                  pl.BlockSpec((tm, w), lambda i: (i, 3)),
                  pl.BlockSpec((tm, sw), lambda i: (i, 4 * w // sw)),
                  pl.BlockSpec((tm, sw), lambda i: (i, 0)),
                  pl.BlockSpec((tm, d), lambda i: (i, 0)),
                  pl.BlockSpec((None, 6, d), lambda i: (i // tpb, 0, 0)),
                  pl.BlockSpec((None, 6, d), lambda i: (nb, 0, 0)),
                  const(1, HEAD_DIM), const(1, sw), const(sw, sw), const(1, sw), const(d, d),
                  const(1, d), const(1, d)],
        out_specs=pl.BlockSpec((tm, d), lambda i: (i, 0)),
        out_shape=jax.ShapeDtypeStruct((t, d), F32),
        scratch_shapes=[pltpu.VMEM((tm, d), BF16)],
        compiler_params=_params("arbitrary"),
        name="even_merge",
    )(o, proj, proj, y, h, mod, mod, gn, d_skip, w_glu, b_glu, w_out, ln_g, ln_b)


def _ffn_body(x_ref, xh_ref, mb_ref, mc_ref, wv_ref, wg_ref, cwv_ref, cwg_ref, cbv_ref, cbg_ref, wd_ref,
              lng_ref, lnb_ref, o_ref, xb_ref, *up_refs, tm, tpb, ctx, period):
    i = pl.program_id(0)
    j = pl.program_id(1)
    rib = _rows_in_batch(i, tm, tpb)
    n = tm + 32

    @pl.when(j == 0)
    def _():
        xb_ref[16:tm + 16, :] = _modulate(x_ref[...], rib < ctx, mb_ref, mc_ref, 3, 4).astype(BF16)
        base = (i % tpb) * tm
        off = lax.broadcasted_iota(jnp.int32, (16, 1), 0)
        xh = xh_ref[...]
        next_ok = (base + tm != period) & (base + tm != ctx)
        prev_ok = (base != 0) & (base != ctx)
        nxt = _modulate(xh[0:16], base + tm + off < ctx, mb_ref, mc_ref, 3, 4)
        prv = _modulate(xh[16:32], base - 16 + off < ctx, mb_ref, mc_ref, 3, 4)
        xb_ref[tm + 16:n, :] = jnp.where(next_ok, nxt, 0.0).astype(BF16)
        xb_ref[0:16, :] = jnp.where(prev_ok, prv, 0.0).astype(BF16)
        o_ref[...] = jnp.zeros_like(o_ref)

    xb = xb_ref[...]
    edge_inside_tile = ctx % tm != 0

    def conv(u_ref, cw_ref, cb_ref, sl):
        cw = cw_ref[:, sl]
        up = u_ref[15:tm + 15, :]
        un = u_ref[17:tm + 17, :]
        if edge_inside_tile:
            up = jnp.where(rib != ctx, up, 0.0)
            un = jnp.where(rib != ctx - 1, un, 0.0)
        return up * cw[0:1] + u_ref[16:tm + 16, :] * cw[1:2] + un * cw[2:3] + cb_ref[:, sl]

    fc = wv_ref.shape[1]
    slabs = [slice(a, a + FFN_SLAB) for a in range(0, fc, FFN_SLAB)]
    for si, sl in enumerate(slabs):
        up_refs[2 * si][...] = _dot(xb, wv_ref[:, sl])
        up_refs[2 * si + 1][...] = _dot(xb, wg_ref[:, sl])
    for si, sl in enumerate(slabs):
        act = conv(up_refs[2 * si], cwv_ref, cbv_ref, sl) * _silu(conv(up_refs[2 * si + 1], cwg_ref, cbg_ref, sl))
        o_ref[...] += _dot(act.astype(BF16), wd_ref[sl, :])

    @pl.when(j == pl.num_programs(1) - 1)
    def _():
        g = jnp.where(rib < ctx, mc_ref[5:6, :], mb_ref[5:6, :])
        o_ref[...] = _layernorm(ALPHA * x_ref[...] + g * o_ref[...], lng_ref[...], lnb_ref[...])


def _ffn_call(h, mod, layer, w_up, conv_w, conv_b, w_down, ln_g, ln_b, nb, period, ctx):
    t, d = h.shape
    f = w_down.shape[1]
    tm = _pick(period, 544, 16)
    tpb = period // tm
    nt = t // tm
    fc = _pick(f, FFN_CHUNK, FFN_SLAB)
    nf = f // fc
    h3 = h.reshape(nt, tm, d)
    pad = jnp.zeros((1, 16, d), h.dtype)
    halo = jnp.concatenate([jnp.concatenate([h3[1:, :16], pad], 0),
                            jnp.concatenate([pad, h3[:-1, tm - 16:]], 0)], axis=1)
    body = functools.partial(_ffn_body, tm=tm, tpb=tpb, ctx=ctx, period=period)
    return pl.pallas_call(
        body,
        grid=(nt, nf),
        in_specs=[pl.BlockSpec((tm, d), lambda i, j: (i, 0)),
                  pl.BlockSpec((None, 32, d), lambda i, j: (i, 0, 0)),
                  pl.BlockSpec((None, 6, d), lambda i, j: (i // tpb, 0, 0)),
                  pl.BlockSpec((None, 6, d), lambda i, j: (nb, 0, 0)),
                  pl.BlockSpec((None, d, fc), lambda i, j: (layer, 0, j)),
                  pl.BlockSpec((None, d, fc), lambda i, j: (layer, 0, nf + j)),
                  pl.BlockSpec((None, 3, fc), lambda i, j: (layer, 0, j)),
                  pl.BlockSpec((None, 3, fc), lambda i, j: (layer, 0, nf + j)),
                  pl.BlockSpec((None, 1, fc), lambda i, j: (layer, 0, j)),
                  pl.BlockSpec((None, 1, fc), lambda i, j: (layer, 0, nf + j)),
                  pl.BlockSpec((None, fc, d), lambda i, j: (layer, j, 0)),
                  pl.BlockSpec((1, d), lambda i, j: (0, 0)),
                  pl.BlockSpec((1, d), lambda i, j: (0, 0))],
        out_specs=pl.BlockSpec((tm, d), lambda i, j: (i, 0)),
        out_shape=jax.ShapeDtypeStruct((t, d), F32),
        scratch_shapes=([pltpu.VMEM((tm + 32, d), BF16)]
                        + [pltpu.VMEM((tm + 32, FFN_SLAB), F32) for _ in range(2 * fc // FFN_SLAB)]),
        compiler_params=_params("arbitrary", "arbitrary"),
        name="conv_ffn",
    )(h, halo, mod, mod, w_up, w_up, conv_w, conv_w, conv_b, conv_b, w_down, ln_g, ln_b)


def _qkv_body(x_ref, mb_ref, mc_ref, w_ref, cos_ref, sin_ref, qn_ref, kn_ref, o_ref, xb_ref,
              *, tm, tpb, ctx, nq_tiles):
    i = pl.program_id(0)
    j = pl.program_id(1)

    @pl.when(j == 0)
    def _():
        rib = _rows_in_batch(i, tm, tpb)
        xb_ref[...] = _modulate(x_ref[...], rib < ctx, mb_ref, mc_ref, 0, 1).astype(BF16)

    cos = cos_ref[...]
    sin = sin_ref[...]
    lane = lax.broadcasted_iota(jnp.int32, (1, HEAD_DIM), 1)
    first = (lane % 64) < 32
    acc = _dot(xb_ref[...], w_ref[...])
    heads = acc.shape[1] // HEAD_DIM

    def normrope(xh, wn, scale):
        xn = xh * lax.rsqrt(jnp.mean(xh * xh, -1, keepdims=True) + RMS_EPS) * wn
        partner = jnp.where(first, pltpu.roll(xn, HEAD_DIM - 32, 1), pltpu.roll(xn, 32, 1))
        return (xn * cos + partner * sin) * scale

    @pl.when(j < nq_tiles)
    def _():
        for hh in range(heads):
            sl = slice(hh * HEAD_DIM, (hh + 1) * HEAD_DIM)
            o_ref[:, sl] = normrope(acc[:, sl], qn_ref[...], Q_SCALE).astype(BF16)

    @pl.when(j == nq_tiles)
    def _():
        for hh in range(heads):
            sl = slice(hh * HEAD_DIM, (hh + 1) * HEAD_DIM)
            if hh < ATT_KV_HEADS:
                o_ref[:, sl] = normrope(acc[:, sl], kn_ref[...], 1.0).astype(BF16)
            else:
                o_ref[:, sl] = acc[:, sl].astype(BF16)


def _qkv_call(h, mod, w_in, cos, sin, qn, kn, nb, period, ctx):
    t, d = h.shape
    n = w_in.shape[1]
    tn = 2 * ATT_KV_HEADS * HEAD_DIM
    nq_tiles = (n - tn) // tn
    tm = _pick(period, PROJ_ROW_TILE, 16)
    tpb = period // tm
    body = functools.partial(_qkv_body, tm=tm, tpb=tpb, ctx=ctx, nq_tiles=nq_tiles)
    return pl.pallas_call(
        body,
        grid=(t // tm, n // tn),
        in_specs=[pl.BlockSpec((tm, d), lambda i, j: (i, 0)),
                  pl.BlockSpec((None, 6, d), lambda i, j: (i // tpb, 0, 0)),
                  pl.BlockSpec((None, 6, d), lambda i, j: (nb, 0, 0)),
                  pl.BlockSpec((d, tn), lambda i, j: (0, j)),
                  pl.BlockSpec((tm, HEAD_DIM), lambda i, j: (i % tpb, 0)),
                  pl.BlockSpec((tm, HEAD_DIM), lambda i, j: (i % tpb, 0)),
                  pl.BlockSpec((1, HEAD_DIM), lambda i, j: (0, 0)),
                  pl.BlockSpec((1, HEAD_DIM), lambda i, j: (0, 0))],
        out_specs=pl.BlockSpec((tm, tn), lambda i, j: (i, j)),
        out_shape=jax.ShapeDtypeStruct((t, n), BF16),
        scratch_shapes=[pltpu.VMEM((tm, d), BF16)],
        compiler_params=_params("arbitrary", "arbitrary"),
        name="odd_qkv",
    )(h, mod, mod, w_in, cos, sin, qn, kn)


def _rope_tables(seq, ctx):
    rows = seq // GRID_W
    row = jnp.repeat(jnp.arange(rows, dtype=F32), GRID_W)
    col = jnp.tile(jnp.arange(GRID_W, dtype=F32), rows)
    half = HEAD_DIM // 4
    inv = ROPE_THETA ** (-jnp.arange(half, dtype=F32) / half)
    ar = row[:, None] * inv
    ac = col[:, None] * inv
    cos = jnp.concatenate([jnp.cos(ar), jnp.cos(ar), jnp.cos(ac), jnp.cos(ac)], -1)
    sin = jnp.concatenate([-jnp.sin(ar), jnp.sin(ar), -jnp.sin(ac), jnp.sin(ac)], -1)
    cos = jnp.concatenate([jnp.ones((ctx, HEAD_DIM), F32), cos], 0)
    sin = jnp.concatenate([jnp.zeros((ctx, HEAD_DIM), F32), sin], 0)
    return cos, sin


def _attn_body(q_ref, k_ref, v_ref, o_ref, *, tq, key_blocks):
    hd = HEAD_DIM
    q = jnp.concatenate([q_ref[:, g * hd:(g + 1) * hd] for g in range(ATT_GROUP)], axis=0)
    rows = ATT_GROUP * tq
    m = jnp.full((rows, 1), -1e30, F32)
    l = jnp.zeros((rows, 1), F32)
    acc = jnp.zeros((rows, hd), F32)
    s_next = _dot_nt(q, k_ref[key_blocks[0][0]:key_blocks[0][1], :])
    for n, (k0, k1) in enumerate(key_blocks):
        s = s_next
        if n + 1 < len(key_blocks):
            s_next = _dot_nt(q, k_ref[key_blocks[n + 1][0]:key_blocks[n + 1][1], :])
        m_new = jnp.maximum(m, jnp.max(s, -1, keepdims=True))
        a = jnp.exp2(m - m_new)
        p = jnp.exp2(s - m_new)
        l = a * l + jnp.sum(p, -1, keepdims=True)
        acc = a * acc + _dot(p.astype(BF16), v_ref[k0:k1, :])
        m = m_new
    out = acc / l
    for g in range(ATT_GROUP):
        o_ref[:, g * hd:(g + 1) * hd] = out[g * tq:(g + 1) * tq].astype(o_ref.dtype)


def _key_blocks(period, count, align):
    tiles = period // align
    assert tiles * align == period and tiles >= count
    sizes = [tiles // count + (1 if n >= count - tiles % count else 0) for n in range(count)]
    edges = [0]
    for sz in sizes:
        edges.append(edges[-1] + sz * align)
    return tuple(zip(edges[:-1], edges[1:]))


def _attn_call(qkv, nb, seq, ctx):
    period = seq + ctx
    tq = _pick(math.gcd(seq, ctx), ATT_Q_TILE, 16)
    gw = ATT_GROUP * HEAD_DIM
    qpb = seq // tq
    rpb = period // tq
    skip = ctx // tq
    kcol = ATT_HEADS
    vcol = ATT_HEADS + ATT_KV_HEADS
    align = math.gcd(period, MXU_TILE)
    blocks = _key_blocks(period, min(ATT_KEY_BLOCKS, period // align), align)
    body = functools.partial(_attn_body, tq=tq, key_blocks=blocks)
    return pl.pallas_call(
        body,
        grid=(nb, ATT_KV_HEADS, qpb),
        in_specs=[pl.BlockSpec((tq, gw), lambda b, kv, qi: (b * rpb + skip + qi, kv)),
                  pl.BlockSpec((period, HEAD_DIM), lambda b, kv, qi: (b, kcol + kv)),
                  pl.BlockSpec((period, HEAD_DIM), lambda b, kv, qi: (b, vcol + kv))],
        out_specs=pl.BlockSpec((tq, gw), lambda b, kv, qi: (b * qpb + qi, kv)),
        out_shape=jax.ShapeDtypeStruct((nb * seq, ATT_HEADS * HEAD_DIM), BF16),
        compiler_params=_params("arbitrary", "arbitrary", "arbitrary"),
        name="gqa_attention",
    )(qkv, qkv, qkv)


def _outproj_body(a_ref, h_ref, mb_ref, w_ref, lng_ref, lnb_ref, o_ref):
    half = a_ref.shape[0] // 2
    parts = (slice(0, half), slice(half, 2 * half))
    mixes = [_dot(a_ref[rows, :], w_ref[...]) for rows in parts]
    for rows, mix in zip(parts, mixes):
        o_ref[rows, :] = _layernorm(ALPHA * h_ref[rows, :] + mb_ref[2:3, :] * mix, lng_ref[...], lnb_ref[...])


def _outproj_call(att, h, mod, w_out, ln_g, ln_b, nb, seq, ctx):
    d = h.shape[1]
    period = seq + ctx
    tm = _pick(math.gcd(seq, ctx), 256, 16)
    qpb = seq // tm
    rpb = period // tm
    skip = ctx // tm
    return pl.pallas_call(
        _outproj_body,
        grid=(nb * qpb,),
        in_specs=[pl.BlockSpec((tm, d), lambda i: (i, 0)),
                  pl.BlockSpec((tm, d), lambda i: ((i // qpb) * rpb + skip + i % qpb, 0)),
                  pl.BlockSpec((None, 6, d), lambda i: (i // qpb, 0, 0)),
                  pl.BlockSpec((d, d), lambda i: (0, 0)),
                  pl.BlockSpec((1, d), lambda i: (0, 0)),
                  pl.BlockSpec((1, d), lambda i: (0, 0))],
        out_specs=pl.BlockSpec((tm, d), lambda i: (i, 0)),
        out_shape=jax.ShapeDtypeStruct((nb * seq, d), F32),
        compiler_params=_params("arbitrary"),
        name="odd_outproj",
    )(att, h, mod, w_out, ln_g, ln_b)


def kernel(x, c, ctx, c_ctx, mod_w, mod_b, ln1_g, ln1_b, ln2_g, ln2_b, ffn_w_up, ffn_conv_w, ffn_conv_b, ffn_w_down, e_w_in, e_conv_qkv, e_a_log, e_dt_bias, e_gdn_norm, e_lam_re, e_lam_im, e_log_dt, e_b_re, e_b_im, e_c_re, e_c_im, e_d_skip, e_w_glu, e_b_glu, e_w_out, o_w_in, o_q_norm, o_k_norm, o_w_out):
    nb, seq, d = x.shape
    nctx = ctx.shape[1]
    period = nctx + seq
    t = nb * period
    w = GDN_WIDTH
    assert mod_w.shape[0] == DEPTH == 2 and nb < 8

    h = jnp.concatenate([ctx, x], axis=1).reshape(t, d)
    cs = jnp.zeros((8, d), F32).at[:nb].set(c).at[nb].set(c_ctx)
    mod = _mod_call(cs, mod_w, mod_b).reshape(DEPTH, 8, 6, d)
    row = lambda a: a.reshape(1, -1)

    w_in = e_w_in[0]
    gates_at = 4 * w
    w_main = jnp.concatenate([w_in[:, :gates_at], w_in[:, gates_at + 4 * GDN_HEADS:]], axis=1).astype(BF16)
    w_ab = jnp.pad(w_in[:, gates_at:gates_at + 4 * GDN_HEADS], ((0, 0), (0, 128 - 4 * GDN_HEADS))).astype(BF16)
    proj, ab = _inproj_even_call(h, mod[0], w_main, w_ab, nb, period, nctx)
    pad_row = lambda a: jnp.pad(a.reshape(1, -1), ((0, 0), (0, 128 - 2 * GDN_HEADS)))
    qkvn, kt3, gh, gct3 = _gdn_prep_call(proj, ab, e_conv_qkv[0], pad_row(e_a_log[0]), pad_row(e_dt_bias[0]),
                                         period, nctx)
    o = _gdn_scan_call(qkvn, kt3, gh, gct3, nb, period, nctx)

    ops = _s5_tables_call(e_lam_re[0], e_lam_im[0], e_log_dt[0], e_b_re[0], e_b_im[0], e_c_re[0], e_c_im[0])
    u_slabs = proj[:, 4 * w:].reshape(t // S5_CHUNK, S5_CHUNK, d - w).transpose(1, 0, 2)
    y = _s5_call(u_slabs, ops, nb, period, nctx).transpose(1, 0, 2).reshape(t, d - w)

    h = _merge_call(o, proj, y, h, mod[0], row(e_gdn_norm[0]), row(e_d_skip[0]), e_w_glu[0].astype(BF16),
                    row(e_b_glu[0]), e_w_out[0].astype(BF16), row(ln1_g[0]), row(ln1_b[0]), nb, period, nctx)
    w_up, w_down = ffn_w_up.astype(BF16), ffn_w_down.astype(BF16)
    conv_b = ffn_conv_b[:, None, :]
    h = _ffn_call(h, mod[0], 0, w_up, ffn_conv_w, conv_b, w_down, row(ln2_g[0]), row(ln2_b[0]), nb, period, nctx)

    cos, sin = _rope_tables(seq, nctx)
    qkv = _qkv_call(h, mod[1], o_w_in[0].astype(BF16), cos, sin, row(o_q_norm[0]), row(o_k_norm[0]),
                    nb, period, nctx)
    att = _attn_call(qkv, nb, seq, nctx)
    hl = _outproj_call(att, h, mod[1], o_w_out[0].astype(BF16), row(ln1_g[1]), row(ln1_b[1]), nb, seq, nctx)
    out = _ffn_call(hl, mod[1], 1, w_up, ffn_conv_w, conv_b, w_down, row(ln2_g[1]), row(ln2_b[1]), nb, seq, 0)
    return out.reshape(nb, seq, d)
```

```python
import functools
import math

import jax
import jax.numpy as jnp
from jax import lax
from jax.experimental import pallas as pl
from jax.experimental.pallas import tpu as pltpu

F32 = jnp.float32
BF16 = jnp.bfloat16

DEPTH = 2
GDN_HEADS = 8
HEAD_DIM = 128
GDN_WIDTH = GDN_HEADS * HEAD_DIM
GDN_CHUNK = 128
GDN_INTRA_BATCH = 17
GDN_SOLVE_BASE = 8
S5_GROUP = 16
S5_STATE = 64
S5_CHUNK = 16
S5_TILE = S5_CHUNK * S5_GROUP
S5_GROUPS_PER_STEP = 4
S5_TABLE_GROUPS = 8
ATT_HEADS = 16
ATT_KV_HEADS = 4
ATT_GROUP = ATT_HEADS // ATT_KV_HEADS
ATT_SCALE = HEAD_DIM ** -0.5
Q_SCALE = ATT_SCALE * math.log2(math.e)
ATT_Q_TILE = 256
ATT_KEY_BLOCKS = 6
MXU_TILE = 256
GRID_W = 64
ROPE_THETA = 10000.0
PROJ_ROW_TILE = 1088
FFN_CHUNK = 512
FFN_SLAB = 256
ALPHA = (2 * DEPTH) ** 0.25
LN_EPS = 1e-6
RMS_EPS = 1e-6
L2_EPS = 1e-6
VMEM_LIMIT_BYTES = 56 * 1024 * 1024


def _pick(n, target, mult):
    best = None
    for d in range(mult, min(n, target) + 1, mult):
        if n % d == 0:
            best = d
    assert best is not None, (n, target, mult)
    return best


def _params(*sem):
    return pltpu.CompilerParams(dimension_semantics=sem, vmem_limit_bytes=VMEM_LIMIT_BYTES)


def _dot(a, b):
    return jnp.dot(a, b, preferred_element_type=F32)


def _dot_nt(a, b):
    return lax.dot_general(a, b, (((1,), (1,)), ((), ())), preferred_element_type=F32)


def _split3(a):
    p0 = a.astype(BF16)
    r = a - p0.astype(F32)
    p1 = r.astype(BF16)
    p2 = (r - p1.astype(F32)).astype(BF16)
    return p0, p1, p2


def _rows_in_batch(i, tm, tiles_per_batch):
    return (i % tiles_per_batch) * tm + lax.broadcasted_iota(jnp.int32, (tm, 1), 0)


def _modulate(x, is_ctx, mb_ref, mc_ref, shift_i, scale_i):
    scale = jnp.where(is_ctx, mc_ref[scale_i:scale_i + 1, :], mb_ref[scale_i:scale_i + 1, :])
    shift = jnp.where(is_ctx, mc_ref[shift_i:shift_i + 1, :], mb_ref[shift_i:shift_i + 1, :])
    return x * (1.0 + scale) + shift


def _layernorm(r, g, b):
    xc = r - jnp.mean(r, -1, keepdims=True)
    var = jnp.mean(xc * xc, -1, keepdims=True)
    return xc * lax.rsqrt(var + LN_EPS) * g + b


def _silu(x):
    return x * jax.nn.sigmoid(x)


def _seq_edges(rib, ctx, period):
    has_prev = (rib != 0) & (rib != ctx)
    has_next = (rib != ctx - 1) & (rib != period - 1)
    return has_prev, has_next


def _mod_body(c_ref, w_ref, b_ref, o_ref):
    s = _silu(c_ref[...])
    o_ref[...] = _dot(s.astype(BF16), w_ref[...].astype(BF16)) + b_ref[...]


def _mod_call(cs, mod_w, mod_b):
    depth, d, n = mod_w.shape
    tn = _pick(n, 1024, 128)
    return pl.pallas_call(
        _mod_body,
        grid=(depth, n // tn),
        in_specs=[pl.BlockSpec((8, d), lambda l, j: (0, 0)),
                  pl.BlockSpec((None, d, tn), lambda l, j: (l, 0, j)),
                  pl.BlockSpec((None, 1, tn), lambda l, j: (l, 0, j))],
        out_specs=pl.BlockSpec((None, 8, tn), lambda l, j: (l, 0, j)),
        out_shape=jax.ShapeDtypeStruct((depth, 8, n), F32),
        compiler_params=_params("arbitrary", "arbitrary"),
        name="adaln_mod",
    )(cs, mod_w, mod_b.reshape(depth, 1, n))


def _inproj_even_body(x_ref, mb_ref, mc_ref, w_ref, wab_ref, o_ref, ab_ref, xb_ref, *, tm, tpb, ctx):
    i = pl.program_id(0)
    rib = _rows_in_batch(i, tm, tpb)
    half = tm // 2
    for rows in (slice(0, half), slice(half, tm)):
        xb_ref[rows, :] = _modulate(x_ref[rows, :], rib[rows] < ctx, mb_ref, mc_ref, 0, 1).astype(BF16)
        o_ref[rows, :] = _dot(xb_ref[rows, :], w_ref[...]).astype(o_ref.dtype)

    @pl.when(pl.program_id(1) == 0)
    def _():
        ab_ref[...] = _dot(xb_ref[...], wab_ref[...])


def _inproj_even_call(h, mod, w_main, w_ab, nb, period, ctx):
    t, d = h.shape
    n = w_main.shape[1]
    tm = _pick(period, PROJ_ROW_TILE, 16)
    tn = _pick(n, 1280, 256)
    tpb = period // tm
    body = functools.partial(_inproj_even_body, tm=tm, tpb=tpb, ctx=ctx)
    return pl.pallas_call(
        body,
        grid=(t // tm, n // tn),
        in_specs=[pl.BlockSpec((tm, d), lambda i, j: (i, 0)),
                  pl.BlockSpec((None, 6, d), lambda i, j: (i // tpb, 0, 0)),
                  pl.BlockSpec((None, 6, d), lambda i, j: (nb, 0, 0)),
                  pl.BlockSpec((d, tn), lambda i, j: (0, j)),
                  pl.BlockSpec((d, 128), lambda i, j: (0, 0))],
        out_specs=[pl.BlockSpec((tm, tn), lambda i, j: (i, j)),
                   pl.BlockSpec((tm, 128), lambda i, j: (i, 0))],
        out_shape=[jax.ShapeDtypeStruct((t, n), BF16), jax.ShapeDtypeStruct((t, 128), F32)],
        scratch_shapes=[pltpu.VMEM((tm, d), BF16)],
        compiler_params=_params("arbitrary", "arbitrary"),
        name="even_inproj",
    )(h, mod, mod, w_main, w_ab)


def _gdn_prep_body(x_ref, xn_ref, xp_ref, cw_ref, ab_ref, alog_ref, dtb_ref, tril_ref, triu_ref, e_ref,
                   qkv_ref, kt_ref, gh_ref, gct_ref, ext_ref, *, tr, tpb, ctx, period):
    i = pl.program_id(0)
    sec = pl.program_id(1)
    rib = _rows_in_batch(i, tr, tpb)
    has_prev, has_next = _seq_edges(rib, ctx, period)
    ext_ref[0:16, :] = xp_ref[...].astype(F32)
    ext_ref[16:tr + 16, :] = x_ref[...].astype(F32)
    ext_ref[tr + 16:tr + 32, :] = xn_ref[...].astype(F32)
    cw = cw_ref[...]
    y = (jnp.where(has_prev, ext_ref[15:tr + 15, :], 0.0) * cw[0:1] + ext_ref[16:tr + 16, :] * cw[1:2]
         + jnp.where(has_next, ext_ref[17:tr + 17, :], 0.0) * cw[2:3])
    y = _silu(y)

    def l2n(scale):
        parts = []
        for hh in range(GDN_HEADS):
            yh = y[:, hh * HEAD_DIM:(hh + 1) * HEAD_DIM]
            parts.append(yh * (lax.rsqrt(jnp.sum(yh * yh, -1, keepdims=True) + L2_EPS) * scale))
        return jnp.concatenate(parts, axis=1)

    @pl.when(sec == 0)
    def _():
        qkv_ref[...] = l2n(HEAD_DIM ** -0.5).astype(BF16)
        ab = ab_ref[...]
        lane = lax.broadcasted_iota(jnp.int32, (1, 128), 1)
        xg = ab + dtb_ref[...]
        softplus = jnp.maximum(xg, 0.0) + jnp.log1p(jnp.exp(-jnp.abs(xg)))
        g = -jnp.exp(alog_ref[...]) * softplus
        g0, g1, g2 = _split3(g)
        tril = tril_ref[...]
        triu = triu_ref[...]
        pre = _dot(tril, g0) + _dot(tril, g1) + _dot(tril, g2)
        suf = _dot(triu, g0) + _dot(triu, g1) + _dot(triu, g2)
        gc = jnp.where(lane < GDN_HEADS, pre, suf)
        vals = jnp.where(lane < 2 * GDN_HEADS, gc, jax.nn.sigmoid(ab))
        v0, v1, v2 = _split3(vals)
        e = e_ref[...]
        gh_ref[...] = _dot(v0, e) + _dot(v1, e) + _dot(v2, e)
        gct = gc.T
        for c in range(tr // GDN_CHUNK):
            gct_ref[c] = gct[:2 * GDN_HEADS, c * GDN_CHUNK:(c + 1) * GDN_CHUNK]

    @pl.when(sec == 1)
    def _():
        kn = l2n(1.0)
        qkv_ref[...] = kn.astype(BF16)
        knt = kn.T
        for c in range(tr // GDN_CHUNK):
            kt_ref[c] = knt[:, c * GDN_CHUNK:(c + 1) * GDN_CHUNK].astype(BF16)

    @pl.when(sec == 2)
    def _():
        qkv_ref[...] = y.astype(BF16)


def _gdn_prep_call(proj, ab, conv_w, alog_row, dtb_row, period, ctx):
    t = proj.shape[0]
    w = GDN_WIDTH
    tr = _pick(period, 256, GDN_CHUNK)
    tpb = period // tr
    hb = tr // 16
    nh = t // 16
    nck = tr // GDN_CHUNK
    r = jnp.arange(tr)
    same = (r[:, None] // GDN_CHUNK) == (r[None, :] // GDN_CHUNK)
    tril = (same & (r[:, None] >= r[None, :])).astype(BF16)
    triu = (same & (r[:, None] <= r[None, :])).astype(BF16)
    src = jnp.arange(128)[:, None]
    dst = jnp.arange(w)[None, :]
    expand = ((dst % HEAD_DIM < 4) & (src == (dst % HEAD_DIM) * GDN_HEADS + dst // HEAD_DIM)).astype(BF16)
    body = functools.partial(_gdn_prep_body, tr=tr, tpb=tpb, ctx=ctx, period=period)
    return pl.pallas_call(
        body,
        grid=(t // tr, 3),
        in_specs=[pl.BlockSpec((tr, w), lambda i, s: (i, s)),
                  pl.BlockSpec((16, w), lambda i, s: (jnp.minimum((i + 1) * hb, nh - 1), s)),
                  pl.BlockSpec((16, w), lambda i, s: (jnp.maximum(i * hb - 1, 0), s)),
                  pl.BlockSpec((3, w), lambda i, s: (0, s)),
                  pl.BlockSpec((tr, 128), lambda i, s: (i, 0)),
                  pl.BlockSpec((1, 128), lambda i, s: (0, 0)),
                  pl.BlockSpec((1, 128), lambda i, s: (0, 0)),
                  pl.BlockSpec((tr, tr), lambda i, s: (0, 0)),
                  pl.BlockSpec((tr, tr), lambda i, s: (0, 0)),
                  pl.BlockSpec((128, w), lambda i, s: (0, 0))],
        out_specs=[pl.BlockSpec((tr, w), lambda i, s: (i, s)),
                   pl.BlockSpec((nck, w, GDN_CHUNK), lambda i, s: (i, 0, 0)),
                   pl.BlockSpec((tr, w), lambda i, s: (i, 0)),
                   pl.BlockSpec((nck, 2 * GDN_HEADS, GDN_CHUNK), lambda i, s: (i, 0, 0))],
        out_shape=[jax.ShapeDtypeStruct((t, 3 * w), BF16),
                   jax.ShapeDtypeStruct((t // GDN_CHUNK, w, GDN_CHUNK), BF16),
                   jax.ShapeDtypeStruct((t, w), F32),
                   jax.ShapeDtypeStruct((t // GDN_CHUNK, 2 * GDN_HEADS, GDN_CHUNK), F32)],
        scratch_shapes=[pltpu.VMEM((tr + 32, w), F32)],
        compiler_params=_params("arbitrary", "arbitrary"),
        name="gdn_prep",
    )(proj, proj, proj, conv_w, ab, alog_row, dtb_row, tril, triu, expand)


def _gdn_scan_body(q_ref, k_ref, v_ref, kt_ref, gh_ref, gct_ref, o_ref,
                   aqf, aqb, bsf, bsb, egf, egb, *, nch, nctx, batch):
    head = pl.program_id(1)
    cs = GDN_CHUNK
    ii = lax.broadcasted_iota(jnp.int32, (cs, cs), 0)
    jj = lax.broadcasted_iota(jnp.int32, (cs, cs), 1)
    masks = ((ii >= jj, ii > jj), (ii <= jj, ii < jj))
    same_block = {}
    width = GDN_SOLVE_BASE
    while width <= cs:
        shift = width.bit_length() - 1
        same_block[width] = jnp.right_shift(ii, shift) == jnp.right_shift(jj, shift)
        width *= 2
    hd = HEAD_DIM
    aqs, bss, egs = (aqf, aqb), (bsf, bsb), (egf, egb)

    def intra(first, count):
        chunks = [first + j for j in range(count)]
        r0s = [pl.multiple_of(c * cs, cs) for c in chunks]
        qs = [q_ref[pl.ds(r0, cs), :] for r0 in r0s]
        ks = [k_ref[pl.ds(r0, cs), :] for r0 in r0s]
        vs = [v_ref[pl.ds(r0, cs), :] for r0 in r0s]
        kts = [kt_ref[c] for c in chunks]
        raws = [_dot(jnp.concatenate([q, k], axis=0), kt) for q, k, kt in zip(qs, ks, kts)]
        raw_q = [r[:cs] for r in raws]
        raw_k = [r[cs:] for r in raws]
        pms, xs, atts, kps, qes, gends = [], [], [], [], [], []
        for j, c in enumerate(chunks):
            g4 = gh_ref[pl.ds(r0s[j], cs), :]
            qf, kf, vf = qs[j].astype(F32), ks[j].astype(F32), vs[j].astype(F32)
            ktf = kts[j].astype(F32)
            for d in (0, 1):
                incl, strict = masks[d]
                gcol = g4[:, d:d + 1]
                bcol = g4[:, 2 + d:3 + d]
                grow = gct_ref[c, pl.ds(d * GDN_HEADS + head, 1), :]
                decay = jnp.where(incl, jnp.exp(jnp.where(incl, gcol - grow, 0.0)), 0.0)
                eg = jnp.exp(gcol)
                gend = grow[:, cs - 1:cs] if d == 0 else grow[:, 0:1]
                pms.append(jnp.where(strict, -(raw_k[j] * bcol) * decay, 0.0))
                xs.append(jnp.concatenate([vf * bcol, kf * (bcol * eg)], axis=1))
                atts.append((raw_q[j] * decay).astype(BF16))
                kps.append((ktf * jnp.exp(gend - grow)).astype(BF16))
                qes.append(qf * eg)
                gends.append(gend)
        bfs = lambda vals: [a.astype(BF16) for a in vals]
        pds = bfs([jnp.where(same_block[GDN_SOLVE_BASE], pm, 0.0) for pm in pms])
        ns = [pd.astype(F32) for pd in pds]
        pw = pds
        span = 1
        while 2 * span < GDN_SOLVE_BASE:
            sq = [_dot(p, p) for p in pw]
            pw = bfs(sq)
            ns = [n + s + _dot(p, n.astype(BF16)) for n, s, p in zip(ns, sq, pw)]
            span *= 2
        width = GDN_SOLVE_BASE
        while width < cs:
            couple = same_block[2 * width] & ~same_block[width]
            qs_ = [jnp.where(couple, pm, 0.0) for pm in pms]
            nbs = bfs(ns)
            bs_ = [q + _dot(nb, q.astype(BF16)) for q, nb in zip(qs_, nbs)]
            ns = [n + b + _dot(b.astype(BF16), nb) for n, b, nb in zip(ns, bs_, nbs)]
            width *= 2
        xs = [x + _dot(n.astype(BF16), x.astype(BF16)) for n, x in zip(ns, xs)]
        xbs = [x.astype(BF16) for x in xs]
        both = [_dot(jnp.concatenate([att, kp], axis=0), xb) for att, kp, xb in zip(atts, kps, xbs)]
        aws = [r[:cs] for r in both]
        kxs = [r[cs:] for r in both]
        for j, c in enumerate(chunks):
            for d in (0, 1):
                n = 2 * j + d
                aqs[d][c, 0:hd, :] = (-kxs[n][:, hd:]).astype(BF16)
                aqs[d][c, hd:hd + cs, :] = (qes[n] - aws[n][:, hd:]).astype(BF16)
                bss[d][c] = kxs[n][:, :hd]
                egs[d][c] = jnp.broadcast_to(jnp.exp(gends[n]), (8, hd))
            o_ref[pl.ds(r0s[j], cs), :] = aws[2 * j][:, :hd] + aws[2 * j + 1][:, :hd]

    def inter(i, carry):
        cb = jnp.where(i < nctx, nctx - 1 - i, nch - 1 - (i - nctx))
        new = []
        for d, c in ((0, i), (1, cb)):
            s = carry[d]
            r0 = pl.multiple_of(c * cs, cs)
            r = _dot(aqs[d][c], s.astype(BF16))
            o_ref[pl.ds(r0, cs), :] += r[hd:]
            new.append(s * egs[d][c][0:1, :] + r[:hd] + bss[d][c])
        return tuple(new)

    full = nch // batch

    def intra_step(it, carry):
        intra(it * batch, batch)
        return carry

    lax.fori_loop(0, full, intra_step, 0)
    if nch > full * batch:
        intra(full * batch, nch - full * batch)
    zero = jnp.zeros((hd, hd), F32)
    lax.fori_loop(0, nch, inter, (zero, zero), unroll=2)


def _gdn_scan_call(qkvn, kt3, gh, gct3, nb, period, ctx):
    t = qkvn.shape[0]
    nch = period // GDN_CHUNK
    nctx = ctx // GDN_CHUNK
    hd = HEAD_DIM
    body = functools.partial(_gdn_scan_body, nch=nch, nctx=nctx, batch=min(nch, GDN_INTRA_BATCH))
    dirs2 = lambda shape, dt: [pltpu.VMEM(shape, dt), pltpu.VMEM(shape, dt)]
    return pl.pallas_call(
        body,
        grid=(nb, GDN_HEADS),
        in_specs=[pl.BlockSpec((period, hd), lambda b, h: (b, h)),
                  pl.BlockSpec((period, hd), lambda b, h: (b, GDN_HEADS + h)),
                  pl.BlockSpec((period, hd), lambda b, h: (b, 2 * GDN_HEADS + h)),
                  pl.BlockSpec((nch, hd, GDN_CHUNK), lambda b, h: (b, h, 0)),
                  pl.BlockSpec((period, hd), lambda b, h: (b, h)),
                  pl.BlockSpec((nch, 2 * GDN_HEADS, GDN_CHUNK), lambda b, h: (b, 0, 0))],
        out_specs=pl.BlockSpec((period, hd), lambda b, h: (b, h)),
        out_shape=jax.ShapeDtypeStruct((t, GDN_WIDTH), F32),
        scratch_shapes=(dirs2((nch, hd + GDN_CHUNK, hd), BF16) + dirs2((nch, hd, hd), F32)
                        + dirs2((nch, 8, hd), F32)),
        compiler_params=_params("arbitrary", "arbitrary"),
        name="gdn_scan",
    )(qkvn, qkvn, qkvn, kt3, gh, gct3)


def _s5_tables_body(*refs):
    lax.fori_loop(0, refs[0].shape[1], functools.partial(_s5_tables_group, refs), 0)


def _s5_tables_group(refs, q, carry):
    (lr_ref, li_ref, ldt_ref, btr_ref, bti_ref, cr_ref, ci_ref,
     wir_ref, wii_ref, wor_ref, woi_ref, ktf_ref, ktb_ref, lam_ref) = refs
    n, cg = S5_CHUNK, S5_GROUP
    kk = lax.broadcasted_iota(jnp.int32, (24, 1), 0).astype(F32)
    lane = lax.broadcasted_iota(jnp.int32, (1, S5_TILE), 1)

    def dot3_nt(a, b):
        a0 = a.astype(BF16)
        a1 = (a - a0.astype(F32)).astype(BF16)
        b0 = b.astype(BF16)
        b1 = (b - b0.astype(F32)).astype(BF16)
        return _dot_nt(a0, b0) + _dot_nt(a0, b1) + _dot_nt(a1, b0)

    win_r, win_i, wout_r, wout_i, lam_r, lam_i = [], [], [], [], [], []
    for d in (0, 1):
        lr = lr_ref[d, q]
        li = li_ref[d, q]
        dt = jnp.exp(ldt_ref[d, q])
        mag = jnp.exp(kk * (lr * dt))
        ang = kk * (li * dt)
        er = mag * jnp.cos(ang)
        ei = mag * jnp.sin(ang)
        nr = er[1:2] - 1.0
        ni = ei[1:2]
        den = lr * lr + li * li
        cfr = (nr * lr + ni * li) / den
        cfi = (ni * lr - nr * li) / den
        btr = btr_ref[d, q]
        bti = bti_ref[d, q]
        bbr = btr * cfr - bti * cfi
        bbi = btr * cfi + bti * cfr
        cr = cr_ref[d, q]
        ci = ci_ref[d, q]
        a_r, a_i, cm_r, cm_i, c0_r, c0_i = [], [], [], [], [], []
        for k in range(n):
            e_r, e_i = er[k:k + 1], ei[k:k + 1]
            a_r.append(bbr * e_r - bbi * e_i)
            a_i.append(bbr * e_i + bbi * e_r)
            c0_r.append(cr * e_r - ci * e_i)
            c0_i.append(cr * e_i + ci * e_r)
            f_r, f_i = er[k + 1:k + 2], ei[k + 1:k + 2]
            cm_r.append(cr * f_r - ci * f_i)
            cm_i.append(-(cr * f_i + ci * f_r))
        fwd = d == 0
        down = list(range(n - 1, -1, -1))
        up = list(range(n))
        pick = lambda parts, order: jnp.concatenate([parts[k] for k in order], axis=0)
        win_r.append(pick(a_r, down if fwd else up))
        win_i.append(pick(a_i, down if fwd else up))
        wout_r.append(pick(cm_r, up if fwd else down))
        wout_i.append(pick(cm_i, up if fwd else down))
        lam_r.append(er[n:n + 1])
        lam_i.append(ei[n:n + 1])
        order = up if fwd else down
        strip = dot3_nt(bbr, pick(c0_r, order)) - dot3_nt(bbi, pick(c0_i, order))
        for s in range(n):
            rows = slice(s * cg, (s + 1) * cg)
            if fwd:
                blk = strip if s == 0 else jnp.where(lane >= s * cg, pltpu.roll(strip, s * cg, 1), 0.0)
                ktf_ref[q, rows, :] = blk.astype(BF16)
            else:
                sh = (n - 1 - s) * cg
                blk = strip if sh == 0 else jnp.where(lane < (s + 1) * cg, pltpu.roll(strip, S5_TILE - sh, 1), 0.0)
                ktb_ref[q, rows, :] = blk.astype(BF16)
    wir_ref[q] = jnp.concatenate(win_r, axis=1).astype(BF16)
    wii_ref[q] = jnp.concatenate(win_i, axis=1).astype(BF16)
    wor_ref[q] = jnp.concatenate(wout_r, axis=1).T.astype(BF16)
    woi_ref[q] = jnp.concatenate(wout_i, axis=1).T.astype(BF16)
    lam_ref[q] = jnp.concatenate([jnp.concatenate(lam_r, axis=1), jnp.concatenate(lam_i, axis=1)], axis=0)
    return carry


def _s5_tables_call(lam_re, lam_im, log_dt, b_re, b_im, c_re, c_im):
    _, g, p = lam_re.shape
    cg = S5_GROUP
    sw = 2 * p
    row4 = lambda a: a.reshape(2, g, 1, p)
    per = _pick(g, S5_TABLE_GROUPS, 1)
    ispec = lambda r: pl.BlockSpec((2, per, r, p), lambda gi: (0, gi, 0, 0))
    ospec = lambda r, c: pl.BlockSpec((per, r, c), lambda gi: (gi, 0, 0))
    ldt = jnp.broadcast_to(log_dt[:, :, None, None], (2, g, 1, p))
    wir, wii, wor, woi, ktf, ktb, lam = pl.pallas_call(
        _s5_tables_body,
        grid=(g // per,),
        in_specs=[ispec(1), ispec(1), ispec(1), ispec(cg), ispec(cg), ispec(cg), ispec(cg)],
        out_specs=[ospec(S5_TILE, sw), ospec(S5_TILE, sw), ospec(sw, S5_TILE), ospec(sw, S5_TILE),
                   ospec(S5_TILE, S5_TILE), ospec(S5_TILE, S5_TILE), ospec(2, sw)],
        out_shape=[jax.ShapeDtypeStruct((g, S5_TILE, sw), BF16)] * 2 + [jax.ShapeDtypeStruct((g, sw, S5_TILE), BF16)] * 2
                  + [jax.ShapeDtypeStruct((g, S5_TILE, S5_TILE), BF16)] * 2 + [jax.ShapeDtypeStruct((g, 2, sw), F32)],
        compiler_params=_params("arbitrary"),
        name="s5_tables",
    )(row4(lam_re), row4(lam_im), ldt, jnp.swapaxes(b_re, -1, -2), jnp.swapaxes(b_im, -1, -2), c_re, c_im)
    gb = S5_GROUPS_PER_STEP
    lam_r = lam[:, 0, :].reshape(g // gb, 1, gb * sw)
    lam_i = lam[:, 1, :].reshape(g // gb, 1, gb * sw)
    return ktf, ktb, wir, wii, wor, woi, lam_r, lam_i


def _s5_gather_chunks(u_ref, l_ref):
    lanes = u_ref.shape[2]
    for s in range(S5_CHUNK):
        l_ref[:, s * lanes:(s + 1) * lanes] = u_ref[s]


def _s5_in_body(u_ref, pm_ref, wr_ref, wi_ref, ub_ref, x_ref, l_ref):
    @pl.when(pl.program_id(1) == 0)
    def _():
        _s5_gather_chunks(u_ref, l_ref)

    ub = _dot(l_ref[...], pm_ref[...]).astype(BF16)
    ub_ref[...] = ub
    sw = 2 * S5_STATE
    for q in range(ub.shape[1] // S5_TILE):
        ug = ub[:, q * S5_TILE:(q + 1) * S5_TILE]
        x_ref[:, 2 * q * sw:(2 * q + 1) * sw] = _dot(ug, wr_ref[q])
        x_ref[:, (2 * q + 1) * sw:(2 * q + 2) * sw] = _dot(ug, wi_ref[q])


def _s5_out_body(ub_ref, s_ref, ktf_ref, ktb_ref, wor_ref, woi_ref, pmt_ref, y_ref, y8_ref, *, col_tiles):
    n = pl.program_id(1)
    sw = 2 * S5_STATE

    @pl.when(n == 0)
    def _():
        for g in range(ub_ref.shape[1] // S5_TILE):
            ug = ub_ref[:, g * S5_TILE:(g + 1) * S5_TILE]
            s_r = s_ref[:, 2 * g * sw:(2 * g + 1) * sw]
            s_i = s_ref[:, (2 * g + 1) * sw:(2 * g + 2) * sw]
            y = _dot(ug, ktf_ref[g]) + _dot(ug, ktb_ref[g]) + _dot(s_r, wor_ref[g]) + _dot(s_i, woi_ref[g])
            y8_ref[:, g * S5_TILE:(g + 1) * S5_TILE] = y.astype(BF16)

    yp = _dot(y8_ref[...], pmt_ref[...])
    lanes = y_ref.shape[2]
    per = yp.shape[1] // lanes
    for nn in range(col_tiles):
        @pl.when(n == nn)
        def _(nn=nn):
            for q in range(per):
                y_ref[nn * per + q] = yp[:, q * lanes:(q + 1) * lanes].astype(y_ref.dtype)


def _s5_scan_body(x_ref, lr_ref, li_ref, s_ref, *scratch, nb, nch, nctx):
    gb = S5_GROUPS_PER_STEP
    sw = 2 * S5_STATE
    xr, xi, sfr, sfi, sbr, sbi = (scratch[k * gb:(k + 1) * gb] for k in range(6))
    for g in range(gb):
        xr[g][...] = x_ref[:, 2 * g * sw:(2 * g + 1) * sw]
        xi[g][...] = x_ref[:, (2 * g + 1) * sw:(2 * g + 2) * sw]
    is_f = lax.broadcasted_iota(jnp.int32, (1, sw), 1) < S5_STATE
    ar = [lr_ref[:, g * sw:(g + 1) * sw] for g in range(gb)]
    ai = [li_ref[:, g * sw:(g + 1) * sw] for g in range(gb)]

    def step(i, carry):
        cb = jnp.where(i < nctx, nctx - 1 - i, nch - 1 - (i - nctx))
        at_f = pl.ds(i, nb, stride=nch)
        at_b = pl.ds(cb, nb, stride=nch)
        new = []
        for g in range(gb):
            sr, si = carry[2 * g], carry[2 * g + 1]
            sfr[g][at_f, :] = sr
            sfi[g][at_f, :] = si
            sbr[g][at_b, :] = sr
            sbi[g][at_b, :] = si
            inr = jnp.where(is_f, xr[g][at_f, :], xr[g][at_b, :])
            ini = jnp.where(is_f, xi[g][at_f, :], xi[g][at_b, :])
            new.append(ar[g] * sr - ai[g] * si + inr)
            new.append(ar[g] * si + ai[g] * sr + ini)
        return tuple(new)

    zero = jnp.zeros((nb, sw), F32)
    lax.fori_loop(0, nch, step, (zero,) * (2 * gb))
    for g in range(gb):
        s_ref[:, 2 * g * sw:(2 * g + 1) * sw] = jnp.where(is_f, sfr[g][...], sbr[g][...]).astype(BF16)
        s_ref[:, (2 * g + 1) * sw:(2 * g + 2) * sw] = jnp.where(is_f, sfi[g][...], sbi[g][...]).astype(BF16)


def _s5_call(u_slabs, ops, nb, period, ctx):
    ktf, ktb, wir, wii, wor, woi, lam_r, lam_i = ops
    _, rows, width = u_slabs.shape
    groups = width // S5_GROUP
    lanes = 128
    lg = lanes // S5_GROUP
    nblk = width // lanes
    blk_w = S5_CHUNK * lanes
    col = 2 * S5_TILE
    col_tiles = blk_w // col
    sw = 2 * S5_STATE
    gb = S5_GROUPS_PER_STEP
    nch = period // S5_CHUNK
    nctx = ctx // S5_CHUNK
    src = jnp.arange(blk_w)
    dst = (src % lanes // S5_GROUP) * S5_TILE + (src // lanes) * S5_GROUP + src % S5_GROUP
    perm = (dst[:, None] == jnp.arange(blk_w)[None, :]).astype(BF16)
    perm_t = perm.T

    u_perm, xin = pl.pallas_call(
        _s5_in_body,
        grid=(nblk, col_tiles),
        in_specs=[pl.BlockSpec((S5_CHUNK, rows, lanes), lambda j, n: (0, 0, j)),
                  pl.BlockSpec((blk_w, col), lambda j, n: (0, n)),
                  pl.BlockSpec((col // S5_TILE, S5_TILE, sw), lambda j, n: (j * col_tiles + n, 0, 0)),
                  pl.BlockSpec((col // S5_TILE, S5_TILE, sw), lambda j, n: (j * col_tiles + n, 0, 0))],
        out_specs=[pl.BlockSpec((rows, col), lambda j, n: (0, j * col_tiles + n)),
                   pl.BlockSpec((rows, col), lambda j, n: (0, j * col_tiles + n))],
        out_shape=[jax.ShapeDtypeStruct((rows, groups * S5_TILE), BF16),
                   jax.ShapeDtypeStruct((rows, groups * 2 * sw), F32)],
        scratch_shapes=[pltpu.VMEM((rows, blk_w), BF16)],
        compiler_params=_params("arbitrary", "arbitrary"),
        name="s5_in",
    )(u_slabs, perm, wir, wii)

    states = pl.pallas_call(
        functools.partial(_s5_scan_body, nb=nb, nch=nch, nctx=nctx),
        grid=(groups // gb,),
        in_specs=[pl.BlockSpec((rows, gb * 2 * sw), lambda i: (0, i)),
                  pl.BlockSpec((None, 1, gb * sw), lambda i: (i, 0, 0)),
                  pl.BlockSpec((None, 1, gb * sw), lambda i: (i, 0, 0))],
        out_specs=pl.BlockSpec((rows, gb * 2 * sw), lambda i: (0, i)),
        out_shape=jax.ShapeDtypeStruct((rows, groups * 2 * sw), BF16),
        scratch_shapes=[pltpu.VMEM((rows, sw), F32) for _ in range(6 * gb)],
        compiler_params=_params("arbitrary"),
        name="s5_scan",
    )(xin, lam_r, lam_i)

    wspec = lambda r, c: pl.BlockSpec((lg, r, c), lambda j, n: (j, 0, 0))
    return pl.pallas_call(
        functools.partial(_s5_out_body, col_tiles=col_tiles),
        grid=(nblk, col_tiles),
        in_specs=[pl.BlockSpec((rows, lg * S5_TILE), lambda j, n: (0, j)),
                  pl.BlockSpec((rows, lg * 2 * sw), lambda j, n: (0, j)),
                  wspec(S5_TILE, S5_TILE), wspec(S5_TILE, S5_TILE), wspec(sw, S5_TILE), wspec(sw, S5_TILE),
                  pl.BlockSpec((blk_w, col), lambda j, n: (0, n))],
        out_specs=pl.BlockSpec((S5_CHUNK, rows, lanes), lambda j, n: (0, 0, j)),
        out_shape=jax.ShapeDtypeStruct((S5_CHUNK, rows, width), BF16),
        scratch_shapes=[pltpu.VMEM((rows, lg * S5_TILE), BF16)],
        compiler_params=_params("arbitrary", "arbitrary"),
        name="s5_out",
    )(u_perm, states, ktf, ktb, wor, woi, perm_t)


def _merge_body(o_ref, z_ref, u_ref, y_ref, h_ref, mb_ref, mc_ref, gn_ref, ds_ref, wglu_ref, bglu_ref,
                wout_ref, lng_ref, lnb_ref, out_ref, cat_ref, *, tm, tpb, ctx):
    rib = _rows_in_batch(pl.program_id(0), tm, tpb)
    gn = gn_ref[...]
    half = tm // 2
    for rows in (slice(0, half), slice(half, tm)):
        for hh in range(GDN_HEADS):
            sl = slice(hh * HEAD_DIM, (hh + 1) * HEAD_DIM)
            oh = o_ref[rows, sl]
            r = lax.rsqrt(jnp.mean(oh * oh, -1, keepdims=True) + RMS_EPS)
            cat_ref[rows, sl] = (oh * r * gn * _silu(z_ref[rows, sl].astype(F32))).astype(BF16)
        s = y_ref[rows, :].astype(F32) + ds_ref[...] * u_ref[rows, :].astype(F32)
        s = 0.5 * s * (1.0 + jnp.tanh(math.sqrt(2.0 / math.pi) * (s + 0.044715 * (s * s * s))))
        s = s * jax.nn.sigmoid(_dot(s.astype(BF16), wglu_ref[...]) + bglu_ref[...])
        cat_ref[rows, GDN_WIDTH:] = s.astype(BF16)
        mix = _dot(cat_ref[rows, :], wout_ref[...])
        gate = jnp.where(rib[rows] < ctx, mc_ref[2:3, :], mb_ref[2:3, :])
        out_ref[rows, :] = _layernorm(ALPHA * h_ref[rows, :] + gate * mix, lng_ref[...], lnb_ref[...])


def _merge_call(o, proj, y, h, mod, gn, d_skip, w_glu, b_glu, w_out, ln_g, ln_b, nb, period, ctx):
    t, d = h.shape
    w = GDN_WIDTH
    sw = d - w
    tm = _pick(period, 544, 32)
    tpb = period // tm
    body = functools.partial(_merge_body, tm=tm, tpb=tpb, ctx=ctx)
    const = lambda r, c: pl.BlockSpec((r, c), lambda i: (0, 0), pipeline_mode=pl.Buffered(1))
    return pl.pallas_call(
        body,
        grid=(t // tm,),
        in_specs=[pl.BlockSpec((tm, w), lambda i: (i, 0)),
                  pl.BlockSpec((tm, w), lambda i: (i, 3)),
                  pl.BlockSpec((tm, sw), lambda i: (i, 4 * w // sw)),
                  pl.BlockSpec((tm, sw), lambda i: (i, 0)),
                  pl.BlockSpec((tm, d), lambda i: (i, 0)),
                  pl.BlockSpec((None, 6, d), lambda i: (i // tpb, 0, 0)),
                  pl.BlockSpec((None, 6, d), lambda i: (nb, 0, 0)),
                  const(1, HEAD_DIM), const(1, sw), const(sw, sw), const(1, sw), const(d, d),
                  const(1, d), const(1, d)],
        out_specs=pl.BlockSpec((tm, d), lambda i: (i, 0)),
        out_shape=jax.ShapeDtypeStruct((t, d), F32),
        scratch_shapes=[pltpu.VMEM((tm, d), BF16)],
        compiler_params=_params("arbitrary"),
        name="even_merge",
    )(o, proj, proj, y, h, mod, mod, gn, d_skip, w_glu, b_glu, w_out, ln_g, ln_b)


def _ffn_body(x_ref, xh_ref, mb_ref, mc_ref, wv_ref, wg_ref, cwv_ref, cwg_ref, cbv_ref, cbg_ref, wd_ref,
              lng_ref, lnb_ref, o_ref, xb_ref, *up_refs, tm, tpb, ctx, period):
    i = pl.program_id(0)
    j = pl.program_id(1)
    rib = _rows_in_batch(i, tm, tpb)
    n = tm + 32

    @pl.when(j == 0)
    def _():
        xb_ref[16:tm + 16, :] = _modulate(x_ref[...], rib < ctx, mb_ref, mc_ref, 3, 4).astype(BF16)
        base = (i % tpb) * tm
        off = lax.broadcasted_iota(jnp.int32, (16, 1), 0)
        xh = xh_ref[...]
        next_ok = (base + tm != period) & (base + tm != ctx)
        prev_ok = (base != 0) & (base != ctx)
        nxt = _modulate(xh[0:16], base + tm + off < ctx, mb_ref, mc_ref, 3, 4)
        prv = _modulate(xh[16:32], base - 16 + off < ctx, mb_ref, mc_ref, 3, 4)
        xb_ref[tm + 16:n, :] = jnp.where(next_ok, nxt, 0.0).astype(BF16)
        xb_ref[0:16, :] = jnp.where(prev_ok, prv, 0.0).astype(BF16)
        o_ref[...] = jnp.zeros_like(o_ref)

    xb = xb_ref[...]
    edge_inside_tile = ctx % tm != 0

    def conv(u_ref, cw_ref, cb_ref, sl):
        cw = cw_ref[:, sl]
        up = u_ref[15:tm + 15, :]
        un = u_ref[17:tm + 17, :]
        if edge_inside_tile:
            up = jnp.where(rib != ctx, up, 0.0)
            un = jnp.where(rib != ctx - 1, un, 0.0)
        return up * cw[0:1] + u_ref[16:tm + 16, :] * cw[1:2] + un * cw[2:3] + cb_ref[:, sl]

    fc = wv_ref.shape[1]
    slabs = [slice(a, a + FFN_SLAB) for a in range(0, fc, FFN_SLAB)]
    for si, sl in enumerate(slabs):
        up_refs[2 * si][...] = _dot(xb, wv_ref[:, sl])
        up_refs[2 * si + 1][...] = _dot(xb, wg_ref[:, sl])
    for si, sl in enumerate(slabs):
        act = conv(up_refs[2 * si], cwv_ref, cbv_ref, sl) * _silu(conv(up_refs[2 * si + 1], cwg_ref, cbg_ref, sl))
        o_ref[...] += _dot(act.astype(BF16), wd_ref[sl, :])

    @pl.when(j == pl.num_programs(1) - 1)
    def _():
        g = jnp.where(rib < ctx, mc_ref[5:6, :], mb_ref[5:6, :])
        o_ref[...] = _layernorm(ALPHA * x_ref[...] + g * o_ref[...], lng_ref[...], lnb_ref[...])


def _ffn_call(h, mod, layer, w_up, conv_w, conv_b, w_down, ln_g, ln_b, nb, period, ctx):
    t, d = h.shape
    f = w_down.shape[1]
    tm = _pick(period, 544, 16)
    tpb = period // tm
    nt = t // tm
    fc = _pick(f, FFN_CHUNK, FFN_SLAB)
    nf = f // fc
    h3 = h.reshape(nt, tm, d)
    pad = jnp.zeros((1, 16, d), h.dtype)
    halo = jnp.concatenate([jnp.concatenate([h3[1:, :16], pad], 0),
                            jnp.concatenate([pad, h3[:-1, tm - 16:]], 0)], axis=1)
    body = functools.partial(_ffn_body, tm=tm, tpb=tpb, ctx=ctx, period=period)
    return pl.pallas_call(
        body,
        grid=(nt, nf),
        in_specs=[pl.BlockSpec((tm, d), lambda i, j: (i, 0)),
                  pl.BlockSpec((None, 32, d), lambda i, j: (i, 0, 0)),
                  pl.BlockSpec((None, 6, d), lambda i, j: (i // tpb, 0, 0)),
                  pl.BlockSpec((None, 6, d), lambda i, j: (nb, 0, 0)),
                  pl.BlockSpec((None, d, fc), lambda i, j: (layer, 0, j)),
                  pl.BlockSpec((None, d, fc), lambda i, j: (layer, 0, nf + j)),
                  pl.BlockSpec((None, 3, fc), lambda i, j: (layer, 0, j)),
                  pl.BlockSpec((None, 3, fc), lambda i, j: (layer, 0, nf + j)),
                  pl.BlockSpec((None, 1, fc), lambda i, j: (layer, 0, j)),
                  pl.BlockSpec((None, 1, fc), lambda i, j: (layer, 0, nf + j)),
                  pl.BlockSpec((None, fc, d), lambda i, j: (layer, j, 0)),
                  pl.BlockSpec((1, d), lambda i, j: (0, 0)),
                  pl.BlockSpec((1, d), lambda i, j: (0, 0))],
        out_specs=pl.BlockSpec((tm, d), lambda i, j: (i, 0)),
        out_shape=jax.ShapeDtypeStruct((t, d), F32),
        scratch_shapes=([pltpu.VMEM((tm + 32, d), BF16)]
                        + [pltpu.VMEM((tm + 32, FFN_SLAB), F32) for _ in range(2 * fc // FFN_SLAB)]),
        compiler_params=_params("arbitrary", "arbitrary"),
        name="conv_ffn",
    )(h, halo, mod, mod, w_up, w_up, conv_w, conv_w, conv_b, conv_b, w_down, ln_g, ln_b)


def _qkv_body(x_ref, mb_ref, mc_ref, w_ref, cos_ref, sin_ref, qn_ref, kn_ref, o_ref, xb_ref,
              *, tm, tpb, ctx, nq_tiles):
    i = pl.program_id(0)
    j = pl.program_id(1)

    @pl.when(j == 0)
    def _():
        rib = _rows_in_batch(i, tm, tpb)
        xb_ref[...] = _modulate(x_ref[...], rib < ctx, mb_ref, mc_ref, 0, 1).astype(BF16)

    cos = cos_ref[...]
    sin = sin_ref[...]
    lane = lax.broadcasted_iota(jnp.int32, (1, HEAD_DIM), 1)
    first = (lane % 64) < 32
    acc = _dot(xb_ref[...], w_ref[...])
    heads = acc.shape[1] // HEAD_DIM

    def normrope(xh, wn, scale):
        xn = xh * lax.rsqrt(jnp.mean(xh * xh, -1, keepdims=True) + RMS_EPS) * wn
        partner = jnp.where(first, pltpu.roll(xn, HEAD_DIM - 32, 1), pltpu.roll(xn, 32, 1))
        return (xn * cos + partner * sin) * scale

    @pl.when(j < nq_tiles)
    def _():
        for hh in range(heads):
            sl = slice(hh * HEAD_DIM, (hh + 1) * HEAD_DIM)
            o_ref[:, sl] = normrope(acc[:, sl], qn_ref[...], Q_SCALE).astype(BF16)

    @pl.when(j == nq_tiles)
    def _():
        for hh in range(heads):
            sl = slice(hh * HEAD_DIM, (hh + 1) * HEAD_DIM)
            if hh < ATT_KV_HEADS:
                o_ref[:, sl] = normrope(acc[:, sl], kn_ref[...], 1.0).astype(BF16)
            else:
                o_ref[:, sl] = acc[:, sl].astype(BF16)


def _qkv_call(h, mod, w_in, cos, sin, qn, kn, nb, period, ctx):
    t, d = h.shape
    n = w_in.shape[1]
    tn = 2 * ATT_KV_HEADS * HEAD_DIM
    nq_tiles = (n - tn) // tn
    tm = _pick(period, PROJ_ROW_TILE, 16)
    tpb = period // tm
    body = functools.partial(_qkv_body, tm=tm, tpb=tpb, ctx=ctx, nq_tiles=nq_tiles)
    return pl.pallas_call(
        body,
        grid=(t // tm, n // tn),
        in_specs=[pl.BlockSpec((tm, d), lambda i, j: (i, 0)),
                  pl.BlockSpec((None, 6, d), lambda i, j: (i // tpb, 0, 0)),
                  pl.BlockSpec((None, 6, d), lambda i, j: (nb, 0, 0)),
                  pl.BlockSpec((d, tn), lambda i, j: (0, j)),
                  pl.BlockSpec((tm, HEAD_DIM), lambda i, j: (i % tpb, 0)),
                  pl.BlockSpec((tm, HEAD_DIM), lambda i, j: (i % tpb, 0)),
                  pl.BlockSpec((1, HEAD_DIM), lambda i, j: (0, 0)),
                  pl.BlockSpec((1, HEAD_DIM), lambda i, j: (0, 0))],
        out_specs=pl.BlockSpec((tm, tn), lambda i, j: (i, j)),
        out_shape=jax.ShapeDtypeStruct((t, n), BF16),
        scratch_shapes=[pltpu.VMEM((tm, d), BF16)],
        compiler_params=_params("arbitrary", "arbitrary"),
        name="odd_qkv",
    )(h, mod, mod, w_in, cos, sin, qn, kn)


def _rope_tables(seq, ctx):
    rows = seq // GRID_W
    row = jnp.repeat(jnp.arange(rows, dtype=F32), GRID_W)
    col = jnp.tile(jnp.arange(GRID_W, dtype=F32), rows)
    half = HEAD_DIM // 4
    inv = ROPE_THETA ** (-jnp.arange(half, dtype=F32) / half)
    ar = row[:, None] * inv
    ac = col[:, None] * inv
    cos = jnp.concatenate([jnp.cos(ar), jnp.cos(ar), jnp.cos(ac), jnp.cos(ac)], -1)
    sin = jnp.concatenate([-jnp.sin(ar), jnp.sin(ar), -jnp.sin(ac), jnp.sin(ac)], -1)
    cos = jnp.concatenate([jnp.ones((ctx, HEAD_DIM), F32), cos], 0)
    sin = jnp.concatenate([jnp.zeros((ctx, HEAD_DIM), F32), sin], 0)
    return cos, sin


def _attn_body(q_ref, k_ref, v_ref, o_ref, *, tq, key_blocks):
    hd = HEAD_DIM
    q = jnp.concatenate([q_ref[:, g * hd:(g + 1) * hd] for g in range(ATT_GROUP)], axis=0)
    rows = ATT_GROUP * tq
    m = jnp.full((rows, 1), -1e30, F32)
    l = jnp.zeros((rows, 1), F32)
    acc = jnp.zeros((rows, hd), F32)
    s_next = _dot_nt(q, k_ref[key_blocks[0][0]:key_blocks[0][1], :])
    for n, (k0, k1) in enumerate(key_blocks):
        s = s_next
        if n + 1 < len(key_blocks):
            s_next = _dot_nt(q, k_ref[key_blocks[n + 1][0]:key_blocks[n + 1][1], :])
        m_new = jnp.maximum(m, jnp.max(s, -1, keepdims=True))
        a = jnp.exp2(m - m_new)
        p = jnp.exp2(s - m_new)
        l = a * l + jnp.sum(p, -1, keepdims=True)
        acc = a * acc + _dot(p.astype(BF16), v_ref[k0:k1, :])
        m = m_new
    out = acc / l
    for g in range(ATT_GROUP):
        o_ref[:, g * hd:(g + 1) * hd] = out[g * tq:(g + 1) * tq].astype(o_ref.dtype)


def _key_blocks(period, count, align):
    tiles = period // align
    assert tiles * align == period and tiles >= count
    sizes = [tiles // count + (1 if n >= count - tiles % count else 0) for n in range(count)]
    edges = [0]
    for sz in sizes:
        edges.append(edges[-1] + sz * align)
    return tuple(zip(edges[:-1], edges[1:]))


def _attn_call(qkv, nb, seq, ctx):
    period = seq + ctx
    tq = _pick(math.gcd(seq, ctx), ATT_Q_TILE, 16)
    gw = ATT_GROUP * HEAD_DIM
    qpb = seq // tq
    rpb = period // tq
    skip = ctx // tq
    kcol = ATT_HEADS
    vcol = ATT_HEADS + ATT_KV_HEADS
    align = math.gcd(period, MXU_TILE)
    blocks = _key_blocks(period, min(ATT_KEY_BLOCKS, period // align), align)
    body = functools.partial(_attn_body, tq=tq, key_blocks=blocks)
    return pl.pallas_call(
        body,
        grid=(nb, ATT_KV_HEADS, qpb),
        in_specs=[pl.BlockSpec((tq, gw), lambda b, kv, qi: (b * rpb + skip + qi, kv)),
                  pl.BlockSpec((period, HEAD_DIM), lambda b, kv, qi: (b, kcol + kv)),
                  pl.BlockSpec((period, HEAD_DIM), lambda b, kv, qi: (b, vcol + kv))],
        out_specs=pl.BlockSpec((tq, gw), lambda b, kv, qi: (b * qpb + qi, kv)),
        out_shape=jax.ShapeDtypeStruct((nb * seq, ATT_HEADS * HEAD_DIM), BF16),
        compiler_params=_params("arbitrary", "arbitrary", "arbitrary"),
        name="gqa_attention",
    )(qkv, qkv, qkv)


def _outproj_body(a_ref, h_ref, mb_ref, w_ref, lng_ref, lnb_ref, o_ref):
    half = a_ref.shape[0] // 2
    parts = (slice(0, half), slice(half, 2 * half))
    mixes = [_dot(a_ref[rows, :], w_ref[...]) for rows in parts]
    for rows, mix in zip(parts, mixes):
        o_ref[rows, :] = _layernorm(ALPHA * h_ref[rows, :] + mb_ref[2:3, :] * mix, lng_ref[...], lnb_ref[...])


def _outproj_call(att, h, mod, w_out, ln_g, ln_b, nb, seq, ctx):
    d = h.shape[1]
    period = seq + ctx
    tm = _pick(math.gcd(seq, ctx), 256, 16)
    qpb = seq // tm
    rpb = period // tm
    skip = ctx // tm
    return pl.pallas_call(
        _outproj_body,
        grid=(nb * qpb,),
        in_specs=[pl.BlockSpec((tm, d), lambda i: (i, 0)),
                  pl.BlockSpec((tm, d), lambda i: ((i // qpb) * rpb + skip + i % qpb, 0)),
                  pl.BlockSpec((None, 6, d), lambda i: (i // qpb, 0, 0)),
                  pl.BlockSpec((d, d), lambda i: (0, 0)),
                  pl.BlockSpec((1, d), lambda i: (0, 0)),
                  pl.BlockSpec((1, d), lambda i: (0, 0))],
        out_specs=pl.BlockSpec((tm, d), lambda i: (i, 0)),
        out_shape=jax.ShapeDtypeStruct((nb * seq, d), F32),
        compiler_params=_params("arbitrary"),
        name="odd_outproj",
    )(att, h, mod, w_out, ln_g, ln_b)


def kernel(x, c, ctx, c_ctx, mod_w, mod_b, ln1_g, ln1_b, ln2_g, ln2_b, ffn_w_up, ffn_conv_w, ffn_conv_b, ffn_w_down, e_w_in, e_conv_qkv, e_a_log, e_dt_bias, e_gdn_norm, e_lam_re, e_lam_im, e_log_dt, e_b_re, e_b_im, e_c_re, e_c_im, e_d_skip, e_w_glu, e_b_glu, e_w_out, o_w_in, o_q_norm, o_k_norm, o_w_out):
    nb, seq, d = x.shape
    nctx = ctx.shape[1]
    period = nctx + seq
    t = nb * period
    w = GDN_WIDTH
    assert mod_w.shape[0] == DEPTH == 2 and nb < 8

    h = jnp.concatenate([ctx, x], axis=1).reshape(t, d)
    cs = jnp.zeros((8, d), F32).at[:nb].set(c).at[nb].set(c_ctx)
    mod = _mod_call(cs, mod_w, mod_b).reshape(DEPTH, 8, 6, d)
    row = lambda a: a.reshape(1, -1)

    w_in = e_w_in[0]
    gates_at = 4 * w
    w_main = jnp.concatenate([w_in[:, :gates_at], w_in[:, gates_at + 4 * GDN_HEADS:]], axis=1).astype(BF16)
    w_ab = jnp.pad(w_in[:, gates_at:gates_at + 4 * GDN_HEADS], ((0, 0), (0, 128 - 4 * GDN_HEADS))).astype(BF16)
    proj, ab = _inproj_even_call(h, mod[0], w_main, w_ab, nb, period, nctx)
    pad_row = lambda a: jnp.pad(a.reshape(1, -1), ((0, 0), (0, 128 - 2 * GDN_HEADS)))
    qkvn, kt3, gh, gct3 = _gdn_prep_call(proj, ab, e_conv_qkv[0], pad_row(e_a_log[0]), pad_row(e_dt_bias[0]),
                                         period, nctx)
    o = _gdn_scan_call(qkvn, kt3, gh, gct3, nb, period, nctx)

    ops = _s5_tables_call(e_lam_re[0], e_lam_im[0], e_log_dt[0], e_b_re[0], e_b_im[0], e_c_re[0], e_c_im[0])
    u_slabs = proj[:, 4 * w:].reshape(t // S5_CHUNK, S5_CHUNK, d - w).transpose(1, 0, 2)
    y = _s5_call(u_slabs, ops, nb, period, nctx).transpose(1, 0, 2).reshape(t, d - w)

    h = _merge_call(o, proj, y, h, mod[0], row(e_gdn_norm[0]), row(e_d_skip[0]), e_w_glu[0].astype(BF16),
                    row(e_b_glu[0]), e_w_out[0].astype(BF16), row(ln1_g[0]), row(ln1_b[0]), nb, period, nctx)
    w_up, w_down = ffn_w_up.astype(BF16), ffn_w_down.astype(BF16)
    conv_b = ffn_conv_b[:, None, :]
    h = _ffn_call(h, mod[0], 0, w_up, ffn_conv_w, conv_b, w_down, row(ln2_g[0]), row(ln2_b[0]), nb, period, nctx)

    cos, sin = _rope_tables(seq, nctx)
    qkv = _qkv_call(h, mod[1], o_w_in[0].astype(BF16), cos, sin, row(o_q_norm[0]), row(o_k_norm[0]),
                    nb, period, nctx)
    att = _attn_call(qkv, nb, seq, nctx)
    hl = _outproj_call(att, h, mod[1], o_w_out[0].astype(BF16), row(ln1_g[1]), row(ln1_b[1]), nb, seq, nctx)
    out = _ffn_call(hl, mod[1], 1, w_up, ffn_conv_w, conv_b, w_down, row(ln2_g[1]), row(ln2_b[1]), nb, seq, 0)
    return out.reshape(nb, seq, d)
```
